```python
import math
import jax, jax.numpy as jnp
from jax import lax
import numpy as np

D_MODEL = 1024
BATCH = 8
SEQ = 2048
DEPTH = 1
DEC_BATCH = 128
DEC_SEQ = 8
PAST_LEN = 16384
PAGE_SIZE = 128

DN_HEADS = 8
DN_DK = 128
DN_DV = 128
DN_CONV = 4
DN_CHUNK = 64
GM_WIDTH = 1024
GM_GROUPS = 8
GM_CHUNK = 128
N_EXPERTS = 32
TOP_K = 4
D_EXPERT = 1024
SWIGLU_LIMIT = 7.0
SWIGLU_ALPHA = 1.702
MOE_BLOCK = 128
PLE_DIM = 256
EPS = 1e-6

QK_W = DN_HEADS * DN_DK
V_W = DN_HEADS * DN_DV
QKV_W = 2 * QK_W + V_W
IN_SIZES = (QKV_W, DN_HEADS, DN_HEADS, V_W, GM_WIDTH, GM_WIDTH, D_MODEL, D_MODEL)
IN_COLS = sum(IN_SIZES)

kernel_name = 'hybrid_gdn_chunkmlp_moe_step'


def rmsnorm(x, g):
    xf = x.astype(jnp.float32)
    y = xf * lax.rsqrt(jnp.mean(xf * xf, axis=-1, keepdims=True) + EPS)
    return (y * g.astype(jnp.float32)).astype(x.dtype)


def layernorm(x, g, b):
    xf = x.astype(jnp.float32)
    mu = jnp.mean(xf, axis=-1, keepdims=True)
    xc = xf - mu
    y = xc * lax.rsqrt(jnp.mean(xc * xc, axis=-1, keepdims=True) + EPS)
    return (y * g.astype(jnp.float32) + b.astype(jnp.float32)).astype(x.dtype)


def l2norm(x):
    xf = x.astype(jnp.float32)
    return xf * lax.rsqrt(jnp.sum(xf * xf, axis=-1, keepdims=True) + EPS)


def short_conv(x, buf, w):
    L = x.shape[1]
    xc = jnp.concatenate([buf.astype(x.dtype), x], axis=1)
    y = xc[:, 0:L] * w[0]
    for i in range(1, DN_CONV):
        y = y + xc[:, i:i + L] * w[i]
    return jax.nn.silu(y), xc[:, -(DN_CONV - 1):]


def gated_delta_rule(q, k, v, g, beta, s0):
    B, L, H, DK = q.shape
    DV = v.shape[-1]
    C = math.gcd(L, DN_CHUNK)
    NC = L // C

    def to_chunks(t):
        t = t.reshape((B, NC, C, H) + t.shape[3:])
        return t.transpose((1, 0, 3, 2) + tuple(range(4, t.ndim)))

    qc, kc, vc = to_chunks(q), to_chunks(k), to_chunks(v)
    bc = to_chunks(beta)
    gcum = jnp.cumsum(to_chunks(g), axis=-1)
    idx = jnp.arange(C)
    incl = idx[:, None] >= idx[None, :]
    strict = idx[:, None] > idx[None, :]
    decay = jnp.exp(jnp.where(incl, gcum[..., :, None] - gcum[..., None, :], -jnp.inf))
    kk = jnp.einsum('nbhcd,nbhsd->nbhcs', kc, kc)
    a_low = jnp.where(strict, kk * decay * bc[..., None, :], 0.0)
    t_mat = a_low + jnp.eye(C, dtype=jnp.float32)
    gam = jnp.exp(gcum)
    rhs = jnp.concatenate([vc, gam[..., None] * kc], axis=-1)
    sol = lax.linalg.triangular_solve(t_mat, rhs, left_side=True, lower=True, unit_diagonal=True)
    u_base, w_state = sol[..., :DV], sol[..., DV:]
    m_intra = jnp.einsum('nbhcd,nbhsd->nbhcs', qc, kc) * decay * bc[..., None, :]
    q_dec = gam[..., None] * qc
    k_dec = kc * (bc * jnp.exp(gcum[..., -1:] - gcum))[..., None]
    g_end = gam[..., -1]

    def step(S, xs):
        ub_c, w_c, m_c, qd_c, kd_c, ge_c = xs
        u = ub_c - jnp.einsum('bhck,bhkv->bhcv', w_c, S)
        o = jnp.einsum('bhck,bhkv->bhcv', qd_c, S) + jnp.einsum('bhcs,bhsv->bhcv', m_c, u)
        S = ge_c[..., None, None] * S + jnp.einsum('bhck,bhcv->bhkv', kd_c, u)
        return S, o

    s_fin, o = lax.scan(step, s0, (u_base, w_state, m_intra, q_dec, k_dec, g_end))
    o = o.transpose(1, 0, 3, 2, 4).reshape(B, L, H, DV)
    return o, s_fin


def deltanet_branch(qkv_pre, a_raw, b_raw, z_gate, conv_buf, s0, conv_w, a_log, dt_bias, onorm_g, w_proj):
    B, L, _ = qkv_pre.shape
    qkv, new_buf = short_conv(qkv_pre, conv_buf, conv_w)
    q = qkv[..., :QK_W].reshape(B, L, DN_HEADS, DN_DK)
    k = qkv[..., QK_W:2 * QK_W].reshape(B, L, DN_HEADS, DN_DK)
    v = qkv[..., 2 * QK_W:].reshape(B, L, DN_HEADS, DN_DV).astype(jnp.float32)
    q = l2norm(q) * (DN_DK ** -0.5)
    k = l2norm(k)
    g = -jnp.exp(a_log.astype(jnp.float32)) * jax.nn.softplus(a_raw.astype(jnp.float32) + dt_bias.astype(jnp.float32))
    beta = jax.nn.sigmoid(b_raw.astype(jnp.float32))
    o, s_new = gated_delta_rule(q, k, v, g, beta, s0.astype(jnp.float32))
    o = rmsnorm(o, onorm_g) * jax.nn.silu(z_gate.reshape(B, L, DN_HEADS, DN_DV).astype(jnp.float32))
    y = o.reshape(B, L, V_W).astype(qkv_pre.dtype) @ w_proj
    return y, s_new.astype(qkv_pre.dtype), new_buf


def chunk_mlp_branch(gu, gv, ln_g, ln_b, ws, bs, w_proj):
    B, L, _ = gu.shape
    C = min(GM_CHUNK, L)
    NC = L // C
    u = jax.nn.gelu(gu, approximate=False)
    v = layernorm(jax.nn.gelu(gv, approximate=False), ln_g, ln_b)
    tri = jnp.tril(jnp.ones((C, C), dtype=bool))
    w_mix = jnp.where(tri, ws[:, :C, :C], 0.0).astype(v.dtype)
    vc = v.reshape(B, NC, C, GM_GROUPS, GM_WIDTH // GM_GROUPS)
    s = jnp.einsum('gts,bnsgc->bntgc', w_mix, vc) + bs[:, :C].T[:, :, None]
    y = (u * s.reshape(B, L, GM_WIDTH)) @ w_proj
    return y, v


def moe_ffn(h, router_w, router_b, w1, b1, w2, b2):
    n, d = h.shape
    nk = n * TOP_K
    logits = h.astype(jnp.float32) @ router_w.astype(jnp.float32) + router_b.astype(jnp.float32)
    top_vals, top_idx = lax.top_k(logits, TOP_K)
    gates = jax.nn.softmax(top_vals, axis=-1).astype(h.dtype)
    flat_e = top_idx.reshape(-1)
    flat_tok = jnp.arange(nk, dtype=jnp.int32) // TOP_K
    order = jnp.argsort(flat_e)
    sorted_e = flat_e[order]
    counts = jnp.bincount(flat_e, length=N_EXPERTS)
    padded = (counts + MOE_BLOCK - 1) // MOE_BLOCK * MOE_BLOCK
    pad_end = jnp.cumsum(padded)
    pad_start = pad_end - padded
    start = jnp.cumsum(counts) - counts
    dest = pad_start[sorted_e] + jnp.arange(nk) - start[sorted_e]
    n_blocks = -(-nk // MOE_BLOCK) + N_EXPERTS
    rows = n_blocks * MOE_BLOCK
    row_tok = jnp.full((rows,), n, jnp.int32).at[dest].set(flat_tok[order])
    row_gate = jnp.zeros((rows,), h.dtype).at[dest].set(gates.reshape(-1)[order])
    block_e = jnp.minimum(jnp.searchsorted(pad_end, jnp.arange(n_blocks) * MOE_BLOCK, side='right'), N_EXPERTS - 1)
    h_pad = jnp.concatenate([h, jnp.zeros((1, d), h.dtype)], axis=0)
    xb = h_pad[row_tok].reshape(n_blocks, MOE_BLOCK, d)

    def expert_block(args):
        xe, e = args
        hid = xe @ w1[e] + b1[e]
        gate = jnp.minimum(hid[:, :D_EXPERT], SWIGLU_LIMIT)
        up = jnp.clip(hid[:, D_EXPERT:], -SWIGLU_LIMIT, SWIGLU_LIMIT)
        act = gate * jax.nn.sigmoid(SWIGLU_ALPHA * gate) * (up + 1.0)
        return act @ w2[e] + b2[e]

    yb = lax.map(expert_block, (xb, block_e)).reshape(rows, d) * row_gate[:, None]
    return jnp.zeros((n + 1, d), h.dtype).at[row_tok].add(yb)[:n]


def hybrid_layer(x, p, s0, conv_buf, lw):
    B, L, D = x.shape
    h = rmsnorm(x, lw['norm1_g'])
    z = h @ lw['w_in']
    cuts = [int(c) for c in np.cumsum(IN_SIZES)[:-1]]
    qkv_pre, a_raw, b_raw, z_gate, gu, gv, m_a, m_b = jnp.split(z, cuts, axis=-1)
    y_a, s_new, buf_new = deltanet_branch(qkv_pre, a_raw, b_raw, z_gate, conv_buf, s0, lw['conv_w'],
                                          lw['a_log'], lw['dt_bias'], lw['dn_norm_g'], lw['w_proj_a'])
    y_b, v_rows = chunk_mlp_branch(gu, gv, lw['gm_ln_g'], lw['gm_ln_b'], lw['gm_ws'], lw['gm_bs'], lw['w_proj_b'])
    mixed = jax.nn.sigmoid(m_a) * y_a + jax.nn.sigmoid(m_b) * y_b
    x = x + mixed @ lw['w_out']
    h2 = rmsnorm(x, lw['norm2_g']).reshape(B * L, D)
    x = x + moe_ffn(h2, lw['router_w'], lw['router_b'], lw['moe_w1'], lw['moe_b1'],
                    lw['moe_w2'], lw['moe_b2']).reshape(B, L, D)
    h3 = rmsnorm(x, lw['norm3_g'])
    x = x + jax.nn.sigmoid(h3 @ lw['ple_gate_w']) * (p.astype(x.dtype) @ lw['ple_w'])
    return x, s_new, buf_new, v_rows


def setup_inputs(seed: int = 0) -> dict:
    key = jax.random.key(seed)
    ks = iter(jax.random.split(key, 40))

    def nrm(shape, scale):
        return jax.random.normal(next(ks), shape, jnp.float32) * scale

    a_log = jnp.log(jax.random.uniform(next(ks), (DEPTH, DN_HEADS), jnp.float32, 1.0, 16.0))
    dt = jnp.exp(jax.random.uniform(next(ks), (DEPTH, DN_HEADS), jnp.float32, math.log(1e-3), math.log(1e-1)))
    dt_bias = dt + jnp.log(-jnp.expm1(-dt))
    return {
        'x_prompt': nrm((BATCH, SEQ, D_MODEL), 1.0),
        'x_sample': nrm((DEC_BATCH, DEC_SEQ, D_MODEL), 1.0),
        'state_delta': nrm((DEPTH, DEC_BATCH, DN_HEADS, DN_DK, DN_DV), 0.5),
        'state_conv': nrm((DEPTH, DEC_BATCH, DN_CONV - 1, QKV_W), 1.0),
        'p_prompt': nrm((DEPTH, BATCH, SEQ, PLE_DIM), 1.0),
        'p_sample': nrm((DEPTH, DEC_BATCH, DEC_SEQ, PLE_DIM), 1.0),
        'norm1_g': 1.0 + nrm((DEPTH, D_MODEL), 0.02),
        'w_in': nrm((DEPTH, D_MODEL, IN_COLS), D_MODEL ** -0.5),
        'conv_w': nrm((DEPTH, DN_CONV, QKV_W), DN_CONV ** -0.5),
        'a_log': a_log,
        'dt_bias': dt_bias,
        'dn_norm_g': 1.0 + nrm((DEPTH, DN_DV), 0.02),
        'w_proj_a': nrm((DEPTH, V_W, D_MODEL), V_W ** -0.5),
        'gm_ln_g': 1.0 + nrm((DEPTH, GM_WIDTH), 0.02),
        'gm_ln_b': nrm((DEPTH, GM_WIDTH), 0.02),
        'gm_ws': nrm((DEPTH, GM_GROUPS, GM_CHUNK, GM_CHUNK), 0.5 * GM_CHUNK ** -0.5),
        'gm_bs': 1.0 + nrm((DEPTH, GM_GROUPS, GM_CHUNK), 0.02),
        'w_proj_b': nrm((DEPTH, GM_WIDTH, D_MODEL), GM_WIDTH ** -0.5),
        'w_out': nrm((DEPTH, D_MODEL, D_MODEL), D_MODEL ** -0.5),
        'norm2_g': 1.0 + nrm((DEPTH, D_MODEL), 0.02),
        'router_w': nrm((DEPTH, D_MODEL, N_EXPERTS), D_MODEL ** -0.5),
        'router_b': nrm((DEPTH, N_EXPERTS), 0.01),
        'moe_w1': nrm((DEPTH, N_EXPERTS, D_MODEL, 2 * D_EXPERT), D_MODEL ** -0.5),
        'moe_b1': nrm((DEPTH, N_EXPERTS, 2 * D_EXPERT), 0.01),
        'moe_w2': nrm((DEPTH, N_EXPERTS, D_EXPERT, D_MODEL), D_EXPERT ** -0.5),
        'moe_b2': nrm((DEPTH, N_EXPERTS, D_MODEL), 0.01),
        'norm3_g': 1.0 + nrm((DEPTH, D_MODEL), 0.02),
        'ple_w': nrm((DEPTH, PLE_DIM, D_MODEL), PLE_DIM ** -0.5),
        'ple_gate_w': nrm((DEPTH, D_MODEL, D_MODEL), D_MODEL ** -0.5),
        'final_norm_g': 1.0 + nrm((D_MODEL,), 0.02),
    }


def reference(x_prompt, x_sample, state_delta, state_conv, p_prompt, p_sample,
              norm1_g, w_in, conv_w, a_log, dt_bias, dn_norm_g, w_proj_a,
              gm_ln_g, gm_ln_b, gm_ws, gm_bs, w_proj_b, w_out, norm2_g,
              router_w, router_b, moe_w1, moe_b1, moe_w2, moe_b2,
              norm3_g, ple_w, ple_gate_w, final_norm_g):
    xp, xs = x_prompt, x_sample
    bp = x_prompt.shape[0]
    sd_p, sc_p, sd_s, sc_s, gv_s = [], [], [], [], []
    for i in range(DEPTH):
        lw = {
            'norm1_g': norm1_g[i], 'w_in': w_in[i], 'conv_w': conv_w[i], 'a_log': a_log[i],
            'dt_bias': dt_bias[i], 'dn_norm_g': dn_norm_g[i], 'w_proj_a': w_proj_a[i],
            'gm_ln_g': gm_ln_g[i], 'gm_ln_b': gm_ln_b[i], 'gm_ws': gm_ws[i], 'gm_bs': gm_bs[i],
            'w_proj_b': w_proj_b[i], 'w_out': w_out[i], 'norm2_g': norm2_g[i],
            'router_w': router_w[i], 'router_b': router_b[i], 'moe_w1': moe_w1[i],
            'moe_b1': moe_b1[i], 'moe_w2': moe_w2[i], 'moe_b2': moe_b2[i],
            'norm3_g': norm3_g[i], 'ple_w': ple_w[i], 'ple_gate_w': ple_gate_w[i],
        }
        s0_p = jnp.zeros((bp, DN_HEADS, DN_DK, DN_DV), jnp.float32)
        buf0_p = jnp.zeros((bp, DN_CONV - 1, QKV_W), xp.dtype)
        xp, s_p, b_p, _ = hybrid_layer(xp, p_prompt[i], s0_p, buf0_p, lw)
        xs, s_s, b_s, v_s = hybrid_layer(xs, p_sample[i], state_delta[i], state_conv[i], lw)
        sd_p.append(s_p)
        sc_p.append(b_p)
        sd_s.append(s_s)
        sc_s.append(b_s)
        gv_s.append(v_s)
    y_prompt = rmsnorm(xp, final_norm_g)
    y_sample = rmsnorm(xs, final_norm_g)
    new_state_delta_prompt = jnp.stack(sd_p, axis=0)
    new_state_conv_prompt = jnp.stack(sc_p, axis=0)
    new_state_delta_sample = jnp.stack(sd_s, axis=0)
    new_state_conv_sample = jnp.stack(sc_s, axis=0)
    new_gm_v_sample = jnp.stack(gv_s, axis=0)
    return (y_prompt, y_sample, new_state_delta_prompt, new_state_conv_prompt,
            new_state_delta_sample, new_state_conv_sample, new_gm_v_sample)
```

```python
import functools
import math

import jax
import jax.numpy as jnp
from jax import lax
from jax.experimental import pallas as pl
from jax.experimental.pallas import tpu as pltpu

F32 = jnp.float32
BF16 = jnp.bfloat16

D_MODEL = 1024
DN_HEADS = 8
DN_DK = 128
DN_DV = 128
DN_CONV = 4
DN_CHUNK = 64
GM_WIDTH = 1024
GM_GROUPS = 8
GM_CHUNK = 128
N_EXPERTS = 32
TOP_K = 4
D_EXPERT = 1024
SWIGLU_LIMIT = 7.0
SWIGLU_ALPHA = 1.702
PLE_DIM = 256
EPS = 1e-6
QK_W = DN_HEADS * DN_DK
V_W = DN_HEADS * DN_DV
QKV_W = 2 * QK_W + V_W

LANES = 128
MOE_ROWS = 256
VMEM_LIMIT = 56 << 20


def _params(*sem):
    return pltpu.CompilerParams(dimension_semantics=sem, vmem_limit_bytes=VMEM_LIMIT)


def _dot(a, b):
    return jnp.dot(a.astype(BF16), b.astype(BF16), preferred_element_type=F32)


def _split(a):
    hi = a.astype(BF16)
    return hi, (a - hi.astype(F32)).astype(BF16)


def _dot3(a, b):
    ah, al = _split(a)
    bh, bl = _split(b)
    d = functools.partial(jnp.dot, preferred_element_type=F32)
    return d(ah, bh) + (d(ah, bl) + d(al, bh))


def _dot_nt(a, b):
    return lax.dot_general(a.astype(BF16), b.astype(BF16), (((1,), (1,)), ((), ())),
                           preferred_element_type=F32)


def _dot_tn(a, b):
    return lax.dot_general(a.astype(BF16), b.astype(BF16), (((0,), (0,)), ((), ())),
                           preferred_element_type=F32)


def _dot3_tn(a, b):
    ah, al = _split(a)
    bh, bl = _split(b)
    d = functools.partial(lax.dot_general, dimension_numbers=(((0,), (0,)), ((), ())),
                          preferred_element_type=F32)
    return d(ah, bh) + (d(ah, bl) + d(al, bh))


def _sigmoid(x):
    return 1.0 / (1.0 + jnp.exp(-x))


def _rms(x, g):
    return x * lax.rsqrt(jnp.mean(x * x, axis=-1, keepdims=True) + EPS) * g


def _gelu(x):
    return 0.5 * x * (1.0 + lax.erf(x * (1.0 / math.sqrt(2.0))))


def _row_tile(n):
    for t in (512, 256, 128):
        if n % t == 0:
            return t
    raise ValueError(f"token count {n} must be a multiple of 128")


def _in_proj_kernel(x_ref, g_ref, w_ref, wab_ref, z_ref, ab_ref, h_scr):
    @pl.when(pl.program_id(1) == 0)
    def _():
        hb = _rms(x_ref[...], g_ref[...]).astype(BF16)
        h_scr[...] = hb
        ab_ref[...] = jnp.dot(hb, wab_ref[...], preferred_element_type=F32)

    z_ref[...] = jnp.dot(h_scr[...], w_ref[...], preferred_element_type=F32)


def _in_proj(x, g, w_main, w_ab):
    n = x.shape[0]
    tm, tn = _row_tile(n), 1024
    cols = w_main.shape[1]
    return pl.pallas_call(
        _in_proj_kernel,
        grid=(n // tm, cols // tn),
        in_specs=[
            pl.BlockSpec((tm, D_MODEL), lambda i, j: (i, 0)),
            pl.BlockSpec((1, D_MODEL), lambda i, j: (0, 0)),
            pl.BlockSpec((D_MODEL, tn), lambda i, j: (0, j)),
            pl.BlockSpec((D_MODEL, LANES), lambda i, j: (0, 0)),
        ],
        out_specs=[
            pl.BlockSpec((tm, tn), lambda i, j: (i, j)),
            pl.BlockSpec((tm, LANES), lambda i, j: (i, 0)),
        ],
        out_shape=[jax.ShapeDtypeStruct((n, cols), F32), jax.ShapeDtypeStruct((n, LANES), F32)],
        scratch_shapes=[pltpu.VMEM((tm, D_MODEL), BF16)],
        compiler_params=_params("parallel", "arbitrary"),
        name="in_proj",
    )(x, g, w_main, w_ab)


def _delta_kernel(*refs, C, precise, aliased):
    (qkv_ref, zg_ref, ab_ref, buf_ref, s0_ref, cw_ref, alog_ref, dtb_ref, ong_ref) = refs[:9]
    refs = refs[9 + (1 if aliased else 0):]
    o_ref, snew_ref, bufnew_ref, xc_scr, s_scr = refs
    H, DK, DV = DN_HEADS, DN_DK, DN_DV
    c = pl.program_id(1)
    last = pl.num_programs(1) - 1
    dotm = _dot3 if precise else _dot
    dotm_tn = _dot3_tn if precise else _dot_tn
    halo = DN_CONV - 1
    base = 8

    @pl.when(c == 0)
    def _():
        xc_scr[base - halo:base, :] = buf_ref[0]
        s_scr[...] = s0_ref[0]

    xc_scr[base:base + C, :] = qkv_ref[...]
    cw = cw_ref[...]
    y = xc_scr[base - halo:base - halo + C, :] * cw[0:1]
    for i in range(1, DN_CONV):
        y = y + xc_scr[base - halo + i:base - halo + i + C, :] * cw[i:i + 1]
    qkv = y * _sigmoid(y)
    tail = xc_scr[base + C - halo:base + C, :]
    xc_scr[base - halo:base, :] = tail

    @pl.when(c == last)
    def _():
        bufnew_ref[0] = tail

    ab = ab_ref[...]
    g_all = -jnp.exp(alog_ref[...]) * jax.nn.softplus(ab + dtb_ref[...])
    beta_all = _sigmoid(ab)
    row = lax.broadcasted_iota(jnp.int32, (C, C), 0)
    col = lax.broadcasted_iota(jnp.int32, (C, C), 1)
    incl, strict, eye = row >= col, row > col, row == col
    eyef = eye.astype(F32)
    gcum = _dot3(incl.astype(F32), g_all)
    for h in range(H):
        q = qkv[:, h * DK:(h + 1) * DK]
        k = qkv[:, QK_W + h * DK:QK_W + (h + 1) * DK]
        v = qkv[:, 2 * QK_W + h * DV:2 * QK_W + (h + 1) * DV]
        q = q * lax.rsqrt(jnp.sum(q * q, axis=-1, keepdims=True) + EPS) * (DK ** -0.5)
        k = k * lax.rsqrt(jnp.sum(k * k, axis=-1, keepdims=True) + EPS)
        gc = gcum[:, h:h + 1]
        bc = beta_all[:, H + h:H + h + 1]
        gr = jnp.sum(jnp.where(eye, gc, 0.0), axis=0, keepdims=True)
        br = jnp.sum(jnp.where(eye, bc, 0.0), axis=0, keepdims=True)
        db = jnp.where(incl, jnp.exp(gc - gr), 0.0) * br
        a_low = jnp.where(strict, _dot_nt(k, k) * db, 0.0)
        m_intra = _dot_nt(q, k) * db
        t_inv = eyef - a_low
        a_pow = _dot3(a_low, a_low)
        n = 2
        while n < C:
            t_inv = t_inv + _dot3(t_inv, a_pow)
            n *= 2
            if n < C:
                a_pow = _dot3(a_pow, a_pow)
        gam = jnp.exp(gc)
        gl = gcum[C - 1:C, h:h + 1]
        u_base = dotm(t_inv, v)
        w_state = dotm(t_inv, gam * k)
        q_dec = gam * q
        k_dec = k * (bc * jnp.exp(gl - gc))
        s = s_scr[h]
        u = u_base - dotm(w_state, s)
        o = dotm(q_dec, s) + dotm(m_intra, u)
        s_scr[h] = jnp.exp(gl) * s + dotm_tn(k_dec, u)
        zg = zg_ref[:, h * DV:(h + 1) * DV]
        o_ref[:, h * DV:(h + 1) * DV] = _rms(o, ong_ref[...]) * (zg * _sigmoid(zg))

    @pl.when(c == last)
    def _():
        snew_ref[0] = s_scr[...]


def _delta_branch(z, ab, conv_buf, s0, conv_w, alog_p, dtb_p, onorm_g, o_prev, *, tok0, L, precise):
    n = z.shape[0]
    B = conv_buf.shape[0]
    C = math.gcd(L, DN_CHUNK)
    nc = L // C
    blk0 = tok0 // C
    aliased = o_prev is not None
    tok = lambda b, c: (blk0 + b * nc + c, 0)
    in_specs = [
        pl.BlockSpec((C, QKV_W), tok),
        pl.BlockSpec((C, V_W), lambda b, c: (blk0 + b * nc + c, QKV_W // V_W)),
        pl.BlockSpec((C, LANES), tok),
        pl.BlockSpec((1, DN_CONV - 1, QKV_W), lambda b, c: (b, 0, 0)),
        pl.BlockSpec((1, DN_HEADS, DN_DK, DN_DV), lambda b, c: (b, 0, 0, 0)),
        pl.BlockSpec((DN_CONV, QKV_W), lambda b, c: (0, 0)),
        pl.BlockSpec((1, LANES), lambda b, c: (0, 0)),
        pl.BlockSpec((1, LANES), lambda b, c: (0, 0)),
        pl.BlockSpec((1, DN_DV), lambda b, c: (0, 0)),
    ]
    args = [z, z, ab, conv_buf, s0, conv_w, alog_p, dtb_p, onorm_g]
    if aliased:
        in_specs.append(pl.BlockSpec(memory_space=pl.ANY))
        args.append(o_prev)
    return pl.pallas_call(
        functools.partial(_delta_kernel, C=C, precise=precise, aliased=aliased),
        grid=(B, nc),
        in_specs=in_specs,
        out_specs=[
            pl.BlockSpec((C, V_W), tok),
            pl.BlockSpec((1, DN_HEADS, DN_DK, DN_DV), lambda b, c: (b, 0, 0, 0)),
            pl.BlockSpec((1, DN_CONV - 1, QKV_W), lambda b, c: (b, 0, 0)),
        ],
        out_shape=[
            jax.ShapeDtypeStruct((n, V_W), F32),
            jax.ShapeDtypeStruct(s0.shape, F32),
            jax.ShapeDtypeStruct(conv_buf.shape, F32),
        ],
        scratch_shapes=[
            pltpu.VMEM((8 + C, QKV_W), F32),
            pltpu.VMEM((DN_HEADS, DN_DK, DN_DV), F32),
        ],
        input_output_aliases={len(args) - 1: 0} if aliased else {},
        compiler_params=_params("parallel", "arbitrary"),
        name=f"delta_rule_c{C}",
    )(*args)


def _chunk_mlp_kernel(gu_ref, gv_ref, lng_ref, lnb_ref, mix_ref, bias_ref, us_ref, v_ref):
    u = _gelu(gu_ref[...])
    a = _gelu(gv_ref[...])
    ac = a - jnp.mean(a, axis=-1, keepdims=True)
    v = ac * lax.rsqrt(jnp.mean(ac * ac, axis=-1, keepdims=True) + EPS) * lng_ref[...] + lnb_ref[...]
    v_ref[...] = v
    vb = v.astype(BF16)
    w = GM_WIDTH // GM_GROUPS
    for g in range(GM_GROUPS):
        sl = slice(g * w, (g + 1) * w)
        s = jnp.dot(mix_ref[0, g], vb[:, sl], preferred_element_type=F32) + bias_ref[0, :, sl]
        us_ref[:, sl] = u[:, sl] * s


def _chunk_mlp(z, ln_g, ln_b, mix, bias, n_prompt):
    n = z.shape[0]
    t = GM_CHUNK
    pb = n_prompt // t
    grp = lambda i: jnp.where(i < pb, 0, 1)
    return pl.pallas_call(
        _chunk_mlp_kernel,
        grid=(n // t,),
        in_specs=[
            pl.BlockSpec((t, GM_WIDTH), lambda i: (i, 4)),
            pl.BlockSpec((t, GM_WIDTH), lambda i: (i, 5)),
            pl.BlockSpec((1, GM_WIDTH), lambda i: (0, 0)),
            pl.BlockSpec((1, GM_WIDTH), lambda i: (0, 0)),
            pl.BlockSpec((1, GM_GROUPS, t, t), lambda i: (grp(i), 0, 0, 0)),
            pl.BlockSpec((1, t, GM_WIDTH), lambda i: (grp(i), 0, 0)),
        ],
        out_specs=[
            pl.BlockSpec((t, GM_WIDTH), lambda i: (i, 0)),
            pl.BlockSpec((t, GM_WIDTH), lambda i: (jnp.maximum(i - pb, 0), 0)),
        ],
        out_shape=[
            jax.ShapeDtypeStruct((n, GM_WIDTH), F32),
            jax.ShapeDtypeStruct((n - n_prompt, GM_WIDTH), F32),
        ],
        compiler_params=_params("arbitrary"),
        name="chunk_mlp",
    )(z, z, ln_g, ln_b, mix, bias)


def _post_mix_kernel(o_ref, us_ref, ma_ref, mb_ref, x_ref, wa_ref, wb_ref, wo_ref, g2_ref,
                     rwh_ref, rwl_ref, rb_ref, x1_ref, h2_ref, gate_ref, idx_ref):
    ya = jnp.dot(o_ref[...].astype(BF16), wa_ref[...], preferred_element_type=F32)
    yb = jnp.dot(us_ref[...].astype(BF16), wb_ref[...], preferred_element_type=F32)
    mixed = _sigmoid(ma_ref[...]) * ya + _sigmoid(mb_ref[...]) * yb
    x1 = x_ref[...] + jnp.dot(mixed.astype(BF16), wo_ref[...], preferred_element_type=F32)
    x1_ref[...] = x1
    h2 = _rms(x1, g2_ref[...])
    h2_ref[...] = h2.astype(BF16)
    hh, hl = _split(h2)
    d = functools.partial(jnp.dot, preferred_element_type=F32)
    logits = d(hh, rwh_ref[...]) + (d(hh, rwl_ref[...]) + d(hl, rwh_ref[...])) + rb_ref[...]
    lane = lax.broadcasted_iota(jnp.int32, logits.shape, 1)
    vals, idxs = [], []
    for _ in range(TOP_K):
        m = jnp.max(logits, axis=-1, keepdims=True)
        i = jnp.min(jnp.where(logits == m, lane, LANES), axis=-1, keepdims=True)
        vals.append(m)
        idxs.append(i)
        logits = jnp.where(lane == i, -jnp.inf, logits)
    es = [jnp.exp(v - vals[0]) for v in vals]
    tot = es[0]
    for e in es[1:]:
        tot = tot + e
    gates = jnp.zeros(logits.shape, F32)
    idx = jnp.zeros(logits.shape, jnp.int32)
    for k in range(TOP_K):
        gates = jnp.where(lane == k, es[k] / tot, gates)
        idx = jnp.where(lane == k, idxs[k], idx)
    gate_ref[...] = gates
    idx_ref[...] = idx


def _post_mix(o, us, z, x, wa, wb, wo, g2, rwh, rwl, rb):
    n = x.shape[0]
    tm = _row_tile(n)
    tok = pl.BlockSpec((tm, D_MODEL), lambda i: (i, 0))
    full = lambda r, c: pl.BlockSpec((r, c), lambda i: (0, 0))
    return pl.pallas_call(
        _post_mix_kernel,
        grid=(n // tm,),
        in_specs=[
            tok, tok,
            pl.BlockSpec((tm, D_MODEL), lambda i: (i, 6)),
            pl.BlockSpec((tm, D_MODEL), lambda i: (i, 7)),
            tok,
            full(V_W, D_MODEL), full(GM_WIDTH, D_MODEL), full(D_MODEL, D_MODEL), full(1, D_MODEL),
            full(D_MODEL, LANES), full(D_MODEL, LANES), full(1, LANES),
        ],
        out_specs=[tok, tok, pl.BlockSpec((tm, LANES), lambda i: (i, 0)),
                   pl.BlockSpec((tm, LANES), lambda i: (i, 0))],
        out_shape=[
            jax.ShapeDtypeStruct((n, D_MODEL), F32),
            jax.ShapeDtypeStruct((n, D_MODEL), BF16),
            jax.ShapeDtypeStruct((n, LANES), F32),
            jax.ShapeDtypeStruct((n, LANES), jnp.int32),
        ],
        compiler_params=_params("parallel"),
        name="post_mix_router",
    )(o, us, z, z, x, wa, wb, wo, g2, rwh, rwl, rb)


def _moe_kernel(be_ref, first_ref, valid_ref, x_ref, w1_ref, b1_ref, w2_ref, b2_ref, y_ref,
                w1_scr, w2_scr):
    b = pl.program_id(0)

    @pl.when(first_ref[b] == 1)
    def _():
        w1_scr[...] = w1_ref[0].astype(BF16)
        w2_scr[...] = w2_ref[0].astype(BF16)

    @pl.when(valid_ref[b] == 1)
    def _():
        hid = jnp.dot(x_ref[...], w1_scr[...], preferred_element_type=F32) + b1_ref[0]
        gate = jnp.minimum(hid[:, :D_EXPERT], SWIGLU_LIMIT)
        up = jnp.clip(hid[:, D_EXPERT:], -SWIGLU_LIMIT, SWIGLU_LIMIT)
        act = gate * _sigmoid(SWIGLU_ALPHA * gate) * (up + 1.0)
        y_ref[...] = jnp.dot(act.astype(BF16), w2_scr[...], preferred_element_type=F32) + b2_ref[0]

    @pl.when(valid_ref[b] == 0)
    def _():
        y_ref[...] = jnp.zeros(y_ref.shape, F32)


def _moe_experts(block_e, first, valid, xb, w1, b1, w2, b2):
    rows = xb.shape[0]
    nb = rows // MOE_ROWS
    return pl.pallas_call(
        _moe_kernel,
        grid_spec=pltpu.PrefetchScalarGridSpec(
            num_scalar_prefetch=3,
            grid=(nb,),
            in_specs=[
                pl.BlockSpec((MOE_ROWS, D_MODEL), lambda b, be, fi, va: (b, 0)),
                pl.BlockSpec((1, D_MODEL, 2 * D_EXPERT), lambda b, be, fi, va: (be[b], 0, 0)),
                pl.BlockSpec((1, 1, 2 * D_EXPERT), lambda b, be, fi, va: (be[b], 0, 0)),
                pl.BlockSpec((1, D_EXPERT, D_MODEL), lambda b, be, fi, va: (be[b], 0, 0)),
                pl.BlockSpec((1, 1, D_MODEL), lambda b, be, fi, va: (be[b], 0, 0)),
            ],
            out_specs=pl.BlockSpec((MOE_ROWS, D_MODEL), lambda b, be, fi, va: (b, 0)),
            scratch_shapes=[
                pltpu.VMEM((D_MODEL, 2 * D_EXPERT), BF16),
                pltpu.VMEM((D_EXPERT, D_MODEL), BF16),
            ],
        ),
        out_shape=jax.ShapeDtypeStruct((rows, D_MODEL), F32),
        compiler_params=_params("arbitrary"),
        name="moe_experts",
    )(block_e, first, valid, xb, w1, b1, w2, b2)


def _moe_dispatch(idx, n):
    onehot = jnp.sum((idx[:, :, None] == jnp.arange(N_EXPERTS, dtype=jnp.int32)).astype(jnp.int32), axis=1)
    cum = jnp.cumsum(onehot, axis=0)
    counts = cum[-1]
    before = cum - onehot
    padded = (counts + MOE_ROWS - 1) // MOE_ROWS * MOE_ROWS
    pad_end = jnp.cumsum(padded)
    pad_start = pad_end - padded
    rank = jnp.take_along_axis(before, idx, axis=1)
    dest = pad_start[idx] + rank
    nb = -(-n * TOP_K // MOE_ROWS) + N_EXPERTS
    rows = nb * MOE_ROWS
    flat_tok = jnp.arange(n * TOP_K, dtype=jnp.int32) // TOP_K
    row_tok = jnp.full((rows,), n, jnp.int32).at[dest.reshape(-1)].set(flat_tok)
    starts = jnp.arange(nb, dtype=jnp.int32) * MOE_ROWS
    valid = (starts < pad_end[-1]).astype(jnp.int32)
    last_e = jnp.minimum(jnp.searchsorted(pad_end, pad_end[-1] - 1, side='right'), N_EXPERTS - 1)
    block_e = jnp.minimum(jnp.searchsorted(pad_end, starts, side='right'), N_EXPERTS - 1)
    block_e = jnp.where(valid == 1, block_e, last_e).astype(jnp.int32)
    first = jnp.concatenate([jnp.ones((1,), jnp.int32),
                             (block_e[1:] != block_e[:-1]).astype(jnp.int32)])
    return dest, row_tok, block_e, first, valid


def _tail_kernel(x1_ref, moe_ref, p_ref, g3_ref, wg_ref, wp_ref, gf_ref, y_ref):
    x2 = x1_ref[...] + moe_ref[...]
    h3 = _rms(x2, g3_ref[...])
    gate = _sigmoid(jnp.dot(h3.astype(BF16), wg_ref[...], preferred_element_type=F32))
    pe = jnp.dot(p_ref[...].astype(BF16), wp_ref[...], preferred_element_type=F32)
    y_ref[...] = _rms(x2 + gate * pe, gf_ref[...])


def _tail(x1, moe, p, g3, wg, wp, gf):
    n = x1.shape[0]
    tm = _row_tile(n)
    tok = pl.BlockSpec((tm, D_MODEL), lambda i: (i, 0))
    full = lambda r, c: pl.BlockSpec((r, c), lambda i: (0, 0))
    return pl.pallas_call(
        _tail_kernel,
        grid=(n // tm,),
        in_specs=[tok, tok, pl.BlockSpec((tm, PLE_DIM), lambda i: (i, 0)),
                  full(1, D_MODEL), full(D_MODEL, D_MODEL), full(PLE_DIM, D_MODEL), full(1, D_MODEL)],
        out_specs=tok,
        out_shape=jax.ShapeDtypeStruct((n, D_MODEL), F32),
        compiler_params=_params("parallel"),
        name="ple_final_norm",
    )(x1, moe, p, g3, wg, wp, gf)


def _lane_pad(v, offset, fill=0.0):
    out = jnp.full((1, LANES), fill, F32)
    return out.at[0, offset:offset + v.shape[0]].set(v.astype(F32))


def kernel(x_prompt, x_sample, state_delta, state_conv, p_prompt, p_sample, norm1_g, w_in, conv_w, a_log, dt_bias, dn_norm_g, w_proj_a, gm_ln_g, gm_ln_b, gm_ws, gm_bs, w_proj_b, w_out, norm2_g, router_w, router_b, moe_w1, moe_b1, moe_w2, moe_b2, norm3_g, ple_w, ple_gate_w, final_norm_g):
    bp, lp, d = x_prompt.shape
    bs, ls, _ = x_sample.shape
    depth = w_in.shape[0]
    assert depth == 1 and d == D_MODEL
    assert lp % GM_CHUNK == 0 and GM_CHUNK % ls == 0 and ls >= DN_CONV - 1
    n_p, n_s = bp * lp, bs * ls
    n = n_p + n_s
    i = 0

    x = jnp.concatenate([x_prompt.reshape(n_p, d), x_sample.reshape(n_s, d)], axis=0)
    p = jnp.concatenate([p_prompt[i].reshape(n_p, PLE_DIM), p_sample[i].reshape(n_s, PLE_DIM)], axis=0)

    ab0 = QKV_W
    w = w_in[i]
    w_main = jnp.concatenate([w[:, :ab0], w[:, ab0 + 2 * DN_HEADS:]], axis=1).astype(BF16)
    w_ab = jnp.pad(w[:, ab0:ab0 + 2 * DN_HEADS], ((0, 0), (0, LANES - 2 * DN_HEADS))).astype(BF16)
    row2 = lambda v: v.reshape(1, -1).astype(F32)

    z, ab = _in_proj(x, row2(norm1_g[i]), w_main, w_ab)

    alog_p = _lane_pad(a_log[i], 0)
    dtb_p = _lane_pad(dt_bias[i], 0)
    cw = conv_w[i].astype(F32)
    ong = row2(dn_norm_g[i])
    zero_s = jnp.zeros((bp, DN_HEADS, DN_DK, DN_DV), F32)
    zero_buf = jnp.zeros((bp, DN_CONV - 1, QKV_W), F32)
    o, sd_p, sc_p = _delta_branch(z, ab, zero_buf, zero_s, cw, alog_p, dtb_p, ong, None,
                                  tok0=0, L=lp, precise=False)
    o, sd_s, sc_s = _delta_branch(z, ab, state_conv[i], state_delta[i], cw, alog_p, dtb_p, ong, o,
                                  tok0=n_p, L=ls, precise=True)

    t = GM_CHUNK
    tri = jnp.tril(jnp.ones((t, t), bool))
    ws = gm_ws[i]
    mix_p = jnp.where(tri, ws, 0.0)
    small = jnp.where(tri[:ls, :ls], ws[:, :ls, :ls], 0.0)
    mix_s = jnp.einsum('ab,gts->gatbs', jnp.eye(t // ls, dtype=F32), small).reshape(GM_GROUPS, t, t)
    mix = jnp.stack([mix_p, mix_s]).astype(BF16)
    gw = GM_WIDTH // GM_GROUPS
    bias_p = jnp.repeat(gm_bs[i].T, gw, axis=1)
    bias_s = jnp.tile(bias_p[:ls], (t // ls, 1))
    bias = jnp.stack([bias_p, bias_s]).astype(F32)
    us, v_s = _chunk_mlp(z, row2(gm_ln_g[i]), row2(gm_ln_b[i]), mix, bias, n_p)

    rw = jnp.pad(router_w[i].astype(F32), ((0, 0), (0, LANES - N_EXPERTS)))
    rwh = rw.astype(BF16)
    rwl = (rw - rwh.astype(F32)).astype(BF16)
    rb = _lane_pad(router_b[i], 0, fill=-jnp.inf)
    x1, h2, gates, idx = _post_mix(o, us, z, x, w_proj_a[i].astype(BF16), w_proj_b[i].astype(BF16),
                                   w_out[i].astype(BF16), row2(norm2_g[i]), rwh, rwl, rb)
    gates, idx = gates[:, :TOP_K], idx[:, :TOP_K]

    dest, row_tok, block_e, first, valid = _moe_dispatch(idx, n)
    h2_pad = jnp.concatenate([h2, jnp.zeros((1, d), BF16)], axis=0)
    xb = jnp.take(h2_pad, row_tok, axis=0)
    yb = _moe_experts(block_e, first, valid, xb, moe_w1[i], moe_b1[i][:, None, :],
                      moe_w2[i], moe_b2[i][:, None, :])
    moe = jnp.sum(jnp.take(yb, dest, axis=0) * gates[:, :, None], axis=1)

    y = _tail(x1, moe, p, row2(norm3_g[i]), ple_gate_w[i].astype(BF16), ple_w[i].astype(BF16),
              row2(final_norm_g))

    return (y[:n_p].reshape(bp, lp, d), y[n_p:].reshape(bs, ls, d),
            sd_p[None], sc_p[None], sd_s[None], sc_s[None], v_s.reshape(1, bs, ls, GM_WIDTH))
```

```python
import functools
import math

import jax
import jax.numpy as jnp
from jax import lax
from jax.experimental import pallas as pl
from jax.experimental.pallas import tpu as pltpu

F32 = jnp.float32
BF16 = jnp.bfloat16

D_MODEL = 1024
DN_HEADS = 8
DN_DK = 128
DN_DV = 128
DN_CONV = 4
DN_CHUNK = 64
GM_WIDTH = 1024
GM_GROUPS = 8
GM_CHUNK = 128
N_EXPERTS = 32
TOP_K = 4
D_EXPERT = 1024
SWIGLU_LIMIT = 7.0
SWIGLU_ALPHA = 1.702
PLE_DIM = 256
EPS = 1e-6
QK_W = DN_HEADS * DN_DK
V_W = DN_HEADS * DN_DV
QKV_W = 2 * QK_W + V_W

LANES = 128
MOE_ROWS = 256
VMEM_LIMIT = 56 << 20


def _params(*sem):
    return pltpu.CompilerParams(dimension_semantics=sem, vmem_limit_bytes=VMEM_LIMIT)


def _dot(a, b):
    return jnp.dot(a.astype(BF16), b.astype(BF16), preferred_element_type=F32)


def _split(a):
    hi = a.astype(BF16)
    return hi, (a - hi.astype(F32)).astype(BF16)


def _dot3(a, b):
    ah, al = _split(a)
    bh, bl = _split(b)
    d = functools.partial(jnp.dot, preferred_element_type=F32)
    return d(ah, bh) + (d(ah, bl) + d(al, bh))


def _dot_nt(a, b):
    return lax.dot_general(a.astype(BF16), b.astype(BF16), (((1,), (1,)), ((), ())),
                           preferred_element_type=F32)


def _dot_tn(a, b):
    return lax.dot_general(a.astype(BF16), b.astype(BF16), (((0,), (0,)), ((), ())),
                           preferred_element_type=F32)


def _dot3_tn(a, b):
    ah, al = _split(a)
    bh, bl = _split(b)
    d = functools.partial(lax.dot_general, dimension_numbers=(((0,), (0,)), ((), ())),
                          preferred_element_type=F32)
    return d(ah, bh) + (d(ah, bl) + d(al, bh))


def _sigmoid(x):
    return 1.0 / (1.0 + jnp.exp(-x))


def _rms(x, g):
    return x * lax.rsqrt(jnp.mean(x * x, axis=-1, keepdims=True) + EPS) * g


def _gelu(x):
    return 0.5 * x * (1.0 + lax.erf(x * (1.0 / math.sqrt(2.0))))


def _row_tile(n):
    for t in (512, 256, 128):
        if n % t == 0:
            return t
    raise ValueError(f"token count {n} must be a multiple of 128")


def _in_proj_kernel(x_ref, g_ref, w_ref, wab_ref, z_ref, ab_ref, h_scr):
    @pl.when(pl.program_id(1) == 0)
    def _():
        hb = _rms(x_ref[...], g_ref[...]).astype(BF16)
        h_scr[...] = hb
        ab_ref[...] = jnp.dot(hb, wab_ref[...], preferred_element_type=F32)

    z_ref[...] = jnp.dot(h_scr[...], w_ref[...], preferred_element_type=F32)


def _in_proj(x, g, w_main, w_ab):
    n = x.shape[0]
    tm, tn = _row_tile(n), 1024
    cols = w_main.shape[1]
    return pl.pallas_call(
        _in_proj_kernel,
        grid=(n // tm, cols // tn),
        in_specs=[
            pl.BlockSpec((tm, D_MODEL), lambda i, j: (i, 0)),
            pl.BlockSpec((1, D_MODEL), lambda i, j: (0, 0)),
            pl.BlockSpec((D_MODEL, tn), lambda i, j: (0, j)),
            pl.BlockSpec((D_MODEL, LANES), lambda i, j: (0, 0)),
        ],
        out_specs=[
            pl.BlockSpec((tm, tn), lambda i, j: (i, j)),
            pl.BlockSpec((tm, LANES), lambda i, j: (i, 0)),
        ],
        out_shape=[jax.ShapeDtypeStruct((n, cols), F32), jax.ShapeDtypeStruct((n, LANES), F32)],
        scratch_shapes=[pltpu.VMEM((tm, D_MODEL), BF16)],
        compiler_params=_params("parallel", "arbitrary"),
        name="in_proj",
    )(x, g, w_main, w_ab)


def _delta_kernel(*refs, C, G, carry, precise, aliased):
    (qkv_ref, zg_ref, ab_ref, buf_ref, s0_ref, cw_ref, alog_ref, dtb_ref, ong_ref) = refs[:9]
    refs = refs[9 + (1 if aliased else 0):]
    o_ref, snew_ref, bufnew_ref, xc_scr = refs[:4]
    H, DK, DV = DN_HEADS, DN_DK, DN_DV
    T = G * C
    dotm = _dot3 if precise else _dot
    dotm_tn = _dot3_tn if precise else _dot_tn
    halo = DN_CONV - 1
    base = 8
    cw = cw_ref[...]

    def conv(window):
        y = window(0) * cw[0:1]
        for i in range(1, DN_CONV):
            y = y + window(i) * cw[i:i + 1]
        return y

    if carry:
        s_scr = refs[4]
        c = pl.program_id(1)
        last = pl.num_programs(1) - 1

        @pl.when(c == 0)
        def _():
            xc_scr[base - halo:base, :] = buf_ref[0]
            s_scr[...] = s0_ref[0]

        xc_scr[base:base + T, :] = qkv_ref[...]
        y = conv(lambda i: xc_scr[base - halo + i:base - halo + i + T, :])
        tail = xc_scr[base + T - halo:base + T, :]
        xc_scr[base - halo:base, :] = tail

        @pl.when(c == last)
        def _():
            bufnew_ref[0] = tail
    else:
        ys = []
        for g in range(G):
            xc_scr[g, base - halo:base, :] = buf_ref[g]
            xc_scr[g, base:base + C, :] = qkv_ref[g * C:(g + 1) * C, :]
            ys.append(conv(lambda i: xc_scr[g, base - halo + i:base - halo + i + C, :]))
            bufnew_ref[g] = xc_scr[g, base + C - halo:base + C, :]
        y = jnp.concatenate(ys, axis=0) if G > 1 else ys[0]
    qkv = y * _sigmoid(y)

    ab = ab_ref[...]
    g_all = -jnp.exp(alog_ref[...]) * jax.nn.softplus(ab + dtb_ref[...])
    beta_all = _sigmoid(ab)
    shift = C.bit_length() - 1
    rt = lax.broadcasted_iota(jnp.int32, (T, T), 0)
    ct = lax.broadcasted_iota(jnp.int32, (T, T), 1)
    chunk_tril = ((rt >> shift) == (ct >> shift)) & (rt >= ct)
    gcum = _dot3(chunk_tril.astype(F32), g_all)
    gam_all = jnp.exp(gcum)

    row = lax.broadcasted_iota(jnp.int32, (C, C), 0)
    col = lax.broadcasted_iota(jnp.int32, (C, C), 1)
    incl, strict, eye = row >= col, row > col, row == col
    eyef = eye.astype(F32)
    units = [(g, h) for g in range(G) for h in range(H)]
    rows = lambda a, g: a[g * C:(g + 1) * C]

    qn, kn, vv = [], [], []
    for h in range(H):
        q = qkv[:, h * DK:(h + 1) * DK]
        k = qkv[:, QK_W + h * DK:QK_W + (h + 1) * DK]
        qn.append(q * lax.rsqrt(jnp.sum(q * q, axis=-1, keepdims=True) + EPS) * (DK ** -0.5))
        kn.append(k * lax.rsqrt(jnp.sum(k * k, axis=-1, keepdims=True) + EPS))
        vv.append(qkv[:, 2 * QK_W + h * DV:2 * QK_W + (h + 1) * DV])

    gc, bc, gl, a_low, m_intra = {}, {}, {}, {}, {}
    for u in units:
        g, h = u
        gc[u] = rows(gcum, g)[:, h:h + 1]
        bc[u] = rows(beta_all, g)[:, H + h:H + h + 1]
        gl[u] = gc[u][C - 1:C, :]
        gr = jnp.sum(jnp.where(eye, gc[u], 0.0), axis=0, keepdims=True)
        br = jnp.sum(jnp.where(eye, bc[u], 0.0), axis=0, keepdims=True)
        db = jnp.where(incl, jnp.exp(gc[u] - gr), 0.0) * br
        k = rows(kn[h], g)
        a_low[u] = jnp.where(strict, _dot_nt(k, k) * db, 0.0)
        m_intra[u] = _dot_nt(rows(qn[h], g), k) * db

    t_inv = {u: eyef - a_low[u] for u in units}
    a_pow = {u: _dot3(a_low[u], a_low[u]) for u in units}
    n = 2
    while n < C:
        t_inv = {u: t_inv[u] + _dot3(t_inv[u], a_pow[u]) for u in units}
        n *= 2
        if n < C:
            a_pow = {u: _dot3(a_pow[u], a_pow[u]) for u in units}

    u_base, w_state, q_dec, k_dec = {}, {}, {}, {}
    for u in units:
        g, h = u
        gam = rows(gam_all, g)[:, h:h + 1]
        k = rows(kn[h], g)
        u_base[u] = dotm(t_inv[u], rows(vv[h], g))
        w_state[u] = dotm(t_inv[u], gam * k)
        q_dec[u] = gam * rows(qn[h], g)
        k_dec[u] = k * (bc[u] * jnp.exp(gl[u] - gc[u]))

    state = [s_scr[h] for h in range(H)] if carry else None
    outs = {}
    for g in range(G):
        s_in = state if carry else [s0_ref[g, h] for h in range(H)]
        us = [u_base[(g, h)] - dotm(w_state[(g, h)], s_in[h]) for h in range(H)]
        for h in range(H):
            outs[(g, h)] = dotm(q_dec[(g, h)], s_in[h]) + dotm(m_intra[(g, h)], us[h])
        s_out = [jnp.exp(gl[(g, h)]) * s_in[h] + dotm_tn(k_dec[(g, h)], us[h]) for h in range(H)]
        if carry:
            state = s_out
        else:
            for h in range(H):
                snew_ref[g, h] = s_out[h]

    for h in range(H):
        o = jnp.concatenate([outs[(g, h)] for g in range(G)], axis=0) if G > 1 else outs[(0, h)]
        zg = zg_ref[:, h * DV:(h + 1) * DV]
        o_ref[:, h * DV:(h + 1) * DV] = _rms(o, ong_ref[...]) * (zg * _sigmoid(zg))

    if carry:
        for h in range(H):
            s_scr[h] = state[h]

        @pl.when(c == last)
        def _():
            for h in range(H):
                snew_ref[0, h] = state[h]


def _delta_branch(z, ab, conv_buf, s0, conv_w, alog_p, dtb_p, onorm_g, o_prev, *, tok0, L, G, precise):
    n = z.shape[0]
    B = conv_buf.shape[0]
    C = math.gcd(L, DN_CHUNK)
    nc = L // C
    carry = nc > 1
    T = G * C
    blk0 = tok0 // T
    if carry:
        assert nc % G == 0
        grid = (B, nc // G)
        tok_blk = lambda b, c: blk0 + b * (nc // G) + c
        seq_blk = lambda b, c: b
        gs = 1
    else:
        assert B % G == 0
        grid = (B // G, 1)
        tok_blk = lambda b, c: blk0 + b
        seq_blk = lambda b, c: b
        gs = G
    aliased = o_prev is not None
    tok = lambda b, c: (tok_blk(b, c), 0)
    in_specs = [
        pl.BlockSpec((T, QKV_W), tok),
        pl.BlockSpec((T, V_W), lambda b, c: (tok_blk(b, c), QKV_W // V_W)),
        pl.BlockSpec((T, LANES), tok),
        pl.BlockSpec((gs, DN_CONV - 1, QKV_W), lambda b, c: (seq_blk(b, c), 0, 0)),
        pl.BlockSpec((gs, DN_HEADS, DN_DK, DN_DV), lambda b, c: (seq_blk(b, c), 0, 0, 0)),
        pl.BlockSpec((DN_CONV, QKV_W), lambda b, c: (0, 0)),
        pl.BlockSpec((1, LANES), lambda b, c: (0, 0)),
        pl.BlockSpec((1, LANES), lambda b, c: (0, 0)),
        pl.BlockSpec((1, DN_DV), lambda b, c: (0, 0)),
    ]
    args = [z, z, ab, conv_buf, s0, conv_w, alog_p, dtb_p, onorm_g]
    if aliased:
        in_specs.append(pl.BlockSpec(memory_space=pl.ANY))
        args.append(o_prev)
    if carry:
        scratch = [pltpu.VMEM((8 + T, QKV_W), F32), pltpu.VMEM((DN_HEADS, DN_DK, DN_DV), F32)]
    else:
        scratch = [pltpu.VMEM((G, 8 + C, QKV_W), F32)]
    return pl.pallas_call(
        functools.partial(_delta_kernel, C=C, G=G, carry=carry, precise=precise, aliased=aliased),
        grid=grid,
        in_specs=in_specs,
        out_specs=[
            pl.BlockSpec((T, V_W), tok),
            pl.BlockSpec((gs, DN_HEADS, DN_DK, DN_DV), lambda b, c: (seq_blk(b, c), 0, 0, 0)),
            pl.BlockSpec((gs, DN_CONV - 1, QKV_W), lambda b, c: (seq_blk(b, c), 0, 0)),
        ],
        out_shape=[
            jax.ShapeDtypeStruct((n, V_W), F32),
            jax.ShapeDtypeStruct(s0.shape, F32),
            jax.ShapeDtypeStruct(conv_buf.shape, F32),
        ],
        scratch_shapes=scratch,
        input_output_aliases={len(args) - 1: 0} if aliased else {},
        compiler_params=_params("parallel", "arbitrary"),
        name=f"delta_rule_c{C}",
    )(*args)


def _chunk_mlp_kernel(gu_ref, gv_ref, lng_ref, lnb_ref, mix_ref, bias_ref, us_ref, v_ref):
    u = _gelu(gu_ref[...])
    a = _gelu(gv_ref[...])
    ac = a - jnp.mean(a, axis=-1, keepdims=True)
    v = ac * lax.rsqrt(jnp.mean(ac * ac, axis=-1, keepdims=True) + EPS) * lng_ref[...] + lnb_ref[...]
    v_ref[...] = v
    vb = v.astype(BF16)
    w = GM_WIDTH // GM_GROUPS
    for g in range(GM_GROUPS):
        sl = slice(g * w, (g + 1) * w)
        s = jnp.dot(mix_ref[0, g], vb[:, sl], preferred_element_type=F32) + bias_ref[0, :, sl]
        us_ref[:, sl] = u[:, sl] * s


def _chunk_mlp(z, ln_g, ln_b, mix, bias, n_prompt):
    n = z.shape[0]
    t = GM_CHUNK
    pb = n_prompt // t
    grp = lambda i: jnp.where(i < pb, 0, 1)
    return pl.pallas_call(
        _chunk_mlp_kernel,
        grid=(n // t,),
        in_specs=[
            pl.BlockSpec((t, GM_WIDTH), lambda i: (i, 4)),
            pl.BlockSpec((t, GM_WIDTH), lambda i: (i, 5)),
            pl.BlockSpec((1, GM_WIDTH), lambda i: (0, 0)),
            pl.BlockSpec((1, GM_WIDTH), lambda i: (0, 0)),
            pl.BlockSpec((1, GM_GROUPS, t, t), lambda i: (grp(i), 0, 0, 0)),
            pl.BlockSpec((1, t, GM_WIDTH), lambda i: (grp(i), 0, 0)),
        ],
        out_specs=[
            pl.BlockSpec((t, GM_WIDTH), lambda i: (i, 0)),
            pl.BlockSpec((t, GM_WIDTH), lambda i: (jnp.maximum(i - pb, 0), 0)),
        ],
        out_shape=[
            jax.ShapeDtypeStruct((n, GM_WIDTH), F32),
            jax.ShapeDtypeStruct((n - n_prompt, GM_WIDTH), F32),
        ],
        compiler_params=_params("arbitrary"),
        name="chunk_mlp",
    )(z, z, ln_g, ln_b, mix, bias)


def _post_mix_kernel(o_ref, us_ref, ma_ref, mb_ref, x_ref, wa_ref, wb_ref, wo_ref, g2_ref,
                     rwh_ref, rwl_ref, rb_ref, x1_ref, h2_ref, gate_ref, idx_ref):
    ya = jnp.dot(o_ref[...].astype(BF16), wa_ref[...], preferred_element_type=F32)
    yb = jnp.dot(us_ref[...].astype(BF16), wb_ref[...], preferred_element_type=F32)
    mixed = _sigmoid(ma_ref[...]) * ya + _sigmoid(mb_ref[...]) * yb
    x1 = x_ref[...] + jnp.dot(mixed.astype(BF16), wo_ref[...], preferred_element_type=F32)
    x1_ref[...] = x1
    h2 = _rms(x1, g2_ref[...])
    h2_ref[...] = h2.astype(BF16)
    hh, hl = _split(h2)
    d = functools.partial(jnp.dot, preferred_element_type=F32)
    logits = d(hh, rwh_ref[...]) + (d(hh, rwl_ref[...]) + d(hl, rwh_ref[...])) + rb_ref[...]
    lane = lax.broadcasted_iota(jnp.int32, logits.shape, 1)
    vals, idxs = [], []
    for _ in range(TOP_K):
        m = jnp.max(logits, axis=-1, keepdims=True)
        i = jnp.min(jnp.where(logits == m, lane, LANES), axis=-1, keepdims=True)
        vals.append(m)
        idxs.append(i)
        logits = jnp.where(lane == i, -jnp.inf, logits)
    es = [jnp.exp(v - vals[0]) for v in vals]
    tot = es[0]
    for e in es[1:]:
        tot = tot + e
    gates = jnp.zeros(logits.shape, F32)
    idx = jnp.zeros(logits.shape, jnp.int32)
    for k in range(TOP_K):
        gates = jnp.where(lane == k, es[k] / tot, gates)
        idx = jnp.where(lane == k, idxs[k], idx)
    gate_ref[...] = gates
    idx_ref[...] = idx


def _post_mix(o, us, z, x, wa, wb, wo, g2, rwh, rwl, rb):
    n = x.shape[0]
    tm = _row_tile(n)
    tok = pl.BlockSpec((tm, D_MODEL), lambda i: (i, 0))
    full = lambda r, c: pl.BlockSpec((r, c), lambda i: (0, 0))
    return pl.pallas_call(
        _post_mix_kernel,
        grid=(n // tm,),
        in_specs=[
            tok, tok,
            pl.BlockSpec((tm, D_MODEL), lambda i: (i, 6)),
            pl.BlockSpec((tm, D_MODEL), lambda i: (i, 7)),
            tok,
            full(V_W, D_MODEL), full(GM_WIDTH, D_MODEL), full(D_MODEL, D_MODEL), full(1, D_MODEL),
            full(D_MODEL, LANES), full(D_MODEL, LANES), full(1, LANES),
        ],
        out_specs=[tok, tok, pl.BlockSpec((tm, LANES), lambda i: (i, 0)),
                   pl.BlockSpec((tm, LANES), lambda i: (i, 0))],
        out_shape=[
            jax.ShapeDtypeStruct((n, D_MODEL), F32),
            jax.ShapeDtypeStruct((n, D_MODEL), BF16),
            jax.ShapeDtypeStruct((n, LANES), F32),
            jax.ShapeDtypeStruct((n, LANES), jnp.int32),
        ],
        compiler_params=_params("parallel"),
        name="post_mix_router",
    )(o, us, z, z, x, wa, wb, wo, g2, rwh, rwl, rb)


def _moe_kernel(be_ref, first_ref, valid_ref, x_ref, w1_ref, b1_ref, w2_ref, b2_ref, y_ref,
                w1_scr, w2_scr):
    b = pl.program_id(0)

    @pl.when(first_ref[b] == 1)
    def _():
        w1_scr[...] = w1_ref[0].astype(BF16)
        w2_scr[...] = w2_ref[0].astype(BF16)

    @pl.when(valid_ref[b] == 1)
    def _():
        hid = jnp.dot(x_ref[...], w1_scr[...], preferred_element_type=F32) + b1_ref[0]
        gate = jnp.minimum(hid[:, :D_EXPERT], SWIGLU_LIMIT)
        up = jnp.clip(hid[:, D_EXPERT:], -SWIGLU_LIMIT, SWIGLU_LIMIT)
        act = gate * _sigmoid(SWIGLU_ALPHA * gate) * (up + 1.0)
        y_ref[...] = jnp.dot(act.astype(BF16), w2_scr[...], preferred_element_type=F32) + b2_ref[0]

    @pl.when(valid_ref[b] == 0)
    def _():
        y_ref[...] = jnp.zeros(y_ref.shape, F32)


def _moe_experts(block_e, first, valid, xb, w1, b1, w2, b2):
    rows = xb.shape[0]
    nb = rows // MOE_ROWS
    return pl.pallas_call(
        _moe_kernel,
        grid_spec=pltpu.PrefetchScalarGridSpec(
            num_scalar_prefetch=3,
            grid=(nb,),
            in_specs=[
                pl.BlockSpec((MOE_ROWS, D_MODEL), lambda b, be, fi, va: (b, 0)),
                pl.BlockSpec((1, D_MODEL, 2 * D_EXPERT), lambda b, be, fi, va: (be[b], 0, 0)),
                pl.BlockSpec((1, 1, 2 * D_EXPERT), lambda b, be, fi, va: (be[b], 0, 0)),
                pl.BlockSpec((1, D_EXPERT, D_MODEL), lambda b, be, fi, va: (be[b], 0, 0)),
                pl.BlockSpec((1, 1, D_MODEL), lambda b, be, fi, va: (be[b], 0, 0)),
            ],
            out_specs=pl.BlockSpec((MOE_ROWS, D_MODEL), lambda b, be, fi, va: (b, 0)),
            scratch_shapes=[
                pltpu.VMEM((D_MODEL, 2 * D_EXPERT), BF16),
                pltpu.VMEM((D_EXPERT, D_MODEL), BF16),
            ],
        ),
        out_shape=jax.ShapeDtypeStruct((rows, D_MODEL), F32),
        compiler_params=_params("arbitrary"),
        name="moe_experts",
    )(block_e, first, valid, xb, w1, b1, w2, b2)


def _moe_dispatch(idx, n):
    experts = jnp.arange(N_EXPERTS, dtype=jnp.int32)
    onehot = jnp.sum((idx[:, :, None] == experts).astype(jnp.int32), axis=1)
    cum = jnp.cumsum(onehot, axis=0)
    counts = cum[-1]
    before = cum - onehot
    padded = (counts + MOE_ROWS - 1) // MOE_ROWS * MOE_ROWS
    pad_end = jnp.cumsum(padded)
    pad_start = pad_end - padded
    rank = jnp.take_along_axis(before, idx, axis=1)
    dest = pad_start[idx] + rank
    nb = -(-n * TOP_K // MOE_ROWS) + N_EXPERTS
    rows = nb * MOE_ROWS
    flat_tok = jnp.arange(n * TOP_K, dtype=jnp.int32) // TOP_K
    row_tok = jnp.full((rows,), n, jnp.int32).at[dest.reshape(-1)].set(flat_tok)
    starts = jnp.arange(nb, dtype=jnp.int32) * MOE_ROWS
    valid = (starts < pad_end[-1]).astype(jnp.int32)
    owner = lambda r: jnp.minimum(jnp.sum((pad_end[None, :] <= r[:, None]).astype(jnp.int32), axis=1),
                                  N_EXPERTS - 1)
    last_e = owner(pad_end[-1:] - 1)[0]
    block_e = jnp.where(valid == 1, owner(starts), last_e).astype(jnp.int32)
    first = jnp.concatenate([jnp.ones((1,), jnp.int32),
                             (block_e[1:] != block_e[:-1]).astype(jnp.int32)])
    return dest, row_tok, block_e, first, valid


def _tail_kernel(x1_ref, moe_ref, p_ref, g3_ref, wg_ref, wp_ref, gf_ref, y_ref):
    x2 = x1_ref[...] + moe_ref[...]
    h3 = _rms(x2, g3_ref[...])
    gate = _sigmoid(jnp.dot(h3.astype(BF16), wg_ref[...], preferred_element_type=F32))
    pe = jnp.dot(p_ref[...].astype(BF16), wp_ref[...], preferred_element_type=F32)
    y_ref[...] = _rms(x2 + gate * pe, gf_ref[...])


def _tail(x1, moe, p, g3, wg, wp, gf):
    n = x1.shape[0]
    tm = _row_tile(n)
    tok = pl.BlockSpec((tm, D_MODEL), lambda i: (i, 0))
    full = lambda r, c: pl.BlockSpec((r, c), lambda i: (0, 0))
    return pl.pallas_call(
        _tail_kernel,
        grid=(n // tm,),
        in_specs=[tok, tok, pl.BlockSpec((tm, PLE_DIM), lambda i: (i, 0)),
                  full(1, D_MODEL), full(D_MODEL, D_MODEL), full(PLE_DIM, D_MODEL), full(1, D_MODEL)],
        out_specs=tok,
        out_shape=jax.ShapeDtypeStruct((n, D_MODEL), F32),
        compiler_params=_params("parallel"),
        name="ple_final_norm",
    )(x1, moe, p, g3, wg, wp, gf)


def _lane_pad(v, offset, fill=0.0):
    out = jnp.full((1, LANES), fill, F32)
    return out.at[0, offset:offset + v.shape[0]].set(v.astype(F32))


def kernel(x_prompt, x_sample, state_delta, state_conv, p_prompt, p_sample, norm1_g, w_in, conv_w, a_log, dt_bias, dn_norm_g, w_proj_a, gm_ln_g, gm_ln_b, gm_ws, gm_bs, w_proj_b, w_out, norm2_g, router_w, router_b, moe_w1, moe_b1, moe_w2, moe_b2, norm3_g, ple_w, ple_gate_w, final_norm_g):
    bp, lp, d = x_prompt.shape
    bs, ls, _ = x_sample.shape
    depth = w_in.shape[0]
    assert depth == 1 and d == D_MODEL
    assert lp % GM_CHUNK == 0 and GM_CHUNK % ls == 0 and ls >= DN_CONV - 1
    n_p, n_s = bp * lp, bs * ls
    n = n_p + n_s
    i = 0

    x = jnp.concatenate([x_prompt.reshape(n_p, d), x_sample.reshape(n_s, d)], axis=0)
    p = jnp.concatenate([p_prompt[i].reshape(n_p, PLE_DIM), p_sample[i].reshape(n_s, PLE_DIM)], axis=0)

    ab0 = QKV_W
    w = w_in[i]
    w_main = jnp.concatenate([w[:, :ab0], w[:, ab0 + 2 * DN_HEADS:]], axis=1).astype(BF16)
    w_ab = jnp.pad(w[:, ab0:ab0 + 2 * DN_HEADS], ((0, 0), (0, LANES - 2 * DN_HEADS))).astype(BF16)
    row2 = lambda v: v.reshape(1, -1).astype(F32)

    z, ab = _in_proj(x, row2(norm1_g[i]), w_main, w_ab)

    alog_p = _lane_pad(a_log[i], 0)
    dtb_p = _lane_pad(dt_bias[i], 0)
    cw = conv_w[i].astype(F32)
    ong = row2(dn_norm_g[i])
    zero_s = jnp.zeros((bp, DN_HEADS, DN_DK, DN_DV), F32)
    zero_buf = jnp.zeros((bp, DN_CONV - 1, QKV_W), F32)
    o, sd_p, sc_p = _delta_branch(z, ab, zero_buf, zero_s, cw, alog_p, dtb_p, ong, None,
                                  tok0=0, L=lp, G=2, precise=False)
    o, sd_s, sc_s = _delta_branch(z, ab, state_conv[i], state_delta[i], cw, alog_p, dtb_p, ong, o,
                                  tok0=n_p, L=ls, G=2, precise=False)

    t = GM_CHUNK
    tri = jnp.tril(jnp.ones((t, t), bool))
    ws = gm_ws[i]
    mix_p = jnp.where(tri, ws, 0.0)
    small = jnp.where(tri[:ls, :ls], ws[:, :ls, :ls], 0.0)
    mix_s = jnp.einsum('ab,gts->gatbs', jnp.eye(t // ls, dtype=F32), small).reshape(GM_GROUPS, t, t)
    mix = jnp.stack([mix_p, mix_s]).astype(BF16)
    gw = GM_WIDTH // GM_GROUPS
    bias_p = jnp.repeat(gm_bs[i].T, gw, axis=1)
    bias_s = jnp.tile(bias_p[:ls], (t // ls, 1))
    bias = jnp.stack([bias_p, bias_s]).astype(F32)
    us, v_s = _chunk_mlp(z, row2(gm_ln_g[i]), row2(gm_ln_b[i]), mix, bias, n_p)

    rw = jnp.pad(router_w[i].astype(F32), ((0, 0), (0, LANES - N_EXPERTS)))
    rwh = rw.astype(BF16)
    rwl = (rw - rwh.astype(F32)).astype(BF16)
    rb = _lane_pad(router_b[i], 0, fill=-jnp.inf)
    x1, h2, gates, idx = _post_mix(o, us, z, x, w_proj_a[i].astype(BF16), w_proj_b[i].astype(BF16),
                                   w_out[i].astype(BF16), row2(norm2_g[i]), rwh, rwl, rb)
    gates, idx = gates[:, :TOP_K], idx[:, :TOP_K]

    dest, row_tok, block_e, first, valid = _moe_dispatch(idx, n)
    h2_pad = jnp.concatenate([h2, jnp.zeros((1, d), BF16)], axis=0)
    xb = jnp.take(h2_pad, row_tok, axis=0)
    yb = _moe_experts(block_e, first, valid, xb, moe_w1[i], moe_b1[i][:, None, :],
                      moe_w2[i], moe_b2[i][:, None, :])
    moe = jnp.sum(jnp.take(yb, dest, axis=0) * gates[:, :, None], axis=1)

    y = _tail(x1, moe, p, row2(norm3_g[i]), ple_gate_w[i].astype(BF16), ple_w[i].astype(BF16),
              row2(final_norm_g))

    return (y[:n_p].reshape(bp, lp, d), y[n_p:].reshape(bs, ls, d),
            sd_p[None], sc_p[None], sd_s[None], sc_s[None], v_s.reshape(1, bs, ls, GM_WIDTH))
```

```python
import functools
import math

import jax
import jax.numpy as jnp
from jax import lax
from jax.experimental import pallas as pl
from jax.experimental.pallas import tpu as pltpu

F32 = jnp.float32
BF16 = jnp.bfloat16

D_MODEL = 1024
DN_HEADS = 8
DN_DK = 128
DN_DV = 128
DN_CONV = 4
DN_CHUNK = 64
GM_WIDTH = 1024
GM_GROUPS = 8
GM_CHUNK = 128
N_EXPERTS = 32
TOP_K = 4
D_EXPERT = 1024
SWIGLU_LIMIT = 7.0
SWIGLU_ALPHA = 1.702
PLE_DIM = 256
EPS = 1e-6
QK_W = DN_HEADS * DN_DK
V_W = DN_HEADS * DN_DV
QKV_W = 2 * QK_W + V_W

LANES = 128
MOE_ROWS = 512
VMEM_LIMIT = 56 << 20


def _params(*sem):
    return pltpu.CompilerParams(dimension_semantics=sem, vmem_limit_bytes=VMEM_LIMIT)


def _dot(a, b):
    return jnp.dot(a.astype(BF16), b.astype(BF16), preferred_element_type=F32)


def _split(a):
    hi = a.astype(BF16)
    return hi, (a - hi.astype(F32)).astype(BF16)


def _dot3(a, b):
    ah, al = _split(a)
    bh, bl = _split(b)
    d = functools.partial(jnp.dot, preferred_element_type=F32)
    return d(ah, bh) + (d(ah, bl) + d(al, bh))


def _dot_nt(a, b):
    return lax.dot_general(a.astype(BF16), b.astype(BF16), (((1,), (1,)), ((), ())),
                           preferred_element_type=F32)


def _dot_tn(a, b):
    return lax.dot_general(a.astype(BF16), b.astype(BF16), (((0,), (0,)), ((), ())),
                           preferred_element_type=F32)


def _dot3_tn(a, b):
    ah, al = _split(a)
    bh, bl = _split(b)
    d = functools.partial(lax.dot_general, dimension_numbers=(((0,), (0,)), ((), ())),
                          preferred_element_type=F32)
    return d(ah, bh) + (d(ah, bl) + d(al, bh))


def _sigmoid(x):
    return 1.0 / (1.0 + jnp.exp(-x))


def _rms(x, g):
    return x * lax.rsqrt(jnp.mean(x * x, axis=-1, keepdims=True) + EPS) * g


def _gelu(x):
    return 0.5 * x * (1.0 + lax.erf(x * (1.0 / math.sqrt(2.0))))


def _row_tile(n_p, n_s, cap=512):
    for t in (1024, 512, 256, 128):
        if t > cap:
            continue
        if n_p % t == 0 and n_s % t == 0:
            return t
    raise ValueError(f"token counts {n_p}, {n_s} must be multiples of 128")


def _group_specs(tm, width, n_p):
    pt = n_p // tm
    return [pl.BlockSpec((tm, width), lambda i, *_: (jnp.minimum(i, pt - 1), 0)),
            pl.BlockSpec((tm, width), lambda i, *_: (jnp.maximum(i - pt, 0), 0))]


def _group_pick(prompt_ref, sample_ref, n_p):
    pt = n_p // prompt_ref.shape[0]
    return jnp.where(pl.program_id(0) < pt, prompt_ref[...], sample_ref[...])


def _in_proj_kernel(xp_ref, xs_ref, g_ref, w_ref, wab_ref, z_ref, ab_ref, h_scr, *, n_p):
    @pl.when(pl.program_id(1) == 0)
    def _():
        hb = _rms(_group_pick(xp_ref, xs_ref, n_p), g_ref[...]).astype(BF16)
        h_scr[...] = hb
        ab_ref[...] = jnp.dot(hb, wab_ref[...], preferred_element_type=F32)

    z_ref[...] = jnp.dot(h_scr[...], w_ref[...], preferred_element_type=F32).astype(z_ref.dtype)


def _in_proj(xp, xs, g, w_main, w_ab):
    n_p, n_s = xp.shape[0], xs.shape[0]
    n = n_p + n_s
    tm, tn = _row_tile(n_p, n_s, cap=1024), 1024
    cols = w_main.shape[1]
    return pl.pallas_call(
        functools.partial(_in_proj_kernel, n_p=n_p),
        grid=(n // tm, cols // tn),
        in_specs=_group_specs(tm, D_MODEL, n_p) + [
            pl.BlockSpec((1, D_MODEL), lambda i, j: (0, 0)),
            pl.BlockSpec((D_MODEL, tn), lambda i, j: (0, j)),
            pl.BlockSpec((D_MODEL, LANES), lambda i, j: (0, 0)),
        ],
        out_specs=[
            pl.BlockSpec((tm, tn), lambda i, j: (i, j)),
            pl.BlockSpec((tm, LANES), lambda i, j: (i, 0)),
        ],
        out_shape=[jax.ShapeDtypeStruct((n, cols), BF16), jax.ShapeDtypeStruct((n, LANES), F32)],
        scratch_shapes=[pltpu.VMEM((tm, D_MODEL), BF16)],
        compiler_params=_params("parallel", "arbitrary"),
        name="in_proj",
    )(xp, xs, g, w_main, w_ab)


def _delta_kernel(*refs, C, G, carry, precise):
    (qkv_ref, zg_ref, ab_ref, buf_ref, s0_ref, cw_ref, alog_ref, dtb_ref, ong_ref) = refs[:9]
    refs = refs[9:]
    o_ref, snew_ref, bufnew_ref, xc_scr = refs[:4]
    H, DK, DV = DN_HEADS, DN_DK, DN_DV
    T = G * C
    dotm = _dot3 if precise else _dot
    dotm_tn = _dot3_tn if precise else _dot_tn
    halo = DN_CONV - 1
    base = 8
    cw = cw_ref[...]

    def conv(window):
        y = window(0) * cw[0:1]
        for i in range(1, DN_CONV):
            y = y + window(i) * cw[i:i + 1]
        return y

    if carry:
        s_scr = refs[4]
        c = pl.program_id(1)
        last = pl.num_programs(1) - 1

        @pl.when(c == 0)
        def _():
            xc_scr[base - halo:base, :] = buf_ref[0]
            s_scr[...] = s0_ref[0]

        xc_scr[base:base + T, :] = qkv_ref[...].astype(F32)
        y = conv(lambda i: xc_scr[base - halo + i:base - halo + i + T, :])
        tail = xc_scr[base + T - halo:base + T, :]
        xc_scr[base - halo:base, :] = tail

        @pl.when(c == last)
        def _():
            bufnew_ref[0] = tail
    else:
        ys = []
        x_new = qkv_ref[...].astype(F32)
        for g in range(G):
            xc_scr[g, base - halo:base, :] = buf_ref[g]
            xc_scr[g, base:base + C, :] = x_new[g * C:(g + 1) * C, :]
            ys.append(conv(lambda i: xc_scr[g, base - halo + i:base - halo + i + C, :]))
            bufnew_ref[g] = xc_scr[g, base + C - halo:base + C, :]
        y = jnp.concatenate(ys, axis=0) if G > 1 else ys[0]
    qkv = y * _sigmoid(y)

    ab = ab_ref[...]
    g_all = -jnp.exp(alog_ref[...]) * jax.nn.softplus(ab + dtb_ref[...])
    beta_all = _sigmoid(ab)
    shift = C.bit_length() - 1
    rt = lax.broadcasted_iota(jnp.int32, (T, T), 0)
    ct = lax.broadcasted_iota(jnp.int32, (T, T), 1)
    chunk_tril = ((rt >> shift) == (ct >> shift)) & (rt >= ct)
    gcum = _dot3(chunk_tril.astype(F32), g_all)
    gam_all = jnp.exp(gcum)

    row = lax.broadcasted_iota(jnp.int32, (C, C), 0)
    col = lax.broadcasted_iota(jnp.int32, (C, C), 1)
    incl, strict, eye = row >= col, row > col, row == col
    eyef = eye.astype(F32)
    units = [(g, h) for g in range(G) for h in range(H)]
    rows = lambda a, g: a[g * C:(g + 1) * C]

    qn, kn, vv = [], [], []
    for h in range(H):
        q = qkv[:, h * DK:(h + 1) * DK]
        k = qkv[:, QK_W + h * DK:QK_W + (h + 1) * DK]
        qn.append(q * lax.rsqrt(jnp.sum(q * q, axis=-1, keepdims=True) + EPS) * (DK ** -0.5))
        kn.append(k * lax.rsqrt(jnp.sum(k * k, axis=-1, keepdims=True) + EPS))
        vv.append(qkv[:, 2 * QK_W + h * DV:2 * QK_W + (h + 1) * DV])

    gc, bc, gl, a_low, m_intra = {}, {}, {}, {}, {}
    for u in units:
        g, h = u
        gc[u] = rows(gcum, g)[:, h:h + 1]
        bc[u] = rows(beta_all, g)[:, H + h:H + h + 1]
        gl[u] = gc[u][C - 1:C, :]
        gr = jnp.sum(jnp.where(eye, gc[u], 0.0), axis=0, keepdims=True)
        br = jnp.sum(jnp.where(eye, bc[u], 0.0), axis=0, keepdims=True)
        db = jnp.where(incl, jnp.exp(gc[u] - gr), 0.0) * br
        k = rows(kn[h], g)
        a_low[u] = jnp.where(strict, _dot_nt(k, k) * db, 0.0)
        m_intra[u] = _dot_nt(rows(qn[h], g), k) * db

    t_inv = {u: eyef - a_low[u] for u in units}
    a_pow = {u: _dot3(a_low[u], a_low[u]) for u in units}
    n = 2
    while n < C:
        t_inv = {u: t_inv[u] + _dot3(t_inv[u], a_pow[u]) for u in units}
        n *= 2
        if n < C:
            a_pow = {u: _dot3(a_pow[u], a_pow[u]) for u in units}

    u_base, w_state, q_dec, k_dec = {}, {}, {}, {}
    for u in units:
        g, h = u
        gam = rows(gam_all, g)[:, h:h + 1]
        k = rows(kn[h], g)
        u_base[u] = dotm(t_inv[u], rows(vv[h], g))
        w_state[u] = dotm(t_inv[u], gam * k)
        q_dec[u] = gam * rows(qn[h], g)
        k_dec[u] = k * (bc[u] * jnp.exp(gl[u] - gc[u]))

    state = [s_scr[h] for h in range(H)] if carry else None
    outs = {}
    for g in range(G):
        s_in = state if carry else [s0_ref[g, h] for h in range(H)]
        us = [u_base[(g, h)] - dotm(w_state[(g, h)], s_in[h]) for h in range(H)]
        for h in range(H):
            outs[(g, h)] = dotm(q_dec[(g, h)], s_in[h]) + dotm(m_intra[(g, h)], us[h])
        s_out = [jnp.exp(gl[(g, h)]) * s_in[h] + dotm_tn(k_dec[(g, h)], us[h]) for h in range(H)]
        if carry:
            state = s_out
        else:
            for h in range(H):
                snew_ref[g, h] = s_out[h]

    for h in range(H):
        o = jnp.concatenate([outs[(g, h)] for g in range(G)], axis=0) if G > 1 else outs[(0, h)]
        zg = zg_ref[:, h * DV:(h + 1) * DV].astype(F32)
        o_ref[:, h * DV:(h + 1) * DV] = _rms(o, ong_ref[...]) * (zg * _sigmoid(zg))

    if carry:
        for h in range(H):
            s_scr[h] = state[h]

        @pl.when(c == last)
        def _():
            for h in range(H):
                snew_ref[0, h] = state[h]


def _delta_branch(z, ab, conv_buf, s0, conv_w, alog_p, dtb_p, onorm_g, *, tok0, L, G, precise):
    B = conv_buf.shape[0]
    C = math.gcd(L, DN_CHUNK)
    nc = L // C
    carry = nc > 1
    T = G * C
    blk0 = tok0 // T
    if carry:
        assert nc % G == 0
        grid = (B, nc // G)
        own_blk = lambda b, c: b * (nc // G) + c
        gs = 1
    else:
        assert B % G == 0
        grid = (B // G, 1)
        own_blk = lambda b, c: b
        gs = G
    tok = lambda b, c: (blk0 + own_blk(b, c), 0)
    seq3 = lambda b, c: (b, 0, 0)
    seq4 = lambda b, c: (b, 0, 0, 0)
    const = lambda b, c: (0, 0)
    if carry:
        scratch = [pltpu.VMEM((8 + T, QKV_W), F32), pltpu.VMEM((DN_HEADS, DN_DK, DN_DV), F32)]
    else:
        scratch = [pltpu.VMEM((G, 8 + C, QKV_W), F32)]
    return pl.pallas_call(
        functools.partial(_delta_kernel, C=C, G=G, carry=carry, precise=precise),
        grid=grid,
        in_specs=[
            pl.BlockSpec((T, QKV_W), tok),
            pl.BlockSpec((T, V_W), lambda b, c: (blk0 + own_blk(b, c), QKV_W // V_W)),
            pl.BlockSpec((T, LANES), tok),
            pl.BlockSpec((gs, DN_CONV - 1, QKV_W), seq3),
            pl.BlockSpec((gs, DN_HEADS, DN_DK, DN_DV), seq4),
            pl.BlockSpec((DN_CONV, QKV_W), const),
            pl.BlockSpec((1, LANES), const),
            pl.BlockSpec((1, LANES), const),
            pl.BlockSpec((1, DN_DV), const),
        ],
        out_specs=[
            pl.BlockSpec((T, V_W), lambda b, c: (own_blk(b, c), 0)),
            pl.BlockSpec((gs, DN_HEADS, DN_DK, DN_DV), seq4),
            pl.BlockSpec((gs, DN_CONV - 1, QKV_W), seq3),
        ],
        out_shape=[
            jax.ShapeDtypeStruct((B * L, V_W), F32),
            jax.ShapeDtypeStruct(s0.shape, F32),
            jax.ShapeDtypeStruct(conv_buf.shape, F32),
        ],
        scratch_shapes=scratch,
        compiler_params=_params("parallel", "arbitrary"),
        name=f"delta_rule_c{C}",
    )(z, z, ab, conv_buf, s0, conv_w, alog_p, dtb_p, onorm_g)


def _chunk_mlp_kernel(gu_ref, gv_ref, lng_ref, lnb_ref, mix_ref, bias_ref, us_ref, v_ref):
    u = _gelu(gu_ref[...].astype(F32))
    a = _gelu(gv_ref[...].astype(F32))
    ac = a - jnp.mean(a, axis=-1, keepdims=True)
    v = ac * lax.rsqrt(jnp.mean(ac * ac, axis=-1, keepdims=True) + EPS) * lng_ref[...] + lnb_ref[...]
    v_ref[...] = v
    vb = v.astype(BF16)
    w = GM_WIDTH // GM_GROUPS
    for g in range(GM_GROUPS):
        sl = slice(g * w, (g + 1) * w)
        s = jnp.dot(mix_ref[0, g], vb[:, sl], preferred_element_type=F32) + bias_ref[0, :, sl]
        us_ref[:, sl] = u[:, sl] * s


def _chunk_mlp(z, ln_g, ln_b, mix, bias, n_prompt):
    n = z.shape[0]
    t = GM_CHUNK
    pb = n_prompt // t
    grp = lambda i: jnp.where(i < pb, 0, 1)
    return pl.pallas_call(
        _chunk_mlp_kernel,
        grid=(n // t,),
        in_specs=[
            pl.BlockSpec((t, GM_WIDTH), lambda i: (i, 4)),
            pl.BlockSpec((t, GM_WIDTH), lambda i: (i, 5)),
            pl.BlockSpec((1, GM_WIDTH), lambda i: (0, 0)),
            pl.BlockSpec((1, GM_WIDTH), lambda i: (0, 0)),
            pl.BlockSpec((1, GM_GROUPS, t, t), lambda i: (grp(i), 0, 0, 0)),
            pl.BlockSpec((1, t, GM_WIDTH), lambda i: (grp(i), 0, 0)),
        ],
        out_specs=[
            pl.BlockSpec((t, GM_WIDTH), lambda i: (i, 0)),
            pl.BlockSpec((t, GM_WIDTH), lambda i: (jnp.maximum(i - pb, 0), 0)),
        ],
        out_shape=[
            jax.ShapeDtypeStruct((n, GM_WIDTH), F32),
            jax.ShapeDtypeStruct((n - n_prompt, GM_WIDTH), F32),
        ],
        compiler_params=_params("arbitrary"),
        name="chunk_mlp",
    )(z, z, ln_g, ln_b, mix, bias)


def _post_mix_kernel(op_ref, os_ref, us_ref, ma_ref, mb_ref, xp_ref, xs_ref, wa_ref, wb_ref, wo_ref, g2_ref,
                     rwh_ref, rwl_ref, rb_ref, x1_ref, h2_ref, gate_ref, idx_ref, *, n_p):
    ya = jnp.dot(_group_pick(op_ref, os_ref, n_p).astype(BF16), wa_ref[...], preferred_element_type=F32)
    yb = jnp.dot(us_ref[...].astype(BF16), wb_ref[...], preferred_element_type=F32)
    mixed = _sigmoid(ma_ref[...].astype(F32)) * ya + _sigmoid(mb_ref[...].astype(F32)) * yb
    x1 = _group_pick(xp_ref, xs_ref, n_p) + jnp.dot(mixed.astype(BF16), wo_ref[...], preferred_element_type=F32)
    x1_ref[...] = x1
    h2 = _rms(x1, g2_ref[...])
    h2_ref[...] = h2
    hh, hl = _split(h2)
    d = functools.partial(jnp.dot, preferred_element_type=F32)
    logits = d(hh, rwh_ref[...]) + (d(hh, rwl_ref[...]) + d(hl, rwh_ref[...])) + rb_ref[...]
    lane = lax.broadcasted_iota(jnp.int32, logits.shape, 1)
    vals, idxs = [], []
    for _ in range(TOP_K):
        m = jnp.max(logits, axis=-1, keepdims=True)
        i = jnp.min(jnp.where(logits == m, lane, LANES), axis=-1, keepdims=True)
        vals.append(m)
        idxs.append(i)
        logits = jnp.where(lane == i, -jnp.inf, logits)
    es = [jnp.exp(v - vals[0]) for v in vals]
    tot = es[0]
    for e in es[1:]:
        tot = tot + e
    gates = jnp.zeros(logits.shape, F32)
    idx = jnp.zeros(logits.shape, jnp.int32)
    for k in range(TOP_K):
        gates = jnp.where(lane == k, es[k] / tot, gates)
        idx = jnp.where(lane == k, idxs[k], idx)
    gate_ref[...] = gates
    idx_ref[...] = idx


def _post_mix(o_p, o_s, us, z, xp, xs, wa, wb, wo, g2, rwh, rwl, rb):
    n_p, n_s = xp.shape[0], xs.shape[0]
    n = n_p + n_s
    tm = _row_tile(n_p, n_s)
    tok = pl.BlockSpec((tm, D_MODEL), lambda i: (i, 0))
    full = lambda r, c: pl.BlockSpec((r, c), lambda i: (0, 0))
    return pl.pallas_call(
        functools.partial(_post_mix_kernel, n_p=n_p),
        grid=(n // tm,),
        in_specs=[
            *_group_specs(tm, V_W, n_p), tok,
            pl.BlockSpec((tm, D_MODEL), lambda i: (i, 6)),
            pl.BlockSpec((tm, D_MODEL), lambda i: (i, 7)),
            *_group_specs(tm, D_MODEL, n_p),
            full(V_W, D_MODEL), full(GM_WIDTH, D_MODEL), full(D_MODEL, D_MODEL), full(1, D_MODEL),
            full(D_MODEL, LANES), full(D_MODEL, LANES), full(1, LANES),
        ],
        out_specs=[tok, tok, pl.BlockSpec((tm, LANES), lambda i: (i, 0)),
                   pl.BlockSpec((tm, LANES), lambda i: (i, 0))],
        out_shape=[
            jax.ShapeDtypeStruct((n, D_MODEL), F32),
            jax.ShapeDtypeStruct((n, D_MODEL), F32),
            jax.ShapeDtypeStruct((n, LANES), F32),
            jax.ShapeDtypeStruct((n, LANES), jnp.int32),
        ],
        compiler_params=_params("parallel"),
        name="post_mix_router",
    )(o_p, o_s, us, z, z, xp, xs, wa, wb, wo, g2, rwh, rwl, rb)


def _moe_kernel(be_ref, first_ref, valid_ref, x_ref, w1_ref, b1_ref, w2_ref, b2_ref, y_ref,
                w1_scr, w2_scr):
    b = pl.program_id(0)

    @pl.when(first_ref[b] == 1)
    def _():
        w1_scr[...] = w1_ref[0].astype(BF16)
        w2_scr[...] = w2_ref[0].astype(BF16)

    @pl.when(valid_ref[b] == 1)
    def _():
        hid = jnp.dot(x_ref[...].astype(BF16), w1_scr[...], preferred_element_type=F32) + b1_ref[0]
        gate = jnp.minimum(hid[:, :D_EXPERT], SWIGLU_LIMIT)
        up = jnp.clip(hid[:, D_EXPERT:], -SWIGLU_LIMIT, SWIGLU_LIMIT)
        act = gate * _sigmoid(SWIGLU_ALPHA * gate) * (up + 1.0)
        y_ref[...] = jnp.dot(act.astype(BF16), w2_scr[...], preferred_element_type=F32) + b2_ref[0]

    @pl.when(valid_ref[b] == 0)
    def _():
        y_ref[...] = jnp.zeros(y_ref.shape, F32)


def _moe_experts(block_e, first, valid, xb, w1, b1, w2, b2):
    rows = xb.shape[0]
    nb = rows // MOE_ROWS
    return pl.pallas_call(
        _moe_kernel,
        grid_spec=pltpu.PrefetchScalarGridSpec(
            num_scalar_prefetch=3,
            grid=(nb,),
            in_specs=[
                pl.BlockSpec((MOE_ROWS, D_MODEL), lambda b, be, fi, va: (b, 0)),
                pl.BlockSpec((1, D_MODEL, 2 * D_EXPERT), lambda b, be, fi, va: (be[b], 0, 0)),
                pl.BlockSpec((1, 1, 2 * D_EXPERT), lambda b, be, fi, va: (be[b], 0, 0)),
                pl.BlockSpec((1, D_EXPERT, D_MODEL), lambda b, be, fi, va: (be[b], 0, 0)),
                pl.BlockSpec((1, 1, D_MODEL), lambda b, be, fi, va: (be[b], 0, 0)),
            ],
            out_specs=pl.BlockSpec((MOE_ROWS, D_MODEL), lambda b, be, fi, va: (b, 0)),
            scratch_shapes=[
                pltpu.VMEM((D_MODEL, 2 * D_EXPERT), BF16),
                pltpu.VMEM((D_EXPERT, D_MODEL), BF16),
            ],
        ),
        out_shape=jax.ShapeDtypeStruct((rows, D_MODEL), F32),
        compiler_params=_params("arbitrary"),
        name="moe_experts",
    )(block_e, first, valid, xb, w1, b1, w2, b2)


def _moe_dispatch(idx, n):
    experts = jnp.arange(N_EXPERTS, dtype=jnp.int32)
    onehot = jnp.sum((idx[:, :, None] == experts).astype(jnp.int32), axis=1)
    cum = jnp.cumsum(onehot, axis=0)
    counts = cum[-1]
    before = cum - onehot
    padded = (counts + MOE_ROWS - 1) // MOE_ROWS * MOE_ROWS
    pad_end = jnp.cumsum(padded)
    pad_start = pad_end - padded
    rank = jnp.take_along_axis(before, idx, axis=1)
    dest = pad_start[idx] + rank
    nb = -(-n * TOP_K // MOE_ROWS) + N_EXPERTS
    rows = nb * MOE_ROWS
    flat_tok = jnp.arange(n * TOP_K, dtype=jnp.int32) // TOP_K
    row_tok = jnp.zeros((rows,), jnp.int32).at[dest.reshape(-1)].set(
        flat_tok, unique_indices=True, mode='promise_in_bounds')
    starts = jnp.arange(nb, dtype=jnp.int32) * MOE_ROWS
    valid = (starts < pad_end[-1]).astype(jnp.int32)
    owner = lambda r: jnp.minimum(jnp.sum((pad_end[None, :] <= r[:, None]).astype(jnp.int32), axis=1),
                                  N_EXPERTS - 1)
    last_e = owner(pad_end[-1:] - 1)[0]
    block_e = jnp.where(valid == 1, owner(starts), last_e).astype(jnp.int32)
    first = jnp.concatenate([jnp.ones((1,), jnp.int32),
                             (block_e[1:] != block_e[:-1]).astype(jnp.int32)])
    return dest, row_tok, block_e, first, valid


def _tail_kernel(x1_ref, yk_ref, gate_ref, pp_ref, ps_ref, g3_ref, wg_ref, wp_ref, gf_ref,
                 yp_ref, ys_ref, *, n_p):
    gates = gate_ref[...]
    moe = yk_ref[0] * gates[:, 0:1]
    for k in range(1, TOP_K):
        moe = moe + yk_ref[k] * gates[:, k:k + 1]
    x2 = x1_ref[...] + moe
    h3 = _rms(x2, g3_ref[...])
    gate = _sigmoid(jnp.dot(h3.astype(BF16), wg_ref[...], preferred_element_type=F32))
    pe = jnp.dot(_group_pick(pp_ref, ps_ref, n_p).astype(BF16), wp_ref[...], preferred_element_type=F32)
    y = _rms(x2 + gate * pe, gf_ref[...])
    in_prompt = pl.program_id(0) < n_p // yp_ref.shape[0]

    @pl.when(in_prompt)
    def _():
        yp_ref[...] = y

    @pl.when(jnp.logical_not(in_prompt))
    def _():
        ys_ref[...] = y


def _tail(x1, yk, gates, pp, ps, g3, wg, wp, gf):
    n_p, n_s = pp.shape[0], ps.shape[0]
    n = n_p + n_s
    tm = _row_tile(n_p, n_s)
    tok = pl.BlockSpec((tm, D_MODEL), lambda i: (i, 0))
    full = lambda r, c: pl.BlockSpec((r, c), lambda i: (0, 0))
    return pl.pallas_call(
        functools.partial(_tail_kernel, n_p=n_p),
        grid=(n // tm,),
        in_specs=[tok, pl.BlockSpec((TOP_K, tm, D_MODEL), lambda i: (0, i, 0)),
                  pl.BlockSpec((tm, LANES), lambda i: (i, 0)),
                  *_group_specs(tm, PLE_DIM, n_p),
                  full(1, D_MODEL), full(D_MODEL, D_MODEL), full(PLE_DIM, D_MODEL), full(1, D_MODEL)],
        out_specs=_group_specs(tm, D_MODEL, n_p),
        out_shape=[jax.ShapeDtypeStruct((n_p, D_MODEL), F32), jax.ShapeDtypeStruct((n_s, D_MODEL), F32)],
        compiler_params=_params("arbitrary"),
        name="ple_final_norm",
    )(x1, yk, gates, pp, ps, g3, wg, wp, gf)


def _lane_pad(v, offset, fill=0.0):
    out = jnp.full((1, LANES), fill, F32)
    return out.at[0, offset:offset + v.shape[0]].set(v.astype(F32))


def kernel(x_prompt, x_sample, state_delta, state_conv, p_prompt, p_sample, norm1_g, w_in, conv_w, a_log, dt_bias, dn_norm_g, w_proj_a, gm_ln_g, gm_ln_b, gm_ws, gm_bs, w_proj_b, w_out, norm2_g, router_w, router_b, moe_w1, moe_b1, moe_w2, moe_b2, norm3_g, ple_w, ple_gate_w, final_norm_g):
    bp, lp, d = x_prompt.shape
    bs, ls, _ = x_sample.shape
    depth = w_in.shape[0]
    assert depth == 1 and d == D_MODEL
    assert lp % GM_CHUNK == 0 and GM_CHUNK % ls == 0 and ls >= DN_CONV - 1
    n_p, n_s = bp * lp, bs * ls
    n = n_p + n_s
    i = 0

    xp, xs = x_prompt.reshape(n_p, d), x_sample.reshape(n_s, d)

    ab0 = QKV_W
    w = w_in[i]
    w_main = jnp.concatenate([w[:, :ab0], w[:, ab0 + 2 * DN_HEADS:]], axis=1).astype(BF16)
    w_ab = jnp.pad(w[:, ab0:ab0 + 2 * DN_HEADS], ((0, 0), (0, LANES - 2 * DN_HEADS))).astype(BF16)
    row2 = lambda v: v.reshape(1, -1).astype(F32)

    z, ab = _in_proj(xp, xs, row2(norm1_g[i]), w_main, w_ab)

    alog_p = _lane_pad(a_log[i], 0)
    dtb_p = _lane_pad(dt_bias[i], 0)
    cw = conv_w[i].astype(F32)
    ong = row2(dn_norm_g[i])
    zero_s = jnp.zeros((bp, DN_HEADS, DN_DK, DN_DV), F32)
    zero_buf = jnp.zeros((bp, DN_CONV - 1, QKV_W), F32)
    o_p, sd_p, sc_p = _delta_branch(z, ab, zero_buf, zero_s, cw, alog_p, dtb_p, ong,
                                    tok0=0, L=lp, G=2, precise=False)
    o_s, sd_s, sc_s = _delta_branch(z, ab, state_conv[i], state_delta[i], cw, alog_p, dtb_p, ong,
                                    tok0=n_p, L=ls, G=2, precise=False)

    t = GM_CHUNK
    tri = jnp.tril(jnp.ones((t, t), bool))
    ws = gm_ws[i]
    mix_p = jnp.where(tri, ws, 0.0)
    small = jnp.where(tri[:ls, :ls], ws[:, :ls, :ls], 0.0)
    mix_s = jnp.einsum('ab,gts->gatbs', jnp.eye(t // ls, dtype=F32), small).reshape(GM_GROUPS, t, t)
    mix = jnp.stack([mix_p, mix_s]).astype(BF16)
    gw = GM_WIDTH // GM_GROUPS
    bias_p = jnp.repeat(gm_bs[i].T, gw, axis=1)
    bias_s = jnp.tile(bias_p[:ls], (t // ls, 1))
    bias = jnp.stack([bias_p, bias_s]).astype(F32)
    us, v_s = _chunk_mlp(z, row2(gm_ln_g[i]), row2(gm_ln_b[i]), mix, bias, n_p)

    rw = jnp.pad(router_w[i].astype(F32), ((0, 0), (0, LANES - N_EXPERTS)))
    rwh = rw.astype(BF16)
    rwl = (rw - rwh.astype(F32)).astype(BF16)
    rb = _lane_pad(router_b[i], 0, fill=-jnp.inf)
    x1, h2, gates, idx = _post_mix(o_p, o_s, us, z, xp, xs, w_proj_a[i].astype(BF16), w_proj_b[i].astype(BF16),
                                   w_out[i].astype(BF16), row2(norm2_g[i]), rwh, rwl, rb)
    dest, row_tok, block_e, first, valid = _moe_dispatch(idx[:, :TOP_K], n)
    xb = h2.at[row_tok].get(mode='promise_in_bounds')
    yb = _moe_experts(block_e, first, valid, xb, moe_w1[i], moe_b1[i][:, None, :],
                      moe_w2[i], moe_b2[i][:, None, :])
    yk = yb.at[dest.T.reshape(-1)].get(mode='promise_in_bounds').reshape(TOP_K, n, d)

    y_p, y_s = _tail(x1, yk, gates, p_prompt[i].reshape(n_p, PLE_DIM), p_sample[i].reshape(n_s, PLE_DIM),
                     row2(norm3_g[i]), ple_gate_w[i].astype(BF16), ple_w[i].astype(BF16), row2(final_norm_g))

    return (y_p.reshape(bp, lp, d), y_s.reshape(bs, ls, d),
            sd_p[None], sc_p[None], sd_s[None], sc_s[None], v_s.reshape(1, bs, ls, GM_WIDTH))
```

```python
import functools
import math

import jax
import jax.numpy as jnp
from jax import lax
from jax.experimental import pallas as pl
from jax.experimental.pallas import tpu as pltpu

F32 = jnp.float32
BF16 = jnp.bfloat16

D_MODEL = 1024
DN_HEADS = 8
DN_DK = 128
DN_DV = 128
DN_CONV = 4
DN_CHUNK = 64
GM_WIDTH = 1024
GM_GROUPS = 8
GM_CHUNK = 128
N_EXPERTS = 32
TOP_K = 4
D_EXPERT = 1024
SWIGLU_LIMIT = 7.0
SWIGLU_ALPHA = 1.702
PLE_DIM = 256
EPS = 1e-6
QK_W = DN_HEADS * DN_DK
V_W = DN_HEADS * DN_DV
QKV_W = 2 * QK_W + V_W

LANES = 128
MOE_ROWS = 512
VMEM_LIMIT = 56 << 20


def _params(*sem):
    return pltpu.CompilerParams(dimension_semantics=sem, vmem_limit_bytes=VMEM_LIMIT)


def _dot(a, b):
    return jnp.dot(a.astype(BF16), b.astype(BF16), preferred_element_type=F32)


def _split(a):
    hi = a.astype(BF16)
    return hi, (a - hi.astype(F32)).astype(BF16)


def _dot3(a, b):
    ah, al = _split(a)
    bh, bl = _split(b)
    d = functools.partial(jnp.dot, preferred_element_type=F32)
    return d(ah, bh) + (d(ah, bl) + d(al, bh))


def _dot_nt(a, b):
    return lax.dot_general(a.astype(BF16), b.astype(BF16), (((1,), (1,)), ((), ())),
                           preferred_element_type=F32)


def _dot_tn(a, b):
    return lax.dot_general(a.astype(BF16), b.astype(BF16), (((0,), (0,)), ((), ())),
                           preferred_element_type=F32)


def _dot3_tn(a, b):
    ah, al = _split(a)
    bh, bl = _split(b)
    d = functools.partial(lax.dot_general, dimension_numbers=(((0,), (0,)), ((), ())),
                          preferred_element_type=F32)
    return d(ah, bh) + (d(ah, bl) + d(al, bh))


def _sigmoid(x):
    return 0.5 * jnp.tanh(0.5 * x) + 0.5


def _rms(x, g):
    return x * lax.rsqrt(jnp.mean(x * x, axis=-1, keepdims=True) + EPS) * g


def _gelu(x):
    return 0.5 * x * (1.0 + lax.erf(x * (1.0 / math.sqrt(2.0))))


def _row_tile(n_p, n_s, cap=512):
    for t in (1024, 512, 256, 128):
        if t > cap:
            continue
        if n_p % t == 0 and n_s % t == 0:
            return t
    raise ValueError(f"token counts {n_p}, {n_s} must be multiples of 128")


def _group_specs(tm, width, n_p):
    pt = n_p // tm
    return [pl.BlockSpec((tm, width), lambda i, *_: (jnp.minimum(i, pt - 1), 0)),
            pl.BlockSpec((tm, width), lambda i, *_: (jnp.maximum(i - pt, 0), 0))]


def _group_pick(prompt_ref, sample_ref, n_p):
    pt = n_p // prompt_ref.shape[0]
    return jnp.where(pl.program_id(0) < pt, prompt_ref[...], sample_ref[...])


def _in_proj_kernel(xp_ref, xs_ref, g_ref, w_ref, wab_ref, z_ref, ab_ref, h_scr, *, n_p):
    @pl.when(pl.program_id(1) == 0)
    def _():
        hb = _rms(_group_pick(xp_ref, xs_ref, n_p), g_ref[...]).astype(BF16)
        h_scr[...] = hb
        ab_ref[...] = jnp.dot(hb, wab_ref[...], preferred_element_type=F32)

    z_ref[...] = jnp.dot(h_scr[...], w_ref[...], preferred_element_type=F32).astype(z_ref.dtype)


def _in_proj(xp, xs, g, w_main, w_ab):
    n_p, n_s = xp.shape[0], xs.shape[0]
    n = n_p + n_s
    tm, tn = _row_tile(n_p, n_s, cap=1024), 1024
    cols = w_main.shape[1]
    return pl.pallas_call(
        functools.partial(_in_proj_kernel, n_p=n_p),
        grid=(n // tm, cols // tn),
        in_specs=_group_specs(tm, D_MODEL, n_p) + [
            pl.BlockSpec((1, D_MODEL), lambda i, j: (0, 0)),
            pl.BlockSpec((D_MODEL, tn), lambda i, j: (0, j)),
            pl.BlockSpec((D_MODEL, LANES), lambda i, j: (0, 0)),
        ],
        out_specs=[
            pl.BlockSpec((tm, tn), lambda i, j: (i, j)),
            pl.BlockSpec((tm, LANES), lambda i, j: (i, 0)),
        ],
        out_shape=[jax.ShapeDtypeStruct((n, cols), BF16), jax.ShapeDtypeStruct((n, LANES), F32)],
        scratch_shapes=[pltpu.VMEM((tm, D_MODEL), BF16)],
        compiler_params=_params("parallel", "arbitrary"),
        name="in_proj",
    )(xp, xs, g, w_main, w_ab)


def _delta_kernel(*refs, C, G, carry, precise):
    (qkv_ref, zg_ref, ab_ref, buf_ref, s0_ref, cw_ref, alog_ref, dtb_ref, ong_ref) = refs[:9]
    refs = refs[9:]
    o_ref, snew_ref, bufnew_ref, xc_scr = refs[:4]
    H, DK, DV = DN_HEADS, DN_DK, DN_DV
    T = G * C
    dotm = _dot3 if precise else _dot
    dotm_tn = _dot3_tn if precise else _dot_tn
    halo = DN_CONV - 1
    base = 8
    cw = cw_ref[...]

    def conv(window):
        y = window(0) * cw[0:1]
        for i in range(1, DN_CONV):
            y = y + window(i) * cw[i:i + 1]
        return y

    if carry:
        s_scr = refs[4]
        c = pl.program_id(1)
        last = pl.num_programs(1) - 1

        @pl.when(c == 0)
        def _():
            xc_scr[base - halo:base, :] = buf_ref[0]
            s_scr[...] = s0_ref[0]

        xc_scr[base:base + T, :] = qkv_ref[...].astype(F32)
        y = conv(lambda i: xc_scr[base - halo + i:base - halo + i + T, :])
        tail = xc_scr[base + T - halo:base + T, :]
        xc_scr[base - halo:base, :] = tail

        @pl.when(c == last)
        def _():
            bufnew_ref[0] = tail
    else:
        ys = []
        x_new = qkv_ref[...].astype(F32)
        for g in range(G):
            xc_scr[g, base - halo:base, :] = buf_ref[g]
            xc_scr[g, base:base + C, :] = x_new[g * C:(g + 1) * C, :]
            ys.append(conv(lambda i: xc_scr[g, base - halo + i:base - halo + i + C, :]))
            bufnew_ref[g] = xc_scr[g, base + C - halo:base + C, :]
        y = jnp.concatenate(ys, axis=0) if G > 1 else ys[0]
    qkv = y * _sigmoid(y)

    ab = ab_ref[...]
    g_all = -jnp.exp(alog_ref[...]) * jax.nn.softplus(ab + dtb_ref[...])
    beta_all = _sigmoid(ab)
    shift = C.bit_length() - 1
    rt = lax.broadcasted_iota(jnp.int32, (T, T), 0)
    ct = lax.broadcasted_iota(jnp.int32, (T, T), 1)
    chunk_tril = ((rt >> shift) == (ct >> shift)) & (rt >= ct)
    gcum = _dot3(chunk_tril.astype(F32), g_all)
    gam_all = jnp.exp(gcum)

    row = lax.broadcasted_iota(jnp.int32, (C, C), 0)
    col = lax.broadcasted_iota(jnp.int32, (C, C), 1)
    incl, strict, eye = row >= col, row > col, row == col
    eyef = eye.astype(F32)
    units = [(g, h) for g in range(G) for h in range(H)]
    rows = lambda a, g: a[g * C:(g + 1) * C]

    qn, kn, vv = [], [], []
    for h in range(H):
        q = qkv[:, h * DK:(h + 1) * DK]
        k = qkv[:, QK_W + h * DK:QK_W + (h + 1) * DK]
        qn.append(q * lax.rsqrt(jnp.sum(q * q, axis=-1, keepdims=True) + EPS) * (DK ** -0.5))
        kn.append(k * lax.rsqrt(jnp.sum(k * k, axis=-1, keepdims=True) + EPS))
        vv.append(qkv[:, 2 * QK_W + h * DV:2 * QK_W + (h + 1) * DV])
    qb = [q.astype(BF16) for q in qn]
    kb = [k.astype(BF16) for k in kn]

    gc, bc, gl, a_low, m_intra = {}, {}, {}, {}, {}
    for u in units:
        g, h = u
        gc[u] = rows(gcum, g)[:, h:h + 1]
        bc[u] = rows(beta_all, g)[:, H + h:H + h + 1]
        gl[u] = gc[u][C - 1:C, :]
        gr = jnp.sum(jnp.where(eye, gc[u], 0.0), axis=0, keepdims=True)
        br = jnp.sum(jnp.where(eye, bc[u], 0.0), axis=0, keepdims=True)
        db = jnp.where(incl, jnp.exp(gc[u] - gr), 0.0) * br
        k = rows(kb[h], g)
        kq = _dot_nt(jnp.concatenate([k, rows(qb[h], g)], axis=0), k)
        a_low[u] = jnp.where(strict, kq[:C] * db, 0.0)
        m_intra[u] = kq[C:] * db

    t_inv = {u: eyef - a_low[u] for u in units}
    a_pow = {u: dotm(a_low[u], a_low[u]) for u in units}
    n = 2
    while n < C:
        t_inv = {u: t_inv[u] + dotm(t_inv[u], a_pow[u]) for u in units}
        n *= 2
        if n < C:
            a_pow = {u: dotm(a_pow[u], a_pow[u]) for u in units}

    u_base, wq, k_dec = {}, {}, {}
    for u in units:
        g, h = u
        gam = rows(gam_all, g)[:, h:h + 1]
        k = rows(kn[h], g)
        sol = dotm(t_inv[u], jnp.concatenate([rows(vv[h], g), gam * k], axis=1))
        u_base[u] = sol[:, :DV]
        wq[u] = jnp.concatenate([sol[:, DV:], gam * rows(qn[h], g)], axis=0)
        k_dec[u] = k * (bc[u] * jnp.exp(gl[u] - gc[u]))

    state = [s_scr[h] for h in range(H)] if carry else None
    outs = {}
    for g in range(G):
        s_in = state if carry else [s0_ref[g, h] for h in range(H)]
        wqs = [dotm(wq[(g, h)], s_in[h]) for h in range(H)]
        us = [u_base[(g, h)] - wqs[h][:C] for h in range(H)]
        for h in range(H):
            outs[(g, h)] = wqs[h][C:] + dotm(m_intra[(g, h)], us[h])
        s_out = [jnp.exp(gl[(g, h)]) * s_in[h] + dotm_tn(k_dec[(g, h)], us[h]) for h in range(H)]
        if carry:
            state = s_out
        else:
            for h in range(H):
                snew_ref[g, h] = s_out[h]

    for h in range(H):
        o = jnp.concatenate([outs[(g, h)] for g in range(G)], axis=0) if G > 1 else outs[(0, h)]
        zg = zg_ref[:, h * DV:(h + 1) * DV].astype(F32)
        o_ref[:, h * DV:(h + 1) * DV] = _rms(o, ong_ref[...]) * (zg * _sigmoid(zg))

    if carry:
        for h in range(H):
            s_scr[h] = state[h]

        @pl.when(c == last)
        def _():
            for h in range(H):
                snew_ref[0, h] = state[h]


def _delta_branch(z, ab, conv_buf, s0, conv_w, alog_p, dtb_p, onorm_g, *, tok0, L, G, precise):
    B = conv_buf.shape[0]
    C = math.gcd(L, DN_CHUNK)
    nc = L // C
    carry = nc > 1
    T = G * C
    blk0 = tok0 // T
    if carry:
        assert nc % G == 0
        grid = (B, nc // G)
        own_blk = lambda b, c: b * (nc // G) + c
        gs = 1
    else:
        assert B % G == 0
        grid = (B // G, 1)
        own_blk = lambda b, c: b
        gs = G
    tok = lambda b, c: (blk0 + own_blk(b, c), 0)
    seq3 = lambda b, c: (b, 0, 0)
    seq4 = lambda b, c: (b, 0, 0, 0)
    const = lambda b, c: (0, 0)
    if carry:
        scratch = [pltpu.VMEM((8 + T, QKV_W), F32), pltpu.VMEM((DN_HEADS, DN_DK, DN_DV), F32)]
    else:
        scratch = [pltpu.VMEM((G, 8 + C, QKV_W), F32)]
    return pl.pallas_call(
        functools.partial(_delta_kernel, C=C, G=G, carry=carry, precise=precise),
        grid=grid,
        in_specs=[
            pl.BlockSpec((T, QKV_W), tok),
            pl.BlockSpec((T, V_W), lambda b, c: (blk0 + own_blk(b, c), QKV_W // V_W)),
            pl.BlockSpec((T, LANES), tok),
            pl.BlockSpec((gs, DN_CONV - 1, QKV_W), seq3),
            pl.BlockSpec((gs, DN_HEADS, DN_DK, DN_DV), seq4),
            pl.BlockSpec((DN_CONV, QKV_W), const),
            pl.BlockSpec((1, LANES), const),
            pl.BlockSpec((1, LANES), const),
            pl.BlockSpec((1, DN_DV), const),
        ],
        out_specs=[
            pl.BlockSpec((T, V_W), lambda b, c: (own_blk(b, c), 0)),
            pl.BlockSpec((gs, DN_HEADS, DN_DK, DN_DV), seq4),
            pl.BlockSpec((gs, DN_CONV - 1, QKV_W), seq3),
        ],
        out_shape=[
            jax.ShapeDtypeStruct((B * L, V_W), F32),
            jax.ShapeDtypeStruct(s0.shape, F32),
            jax.ShapeDtypeStruct(conv_buf.shape, F32),
        ],
        scratch_shapes=scratch,
        compiler_params=_params("parallel", "arbitrary"),
        name=f"delta_rule_c{C}",
    )(z, z, ab, conv_buf, s0, conv_w, alog_p, dtb_p, onorm_g)


def _post_mix_kernel(op_ref, os_ref, gu_ref, gv_ref, ma_ref, mb_ref, xp_ref, xs_ref, lng_ref, lnb_ref,
                     mix_ref, bias_ref, wa_ref, wb_ref, wo_ref, g2_ref, rwh_ref, rwl_ref, rb_ref,
                     x1_ref, h2_ref, gate_ref, idx_ref, v_ref, us_scr, *, n_p):
    u = _gelu(gu_ref[...].astype(F32))
    a = _gelu(gv_ref[...].astype(F32))
    ac = a - jnp.mean(a, axis=-1, keepdims=True)
    v = ac * lax.rsqrt(jnp.mean(ac * ac, axis=-1, keepdims=True) + EPS) * lng_ref[...] + lnb_ref[...]
    v_ref[...] = v
    vb = v.astype(BF16)
    gw = GM_WIDTH // GM_GROUPS
    for c in range(u.shape[0] // GM_CHUNK):
        rs = slice(c * GM_CHUNK, (c + 1) * GM_CHUNK)
        for g in range(GM_GROUPS):
            sl = slice(g * gw, (g + 1) * gw)
            s = jnp.dot(mix_ref[0, g], vb[rs, sl], preferred_element_type=F32) + bias_ref[0, :, sl]
            us_scr[rs, sl] = (u[rs, sl] * s).astype(BF16)

    ya = jnp.dot(_group_pick(op_ref, os_ref, n_p).astype(BF16), wa_ref[...], preferred_element_type=F32)
    yb = jnp.dot(us_scr[...], wb_ref[...], preferred_element_type=F32)
    mixed = _sigmoid(ma_ref[...].astype(F32)) * ya + _sigmoid(mb_ref[...].astype(F32)) * yb
    x1 = _group_pick(xp_ref, xs_ref, n_p) + jnp.dot(mixed.astype(BF16), wo_ref[...], preferred_element_type=F32)
    x1_ref[...] = x1
    h2 = _rms(x1, g2_ref[...])
    h2_ref[...] = h2
    hh, hl = _split(h2)
    d = functools.partial(jnp.dot, preferred_element_type=F32)
    logits = d(hh, rwh_ref[...]) + (d(hh, rwl_ref[...]) + d(hl, rwh_ref[...])) + rb_ref[...]
    lane = lax.broadcasted_iota(jnp.int32, logits.shape, 1)
    vals, idxs = [], []
    for _ in range(TOP_K):
        m = jnp.max(logits, axis=-1, keepdims=True)
        i = jnp.min(jnp.where(logits == m, lane, LANES), axis=-1, keepdims=True)
        vals.append(m)
        idxs.append(i)
        logits = jnp.where(lane == i, -jnp.inf, logits)
    es = [jnp.exp(v - vals[0]) for v in vals]
    tot = es[0]
    for e in es[1:]:
        tot = tot + e
    gates = jnp.zeros(logits.shape, F32)
    idx = jnp.zeros(logits.shape, jnp.int32)
    for k in range(TOP_K):
        gates = jnp.where(lane == k, es[k] / tot, gates)
        idx = jnp.where(lane == k, idxs[k], idx)
    gate_ref[...] = gates
    idx_ref[...] = idx


def _post_mix(o_p, o_s, z, xp, xs, ln_g, ln_b, mix, bias, wa, wb, wo, g2, rwh, rwl, rb):
    n_p, n_s = xp.shape[0], xs.shape[0]
    n = n_p + n_s
    tm = _row_tile(n_p, n_s)
    pt = n_p // tm
    grp = lambda i: jnp.where(i < pt, 0, 1)
    once = pl.Buffered(1)
    tok = pl.BlockSpec((tm, D_MODEL), lambda i: (i, 0))
    zcol = lambda c: pl.BlockSpec((tm, D_MODEL), lambda i: (i, c))
    full = lambda r, c: pl.BlockSpec((r, c), lambda i: (0, 0), pipeline_mode=once)
    narrow = pl.BlockSpec((tm, LANES), lambda i: (i, 0))
    op_spec, os_spec = _group_specs(tm, V_W, n_p)
    xp_spec, xs_spec = _group_specs(tm, D_MODEL, n_p)
    return pl.pallas_call(
        functools.partial(_post_mix_kernel, n_p=n_p),
        grid=(n // tm,),
        in_specs=[
            op_spec, os_spec, zcol(4), zcol(5), zcol(6), zcol(7), xp_spec, xs_spec,
            full(1, GM_WIDTH), full(1, GM_WIDTH),
            pl.BlockSpec((1, GM_GROUPS, GM_CHUNK, GM_CHUNK), lambda i: (grp(i), 0, 0, 0)),
            pl.BlockSpec((1, GM_CHUNK, GM_WIDTH), lambda i: (grp(i), 0, 0)),
            full(V_W, D_MODEL), full(GM_WIDTH, D_MODEL), full(D_MODEL, D_MODEL), full(1, D_MODEL),
            full(D_MODEL, LANES), full(D_MODEL, LANES), full(1, LANES),
        ],
        out_specs=[tok, tok, narrow, narrow,
                   pl.BlockSpec((tm, GM_WIDTH), lambda i: (jnp.maximum(i - pt, 0), 0))],
        out_shape=[
            jax.ShapeDtypeStruct((n, D_MODEL), F32),
            jax.ShapeDtypeStruct((n, D_MODEL), F32),
            jax.ShapeDtypeStruct((n, LANES), F32),
            jax.ShapeDtypeStruct((n, LANES), jnp.int32),
            jax.ShapeDtypeStruct((n_s, GM_WIDTH), F32),
        ],
        scratch_shapes=[pltpu.VMEM((tm, GM_WIDTH), BF16)],
        compiler_params=_params("arbitrary"),
        name="chunk_mlp_post_mix_router",
    )(o_p, o_s, z, z, z, z, xp, xs, ln_g, ln_b, mix, bias, wa, wb, wo, g2, rwh, rwl, rb)


def _moe_kernel(be_ref, first_ref, valid_ref, x_ref, w1_ref, b1_ref, w2_ref, b2_ref, y_ref,
                w1_scr, w2_scr):
    b = pl.program_id(0)

    @pl.when(first_ref[b] == 1)
    def _():
        w1_scr[...] = w1_ref[0].astype(BF16)
        w2_scr[...] = w2_ref[0].astype(BF16)

    @pl.when(valid_ref[b] == 1)
    def _():
        hid = jnp.dot(x_ref[...].astype(BF16), w1_scr[...], preferred_element_type=F32) + b1_ref[0]
        gate = jnp.minimum(hid[:, :D_EXPERT], SWIGLU_LIMIT)
        up = jnp.clip(hid[:, D_EXPERT:], -SWIGLU_LIMIT, SWIGLU_LIMIT)
        act = gate * _sigmoid(SWIGLU_ALPHA * gate) * (up + 1.0)
        y_ref[...] = jnp.dot(act.astype(BF16), w2_scr[...], preferred_element_type=F32) + b2_ref[0]

    @pl.when(valid_ref[b] == 0)
    def _():
        y_ref[...] = jnp.zeros(y_ref.shape, F32)


def _moe_experts(block_e, first, valid, xb, w1, b1, w2, b2):
    rows = xb.shape[0]
    nb = rows // MOE_ROWS
    return pl.pallas_call(
        _moe_kernel,
        grid_spec=pltpu.PrefetchScalarGridSpec(
            num_scalar_prefetch=3,
            grid=(nb,),
            in_specs=[
                pl.BlockSpec((MOE_ROWS, D_MODEL), lambda b, be, fi, va: (b, 0)),
                pl.BlockSpec((1, D_MODEL, 2 * D_EXPERT), lambda b, be, fi, va: (be[b], 0, 0)),
                pl.BlockSpec((1, 1, 2 * D_EXPERT), lambda b, be, fi, va: (be[b], 0, 0)),
                pl.BlockSpec((1, D_EXPERT, D_MODEL), lambda b, be, fi, va: (be[b], 0, 0)),
                pl.BlockSpec((1, 1, D_MODEL), lambda b, be, fi, va: (be[b], 0, 0)),
            ],
            out_specs=pl.BlockSpec((MOE_ROWS, D_MODEL), lambda b, be, fi, va: (b, 0)),
            scratch_shapes=[
                pltpu.VMEM((D_MODEL, 2 * D_EXPERT), BF16),
                pltpu.VMEM((D_EXPERT, D_MODEL), BF16),
            ],
        ),
        out_shape=jax.ShapeDtypeStruct((rows, D_MODEL), F32),
        compiler_params=_params("arbitrary"),
        name="moe_experts",
    )(block_e, first, valid, xb, w1, b1, w2, b2)


def _moe_dispatch(idx, n):
    experts = jnp.arange(N_EXPERTS, dtype=jnp.int32)
    onehot = jnp.sum((idx[:, :, None] == experts).astype(jnp.int32), axis=1)
    cum = jnp.cumsum(onehot, axis=0)
    counts = cum[-1]
    before = cum - onehot
    padded = (counts + MOE_ROWS - 1) // MOE_ROWS * MOE_ROWS
    pad_end = jnp.cumsum(padded)
    pad_start = pad_end - padded
    rank = jnp.take_along_axis(before, idx, axis=1)
    dest = pad_start[idx] + rank
    nb = -(-n * TOP_K // MOE_ROWS) + N_EXPERTS
    rows = nb * MOE_ROWS
    flat_tok = jnp.arange(n * TOP_K, dtype=jnp.int32) // TOP_K
    row_tok = jnp.zeros((rows,), jnp.int32).at[dest.reshape(-1)].set(
        flat_tok, unique_indices=True, mode='promise_in_bounds')
    starts = jnp.arange(nb, dtype=jnp.int32) * MOE_ROWS
    valid = (starts < pad_end[-1]).astype(jnp.int32)
    owner = lambda r: jnp.minimum(jnp.sum((pad_end[None, :] <= r[:, None]).astype(jnp.int32), axis=1),
                                  N_EXPERTS - 1)
    last_e = owner(pad_end[-1:] - 1)[0]
    block_e = jnp.where(valid == 1, owner(starts), last_e).astype(jnp.int32)
    first = jnp.concatenate([jnp.ones((1,), jnp.int32),
                             (block_e[1:] != block_e[:-1]).astype(jnp.int32)])
    return dest, row_tok, block_e, first, valid


def _tail_kernel(x1_ref, yk_ref, gate_ref, pp_ref, ps_ref, g3_ref, wg_ref, wp_ref, gf_ref,
                 yp_ref, ys_ref, *, n_p):
    gates = gate_ref[...]
    moe = yk_ref[0] * gates[:, 0:1]
    for k in range(1, TOP_K):
        moe = moe + yk_ref[k] * gates[:, k:k + 1]
    x2 = x1_ref[...] + moe
    h3 = _rms(x2, g3_ref[...])
    gate = _sigmoid(jnp.dot(h3.astype(BF16), wg_ref[...], preferred_element_type=F32))
    pe = jnp.dot(_group_pick(pp_ref, ps_ref, n_p).astype(BF16), wp_ref[...], preferred_element_type=F32)
    y = _rms(x2 + gate * pe, gf_ref[...])
    in_prompt = pl.program_id(0) < n_p // yp_ref.shape[0]

    @pl.when(in_prompt)
    def _():
        yp_ref[...] = y

    @pl.when(jnp.logical_not(in_prompt))
    def _():
        ys_ref[...] = y


def _tail(x1, yk, gates, pp, ps, g3, wg, wp, gf):
    n_p, n_s = pp.shape[0], ps.shape[0]
    n = n_p + n_s
    tm = _row_tile(n_p, n_s)
    tok = pl.BlockSpec((tm, D_MODEL), lambda i: (i, 0))
    full = lambda r, c: pl.BlockSpec((r, c), lambda i: (0, 0))
    return pl.pallas_call(
        functools.partial(_tail_kernel, n_p=n_p),
        grid=(n // tm,),
        in_specs=[tok, pl.BlockSpec((TOP_K, tm, D_MODEL), lambda i: (0, i, 0)),
                  pl.BlockSpec((tm, LANES), lambda i: (i, 0)),
                  *_group_specs(tm, PLE_DIM, n_p),
                  full(1, D_MODEL), full(D_MODEL, D_MODEL), full(PLE_DIM, D_MODEL), full(1, D_MODEL)],
        out_specs=_group_specs(tm, D_MODEL, n_p),
        out_shape=[jax.ShapeDtypeStruct((n_p, D_MODEL), F32), jax.ShapeDtypeStruct((n_s, D_MODEL), F32)],
        compiler_params=_params("arbitrary"),
        name="ple_final_norm",
    )(x1, yk, gates, pp, ps, g3, wg, wp, gf)


def _lane_pad(v, offset, fill=0.0):
    out = jnp.full((1, LANES), fill, F32)
    return out.at[0, offset:offset + v.shape[0]].set(v.astype(F32))


def kernel(x_prompt, x_sample, state_delta, state_conv, p_prompt, p_sample, norm1_g, w_in, conv_w, a_log, dt_bias, dn_norm_g, w_proj_a, gm_ln_g, gm_ln_b, gm_ws, gm_bs, w_proj_b, w_out, norm2_g, router_w, router_b, moe_w1, moe_b1, moe_w2, moe_b2, norm3_g, ple_w, ple_gate_w, final_norm_g):
    bp, lp, d = x_prompt.shape
    bs, ls, _ = x_sample.shape
    depth = w_in.shape[0]
    assert depth == 1 and d == D_MODEL
    assert lp % GM_CHUNK == 0 and GM_CHUNK % ls == 0 and ls >= DN_CONV - 1
    n_p, n_s = bp * lp, bs * ls
    n = n_p + n_s
    i = 0

    xp, xs = x_prompt.reshape(n_p, d), x_sample.reshape(n_s, d)

    ab0 = QKV_W
    w = w_in[i]
    w_main = jnp.concatenate([w[:, :ab0], w[:, ab0 + 2 * DN_HEADS:]], axis=1).astype(BF16)
    w_ab = jnp.pad(w[:, ab0:ab0 + 2 * DN_HEADS], ((0, 0), (0, LANES - 2 * DN_HEADS))).astype(BF16)
    row2 = lambda v: v.reshape(1, -1).astype(F32)

    z, ab = _in_proj(xp, xs, row2(norm1_g[i]), w_main, w_ab)

    alog_p = _lane_pad(a_log[i], 0)
    dtb_p = _lane_pad(dt_bias[i], 0)
    cw = conv_w[i].astype(F32)
    ong = row2(dn_norm_g[i])
    zero_s = jnp.zeros((bp, DN_HEADS, DN_DK, DN_DV), F32)
    zero_buf = jnp.zeros((bp, DN_CONV - 1, QKV_W), F32)
    o_p, sd_p, sc_p = _delta_branch(z, ab, zero_buf, zero_s, cw, alog_p, dtb_p, ong,
                                    tok0=0, L=lp, G=2, precise=False)
    o_s, sd_s, sc_s = _delta_branch(z, ab, state_conv[i], state_delta[i], cw, alog_p, dtb_p, ong,
                                    tok0=n_p, L=ls, G=2, precise=False)

    t = GM_CHUNK
    tri = jnp.tril(jnp.ones((t, t), bool))
    ws = gm_ws[i]
    mix_p = jnp.where(tri, ws, 0.0)
    small = jnp.where(tri[:ls, :ls], ws[:, :ls, :ls], 0.0)
    mix_s = jnp.einsum('ab,gts->gatbs', jnp.eye(t // ls, dtype=F32), small).reshape(GM_GROUPS, t, t)
    mix = jnp.stack([mix_p, mix_s]).astype(BF16)
    gw = GM_WIDTH // GM_GROUPS
    bias_p = jnp.repeat(gm_bs[i].T, gw, axis=1)
    bias_s = jnp.tile(bias_p[:ls], (t // ls, 1))
    bias = jnp.stack([bias_p, bias_s]).astype(F32)
    rw = jnp.pad(router_w[i].astype(F32), ((0, 0), (0, LANES - N_EXPERTS)))
    rwh = rw.astype(BF16)
    rwl = (rw - rwh.astype(F32)).astype(BF16)
    rb = _lane_pad(router_b[i], 0, fill=-jnp.inf)
    x1, h2, gates, idx, v_s = _post_mix(o_p, o_s, z, xp, xs, row2(gm_ln_g[i]), row2(gm_ln_b[i]), mix, bias,
                                        w_proj_a[i].astype(BF16), w_proj_b[i].astype(BF16),
                                        w_out[i].astype(BF16), row2(norm2_g[i]), rwh, rwl, rb)
    dest, row_tok, block_e, first, valid = _moe_dispatch(idx[:, :TOP_K], n)
    xb = h2.at[row_tok].get(mode='promise_in_bounds')
    yb = _moe_experts(block_e, first, valid, xb, moe_w1[i], moe_b1[i][:, None, :],
                      moe_w2[i], moe_b2[i][:, None, :])
    yk = yb.at[dest.T.reshape(-1)].get(mode='promise_in_bounds').reshape(TOP_K, n, d)

    y_p, y_s = _tail(x1, yk, gates, p_prompt[i].reshape(n_p, PLE_DIM), p_sample[i].reshape(n_s, PLE_DIM),
                     row2(norm3_g[i]), ple_gate_w[i].astype(BF16), ple_w[i].astype(BF16), row2(final_norm_g))

    return (y_p.reshape(bp, lp, d), y_s.reshape(bs, ls, d),
            sd_p[None], sc_p[None], sd_s[None], sc_s[None], v_s.reshape(1, bs, ls, GM_WIDTH))
```

```python
import functools
import math

import jax
import jax.numpy as jnp
from jax import lax
from jax.experimental import pallas as pl
from jax.experimental.pallas import tpu as pltpu
from jax.experimental.pallas import tpu_sc as plsc

F32 = jnp.float32
BF16 = jnp.bfloat16

D_MODEL = 1024
DN_HEADS = 8
DN_DK = 128
DN_DV = 128
DN_CONV = 4
DN_CHUNK = 64
GM_WIDTH = 1024
GM_GROUPS = 8
GM_CHUNK = 128
N_EXPERTS = 32
TOP_K = 4
D_EXPERT = 1024
SWIGLU_LIMIT = 7.0
SWIGLU_ALPHA = 1.702
PLE_DIM = 256
EPS = 1e-6
QK_W = DN_HEADS * DN_DK
V_W = DN_HEADS * DN_DV
QKV_W = 2 * QK_W + V_W

LANES = 128
MOE_ROWS = 512
VMEM_LIMIT = 56 << 20
SC_CORES, SC_SUBCORES = 2, 16
SC_WINDOW = 128
SC_CHUNK = 32


def _params(*sem):
    return pltpu.CompilerParams(dimension_semantics=sem, vmem_limit_bytes=VMEM_LIMIT)


def _dot(a, b):
    return jnp.dot(a.astype(BF16), b.astype(BF16), preferred_element_type=F32)


def _split(a):
    hi = a.astype(BF16)
    return hi, (a - hi.astype(F32)).astype(BF16)


def _dot3(a, b):
    ah, al = _split(a)
    bh, bl = _split(b)
    d = functools.partial(jnp.dot, preferred_element_type=F32)
    return d(ah, bh) + (d(ah, bl) + d(al, bh))


def _dot_nt(a, b):
    return lax.dot_general(a.astype(BF16), b.astype(BF16), (((1,), (1,)), ((), ())),
                           preferred_element_type=F32)


def _dot_tn(a, b):
    return lax.dot_general(a.astype(BF16), b.astype(BF16), (((0,), (0,)), ((), ())),
                           preferred_element_type=F32)


def _dot3_tn(a, b):
    ah, al = _split(a)
    bh, bl = _split(b)
    d = functools.partial(lax.dot_general, dimension_numbers=(((0,), (0,)), ((), ())),
                          preferred_element_type=F32)
    return d(ah, bh) + (d(ah, bl) + d(al, bh))


def _sigmoid(x):
    return 0.5 * jnp.tanh(0.5 * x) + 0.5


def _rms(x, g):
    return x * lax.rsqrt(jnp.mean(x * x, axis=-1, keepdims=True) + EPS) * g


def _gelu(x):
    return 0.5 * x * (1.0 + lax.erf(x * (1.0 / math.sqrt(2.0))))


def _row_tile(n_p, n_s, cap=512):
    for t in (1024, 512, 256, 128):
        if t > cap:
            continue
        if n_p % t == 0 and n_s % t == 0:
            return t
    raise ValueError(f"token counts {n_p}, {n_s} must be multiples of 128")


def _group_specs(tm, width, n_p):
    pt = n_p // tm
    return [pl.BlockSpec((tm, width), lambda i, *_: (jnp.minimum(i, pt - 1), 0)),
            pl.BlockSpec((tm, width), lambda i, *_: (jnp.maximum(i - pt, 0), 0))]


def _group_pick(prompt_ref, sample_ref, n_p):
    pt = n_p // prompt_ref.shape[0]
    return jnp.where(pl.program_id(0) < pt, prompt_ref[...], sample_ref[...])


def _in_proj_kernel(xp_ref, xs_ref, g_ref, w_ref, wab_ref, z_ref, ab_ref, h_scr, *, n_p):
    @pl.when(pl.program_id(1) == 0)
    def _():
        hb = _rms(_group_pick(xp_ref, xs_ref, n_p), g_ref[...]).astype(BF16)
        h_scr[...] = hb
        ab_ref[...] = jnp.dot(hb, wab_ref[...], preferred_element_type=F32)

    z_ref[...] = jnp.dot(h_scr[...], w_ref[...], preferred_element_type=F32).astype(z_ref.dtype)


def _in_proj(xp, xs, g, w_main, w_ab):
    n_p, n_s = xp.shape[0], xs.shape[0]
    n = n_p + n_s
    tm, tn = _row_tile(n_p, n_s, cap=1024), 1024
    cols = w_main.shape[1]
    return pl.pallas_call(
        functools.partial(_in_proj_kernel, n_p=n_p),
        grid=(n // tm, cols // tn),
        in_specs=_group_specs(tm, D_MODEL, n_p) + [
            pl.BlockSpec((1, D_MODEL), lambda i, j: (0, 0)),
            pl.BlockSpec((D_MODEL, tn), lambda i, j: (0, j)),
            pl.BlockSpec((D_MODEL, LANES), lambda i, j: (0, 0)),
        ],
        out_specs=[
            pl.BlockSpec((tm, tn), lambda i, j: (i, j)),
            pl.BlockSpec((tm, LANES), lambda i, j: (i, 0)),
        ],
        out_shape=[jax.ShapeDtypeStruct((n, cols), BF16), jax.ShapeDtypeStruct((n, LANES), F32)],
        scratch_shapes=[pltpu.VMEM((tm, D_MODEL), BF16)],
        compiler_params=_params("parallel", "arbitrary"),
        name="in_proj",
    )(xp, xs, g, w_main, w_ab)


def _delta_kernel(*refs, C, G, carry, precise):
    (qkv_ref, zg_ref, ab_ref, buf_ref, s0_ref, cw_ref, alog_ref, dtb_ref, ong_ref) = refs[:9]
    refs = refs[9:]
    o_ref, snew_ref, bufnew_ref, xc_scr = refs[:4]
    H, DK, DV = DN_HEADS, DN_DK, DN_DV
    T = G * C
    dotm = _dot3 if precise else _dot
    dotm_tn = _dot3_tn if precise else _dot_tn
    halo = DN_CONV - 1
    base = 8
    cw = cw_ref[...]

    def conv(window):
        y = window(0) * cw[0:1]
        for i in range(1, DN_CONV):
            y = y + window(i) * cw[i:i + 1]
        return y

    if carry:
        s_scr = refs[4]
        c = pl.program_id(1)
        last = pl.num_programs(1) - 1

        @pl.when(c == 0)
        def _():
            xc_scr[base - halo:base, :] = buf_ref[0]
            s_scr[...] = s0_ref[0]

        xc_scr[base:base + T, :] = qkv_ref[...].astype(F32)
        y = conv(lambda i: xc_scr[base - halo + i:base - halo + i + T, :])
        tail = xc_scr[base + T - halo:base + T, :]
        xc_scr[base - halo:base, :] = tail

        @pl.when(c == last)
        def _():
            bufnew_ref[0] = tail
    else:
        ys = []
        x_new = qkv_ref[...].astype(F32)
        for g in range(G):
            xc_scr[g, base - halo:base, :] = buf_ref[g]
            xc_scr[g, base:base + C, :] = x_new[g * C:(g + 1) * C, :]
            ys.append(conv(lambda i: xc_scr[g, base - halo + i:base - halo + i + C, :]))
            bufnew_ref[g] = xc_scr[g, base + C - halo:base + C, :]
        y = jnp.concatenate(ys, axis=0) if G > 1 else ys[0]
    qkv = y * _sigmoid(y)

    ab = ab_ref[...]
    g_all = -jnp.exp(alog_ref[...]) * jax.nn.softplus(ab + dtb_ref[...])
    beta_all = _sigmoid(ab)
    shift = C.bit_length() - 1
    rt = lax.broadcasted_iota(jnp.int32, (T, T), 0)
    ct = lax.broadcasted_iota(jnp.int32, (T, T), 1)
    chunk_tril = ((rt >> shift) == (ct >> shift)) & (rt >= ct)
    gcum = _dot3(chunk_tril.astype(F32), g_all)
    gam_all = jnp.exp(gcum)

    row = lax.broadcasted_iota(jnp.int32, (C, C), 0)
    col = lax.broadcasted_iota(jnp.int32, (C, C), 1)
    incl, strict, eye = row >= col, row > col, row == col
    eyef = eye.astype(F32)
    units = [(g, h) for g in range(G) for h in range(H)]
    rows = lambda a, g: a[g * C:(g + 1) * C]

    qn, kn, vv = [], [], []
    for h in range(H):
        q = qkv[:, h * DK:(h + 1) * DK]
        k = qkv[:, QK_W + h * DK:QK_W + (h + 1) * DK]
        qn.append(q * lax.rsqrt(jnp.sum(q * q, axis=-1, keepdims=True) + EPS) * (DK ** -0.5))
        kn.append(k * lax.rsqrt(jnp.sum(k * k, axis=-1, keepdims=True) + EPS))
        vv.append(qkv[:, 2 * QK_W + h * DV:2 * QK_W + (h + 1) * DV])
    qb = [q.astype(BF16) for q in qn]
    kb = [k.astype(BF16) for k in kn]

    gc, bc, gl, a_low, m_intra = {}, {}, {}, {}, {}
    for u in units:
        g, h = u
        gc[u] = rows(gcum, g)[:, h:h + 1]
        bc[u] = rows(beta_all, g)[:, H + h:H + h + 1]
        gl[u] = gc[u][C - 1:C, :]
        gr = jnp.sum(jnp.where(eye, gc[u], 0.0), axis=0, keepdims=True)
        br = jnp.sum(jnp.where(eye, bc[u], 0.0), axis=0, keepdims=True)
        db = jnp.where(incl, jnp.exp(gc[u] - gr), 0.0) * br
        k = rows(kb[h], g)
        kq = _dot_nt(jnp.concatenate([k, rows(qb[h], g)], axis=0), k)
        a_low[u] = jnp.where(strict, kq[:C] * db, 0.0)
        m_intra[u] = kq[C:] * db

    t_inv = {u: eyef - a_low[u] for u in units}
    a_pow = {u: dotm(a_low[u], a_low[u]) for u in units}
    n = 2
    while n < C:
        t_inv = {u: t_inv[u] + dotm(t_inv[u], a_pow[u]) for u in units}
        n *= 2
        if n < C:
            a_pow = {u: dotm(a_pow[u], a_pow[u]) for u in units}

    u_base, wq, k_dec = {}, {}, {}
    for u in units:
        g, h = u
        gam = rows(gam_all, g)[:, h:h + 1]
        k = rows(kn[h], g)
        sol = dotm(t_inv[u], jnp.concatenate([rows(vv[h], g), gam * k], axis=1))
        u_base[u] = sol[:, :DV]
        wq[u] = jnp.concatenate([sol[:, DV:], gam * rows(qn[h], g)], axis=0)
        k_dec[u] = k * (bc[u] * jnp.exp(gl[u] - gc[u]))

    state = [s_scr[h] for h in range(H)] if carry else None
    outs = {}
    for g in range(G):
        s_in = state if carry else [s0_ref[g, h] for h in range(H)]
        wqs = [dotm(wq[(g, h)], s_in[h]) for h in range(H)]
        us = [u_base[(g, h)] - wqs[h][:C] for h in range(H)]
        for h in range(H):
            outs[(g, h)] = wqs[h][C:] + dotm(m_intra[(g, h)], us[h])
        s_out = [jnp.exp(gl[(g, h)]) * s_in[h] + dotm_tn(k_dec[(g, h)], us[h]) for h in range(H)]
        if carry:
            state = s_out
        else:
            for h in range(H):
                snew_ref[g, h] = s_out[h]

    for h in range(H):
        o = jnp.concatenate([outs[(g, h)] for g in range(G)], axis=0) if G > 1 else outs[(0, h)]
        zg = zg_ref[:, h * DV:(h + 1) * DV].astype(F32)
        o_ref[:, h * DV:(h + 1) * DV] = _rms(o, ong_ref[...]) * (zg * _sigmoid(zg))

    if carry:
        for h in range(H):
            s_scr[h] = state[h]

        @pl.when(c == last)
        def _():
            for h in range(H):
                snew_ref[0, h] = state[h]


def _delta_branch(z, ab, conv_buf, s0, conv_w, alog_p, dtb_p, onorm_g, *, tok0, L, G, precise):
    B = conv_buf.shape[0]
    C = math.gcd(L, DN_CHUNK)
    nc = L // C
    carry = nc > 1
    T = G * C
    blk0 = tok0 // T
    if carry:
        assert nc % G == 0
        grid = (B, nc // G)
        own_blk = lambda b, c: b * (nc // G) + c
        gs = 1
    else:
        assert B % G == 0
        grid = (B // G, 1)
        own_blk = lambda b, c: b
        gs = G
    tok = lambda b, c: (blk0 + own_blk(b, c), 0)
    seq3 = lambda b, c: (b, 0, 0)
    seq4 = lambda b, c: (b, 0, 0, 0)
    const = lambda b, c: (0, 0)
    if carry:
        scratch = [pltpu.VMEM((8 + T, QKV_W), F32), pltpu.VMEM((DN_HEADS, DN_DK, DN_DV), F32)]
    else:
        scratch = [pltpu.VMEM((G, 8 + C, QKV_W), F32)]
    return pl.pallas_call(
        functools.partial(_delta_kernel, C=C, G=G, carry=carry, precise=precise),
        grid=grid,
        in_specs=[
            pl.BlockSpec((T, QKV_W), tok),
            pl.BlockSpec((T, V_W), lambda b, c: (blk0 + own_blk(b, c), QKV_W // V_W)),
            pl.BlockSpec((T, LANES), tok),
            pl.BlockSpec((gs, DN_CONV - 1, QKV_W), seq3),
            pl.BlockSpec((gs, DN_HEADS, DN_DK, DN_DV), seq4),
            pl.BlockSpec((DN_CONV, QKV_W), const),
            pl.BlockSpec((1, LANES), const),
            pl.BlockSpec((1, LANES), const),
            pl.BlockSpec((1, DN_DV), const),
        ],
        out_specs=[
            pl.BlockSpec((T, V_W), lambda b, c: (own_blk(b, c), 0)),
            pl.BlockSpec((gs, DN_HEADS, DN_DK, DN_DV), seq4),
            pl.BlockSpec((gs, DN_CONV - 1, QKV_W), seq3),
        ],
        out_shape=[
            jax.ShapeDtypeStruct((B * L, V_W), F32),
            jax.ShapeDtypeStruct(s0.shape, F32),
            jax.ShapeDtypeStruct(conv_buf.shape, F32),
        ],
        scratch_shapes=scratch,
        compiler_params=_params("parallel", "arbitrary"),
        name=f"delta_rule_c{C}",
    )(z, z, ab, conv_buf, s0, conv_w, alog_p, dtb_p, onorm_g)


def _post_mix_kernel(op_ref, os_ref, gu_ref, gv_ref, ma_ref, mb_ref, xp_ref, xs_ref, lng_ref, lnb_ref,
                     mix_ref, bias_ref, wa_ref, wb_ref, wo_ref, g2_ref, rwh_ref, rwl_ref, rb_ref,
                     x1_ref, h2_ref, gate_ref, idx_ref, v_ref, us_scr, *, n_p):
    u = _gelu(gu_ref[...].astype(F32))
    a = _gelu(gv_ref[...].astype(F32))
    ac = a - jnp.mean(a, axis=-1, keepdims=True)
    v = ac * lax.rsqrt(jnp.mean(ac * ac, axis=-1, keepdims=True) + EPS) * lng_ref[...] + lnb_ref[...]
    v_ref[...] = v
    vb = v.astype(BF16)
    gw = GM_WIDTH // GM_GROUPS
    for c in range(u.shape[0] // GM_CHUNK):
        rs = slice(c * GM_CHUNK, (c + 1) * GM_CHUNK)
        for g in range(GM_GROUPS):
            sl = slice(g * gw, (g + 1) * gw)
            s = jnp.dot(mix_ref[0, g], vb[rs, sl], preferred_element_type=F32) + bias_ref[0, :, sl]
            us_scr[rs, sl] = (u[rs, sl] * s).astype(BF16)

    ya = jnp.dot(_group_pick(op_ref, os_ref, n_p).astype(BF16), wa_ref[...], preferred_element_type=F32)
    yb = jnp.dot(us_scr[...], wb_ref[...], preferred_element_type=F32)
    mixed = _sigmoid(ma_ref[...].astype(F32)) * ya + _sigmoid(mb_ref[...].astype(F32)) * yb
    x1 = _group_pick(xp_ref, xs_ref, n_p) + jnp.dot(mixed.astype(BF16), wo_ref[...], preferred_element_type=F32)
    x1_ref[...] = x1
    h2 = _rms(x1, g2_ref[...])
    h2_ref[...] = h2
    hh, hl = _split(h2)
    d = functools.partial(jnp.dot, preferred_element_type=F32)
    logits = d(hh, rwh_ref[...]) + (d(hh, rwl_ref[...]) + d(hl, rwh_ref[...])) + rb_ref[...]
    lane = lax.broadcasted_iota(jnp.int32, logits.shape, 1)
    vals, idxs = [], []
    for _ in range(TOP_K):
        m = jnp.max(logits, axis=-1, keepdims=True)
        i = jnp.min(jnp.where(logits == m, lane, LANES), axis=-1, keepdims=True)
        vals.append(m)
        idxs.append(i)
        logits = jnp.where(lane == i, -jnp.inf, logits)
    es = [jnp.exp(v - vals[0]) for v in vals]
    tot = es[0]
    for e in es[1:]:
        tot = tot + e
    gates = jnp.zeros(logits.shape, F32)
    idx = jnp.zeros(logits.shape, jnp.int32)
    for k in range(TOP_K):
        gates = jnp.where(lane == k, es[k] / tot, gates)
        idx = jnp.where(lane == k, idxs[k], idx)
    gate_ref[...] = gates
    idx_ref[...] = idx


def _post_mix(o_p, o_s, z, xp, xs, ln_g, ln_b, mix, bias, wa, wb, wo, g2, rwh, rwl, rb):
    n_p, n_s = xp.shape[0], xs.shape[0]
    n = n_p + n_s
    tm = _row_tile(n_p, n_s)
    pt = n_p // tm
    grp = lambda i: jnp.where(i < pt, 0, 1)
    once = pl.Buffered(1)
    tok = pl.BlockSpec((tm, D_MODEL), lambda i: (i, 0))
    zcol = lambda c: pl.BlockSpec((tm, D_MODEL), lambda i: (i, c))
    full = lambda r, c: pl.BlockSpec((r, c), lambda i: (0, 0), pipeline_mode=once)
    narrow = pl.BlockSpec((tm, LANES), lambda i: (i, 0))
    op_spec, os_spec = _group_specs(tm, V_W, n_p)
    xp_spec, xs_spec = _group_specs(tm, D_MODEL, n_p)
    return pl.pallas_call(
        functools.partial(_post_mix_kernel, n_p=n_p),
        grid=(n // tm,),
        in_specs=[
            op_spec, os_spec, zcol(4), zcol(5), zcol(6), zcol(7), xp_spec, xs_spec,
            full(1, GM_WIDTH), full(1, GM_WIDTH),
            pl.BlockSpec((1, GM_GROUPS, GM_CHUNK, GM_CHUNK), lambda i: (grp(i), 0, 0, 0)),
            pl.BlockSpec((1, GM_CHUNK, GM_WIDTH), lambda i: (grp(i), 0, 0)),
            full(V_W, D_MODEL), full(GM_WIDTH, D_MODEL), full(D_MODEL, D_MODEL), full(1, D_MODEL),
            full(D_MODEL, LANES), full(D_MODEL, LANES), full(1, LANES),
        ],
        out_specs=[tok, tok, narrow, narrow,
                   pl.BlockSpec((tm, GM_WIDTH), lambda i: (jnp.maximum(i - pt, 0), 0))],
        out_shape=[
            jax.ShapeDtypeStruct((n, D_MODEL), F32),
            jax.ShapeDtypeStruct((n, D_MODEL), F32),
            jax.ShapeDtypeStruct((n, LANES), F32),
            jax.ShapeDtypeStruct((n, LANES), jnp.int32),
            jax.ShapeDtypeStruct((n_s, GM_WIDTH), F32),
        ],
        scratch_shapes=[pltpu.VMEM((tm, GM_WIDTH), BF16)],
        compiler_params=_params("arbitrary"),
        name="chunk_mlp_post_mix_router",
    )(o_p, o_s, z, z, z, z, xp, xs, ln_g, ln_b, mix, bias, wa, wb, wo, g2, rwh, rwl, rb)


def _moe_kernel(be_ref, first_ref, valid_ref, x_ref, w1_ref, b1_ref, w2_ref, b2_ref, y_ref,
                w1_scr, w2_scr):
    b = pl.program_id(0)

    @pl.when(first_ref[b] == 1)
    def _():
        w1_scr[...] = w1_ref[0].astype(BF16)
        w2_scr[...] = w2_ref[0].astype(BF16)

    @pl.when(valid_ref[b] == 1)
    def _():
        hid = jnp.dot(x_ref[...].astype(BF16), w1_scr[...], preferred_element_type=F32) + b1_ref[0]
        gate = jnp.minimum(hid[:, :D_EXPERT], SWIGLU_LIMIT)
        up = jnp.clip(hid[:, D_EXPERT:], -SWIGLU_LIMIT, SWIGLU_LIMIT)
        act = gate * _sigmoid(SWIGLU_ALPHA * gate) * (up + 1.0)
        y_ref[...] = jnp.dot(act.astype(BF16), w2_scr[...], preferred_element_type=F32) + b2_ref[0]

    @pl.when(valid_ref[b] == 0)
    def _():
        y_ref[...] = jnp.zeros(y_ref.shape, F32)


def _moe_experts(block_e, first, valid, xb, w1, b1, w2, b2):
    rows = xb.shape[0]
    nb = rows // MOE_ROWS
    return pl.pallas_call(
        _moe_kernel,
        grid_spec=pltpu.PrefetchScalarGridSpec(
            num_scalar_prefetch=3,
            grid=(nb,),
            in_specs=[
                pl.BlockSpec((MOE_ROWS, D_MODEL), lambda b, be, fi, va: (b, 0)),
                pl.BlockSpec((1, D_MODEL, 2 * D_EXPERT), lambda b, be, fi, va: (be[b], 0, 0)),
                pl.BlockSpec((1, 1, 2 * D_EXPERT), lambda b, be, fi, va: (be[b], 0, 0)),
                pl.BlockSpec((1, D_EXPERT, D_MODEL), lambda b, be, fi, va: (be[b], 0, 0)),
                pl.BlockSpec((1, 1, D_MODEL), lambda b, be, fi, va: (be[b], 0, 0)),
            ],
            out_specs=pl.BlockSpec((MOE_ROWS, D_MODEL), lambda b, be, fi, va: (b, 0)),
            scratch_shapes=[
                pltpu.VMEM((D_MODEL, 2 * D_EXPERT), BF16),
                pltpu.VMEM((D_EXPERT, D_MODEL), BF16),
            ],
        ),
        out_shape=jax.ShapeDtypeStruct((rows, D_MODEL), F32),
        compiler_params=_params("arbitrary"),
        name="moe_experts",
    )(block_e, first, valid, xb, w1, b1, w2, b2)


def _moe_dispatch(idx, n):
    experts = jnp.arange(N_EXPERTS, dtype=jnp.int32)
    onehot = jnp.sum((idx[:, :, None] == experts).astype(jnp.int32), axis=1)
    cum = jnp.cumsum(onehot, axis=0)
    counts = cum[-1]
    before = cum - onehot
    padded = (counts + MOE_ROWS - 1) // MOE_ROWS * MOE_ROWS
    pad_end = jnp.cumsum(padded)
    pad_start = pad_end - padded
    rank = jnp.take_along_axis(before, idx, axis=1)
    dest = pad_start[idx] + rank
    nb = -(-n * TOP_K // MOE_ROWS) + N_EXPERTS
    rows = nb * MOE_ROWS
    starts = jnp.arange(nb, dtype=jnp.int32) * MOE_ROWS
    valid = (starts < pad_end[-1]).astype(jnp.int32)
    owner = lambda r: jnp.minimum(jnp.sum((pad_end[None, :] <= r[:, None]).astype(jnp.int32), axis=1),
                                  N_EXPERTS - 1)
    last_e = owner(pad_end[-1:] - 1)[0]
    block_e = jnp.where(valid == 1, owner(starts), last_e).astype(jnp.int32)
    first = jnp.concatenate([jnp.ones((1,), jnp.int32),
                             (block_e[1:] != block_e[:-1]).astype(jnp.int32)])
    return dest.T, rows, block_e, first, valid


def _sc_copy_rows(src, src_idx, dst_idx, out_rows):
    m = src_idx.shape[1]
    d = src.shape[1]
    mesh = plsc.VectorSubcoreMesh(core_axis_name="core", subcore_axis_name="subcore",
                                  num_cores=SC_CORES, num_subcores=SC_SUBCORES)

    @pl.kernel(out_type=jax.ShapeDtypeStruct((out_rows, d), src.dtype), mesh=mesh,
               scratch_types=[pltpu.VMEM((SC_CHUNK, d), src.dtype)])
    def copy_rows(src_hbm, si_hbm, di_hbm, out_hbm, buf):
        def body(si_vmem, di_vmem):
            for c in range(SC_WINDOW // SC_CHUNK):
                sl = pl.ds(c * SC_CHUNK, SC_CHUNK)
                pltpu.sync_copy(src_hbm.at[si_vmem.at[0, sl]], buf)
                pltpu.sync_copy(buf, out_hbm.at[di_vmem.at[0, sl]])

        pltpu.emit_pipeline(
            body,
            grid=(m // SC_WINDOW,),
            in_specs=[pl.BlockSpec((1, SC_WINDOW), lambda i: (0, i)),
                      pl.BlockSpec((1, SC_WINDOW), lambda i: (0, i))],
            out_specs=[],
            core_axis_name=("core", "subcore"),
            dimension_semantics=(pltpu.PARALLEL,),
        )(si_hbm, di_hbm)

    return copy_rows(src, src_idx, dst_idx)


def _tail_kernel(x1_ref, yk_ref, gate_ref, pp_ref, ps_ref, g3_ref, wg_ref, wp_ref, gf_ref,
                 yp_ref, ys_ref, *, n_p):
    gates = gate_ref[...]
    moe = yk_ref[0] * gates[:, 0:1]
    for k in range(1, TOP_K):
        moe = moe + yk_ref[k] * gates[:, k:k + 1]
    x2 = x1_ref[...] + moe
    h3 = _rms(x2, g3_ref[...])
    gate = _sigmoid(jnp.dot(h3.astype(BF16), wg_ref[...], preferred_element_type=F32))
    pe = jnp.dot(_group_pick(pp_ref, ps_ref, n_p).astype(BF16), wp_ref[...], preferred_element_type=F32)
    y = _rms(x2 + gate * pe, gf_ref[...])
    in_prompt = pl.program_id(0) < n_p // yp_ref.shape[0]

    @pl.when(in_prompt)
    def _():
        yp_ref[...] = y

    @pl.when(jnp.logical_not(in_prompt))
    def _():
        ys_ref[...] = y


def _tail(x1, yk, gates, pp, ps, g3, wg, wp, gf):
    n_p, n_s = pp.shape[0], ps.shape[0]
    n = n_p + n_s
    tm = _row_tile(n_p, n_s)
    tok = pl.BlockSpec((tm, D_MODEL), lambda i: (i, 0))
    full = lambda r, c: pl.BlockSpec((r, c), lambda i: (0, 0))
    return pl.pallas_call(
        functools.partial(_tail_kernel, n_p=n_p),
        grid=(n // tm,),
        in_specs=[tok, pl.BlockSpec((TOP_K, tm, D_MODEL), lambda i: (0, i, 0)),
                  pl.BlockSpec((tm, LANES), lambda i: (i, 0)),
                  *_group_specs(tm, PLE_DIM, n_p),
                  full(1, D_MODEL), full(D_MODEL, D_MODEL), full(PLE_DIM, D_MODEL), full(1, D_MODEL)],
        out_specs=_group_specs(tm, D_MODEL, n_p),
        out_shape=[jax.ShapeDtypeStruct((n_p, D_MODEL), F32), jax.ShapeDtypeStruct((n_s, D_MODEL), F32)],
        compiler_params=_params("arbitrary"),
        name="ple_final_norm",
    )(x1, yk, gates, pp, ps, g3, wg, wp, gf)


def _lane_pad(v, offset, fill=0.0):
    out = jnp.full((1, LANES), fill, F32)
    return out.at[0, offset:offset + v.shape[0]].set(v.astype(F32))


def kernel(x_prompt, x_sample, state_delta, state_conv, p_prompt, p_sample, norm1_g, w_in, conv_w, a_log, dt_bias, dn_norm_g, w_proj_a, gm_ln_g, gm_ln_b, gm_ws, gm_bs, w_proj_b, w_out, norm2_g, router_w, router_b, moe_w1, moe_b1, moe_w2, moe_b2, norm3_g, ple_w, ple_gate_w, final_norm_g):
    bp, lp, d = x_prompt.shape
    bs, ls, _ = x_sample.shape
    depth = w_in.shape[0]
    assert depth == 1 and d == D_MODEL
    assert lp % GM_CHUNK == 0 and GM_CHUNK % ls == 0 and ls >= DN_CONV - 1
    n_p, n_s = bp * lp, bs * ls
    n = n_p + n_s
    i = 0

    xp, xs = x_prompt.reshape(n_p, d), x_sample.reshape(n_s, d)

    ab0 = QKV_W
    w = w_in[i]
    w_main = jnp.concatenate([w[:, :ab0], w[:, ab0 + 2 * DN_HEADS:]], axis=1).astype(BF16)
    w_ab = jnp.pad(w[:, ab0:ab0 + 2 * DN_HEADS], ((0, 0), (0, LANES - 2 * DN_HEADS))).astype(BF16)
    row2 = lambda v: v.reshape(1, -1).astype(F32)

    z, ab = _in_proj(xp, xs, row2(norm1_g[i]), w_main, w_ab)

    alog_p = _lane_pad(a_log[i], 0)
    dtb_p = _lane_pad(dt_bias[i], 0)
    cw = conv_w[i].astype(F32)
    ong = row2(dn_norm_g[i])
    zero_s = jnp.zeros((bp, DN_HEADS, DN_DK, DN_DV), F32)
    zero_buf = jnp.zeros((bp, DN_CONV - 1, QKV_W), F32)
    o_p, sd_p, sc_p = _delta_branch(z, ab, zero_buf, zero_s, cw, alog_p, dtb_p, ong,
                                    tok0=0, L=lp, G=2, precise=False)
    o_s, sd_s, sc_s = _delta_branch(z, ab, state_conv[i], state_delta[i], cw, alog_p, dtb_p, ong,
                                    tok0=n_p, L=ls, G=2, precise=False)

    t = GM_CHUNK
    tri = jnp.tril(jnp.ones((t, t), bool))
    ws = gm_ws[i]
    mix_p = jnp.where(tri, ws, 0.0)
    small = jnp.where(tri[:ls, :ls], ws[:, :ls, :ls], 0.0)
    mix_s = jnp.einsum('ab,gts->gatbs', jnp.eye(t // ls, dtype=F32), small).reshape(GM_GROUPS, t, t)
    mix = jnp.stack([mix_p, mix_s]).astype(BF16)
    gw = GM_WIDTH // GM_GROUPS
    bias_p = jnp.repeat(gm_bs[i].T, gw, axis=1)
    bias_s = jnp.tile(bias_p[:ls], (t // ls, 1))
    bias = jnp.stack([bias_p, bias_s]).astype(F32)
    rw = jnp.pad(router_w[i].astype(F32), ((0, 0), (0, LANES - N_EXPERTS)))
    rwh = rw.astype(BF16)
    rwl = (rw - rwh.astype(F32)).astype(BF16)
    rb = _lane_pad(router_b[i], 0, fill=-jnp.inf)
    x1, h2, gates, idx, v_s = _post_mix(o_p, o_s, z, xp, xs, row2(gm_ln_g[i]), row2(gm_ln_b[i]), mix, bias,
                                        w_proj_a[i].astype(BF16), w_proj_b[i].astype(BF16),
                                        w_out[i].astype(BF16), row2(norm2_g[i]), rwh, rwl, rb)
    dest_t, rows, block_e, first, valid = _moe_dispatch(idx[:, :TOP_K], n)
    flat_dest = dest_t.reshape(1, TOP_K * n)
    tok_ids = jnp.tile(jnp.arange(n, dtype=jnp.int32), TOP_K).reshape(1, TOP_K * n)
    xb = _sc_copy_rows(h2, tok_ids, flat_dest, rows)
    yb = _moe_experts(block_e, first, valid, xb, moe_w1[i], moe_b1[i][:, None, :],
                      moe_w2[i], moe_b2[i][:, None, :])
    slots = jnp.arange(TOP_K * n, dtype=jnp.int32).reshape(1, TOP_K * n)
    yk = _sc_copy_rows(yb, flat_dest, slots, TOP_K * n).reshape(TOP_K, n, d)

    y_p, y_s = _tail(x1, yk, gates, p_prompt[i].reshape(n_p, PLE_DIM), p_sample[i].reshape(n_s, PLE_DIM),
                     row2(norm3_g[i]), ple_gate_w[i].astype(BF16), ple_w[i].astype(BF16), row2(final_norm_g))

    return (y_p.reshape(bp, lp, d), y_s.reshape(bs, ls, d),
            sd_p[None], sc_p[None], sd_s[None], sc_s[None], v_s.reshape(1, bs, ls, GM_WIDTH))
```

```python
import functools
import math

import jax
import jax.numpy as jnp
from jax import lax
from jax.experimental import pallas as pl
from jax.experimental.pallas import tpu as pltpu
from jax.experimental.pallas import tpu_sc as plsc

F32 = jnp.float32
BF16 = jnp.bfloat16

D_MODEL = 1024
DN_HEADS = 8
DN_DK = 128
DN_DV = 128
DN_CONV = 4
DN_CHUNK = 64
GM_WIDTH = 1024
GM_GROUPS = 8
GM_CHUNK = 128
N_EXPERTS = 32
TOP_K = 4
D_EXPERT = 1024
SWIGLU_LIMIT = 7.0
SWIGLU_ALPHA = 1.702
PLE_DIM = 256
EPS = 1e-6
QK_W = DN_HEADS * DN_DK
V_W = DN_HEADS * DN_DV
QKV_W = 2 * QK_W + V_W

LANES = 128
MOE_ROWS = 512
VMEM_LIMIT = 56 << 20
SC_CORES, SC_SUBCORES = 2, 16
SC_WINDOW = 128
SC_CHUNK = 64


def _params(*sem):
    return pltpu.CompilerParams(dimension_semantics=sem, vmem_limit_bytes=VMEM_LIMIT)


def _dot(a, b):
    return jnp.dot(a.astype(BF16), b.astype(BF16), preferred_element_type=F32)


def _split(a):
    hi = a.astype(BF16)
    return hi, (a - hi.astype(F32)).astype(BF16)


def _dot3(a, b):
    ah, al = _split(a)
    bh, bl = _split(b)
    d = functools.partial(jnp.dot, preferred_element_type=F32)
    return d(ah, bh) + (d(ah, bl) + d(al, bh))


def _dot_nt(a, b):
    return lax.dot_general(a.astype(BF16), b.astype(BF16), (((1,), (1,)), ((), ())),
                           preferred_element_type=F32)


def _dot_tn(a, b):
    return lax.dot_general(a.astype(BF16), b.astype(BF16), (((0,), (0,)), ((), ())),
                           preferred_element_type=F32)


def _dot3_tn(a, b):
    ah, al = _split(a)
    bh, bl = _split(b)
    d = functools.partial(lax.dot_general, dimension_numbers=(((0,), (0,)), ((), ())),
                          preferred_element_type=F32)
    return d(ah, bh) + (d(ah, bl) + d(al, bh))


def _sigmoid(x):
    return 0.5 * jnp.tanh(0.5 * x) + 0.5


def _pack_bf16_pairs(x):
    bits = lax.bitcast_convert_type(x.astype(BF16).astype(F32), jnp.uint32)
    half = x.shape[1] // 2
    return lax.bitcast_convert_type((bits[:, :half] >> 16) | bits[:, half:], jnp.int32)


def _unpack_bf16_pairs(w):
    bits = lax.bitcast_convert_type(w, jnp.uint32)
    return (lax.bitcast_convert_type(bits << 16, F32),
            lax.bitcast_convert_type(bits & jnp.uint32(0xFFFF0000), F32))


def _rms(x, g):
    return x * lax.rsqrt(jnp.mean(x * x, axis=-1, keepdims=True) + EPS) * g


def _gelu(x):
    return 0.5 * x * (1.0 + lax.erf(x * (1.0 / math.sqrt(2.0))))


def _row_tile(n_p, n_s, cap=512):
    for t in (1024, 512, 256, 128):
        if t > cap:
            continue
        if n_p % t == 0 and n_s % t == 0:
            return t
    raise ValueError(f"token counts {n_p}, {n_s} must be multiples of 128")


def _group_specs(tm, width, n_p):
    pt = n_p // tm
    return [pl.BlockSpec((tm, width), lambda i, *_: (jnp.minimum(i, pt - 1), 0)),
            pl.BlockSpec((tm, width), lambda i, *_: (jnp.maximum(i - pt, 0), 0))]


def _group_pick(prompt_ref, sample_ref, n_p):
    pt = n_p // prompt_ref.shape[0]
    return jnp.where(pl.program_id(0) < pt, prompt_ref[...], sample_ref[...])


def _in_proj_kernel(xp_ref, xs_ref, g_ref, w_ref, wab_ref, z_ref, ab_ref, h_scr, *, n_p):
    @pl.when(pl.program_id(1) == 0)
    def _():
        hb = _rms(_group_pick(xp_ref, xs_ref, n_p), g_ref[...]).astype(BF16)
        h_scr[...] = hb
        ab_ref[...] = jnp.dot(hb, wab_ref[...], preferred_element_type=F32)

    z_ref[...] = jnp.dot(h_scr[...], w_ref[...], preferred_element_type=F32).astype(z_ref.dtype)


def _in_proj(xp, xs, g, w_main, w_ab):
    n_p, n_s = xp.shape[0], xs.shape[0]
    n = n_p + n_s
    tm, tn = _row_tile(n_p, n_s, cap=1024), 2048
    cols = w_main.shape[1]
    return pl.pallas_call(
        functools.partial(_in_proj_kernel, n_p=n_p),
        grid=(n // tm, cols // tn),
        in_specs=_group_specs(tm, D_MODEL, n_p) + [
            pl.BlockSpec((1, D_MODEL), lambda i, j: (0, 0)),
            pl.BlockSpec((D_MODEL, tn), lambda i, j: (0, j)),
            pl.BlockSpec((D_MODEL, LANES), lambda i, j: (0, 0)),
        ],
        out_specs=[
            pl.BlockSpec((tm, tn), lambda i, j: (i, j)),
            pl.BlockSpec((tm, LANES), lambda i, j: (i, 0)),
        ],
        out_shape=[jax.ShapeDtypeStruct((n, cols), BF16), jax.ShapeDtypeStruct((n, LANES), F32)],
        scratch_shapes=[pltpu.VMEM((tm, D_MODEL), BF16)],
        compiler_params=_params("parallel", "arbitrary"),
        name="in_proj",
    )(xp, xs, g, w_main, w_ab)


def _delta_kernel(*refs, C, G, carry, precise):
    (qkv_ref, zg_ref, ab_ref, buf_ref, s0_ref, cw_ref, alog_ref, dtb_ref, ong_ref) = refs[:9]
    refs = refs[9:]
    o_ref, snew_ref, bufnew_ref, xc_scr = refs[:4]
    H, DK, DV = DN_HEADS, DN_DK, DN_DV
    T = G * C
    dotm = _dot3 if precise else _dot
    dotm_tn = _dot3_tn if precise else _dot_tn
    halo = DN_CONV - 1
    base = 8
    cw = cw_ref[...]

    def conv(window):
        y = window(0) * cw[0:1]
        for i in range(1, DN_CONV):
            y = y + window(i) * cw[i:i + 1]
        return y

    if carry:
        s_scr = refs[4]
        c = pl.program_id(1)
        last = pl.num_programs(1) - 1

        @pl.when(c == 0)
        def _():
            xc_scr[base - halo:base, :] = buf_ref[0]
            s_scr[...] = s0_ref[0]

        xc_scr[base:base + T, :] = qkv_ref[...].astype(F32)
        y = conv(lambda i: xc_scr[base - halo + i:base - halo + i + T, :])
        tail = xc_scr[base + T - halo:base + T, :]
        xc_scr[base - halo:base, :] = tail

        @pl.when(c == last)
        def _():
            bufnew_ref[0] = tail
    else:
        ys = []
        x_new = qkv_ref[...].astype(F32)
        for g in range(G):
            xc_scr[g, base - halo:base, :] = buf_ref[g]
            xc_scr[g, base:base + C, :] = x_new[g * C:(g + 1) * C, :]
            ys.append(conv(lambda i: xc_scr[g, base - halo + i:base - halo + i + C, :]))
            bufnew_ref[g] = xc_scr[g, base + C - halo:base + C, :]
        y = jnp.concatenate(ys, axis=0) if G > 1 else ys[0]
    qkv = y * _sigmoid(y)

    ab = ab_ref[...]
    g_all = -jnp.exp(alog_ref[...]) * jax.nn.softplus(ab + dtb_ref[...])
    beta_all = _sigmoid(ab)
    shift = C.bit_length() - 1
    rt = lax.broadcasted_iota(jnp.int32, (T, T), 0)
    ct = lax.broadcasted_iota(jnp.int32, (T, T), 1)
    chunk_tril = ((rt >> shift) == (ct >> shift)) & (rt >= ct)
    gcum = _dot3(chunk_tril.astype(F32), g_all)
    gam_all = jnp.exp(gcum)

    row = lax.broadcasted_iota(jnp.int32, (C, C), 0)
    col = lax.broadcasted_iota(jnp.int32, (C, C), 1)
    incl, strict, eye = row >= col, row > col, row == col
    eyef = eye.astype(F32)
    units = [(g, h) for g in range(G) for h in range(H)]
    rows = lambda a, g: a[g * C:(g + 1) * C]

    qn, kn, vv = [], [], []
    for h in range(H):
        q = qkv[:, h * DK:(h + 1) * DK]
        k = qkv[:, QK_W + h * DK:QK_W + (h + 1) * DK]
        qn.append(q * lax.rsqrt(jnp.sum(q * q, axis=-1, keepdims=True) + EPS) * (DK ** -0.5))
        kn.append(k * lax.rsqrt(jnp.sum(k * k, axis=-1, keepdims=True) + EPS))
        vv.append(qkv[:, 2 * QK_W + h * DV:2 * QK_W + (h + 1) * DV])
    qb = [q.astype(BF16) for q in qn]
    kb = [k.astype(BF16) for k in kn]

    gc, bc, gl, a_low, m_intra = {}, {}, {}, {}, {}
    for u in units:
        g, h = u
        gc[u] = rows(gcum, g)[:, h:h + 1]
        bc[u] = rows(beta_all, g)[:, H + h:H + h + 1]
        gl[u] = gc[u][C - 1:C, :]
        gr = jnp.sum(jnp.where(eye, gc[u], 0.0), axis=0, keepdims=True)
        br = jnp.sum(jnp.where(eye, bc[u], 0.0), axis=0, keepdims=True)
        db = jnp.where(incl, jnp.exp(gc[u] - gr), 0.0) * br
        k = rows(kb[h], g)
        kq = _dot_nt(jnp.concatenate([k, rows(qb[h], g)], axis=0), k)
        a_low[u] = jnp.where(strict, kq[:C] * db, 0.0)
        m_intra[u] = kq[C:] * db

    t_inv = {u: eyef - a_low[u] for u in units}
    a_pow = {u: dotm(a_low[u], a_low[u]) for u in units}
    n = 2
    while n < C:
        t_inv = {u: t_inv[u] + dotm(t_inv[u], a_pow[u]) for u in units}
        n *= 2
        if n < C:
            a_pow = {u: dotm(a_pow[u], a_pow[u]) for u in units}

    u_base, wq, k_dec = {}, {}, {}
    for u in units:
        g, h = u
        gam = rows(gam_all, g)[:, h:h + 1]
        k = rows(kn[h], g)
        sol = dotm(t_inv[u], jnp.concatenate([rows(vv[h], g), gam * k], axis=1))
        u_base[u] = sol[:, :DV]
        wq[u] = jnp.concatenate([sol[:, DV:], gam * rows(qn[h], g)], axis=0)
        k_dec[u] = k * (bc[u] * jnp.exp(gl[u] - gc[u]))

    state = [s_scr[h] for h in range(H)] if carry else None
    outs = {}
    for g in range(G):
        s_in = state if carry else [s0_ref[g, h] for h in range(H)]
        wqs = [dotm(wq[(g, h)], s_in[h]) for h in range(H)]
        us = [u_base[(g, h)] - wqs[h][:C] for h in range(H)]
        for h in range(H):
            outs[(g, h)] = wqs[h][C:] + dotm(m_intra[(g, h)], us[h])
        s_out = [jnp.exp(gl[(g, h)]) * s_in[h] + dotm_tn(k_dec[(g, h)], us[h]) for h in range(H)]
        if carry:
            state = s_out
        else:
            for h in range(H):
                snew_ref[g, h] = s_out[h]

    for h in range(H):
        o = jnp.concatenate([outs[(g, h)] for g in range(G)], axis=0) if G > 1 else outs[(0, h)]
        zg = zg_ref[:, h * DV:(h + 1) * DV].astype(F32)
        o_ref[:, h * DV:(h + 1) * DV] = _rms(o, ong_ref[...]) * (zg * _sigmoid(zg))

    if carry:
        for h in range(H):
            s_scr[h] = state[h]

        @pl.when(c == last)
        def _():
            for h in range(H):
                snew_ref[0, h] = state[h]


def _delta_branch(z, ab, conv_buf, s0, conv_w, alog_p, dtb_p, onorm_g, *, tok0, L, G, precise):
    B = conv_buf.shape[0]
    C = math.gcd(L, DN_CHUNK)
    nc = L // C
    carry = nc > 1
    T = G * C
    blk0 = tok0 // T
    if carry:
        assert nc % G == 0
        grid = (B, nc // G)
        own_blk = lambda b, c: b * (nc // G) + c
        gs = 1
    else:
        assert B % G == 0
        grid = (B // G, 1)
        own_blk = lambda b, c: b
        gs = G
    tok = lambda b, c: (blk0 + own_blk(b, c), 0)
    seq3 = lambda b, c: (b, 0, 0)
    seq4 = lambda b, c: (b, 0, 0, 0)
    const = lambda b, c: (0, 0)
    if carry:
        scratch = [pltpu.VMEM((8 + T, QKV_W), F32), pltpu.VMEM((DN_HEADS, DN_DK, DN_DV), F32)]
    else:
        scratch = [pltpu.VMEM((G, 8 + C, QKV_W), F32)]
    return pl.pallas_call(
        functools.partial(_delta_kernel, C=C, G=G, carry=carry, precise=precise),
        grid=grid,
        in_specs=[
            pl.BlockSpec((T, QKV_W), tok),
            pl.BlockSpec((T, V_W), lambda b, c: (blk0 + own_blk(b, c), QKV_W // V_W)),
            pl.BlockSpec((T, LANES), tok),
            pl.BlockSpec((gs, DN_CONV - 1, QKV_W), seq3),
            pl.BlockSpec((gs, DN_HEADS, DN_DK, DN_DV), seq4),
            pl.BlockSpec((DN_CONV, QKV_W), const),
            pl.BlockSpec((1, LANES), const),
            pl.BlockSpec((1, LANES), const),
            pl.BlockSpec((1, DN_DV), const),
        ],
        out_specs=[
            pl.BlockSpec((T, V_W), lambda b, c: (own_blk(b, c), 0)),
            pl.BlockSpec((gs, DN_HEADS, DN_DK, DN_DV), seq4),
            pl.BlockSpec((gs, DN_CONV - 1, QKV_W), seq3),
        ],
        out_shape=[
            jax.ShapeDtypeStruct((B * L, V_W), F32),
            jax.ShapeDtypeStruct(s0.shape, F32),
            jax.ShapeDtypeStruct(conv_buf.shape, F32),
        ],
        scratch_shapes=scratch,
        compiler_params=_params("parallel", "arbitrary"),
        name=f"delta_rule_c{C}",
    )(z, z, ab, conv_buf, s0, conv_w, alog_p, dtb_p, onorm_g)


def _post_mix_kernel(op_ref, os_ref, gu_ref, gv_ref, ma_ref, mb_ref, xp_ref, xs_ref, lng_ref, lnb_ref,
                     mix_ref, bias_ref, wa_ref, wb_ref, wo_ref, g2_ref, rwh_ref, rwl_ref, rb_ref,
                     x1_ref, h2_ref, gate_ref, idx_ref, v_ref, us_scr, *, n_p):
    u = _gelu(gu_ref[...].astype(F32))
    a = _gelu(gv_ref[...].astype(F32))
    ac = a - jnp.mean(a, axis=-1, keepdims=True)
    v = ac * lax.rsqrt(jnp.mean(ac * ac, axis=-1, keepdims=True) + EPS) * lng_ref[...] + lnb_ref[...]
    v_ref[...] = v
    vb = v.astype(BF16)
    gw = GM_WIDTH // GM_GROUPS
    for c in range(u.shape[0] // GM_CHUNK):
        rs = slice(c * GM_CHUNK, (c + 1) * GM_CHUNK)
        for g in range(GM_GROUPS):
            sl = slice(g * gw, (g + 1) * gw)
            s = jnp.dot(mix_ref[0, g], vb[rs, sl], preferred_element_type=F32) + bias_ref[0, :, sl]
            us_scr[rs, sl] = (u[rs, sl] * s).astype(BF16)

    ya = jnp.dot(_group_pick(op_ref, os_ref, n_p).astype(BF16), wa_ref[...], preferred_element_type=F32)
    yb = jnp.dot(us_scr[...], wb_ref[...], preferred_element_type=F32)
    mixed = _sigmoid(ma_ref[...].astype(F32)) * ya + _sigmoid(mb_ref[...].astype(F32)) * yb
    x1 = _group_pick(xp_ref, xs_ref, n_p) + jnp.dot(mixed.astype(BF16), wo_ref[...], preferred_element_type=F32)
    x1_ref[...] = x1
    h2 = _rms(x1, g2_ref[...])
    h2_ref[...] = _pack_bf16_pairs(h2)
    hh, hl = _split(h2)
    d = functools.partial(jnp.dot, preferred_element_type=F32)
    logits = d(hh, rwh_ref[...]) + (d(hh, rwl_ref[...]) + d(hl, rwh_ref[...])) + rb_ref[...]
    lane = lax.broadcasted_iota(jnp.int32, logits.shape, 1)
    vals, idxs = [], []
    for _ in range(TOP_K):
        m = jnp.max(logits, axis=-1, keepdims=True)
        i = jnp.min(jnp.where(logits == m, lane, LANES), axis=-1, keepdims=True)
        vals.append(m)
        idxs.append(i)
        logits = jnp.where(lane == i, -jnp.inf, logits)
    es = [jnp.exp(v - vals[0]) for v in vals]
    tot = es[0]
    for e in es[1:]:
        tot = tot + e
    gates = jnp.zeros(logits.shape, F32)
    idx = jnp.zeros(logits.shape, jnp.int32)
    for k in range(TOP_K):
        gates = jnp.where(lane == k, es[k] / tot, gates)
        idx = jnp.where(lane == k, idxs[k], idx)
    gate_ref[...] = gates
    idx_ref[...] = idx


def _post_mix(o_p, o_s, z, xp, xs, ln_g, ln_b, mix, bias, wa, wb, wo, g2, rwh, rwl, rb):
    n_p, n_s = xp.shape[0], xs.shape[0]
    n = n_p + n_s
    tm = _row_tile(n_p, n_s)
    pt = n_p // tm
    grp = lambda i: jnp.where(i < pt, 0, 1)
    once = pl.Buffered(1)
    tok = pl.BlockSpec((tm, D_MODEL), lambda i: (i, 0))
    zcol = lambda c: pl.BlockSpec((tm, D_MODEL), lambda i: (i, c))
    full = lambda r, c: pl.BlockSpec((r, c), lambda i: (0, 0), pipeline_mode=once)
    narrow = pl.BlockSpec((tm, LANES), lambda i: (i, 0))
    op_spec, os_spec = _group_specs(tm, V_W, n_p)
    xp_spec, xs_spec = _group_specs(tm, D_MODEL, n_p)
    return pl.pallas_call(
        functools.partial(_post_mix_kernel, n_p=n_p),
        grid=(n // tm,),
        in_specs=[
            op_spec, os_spec, zcol(4), zcol(5), zcol(6), zcol(7), xp_spec, xs_spec,
            full(1, GM_WIDTH), full(1, GM_WIDTH),
            pl.BlockSpec((1, GM_GROUPS, GM_CHUNK, GM_CHUNK), lambda i: (grp(i), 0, 0, 0)),
            pl.BlockSpec((1, GM_CHUNK, GM_WIDTH), lambda i: (grp(i), 0, 0)),
            full(V_W, D_MODEL), full(GM_WIDTH, D_MODEL), full(D_MODEL, D_MODEL), full(1, D_MODEL),
            full(D_MODEL, LANES), full(D_MODEL, LANES), full(1, LANES),
        ],
        out_specs=[tok, pl.BlockSpec((tm, D_MODEL // 2), lambda i: (i, 0)), narrow, narrow,
                   pl.BlockSpec((tm, GM_WIDTH), lambda i: (jnp.maximum(i - pt, 0), 0))],
        out_shape=[
            jax.ShapeDtypeStruct((n, D_MODEL), F32),
            jax.ShapeDtypeStruct((n, D_MODEL // 2), jnp.int32),
            jax.ShapeDtypeStruct((n, LANES), F32),
            jax.ShapeDtypeStruct((n, LANES), jnp.int32),
            jax.ShapeDtypeStruct((n_s, GM_WIDTH), F32),
        ],
        scratch_shapes=[pltpu.VMEM((tm, GM_WIDTH), BF16)],
        compiler_params=_params("arbitrary"),
        name="chunk_mlp_post_mix_router",
    )(o_p, o_s, z, z, z, z, xp, xs, ln_g, ln_b, mix, bias, wa, wb, wo, g2, rwh, rwl, rb)


def _moe_kernel(be_ref, first_ref, valid_ref, x_ref, w1_ref, b1_ref, w2_ref, b2_ref, y_ref,
                w1_scr, w2_scr):
    b = pl.program_id(0)

    @pl.when(first_ref[b] == 1)
    def _():
        w1_scr[...] = w1_ref[0].astype(BF16)
        w2_scr[...] = w2_ref[0].astype(BF16)

    @pl.when(valid_ref[b] == 1)
    def _():
        x = jnp.concatenate(_unpack_bf16_pairs(x_ref[...]), axis=1).astype(BF16)
        hid = jnp.dot(x, w1_scr[...], preferred_element_type=F32) + b1_ref[0]
        gate = jnp.minimum(hid[:, :D_EXPERT], SWIGLU_LIMIT)
        up = jnp.clip(hid[:, D_EXPERT:], -SWIGLU_LIMIT, SWIGLU_LIMIT)
        act = gate * _sigmoid(SWIGLU_ALPHA * gate) * (up + 1.0)
        y = jnp.dot(act.astype(BF16), w2_scr[...], preferred_element_type=F32) + b2_ref[0]
        y_ref[...] = _pack_bf16_pairs(y)

    @pl.when(valid_ref[b] == 0)
    def _():
        y_ref[...] = jnp.zeros(y_ref.shape, jnp.int32)


def _moe_experts(block_e, first, valid, xb, w1, b1, w2, b2):
    rows = xb.shape[0]
    nb = rows // MOE_ROWS
    return pl.pallas_call(
        _moe_kernel,
        grid_spec=pltpu.PrefetchScalarGridSpec(
            num_scalar_prefetch=3,
            grid=(nb,),
            in_specs=[
                pl.BlockSpec((MOE_ROWS, D_MODEL // 2), lambda b, be, fi, va: (b, 0)),
                pl.BlockSpec((1, D_MODEL, 2 * D_EXPERT), lambda b, be, fi, va: (be[b], 0, 0)),
                pl.BlockSpec((1, 1, 2 * D_EXPERT), lambda b, be, fi, va: (be[b], 0, 0)),
                pl.BlockSpec((1, D_EXPERT, D_MODEL), lambda b, be, fi, va: (be[b], 0, 0)),
                pl.BlockSpec((1, 1, D_MODEL), lambda b, be, fi, va: (be[b], 0, 0)),
            ],
            out_specs=pl.BlockSpec((MOE_ROWS, D_MODEL // 2), lambda b, be, fi, va: (b, 0)),
            scratch_shapes=[
                pltpu.VMEM((D_MODEL, 2 * D_EXPERT), BF16),
                pltpu.VMEM((D_EXPERT, D_MODEL), BF16),
            ],
        ),
        out_shape=jax.ShapeDtypeStruct((rows, D_MODEL // 2), jnp.int32),
        compiler_params=_params("arbitrary"),
        name="moe_experts",
    )(block_e, first, valid, xb, w1, b1, w2, b2)


def _moe_dispatch(idx, n):
    experts = jnp.arange(N_EXPERTS, dtype=jnp.int32)
    onehot = jnp.sum((idx[:, :, None] == experts).astype(jnp.int32), axis=1)
    cum = jnp.cumsum(onehot, axis=0)
    counts = cum[-1]
    before = cum - onehot
    padded = (counts + MOE_ROWS - 1) // MOE_ROWS * MOE_ROWS
    pad_end = jnp.cumsum(padded)
    pad_start = pad_end - padded
    rank = jnp.take_along_axis(before, idx, axis=1)
    dest = pad_start[idx] + rank
    nb = -(-n * TOP_K // MOE_ROWS) + N_EXPERTS
    rows = nb * MOE_ROWS
    starts = jnp.arange(nb, dtype=jnp.int32) * MOE_ROWS
    valid = (starts < pad_end[-1]).astype(jnp.int32)
    owner = lambda r: jnp.minimum(jnp.sum((pad_end[None, :] <= r[:, None]).astype(jnp.int32), axis=1),
                                  N_EXPERTS - 1)
    last_e = owner(pad_end[-1:] - 1)[0]
    block_e = jnp.where(valid == 1, owner(starts), last_e).astype(jnp.int32)
    first = jnp.concatenate([jnp.ones((1,), jnp.int32),
                             (block_e[1:] != block_e[:-1]).astype(jnp.int32)])
    return dest.T, rows, block_e, first, valid


def _sc_copy_rows(src, src_idx, dst_idx, out_rows):
    m = src_idx.shape[1]
    d = src.shape[1]
    mesh = plsc.VectorSubcoreMesh(core_axis_name="core", subcore_axis_name="subcore",
                                  num_cores=SC_CORES, num_subcores=SC_SUBCORES)

    @pl.kernel(out_type=jax.ShapeDtypeStruct((out_rows, d), src.dtype), mesh=mesh,
               scratch_types=[pltpu.VMEM((SC_CHUNK, d), src.dtype)])
    def copy_rows(src_hbm, si_hbm, di_hbm, out_hbm, buf):
        def body(si_vmem, di_vmem):
            for c in range(SC_WINDOW // SC_CHUNK):
                sl = pl.ds(c * SC_CHUNK, SC_CHUNK)
                pltpu.sync_copy(src_hbm.at[si_vmem.at[0, sl]], buf)
                pltpu.sync_copy(buf, out_hbm.at[di_vmem.at[0, sl]])

        pltpu.emit_pipeline(
            body,
            grid=(m // SC_WINDOW,),
            in_specs=[pl.BlockSpec((1, SC_WINDOW), lambda i: (0, i)),
                      pl.BlockSpec((1, SC_WINDOW), lambda i: (0, i))],
            out_specs=[],
            core_axis_name=("core", "subcore"),
            dimension_semantics=(pltpu.PARALLEL,),
        )(si_hbm, di_hbm)

    return copy_rows(src, src_idx, dst_idx)


def _tail_kernel(x1_ref, yk_ref, gate_ref, pp_ref, ps_ref, g3_ref, wg_ref, wp_ref, gf_ref,
                 yp_ref, ys_ref, *, n_p):
    gates = gate_ref[...]
    lo, hi = _unpack_bf16_pairs(yk_ref[0])
    lo, hi = lo * gates[:, 0:1], hi * gates[:, 0:1]
    for k in range(1, TOP_K):
        lo_k, hi_k = _unpack_bf16_pairs(yk_ref[k])
        lo, hi = lo + lo_k * gates[:, k:k + 1], hi + hi_k * gates[:, k:k + 1]
    moe = jnp.concatenate([lo, hi], axis=1)
    x2 = x1_ref[...] + moe
    h3 = _rms(x2, g3_ref[...])
    gate = _sigmoid(jnp.dot(h3.astype(BF16), wg_ref[...], preferred_element_type=F32))
    pe = jnp.dot(_group_pick(pp_ref, ps_ref, n_p).astype(BF16), wp_ref[...], preferred_element_type=F32)
    y = _rms(x2 + gate * pe, gf_ref[...])
    in_prompt = pl.program_id(0) < n_p // yp_ref.shape[0]

    @pl.when(in_prompt)
    def _():
        yp_ref[...] = y

    @pl.when(jnp.logical_not(in_prompt))
    def _():
        ys_ref[...] = y


def _tail(x1, yk, gates, pp, ps, g3, wg, wp, gf):
    n_p, n_s = pp.shape[0], ps.shape[0]
    n = n_p + n_s
    tm = _row_tile(n_p, n_s)
    tok = pl.BlockSpec((tm, D_MODEL), lambda i: (i, 0))
    full = lambda r, c: pl.BlockSpec((r, c), lambda i: (0, 0))
    return pl.pallas_call(
        functools.partial(_tail_kernel, n_p=n_p),
        grid=(n // tm,),
        in_specs=[tok, pl.BlockSpec((TOP_K, tm, D_MODEL // 2), lambda i: (0, i, 0)),
                  pl.BlockSpec((tm, LANES), lambda i: (i, 0)),
                  *_group_specs(tm, PLE_DIM, n_p),
                  full(1, D_MODEL), full(D_MODEL, D_MODEL), full(PLE_DIM, D_MODEL), full(1, D_MODEL)],
        out_specs=_group_specs(tm, D_MODEL, n_p),
        out_shape=[jax.ShapeDtypeStruct((n_p, D_MODEL), F32), jax.ShapeDtypeStruct((n_s, D_MODEL), F32)],
        compiler_params=_params("arbitrary"),
        name="ple_final_norm",
    )(x1, yk, gates, pp, ps, g3, wg, wp, gf)


def _lane_pad(v, offset, fill=0.0):
    out = jnp.full((1, LANES), fill, F32)
    return out.at[0, offset:offset + v.shape[0]].set(v.astype(F32))


def kernel(x_prompt, x_sample, state_delta, state_conv, p_prompt, p_sample, norm1_g, w_in, conv_w, a_log, dt_bias, dn_norm_g, w_proj_a, gm_ln_g, gm_ln_b, gm_ws, gm_bs, w_proj_b, w_out, norm2_g, router_w, router_b, moe_w1, moe_b1, moe_w2, moe_b2, norm3_g, ple_w, ple_gate_w, final_norm_g):
    bp, lp, d = x_prompt.shape
    bs, ls, _ = x_sample.shape
    depth = w_in.shape[0]
    assert depth == 1 and d == D_MODEL
    assert lp % GM_CHUNK == 0 and GM_CHUNK % ls == 0 and ls >= DN_CONV - 1
    n_p, n_s = bp * lp, bs * ls
    n = n_p + n_s
    i = 0

    xp, xs = x_prompt.reshape(n_p, d), x_sample.reshape(n_s, d)

    ab0 = QKV_W
    w = w_in[i]
    w_main = jnp.concatenate([w[:, :ab0], w[:, ab0 + 2 * DN_HEADS:]], axis=1).astype(BF16)
    w_ab = jnp.pad(w[:, ab0:ab0 + 2 * DN_HEADS], ((0, 0), (0, LANES - 2 * DN_HEADS))).astype(BF16)
    row2 = lambda v: v.reshape(1, -1).astype(F32)

    z, ab = _in_proj(xp, xs, row2(norm1_g[i]), w_main, w_ab)

    alog_p = _lane_pad(a_log[i], 0)
    dtb_p = _lane_pad(dt_bias[i], 0)
    cw = conv_w[i].astype(F32)
    ong = row2(dn_norm_g[i])
    zero_s = jnp.zeros((bp, DN_HEADS, DN_DK, DN_DV), F32)
    zero_buf = jnp.zeros((bp, DN_CONV - 1, QKV_W), F32)
    o_p, sd_p, sc_p = _delta_branch(z, ab, zero_buf, zero_s, cw, alog_p, dtb_p, ong,
                                    tok0=0, L=lp, G=2, precise=False)
    o_s, sd_s, sc_s = _delta_branch(z, ab, state_conv[i], state_delta[i], cw, alog_p, dtb_p, ong,
                                    tok0=n_p, L=ls, G=2, precise=False)

    t = GM_CHUNK
    tri = jnp.tril(jnp.ones((t, t), bool))
    ws = gm_ws[i]
    mix_p = jnp.where(tri, ws, 0.0)
    small = jnp.where(tri[:ls, :ls], ws[:, :ls, :ls], 0.0)
    mix_s = jnp.einsum('ab,gts->gatbs', jnp.eye(t // ls, dtype=F32), small).reshape(GM_GROUPS, t, t)
    mix = jnp.stack([mix_p, mix_s]).astype(BF16)
    gw = GM_WIDTH // GM_GROUPS
    bias_p = jnp.repeat(gm_bs[i].T, gw, axis=1)
    bias_s = jnp.tile(bias_p[:ls], (t // ls, 1))
    bias = jnp.stack([bias_p, bias_s]).astype(F32)
    rw = jnp.pad(router_w[i].astype(F32), ((0, 0), (0, LANES - N_EXPERTS)))
    rwh = rw.astype(BF16)
    rwl = (rw - rwh.astype(F32)).astype(BF16)
    rb = _lane_pad(router_b[i], 0, fill=-jnp.inf)
    x1, h2, gates, idx, v_s = _post_mix(o_p, o_s, z, xp, xs, row2(gm_ln_g[i]), row2(gm_ln_b[i]), mix, bias,
                                        w_proj_a[i].astype(BF16), w_proj_b[i].astype(BF16),
                                        w_out[i].astype(BF16), row2(norm2_g[i]), rwh, rwl, rb)
    dest_t, rows, block_e, first, valid = _moe_dispatch(idx[:, :TOP_K], n)
    flat_dest = dest_t.reshape(1, TOP_K * n)
    tok_ids = jnp.tile(jnp.arange(n, dtype=jnp.int32), TOP_K).reshape(1, TOP_K * n)
    xb = _sc_copy_rows(h2, tok_ids, flat_dest, rows)
    yb = _moe_experts(block_e, first, valid, xb, moe_w1[i], moe_b1[i][:, None, :],
                      moe_w2[i], moe_b2[i][:, None, :])
    slots = jnp.arange(TOP_K * n, dtype=jnp.int32).reshape(1, TOP_K * n)
    yk = _sc_copy_rows(yb, flat_dest, slots, TOP_K * n).reshape(TOP_K, n, d // 2)

    y_p, y_s = _tail(x1, yk, gates, p_prompt[i].reshape(n_p, PLE_DIM), p_sample[i].reshape(n_s, PLE_DIM),
                     row2(norm3_g[i]), ple_gate_w[i].astype(BF16), ple_w[i].astype(BF16), row2(final_norm_g))

    return (y_p.reshape(bp, lp, d), y_s.reshape(bs, ls, d),
            sd_p[None], sc_p[None], sd_s[None], sc_s[None], v_s.reshape(1, bs, ls, GM_WIDTH))
```

```python
import functools
import math

import jax
import jax.numpy as jnp
from jax import lax
from jax.experimental import pallas as pl
from jax.experimental.pallas import tpu as pltpu
from jax.experimental.pallas import tpu_sc as plsc

F32 = jnp.float32
BF16 = jnp.bfloat16

D_MODEL = 1024
DN_HEADS = 8
DN_DK = 128
DN_DV = 128
DN_CONV = 4
DN_CHUNK = 64
GM_WIDTH = 1024
GM_GROUPS = 8
GM_CHUNK = 128
N_EXPERTS = 32
TOP_K = 4
D_EXPERT = 1024
SWIGLU_LIMIT = 7.0
SWIGLU_ALPHA = 1.702
PLE_DIM = 256
EPS = 1e-6
QK_W = DN_HEADS * DN_DK
V_W = DN_HEADS * DN_DV
QKV_W = 2 * QK_W + V_W

LANES = 128
MOE_ROWS = 512
VMEM_LIMIT = 56 << 20
SC_CORES, SC_SUBCORES = 2, 16
SC_WINDOW = 128
SC_CHUNK = 64


def _params(*sem):
    return pltpu.CompilerParams(dimension_semantics=sem, vmem_limit_bytes=VMEM_LIMIT)


def _dot(a, b):
    return jnp.dot(a.astype(BF16), b.astype(BF16), preferred_element_type=F32)


def _split(a):
    hi = a.astype(BF16)
    return hi, (a - hi.astype(F32)).astype(BF16)


def _dot3(a, b):
    ah, al = _split(a)
    bh, bl = _split(b)
    d = functools.partial(jnp.dot, preferred_element_type=F32)
    return d(ah, bh) + (d(ah, bl) + d(al, bh))


def _dot_nt(a, b):
    return lax.dot_general(a.astype(BF16), b.astype(BF16), (((1,), (1,)), ((), ())),
                           preferred_element_type=F32)


def _dot_tn(a, b):
    return lax.dot_general(a.astype(BF16), b.astype(BF16), (((0,), (0,)), ((), ())),
                           preferred_element_type=F32)


def _dot3_tn(a, b):
    ah, al = _split(a)
    bh, bl = _split(b)
    d = functools.partial(lax.dot_general, dimension_numbers=(((0,), (0,)), ((), ())),
                          preferred_element_type=F32)
    return d(ah, bh) + (d(ah, bl) + d(al, bh))


def _sigmoid(x):
    return 0.5 * jnp.tanh(0.5 * x) + 0.5


def _pack_bf16_pairs(x):
    bits = lax.bitcast_convert_type(x.astype(BF16).astype(F32), jnp.uint32)
    half = x.shape[1] // 2
    return lax.bitcast_convert_type((bits[:, :half] >> 16) | bits[:, half:], jnp.int32)


def _unpack_bf16_pairs(w):
    bits = lax.bitcast_convert_type(w, jnp.uint32)
    return (lax.bitcast_convert_type(bits << 16, F32),
            lax.bitcast_convert_type(bits & jnp.uint32(0xFFFF0000), F32))


def _rms(x, g):
    return x * lax.rsqrt(jnp.mean(x * x, axis=-1, keepdims=True) + EPS) * g


def _gelu(x):
    return 0.5 * x * (1.0 + lax.erf(x * (1.0 / math.sqrt(2.0))))


def _row_tile(n_p, n_s, cap=512):
    for t in (1024, 512, 256, 128):
        if t > cap:
            continue
        if n_p % t == 0 and n_s % t == 0:
            return t
    raise ValueError(f"token counts {n_p}, {n_s} must be multiples of 128")


def _group_specs(tm, width, n_p):
    pt = n_p // tm
    return [pl.BlockSpec((tm, width), lambda i, *_: (jnp.minimum(i, pt - 1), 0)),
            pl.BlockSpec((tm, width), lambda i, *_: (jnp.maximum(i - pt, 0), 0))]


def _group_pick(prompt_ref, sample_ref, n_p):
    pt = n_p // prompt_ref.shape[0]
    return lax.cond(pl.program_id(0) < pt, lambda: prompt_ref[...], lambda: sample_ref[...])


def _in_proj_kernel(xp_ref, xs_ref, g_ref, w_ref, wab_ref, z_ref, ab_ref, h_scr, *, n_p):
    @pl.when(pl.program_id(1) == 0)
    def _():
        hb = _rms(_group_pick(xp_ref, xs_ref, n_p), g_ref[...]).astype(BF16)
        h_scr[...] = hb
        ab_ref[...] = jnp.dot(hb, wab_ref[...], preferred_element_type=F32)

    z_ref[...] = jnp.dot(h_scr[...], w_ref[...], preferred_element_type=F32).astype(z_ref.dtype)


def _in_proj(xp, xs, g, w_main, w_ab):
    n_p, n_s = xp.shape[0], xs.shape[0]
    n = n_p + n_s
    tm, tn = _row_tile(n_p, n_s, cap=1024), 2048
    cols = w_main.shape[1]
    return pl.pallas_call(
        functools.partial(_in_proj_kernel, n_p=n_p),
        grid=(n // tm, cols // tn),
        in_specs=_group_specs(tm, D_MODEL, n_p) + [
            pl.BlockSpec((1, D_MODEL), lambda i, j: (0, 0)),
            pl.BlockSpec((D_MODEL, tn), lambda i, j: (0, j)),
            pl.BlockSpec((D_MODEL, LANES), lambda i, j: (0, 0)),
        ],
        out_specs=[
            pl.BlockSpec((tm, tn), lambda i, j: (i, j)),
            pl.BlockSpec((tm, LANES), lambda i, j: (i, 0)),
        ],
        out_shape=[jax.ShapeDtypeStruct((n, cols), BF16), jax.ShapeDtypeStruct((n, LANES), F32)],
        scratch_shapes=[pltpu.VMEM((tm, D_MODEL), BF16)],
        compiler_params=_params("parallel", "arbitrary"),
        name="in_proj",
    )(xp, xs, g, w_main, w_ab)


def _delta_kernel(*refs, C, G, carry, precise):
    (qkv_ref, zg_ref, ab_ref, buf_ref, s0_ref, cw_ref, alog_ref, dtb_ref, ong_ref) = refs[:9]
    refs = refs[9:]
    o_ref, snew_ref, bufnew_ref, xc_scr = refs[:4]
    H, DK, DV = DN_HEADS, DN_DK, DN_DV
    T = G * C
    dotm = _dot3 if precise else _dot
    dotm_tn = _dot3_tn if precise else _dot_tn
    halo = DN_CONV - 1
    base = 8
    cw = cw_ref[...]

    def conv(window):
        y = window(0) * cw[0:1]
        for i in range(1, DN_CONV):
            y = y + window(i) * cw[i:i + 1]
        return y

    if carry:
        s_scr = refs[4]
        c = pl.program_id(1)
        last = pl.num_programs(1) - 1

        @pl.when(c == 0)
        def _():
            xc_scr[base - halo:base, :] = buf_ref[0]
            s_scr[...] = s0_ref[0]

        xc_scr[base:base + T, :] = qkv_ref[...].astype(F32)
        y = conv(lambda i: xc_scr[base - halo + i:base - halo + i + T, :])
        tail = xc_scr[base + T - halo:base + T, :]
        xc_scr[base - halo:base, :] = tail

        @pl.when(c == last)
        def _():
            bufnew_ref[0] = tail
    else:
        ys = []
        x_new = qkv_ref[...].astype(F32)
        for g in range(G):
            xc_scr[g, base - halo:base, :] = buf_ref[g]
            xc_scr[g, base:base + C, :] = x_new[g * C:(g + 1) * C, :]
            ys.append(conv(lambda i: xc_scr[g, base - halo + i:base - halo + i + C, :]))
            bufnew_ref[g] = xc_scr[g, base + C - halo:base + C, :]
        y = jnp.concatenate(ys, axis=0) if G > 1 else ys[0]
    qkv = y * _sigmoid(y)

    ab = ab_ref[...]
    g_all = -jnp.exp(alog_ref[...]) * jax.nn.softplus(ab + dtb_ref[...])
    beta_all = _sigmoid(ab)
    shift = C.bit_length() - 1
    rt = lax.broadcasted_iota(jnp.int32, (T, T), 0)
    ct = lax.broadcasted_iota(jnp.int32, (T, T), 1)
    chunk_tril = ((rt >> shift) == (ct >> shift)) & (rt >= ct)
    gcum = _dot3(chunk_tril.astype(F32), g_all)
    gam_all = jnp.exp(gcum)

    row = lax.broadcasted_iota(jnp.int32, (C, C), 0)
    col = lax.broadcasted_iota(jnp.int32, (C, C), 1)
    incl, strict, eye = row >= col, row > col, row == col
    eyef = eye.astype(F32)
    units = [(g, h) for g in range(G) for h in range(H)]
    rows = lambda a, g: a[g * C:(g + 1) * C]

    qn, kn, vv = [], [], []
    for h in range(H):
        q = qkv[:, h * DK:(h + 1) * DK]
        k = qkv[:, QK_W + h * DK:QK_W + (h + 1) * DK]
        qn.append(q * (lax.rsqrt(jnp.sum(q * q, axis=-1, keepdims=True) + EPS) * (DK ** -0.5)))
        kn.append(k * lax.rsqrt(jnp.sum(k * k, axis=-1, keepdims=True) + EPS))
        vv.append(qkv[:, 2 * QK_W + h * DV:2 * QK_W + (h + 1) * DV])
    qb = [q.astype(BF16) for q in qn]
    kb = [k.astype(BF16) for k in kn]

    gc, bc, gl, a_low, m_intra = {}, {}, {}, {}, {}
    for u in units:
        g, h = u
        gc[u] = rows(gcum, g)[:, h:h + 1]
        bc[u] = rows(beta_all, g)[:, H + h:H + h + 1]
        gl[u] = gc[u][C - 1:C, :]
        gr = jnp.sum(jnp.where(eye, gc[u], 0.0), axis=0, keepdims=True)
        br = jnp.sum(jnp.where(eye, bc[u], 0.0), axis=0, keepdims=True)
        db = jnp.where(incl, jnp.exp(gc[u] - gr), 0.0) * br
        k = rows(kb[h], g)
        kq = _dot_nt(jnp.concatenate([k, rows(qb[h], g)], axis=0), k)
        a_low[u] = jnp.where(strict, kq[:C] * db, 0.0)
        m_intra[u] = kq[C:] * db

    t_inv = {u: eyef - a_low[u] for u in units}
    a_pow = {u: dotm(a_low[u], a_low[u]) for u in units}
    n = 2
    while n < C:
        t_inv = {u: t_inv[u] + dotm(t_inv[u], a_pow[u]) for u in units}
        n *= 2
        if n < C:
            a_pow = {u: dotm(a_pow[u], a_pow[u]) for u in units}

    u_base, wq, k_dec = {}, {}, {}
    for u in units:
        g, h = u
        gam = rows(gam_all, g)[:, h:h + 1]
        k = rows(kn[h], g)
        sol = dotm(t_inv[u], jnp.concatenate([rows(vv[h], g), gam * k], axis=1))
        u_base[u] = sol[:, :DV]
        wq[u] = jnp.concatenate([sol[:, DV:], gam * rows(qn[h], g)], axis=0)
        k_dec[u] = k * (bc[u] * jnp.exp(gl[u] - gc[u]))

    state = [s_scr[h] for h in range(H)] if carry else None
    outs = {}
    for g in range(G):
        s_in = state if carry else [s0_ref[g, h] for h in range(H)]
        wqs = [dotm(wq[(g, h)], s_in[h]) for h in range(H)]
        us = [u_base[(g, h)] - wqs[h][:C] for h in range(H)]
        for h in range(H):
            outs[(g, h)] = wqs[h][C:] + dotm(m_intra[(g, h)], us[h])
        s_out = [jnp.exp(gl[(g, h)]) * s_in[h] + dotm_tn(k_dec[(g, h)], us[h]) for h in range(H)]
        if carry:
            state = s_out
        else:
            for h in range(H):
                snew_ref[g, h] = s_out[h]

    for h in range(H):
        o = jnp.concatenate([outs[(g, h)] for g in range(G)], axis=0) if G > 1 else outs[(0, h)]
        zg = zg_ref[:, h * DV:(h + 1) * DV].astype(F32)
        o_ref[:, h * DV:(h + 1) * DV] = _rms(o, ong_ref[...]) * (zg * _sigmoid(zg))

    if carry:
        for h in range(H):
            s_scr[h] = state[h]

        @pl.when(c == last)
        def _():
            for h in range(H):
                snew_ref[0, h] = state[h]


def _delta_branch(z, ab, conv_buf, s0, conv_w, alog_p, dtb_p, onorm_g, *, tok0, L, G, precise):
    B = conv_buf.shape[0]
    C = math.gcd(L, DN_CHUNK)
    nc = L // C
    carry = nc > 1
    T = G * C
    blk0 = tok0 // T
    if carry:
        assert nc % G == 0
        grid = (B, nc // G)
        own_blk = lambda b, c: b * (nc // G) + c
        gs = 1
    else:
        assert B % G == 0
        grid = (B // G, 1)
        own_blk = lambda b, c: b
        gs = G
    tok = lambda b, c: (blk0 + own_blk(b, c), 0)
    seq3 = lambda b, c: (b, 0, 0)
    seq4 = lambda b, c: (b, 0, 0, 0)
    const = lambda b, c: (0, 0)
    if carry:
        scratch = [pltpu.VMEM((8 + T, QKV_W), F32), pltpu.VMEM((DN_HEADS, DN_DK, DN_DV), F32)]
    else:
        scratch = [pltpu.VMEM((G, 8 + C, QKV_W), F32)]
    return pl.pallas_call(
        functools.partial(_delta_kernel, C=C, G=G, carry=carry, precise=precise),
        grid=grid,
        in_specs=[
            pl.BlockSpec((T, QKV_W), tok),
            pl.BlockSpec((T, V_W), lambda b, c: (blk0 + own_blk(b, c), QKV_W // V_W)),
            pl.BlockSpec((T, LANES), tok),
            pl.BlockSpec((gs, DN_CONV - 1, QKV_W), seq3),
            pl.BlockSpec((gs, DN_HEADS, DN_DK, DN_DV), seq4),
            pl.BlockSpec((DN_CONV, QKV_W), const),
            pl.BlockSpec((1, LANES), const),
            pl.BlockSpec((1, LANES), const),
            pl.BlockSpec((1, DN_DV), const),
        ],
        out_specs=[
            pl.BlockSpec((T, V_W), lambda b, c: (own_blk(b, c), 0)),
            pl.BlockSpec((gs, DN_HEADS, DN_DK, DN_DV), seq4),
            pl.BlockSpec((gs, DN_CONV - 1, QKV_W), seq3),
        ],
        out_shape=[
            jax.ShapeDtypeStruct((B * L, V_W), F32),
            jax.ShapeDtypeStruct(s0.shape, F32),
            jax.ShapeDtypeStruct(conv_buf.shape, F32),
        ],
        scratch_shapes=scratch,
        compiler_params=_params("parallel", "arbitrary"),
        name=f"delta_rule_c{C}",
    )(z, z, ab, conv_buf, s0, conv_w, alog_p, dtb_p, onorm_g)


def _post_mix_kernel(op_ref, os_ref, gu_ref, gv_ref, ma_ref, mb_ref, xp_ref, xs_ref, lng_ref, lnb_ref,
                     mix_ref, bias_ref, wa_ref, wb_ref, wo_ref, g2_ref, rwh_ref, rwl_ref, rb_ref, tri_ref,
                     x1_ref, h2_ref, gate_ref, idx_ref, v_ref, cnt_ref, us_scr, cnt_scr, *, n_p):
    u = _gelu(gu_ref[...].astype(F32))
    a = _gelu(gv_ref[...].astype(F32))
    ac = a - jnp.mean(a, axis=-1, keepdims=True)
    v = ac * lax.rsqrt(jnp.mean(ac * ac, axis=-1, keepdims=True) + EPS) * lng_ref[...] + lnb_ref[...]
    v_ref[...] = v
    vb = v.astype(BF16)
    gw = GM_WIDTH // GM_GROUPS
    for c in range(u.shape[0] // GM_CHUNK):
        rs = slice(c * GM_CHUNK, (c + 1) * GM_CHUNK)
        for g in range(GM_GROUPS):
            sl = slice(g * gw, (g + 1) * gw)
            s = jnp.dot(mix_ref[0, g], vb[rs, sl], preferred_element_type=F32) + bias_ref[0, :, sl]
            us_scr[rs, sl] = (u[rs, sl] * s).astype(BF16)

    ya = jnp.dot(_group_pick(op_ref, os_ref, n_p).astype(BF16), wa_ref[...], preferred_element_type=F32)
    yb = jnp.dot(us_scr[...], wb_ref[...], preferred_element_type=F32)
    mixed = _sigmoid(ma_ref[...].astype(F32)) * ya + _sigmoid(mb_ref[...].astype(F32)) * yb
    x1 = _group_pick(xp_ref, xs_ref, n_p) + jnp.dot(mixed.astype(BF16), wo_ref[...], preferred_element_type=F32)
    x1_ref[...] = x1
    h2 = _rms(x1, g2_ref[...])
    h2_ref[...] = _pack_bf16_pairs(h2)
    hh, hl = _split(h2)
    d = functools.partial(jnp.dot, preferred_element_type=F32)
    logits = d(hh, rwh_ref[...]) + (d(hh, rwl_ref[...]) + d(hl, rwh_ref[...])) + rb_ref[...]
    lane = lax.broadcasted_iota(jnp.int32, logits.shape, 1).astype(F32)
    vals, idxs = [], []
    for _ in range(TOP_K):
        m = jnp.max(logits, axis=-1, keepdims=True)
        i = jnp.min(jnp.where(logits == m, lane, float(LANES)), axis=-1, keepdims=True)
        vals.append(m)
        idxs.append(i)
        logits = jnp.where(lane == i, -jnp.inf, logits)
    es = [jnp.exp(v - vals[0]) for v in vals]
    tot = es[0]
    for e in es[1:]:
        tot = tot + e

    @pl.when(pl.program_id(0) == 0)
    def _():
        cnt_scr[...] = jnp.zeros(cnt_scr.shape, F32)

    onehot = (lane == idxs[0]).astype(F32)
    for k in range(1, TOP_K):
        onehot = onehot + (lane == idxs[k]).astype(F32)
    before = jnp.dot(tri_ref[...], onehot.astype(BF16), preferred_element_type=F32) + cnt_scr[...]
    cnt_scr[...] = cnt_scr[...] + jnp.sum(onehot, axis=0, keepdims=True)
    cnt_ref[...] = cnt_scr[...]

    gates = jnp.zeros(logits.shape, F32)
    route = jnp.zeros(logits.shape, F32)
    for k in range(TOP_K):
        rank = jnp.sum(jnp.where(lane == idxs[k], before, 0.0), axis=-1, keepdims=True)
        gates = jnp.where(lane == float(k), es[k] / tot, gates)
        route = jnp.where(lane == float(k), idxs[k], route)
        route = jnp.where(lane == float(TOP_K + k), rank, route)
    gate_ref[...] = gates
    idx_ref[...] = route.astype(jnp.int32)


def _post_mix(o_p, o_s, z, xp, xs, ln_g, ln_b, mix, bias, wa, wb, wo, g2, rwh, rwl, rb):
    n_p, n_s = xp.shape[0], xs.shape[0]
    n = n_p + n_s
    tm = _row_tile(n_p, n_s)
    pt = n_p // tm
    tri = jnp.tril(jnp.ones((tm, tm), BF16), k=-1)
    grp = lambda i: jnp.where(i < pt, 0, 1)
    once = pl.Buffered(1)
    tok = pl.BlockSpec((tm, D_MODEL), lambda i: (i, 0))
    zcol = lambda c: pl.BlockSpec((tm, D_MODEL), lambda i: (i, c))
    full = lambda r, c: pl.BlockSpec((r, c), lambda i: (0, 0), pipeline_mode=once)
    narrow = pl.BlockSpec((tm, LANES), lambda i: (i, 0))
    op_spec, os_spec = _group_specs(tm, V_W, n_p)
    xp_spec, xs_spec = _group_specs(tm, D_MODEL, n_p)
    return pl.pallas_call(
        functools.partial(_post_mix_kernel, n_p=n_p),
        grid=(n // tm,),
        in_specs=[
            op_spec, os_spec, zcol(4), zcol(5), zcol(6), zcol(7), xp_spec, xs_spec,
            full(1, GM_WIDTH), full(1, GM_WIDTH),
            pl.BlockSpec((1, GM_GROUPS, GM_CHUNK, GM_CHUNK), lambda i: (grp(i), 0, 0, 0)),
            pl.BlockSpec((1, GM_CHUNK, GM_WIDTH), lambda i: (grp(i), 0, 0)),
            full(V_W, D_MODEL), full(GM_WIDTH, D_MODEL), full(D_MODEL, D_MODEL), full(1, D_MODEL),
            full(D_MODEL, LANES), full(D_MODEL, LANES), full(1, LANES), full(tm, tm),
        ],
        out_specs=[tok, pl.BlockSpec((tm, D_MODEL // 2), lambda i: (i, 0)), narrow, narrow,
                   pl.BlockSpec((tm, GM_WIDTH), lambda i: (jnp.maximum(i - pt, 0), 0)),
                   pl.BlockSpec((1, LANES), lambda i: (0, 0))],
        out_shape=[
            jax.ShapeDtypeStruct((n, D_MODEL), F32),
            jax.ShapeDtypeStruct((n, D_MODEL // 2), jnp.int32),
            jax.ShapeDtypeStruct((n, LANES), F32),
            jax.ShapeDtypeStruct((n, LANES), jnp.int32),
            jax.ShapeDtypeStruct((n_s, GM_WIDTH), F32),
            jax.ShapeDtypeStruct((1, LANES), F32),
        ],
        scratch_shapes=[pltpu.VMEM((tm, GM_WIDTH), BF16), pltpu.VMEM((1, LANES), F32)],
        compiler_params=_params("arbitrary"),
        name="chunk_mlp_post_mix_router",
    )(o_p, o_s, z, z, z, z, xp, xs, ln_g, ln_b, mix, bias, wa, wb, wo, g2, rwh, rwl, rb, tri)


def _moe_kernel(be_ref, first_ref, valid_ref, x_ref, w1_ref, b1_ref, w2_ref, b2_ref, y_ref,
                w1_scr, w2_scr):
    b = pl.program_id(0)

    @pl.when(first_ref[b] == 1)
    def _():
        w1_scr[...] = w1_ref[0].astype(BF16)
        w2_scr[...] = w2_ref[0].astype(BF16)

    @pl.when(valid_ref[b] == 1)
    def _():
        x = jnp.concatenate(_unpack_bf16_pairs(x_ref[...]), axis=1).astype(BF16)
        hid = jnp.dot(x, w1_scr[...], preferred_element_type=F32) + b1_ref[0]
        gate = jnp.minimum(hid[:, :D_EXPERT], SWIGLU_LIMIT)
        up = jnp.clip(hid[:, D_EXPERT:], -SWIGLU_LIMIT, SWIGLU_LIMIT)
        act = gate * _sigmoid(SWIGLU_ALPHA * gate) * (up + 1.0)
        y = jnp.dot(act.astype(BF16), w2_scr[...], preferred_element_type=F32) + b2_ref[0]
        y_ref[...] = _pack_bf16_pairs(y)

    @pl.when(valid_ref[b] == 0)
    def _():
        y_ref[...] = jnp.zeros(y_ref.shape, jnp.int32)


def _moe_experts(block_e, first, valid, xb, w1, b1, w2, b2):
    rows = xb.shape[0]
    nb = rows // MOE_ROWS
    return pl.pallas_call(
        _moe_kernel,
        grid_spec=pltpu.PrefetchScalarGridSpec(
            num_scalar_prefetch=3,
            grid=(nb,),
            in_specs=[
                pl.BlockSpec((MOE_ROWS, D_MODEL // 2), lambda b, be, fi, va: (b, 0)),
                pl.BlockSpec((1, D_MODEL, 2 * D_EXPERT), lambda b, be, fi, va: (be[b], 0, 0)),
                pl.BlockSpec((1, 1, 2 * D_EXPERT), lambda b, be, fi, va: (be[b], 0, 0)),
                pl.BlockSpec((1, D_EXPERT, D_MODEL), lambda b, be, fi, va: (be[b], 0, 0)),
                pl.BlockSpec((1, 1, D_MODEL), lambda b, be, fi, va: (be[b], 0, 0)),
            ],
            out_specs=pl.BlockSpec((MOE_ROWS, D_MODEL // 2), lambda b, be, fi, va: (b, 0)),
            scratch_shapes=[
                pltpu.VMEM((D_MODEL, 2 * D_EXPERT), BF16),
                pltpu.VMEM((D_EXPERT, D_MODEL), BF16),
            ],
        ),
        out_shape=jax.ShapeDtypeStruct((rows, D_MODEL // 2), jnp.int32),
        compiler_params=_params("arbitrary"),
        name="moe_experts",
    )(block_e, first, valid, xb, w1, b1, w2, b2)


def _moe_dispatch(idx, rank, counts, n):
    experts = jnp.arange(N_EXPERTS, dtype=jnp.int32)
    padded = (counts + MOE_ROWS - 1) // MOE_ROWS * MOE_ROWS
    pad_end = jnp.cumsum(padded)
    pad_start = pad_end - padded
    start_of = jnp.sum(jnp.where(idx[:, :, None] == experts, pad_start, 0), axis=-1)
    dest = start_of + rank
    nb = -(-n * TOP_K // MOE_ROWS) + N_EXPERTS
    rows = nb * MOE_ROWS
    starts = jnp.arange(nb, dtype=jnp.int32) * MOE_ROWS
    valid = (starts < pad_end[-1]).astype(jnp.int32)
    owner = lambda r: jnp.minimum(jnp.sum((pad_end[None, :] <= r[:, None]).astype(jnp.int32), axis=1),
                                  N_EXPERTS - 1)
    last_e = owner(pad_end[-1:] - 1)[0]
    block_e = jnp.where(valid == 1, owner(starts), last_e).astype(jnp.int32)
    first = jnp.concatenate([jnp.ones((1,), jnp.int32),
                             (block_e[1:] != block_e[:-1]).astype(jnp.int32)])
    return dest.T, rows, block_e, first, valid


def _sc_copy_rows(src, src_idx, dst_idx, out_rows):
    m = src_idx.shape[1]
    d = src.shape[1]
    mesh = plsc.VectorSubcoreMesh(core_axis_name="core", subcore_axis_name="subcore",
                                  num_cores=SC_CORES, num_subcores=SC_SUBCORES)

    @pl.kernel(out_type=jax.ShapeDtypeStruct((out_rows, d), src.dtype), mesh=mesh,
               scratch_types=[pltpu.VMEM((SC_CHUNK, d), src.dtype)])
    def copy_rows(src_hbm, si_hbm, di_hbm, out_hbm, buf):
        def body(si_vmem, di_vmem):
            for c in range(SC_WINDOW // SC_CHUNK):
                sl = pl.ds(c * SC_CHUNK, SC_CHUNK)
                pltpu.sync_copy(src_hbm.at[si_vmem.at[0, sl]], buf)
                pltpu.sync_copy(buf, out_hbm.at[di_vmem.at[0, sl]])

        pltpu.emit_pipeline(
            body,
            grid=(m // SC_WINDOW,),
            in_specs=[pl.BlockSpec((1, SC_WINDOW), lambda i: (0, i)),
                      pl.BlockSpec((1, SC_WINDOW), lambda i: (0, i))],
            out_specs=[],
            core_axis_name=("core", "subcore"),
            dimension_semantics=(pltpu.PARALLEL,),
        )(si_hbm, di_hbm)

    return copy_rows(src, src_idx, dst_idx)


def _tail_kernel(x1_ref, yk_ref, gate_ref, pp_ref, ps_ref, g3_ref, wg_ref, wp_ref, gf_ref,
                 yp_ref, ys_ref, *, n_p):
    gates = gate_ref[...]
    lo, hi = _unpack_bf16_pairs(yk_ref[0])
    lo, hi = lo * gates[:, 0:1], hi * gates[:, 0:1]
    for k in range(1, TOP_K):
        lo_k, hi_k = _unpack_bf16_pairs(yk_ref[k])
        lo, hi = lo + lo_k * gates[:, k:k + 1], hi + hi_k * gates[:, k:k + 1]
    moe = jnp.concatenate([lo, hi], axis=1)
    x2 = x1_ref[...] + moe
    h3 = _rms(x2, g3_ref[...])
    gate = _sigmoid(jnp.dot(h3.astype(BF16), wg_ref[...], preferred_element_type=F32))
    pe = jnp.dot(_group_pick(pp_ref, ps_ref, n_p).astype(BF16), wp_ref[...], preferred_element_type=F32)
    y = _rms(x2 + gate * pe, gf_ref[...])
    in_prompt = pl.program_id(0) < n_p // yp_ref.shape[0]

    @pl.when(in_prompt)
    def _():
        yp_ref[...] = y

    @pl.when(jnp.logical_not(in_prompt))
    def _():
        ys_ref[...] = y


def _tail(x1, yk, gates, pp, ps, g3, wg, wp, gf):
    n_p, n_s = pp.shape[0], ps.shape[0]
    n = n_p + n_s
    tm = _row_tile(n_p, n_s)
    tok = pl.BlockSpec((tm, D_MODEL), lambda i: (i, 0))
    full = lambda r, c: pl.BlockSpec((r, c), lambda i: (0, 0))
    return pl.pallas_call(
        functools.partial(_tail_kernel, n_p=n_p),
        grid=(n // tm,),
        in_specs=[tok, pl.BlockSpec((TOP_K, tm, D_MODEL // 2), lambda i: (0, i, 0)),
                  pl.BlockSpec((tm, LANES), lambda i: (i, 0)),
                  *_group_specs(tm, PLE_DIM, n_p),
                  full(1, D_MODEL), full(D_MODEL, D_MODEL), full(PLE_DIM, D_MODEL), full(1, D_MODEL)],
        out_specs=_group_specs(tm, D_MODEL, n_p),
        out_shape=[jax.ShapeDtypeStruct((n_p, D_MODEL), F32), jax.ShapeDtypeStruct((n_s, D_MODEL), F32)],
        compiler_params=_params("arbitrary"),
        name="ple_final_norm",
    )(x1, yk, gates, pp, ps, g3, wg, wp, gf)


def _lane_pad(v, offset, fill=0.0):
    out = jnp.full((1, LANES), fill, F32)
    return out.at[0, offset:offset + v.shape[0]].set(v.astype(F32))


def kernel(x_prompt, x_sample, state_delta, state_conv, p_prompt, p_sample, norm1_g, w_in, conv_w, a_log, dt_bias, dn_norm_g, w_proj_a, gm_ln_g, gm_ln_b, gm_ws, gm_bs, w_proj_b, w_out, norm2_g, router_w, router_b, moe_w1, moe_b1, moe_w2, moe_b2, norm3_g, ple_w, ple_gate_w, final_norm_g):
    bp, lp, d = x_prompt.shape
    bs, ls, _ = x_sample.shape
    depth = w_in.shape[0]
    assert depth == 1 and d == D_MODEL
    assert lp % GM_CHUNK == 0 and GM_CHUNK % ls == 0 and ls >= DN_CONV - 1
    n_p, n_s = bp * lp, bs * ls
    n = n_p + n_s
    i = 0

    xp, xs = x_prompt.reshape(n_p, d), x_sample.reshape(n_s, d)

    ab0 = QKV_W
    w = w_in[i]
    w_main = jnp.concatenate([w[:, :ab0], w[:, ab0 + 2 * DN_HEADS:]], axis=1).astype(BF16)
    w_ab = jnp.pad(w[:, ab0:ab0 + 2 * DN_HEADS], ((0, 0), (0, LANES - 2 * DN_HEADS))).astype(BF16)
    row2 = lambda v: v.reshape(1, -1).astype(F32)

    z, ab = _in_proj(xp, xs, row2(norm1_g[i]), w_main, w_ab)

    alog_p = _lane_pad(a_log[i], 0)
    dtb_p = _lane_pad(dt_bias[i], 0)
    cw = conv_w[i].astype(F32)
    ong = row2(dn_norm_g[i])
    zero_s = jnp.zeros((bp, DN_HEADS, DN_DK, DN_DV), F32)
    zero_buf = jnp.zeros((bp, DN_CONV - 1, QKV_W), F32)
    o_p, sd_p, sc_p = _delta_branch(z, ab, zero_buf, zero_s, cw, alog_p, dtb_p, ong,
                                    tok0=0, L=lp, G=2, precise=False)
    o_s, sd_s, sc_s = _delta_branch(z, ab, state_conv[i], state_delta[i], cw, alog_p, dtb_p, ong,
                                    tok0=n_p, L=ls, G=2, precise=False)

    t = GM_CHUNK
    tri = jnp.tril(jnp.ones((t, t), bool))
    ws = gm_ws[i]
    mix_p = jnp.where(tri, ws, 0.0)
    small = jnp.where(tri[:ls, :ls], ws[:, :ls, :ls], 0.0)
    mix_s = jnp.einsum('ab,gts->gatbs', jnp.eye(t // ls, dtype=F32), small).reshape(GM_GROUPS, t, t)
    mix = jnp.stack([mix_p, mix_s]).astype(BF16)
    gw = GM_WIDTH // GM_GROUPS
    bias_p = jnp.repeat(gm_bs[i].T, gw, axis=1)
    bias_s = jnp.tile(bias_p[:ls], (t // ls, 1))
    bias = jnp.stack([bias_p, bias_s]).astype(F32)
    rw = jnp.pad(router_w[i].astype(F32), ((0, 0), (0, LANES - N_EXPERTS)))
    rwh = rw.astype(BF16)
    rwl = (rw - rwh.astype(F32)).astype(BF16)
    rb = _lane_pad(router_b[i], 0, fill=-jnp.inf)
    x1, h2, gates, route, v_s, counts = _post_mix(
        o_p, o_s, z, xp, xs, row2(gm_ln_g[i]), row2(gm_ln_b[i]), mix, bias, w_proj_a[i].astype(BF16),
        w_proj_b[i].astype(BF16), w_out[i].astype(BF16), row2(norm2_g[i]), rwh, rwl, rb)
    dest_t, rows, block_e, first, valid = _moe_dispatch(
        route[:, :TOP_K], route[:, TOP_K:2 * TOP_K], counts[0, :N_EXPERTS].astype(jnp.int32), n)
    flat_dest = dest_t.reshape(1, TOP_K * n)
    tok_ids = jnp.tile(jnp.arange(n, dtype=jnp.int32), TOP_K).reshape(1, TOP_K * n)
    xb = _sc_copy_rows(h2, tok_ids, flat_dest, rows)
    yb = _moe_experts(block_e, first, valid, xb, moe_w1[i], moe_b1[i][:, None, :],
                      moe_w2[i], moe_b2[i][:, None, :])
    slots = jnp.arange(TOP_K * n, dtype=jnp.int32).reshape(1, TOP_K * n)
    yk = _sc_copy_rows(yb, flat_dest, slots, TOP_K * n).reshape(TOP_K, n, d // 2)

    y_p, y_s = _tail(x1, yk, gates, p_prompt[i].reshape(n_p, PLE_DIM), p_sample[i].reshape(n_s, PLE_DIM),
                     row2(norm3_g[i]), ple_gate_w[i].astype(BF16), ple_w[i].astype(BF16), row2(final_norm_g))

    return (y_p.reshape(bp, lp, d), y_s.reshape(bs, ls, d),
            sd_p[None], sc_p[None], sd_s[None], sc_s[None], v_s.reshape(1, bs, ls, GM_WIDTH))
```

```python
import functools
import math

import jax
import jax.numpy as jnp
from jax import lax
from jax.experimental import pallas as pl
from jax.experimental.pallas import tpu as pltpu
from jax.experimental.pallas import tpu_sc as plsc

F32 = jnp.float32
BF16 = jnp.bfloat16

D_MODEL = 1024
DN_HEADS = 8
DN_DK = 128
DN_DV = 128
DN_CONV = 4
DN_CHUNK = 64
GM_WIDTH = 1024
GM_GROUPS = 8
GM_CHUNK = 128
N_EXPERTS = 32
TOP_K = 4
D_EXPERT = 1024
SWIGLU_LIMIT = 7.0
SWIGLU_ALPHA = 1.702
PLE_DIM = 256
EPS = 1e-6
QK_W = DN_HEADS * DN_DK
V_W = DN_HEADS * DN_DV
QKV_W = 2 * QK_W + V_W

LANES = 128
MOE_ROWS = 512
VMEM_LIMIT = 56 << 20
SC_CORES, SC_SUBCORES = 2, 16
SC_WINDOW = 128
SC_CHUNK = 64


def _params(*sem):
    return pltpu.CompilerParams(dimension_semantics=sem, vmem_limit_bytes=VMEM_LIMIT)


def _dot(a, b):
    return jnp.dot(a.astype(BF16), b.astype(BF16), preferred_element_type=F32)


def _split(a):
    hi = a.astype(BF16)
    return hi, (a - hi.astype(F32)).astype(BF16)


def _dot3(a, b):
    ah, al = _split(a)
    bh, bl = _split(b)
    d = functools.partial(jnp.dot, preferred_element_type=F32)
    return d(ah, bh) + (d(ah, bl) + d(al, bh))


def _dot_nt(a, b):
    return lax.dot_general(a.astype(BF16), b.astype(BF16), (((1,), (1,)), ((), ())),
                           preferred_element_type=F32)


def _dot_tn(a, b):
    return lax.dot_general(a.astype(BF16), b.astype(BF16), (((0,), (0,)), ((), ())),
                           preferred_element_type=F32)


def _dot3_tn(a, b):
    ah, al = _split(a)
    bh, bl = _split(b)
    d = functools.partial(lax.dot_general, dimension_numbers=(((0,), (0,)), ((), ())),
                          preferred_element_type=F32)
    return d(ah, bh) + (d(ah, bl) + d(al, bh))


def _sigmoid(x):
    return 0.5 * jnp.tanh(0.5 * x) + 0.5


def _pack_bf16_pairs(x):
    bits = lax.bitcast_convert_type(x.astype(BF16).astype(F32), jnp.uint32)
    half = x.shape[1] // 2
    return lax.bitcast_convert_type((bits[:, :half] >> 16) | bits[:, half:], jnp.int32)


def _unpack_bf16_pairs(w):
    bits = lax.bitcast_convert_type(w, jnp.uint32)
    return (lax.bitcast_convert_type(bits << 16, F32),
            lax.bitcast_convert_type(bits & jnp.uint32(0xFFFF0000), F32))


def _rms(x, g):
    return x * lax.rsqrt(jnp.mean(x * x, axis=-1, keepdims=True) + EPS) * g


def _gelu(x):
    return 0.5 * x * (1.0 + lax.erf(x * (1.0 / math.sqrt(2.0))))


def _row_tile(n_p, n_s, cap=512):
    for t in (1024, 512, 256, 128):
        if t > cap:
            continue
        if n_p % t == 0 and n_s % t == 0:
            return t
    raise ValueError(f"token counts {n_p}, {n_s} must be multiples of 128")


def _group_specs(tm, width, n_p):
    pt = n_p // tm
    return [pl.BlockSpec((tm, width), lambda i, *_: (jnp.minimum(i, pt - 1), 0)),
            pl.BlockSpec((tm, width), lambda i, *_: (jnp.maximum(i - pt, 0), 0))]


def _group_pick(prompt_ref, sample_ref, n_p):
    pt = n_p // prompt_ref.shape[0]
    return jnp.where(pl.program_id(0) < pt, prompt_ref[...], sample_ref[...])


def _in_proj_kernel(xp_ref, xs_ref, g_ref, w_ref, wab_ref, z_ref, ab_ref, h_scr, *, n_p):
    @pl.when(pl.program_id(1) == 0)
    def _():
        hb = _rms(_group_pick(xp_ref, xs_ref, n_p), g_ref[...]).astype(BF16)
        h_scr[...] = hb
        ab_ref[...] = jnp.dot(hb, wab_ref[...], preferred_element_type=F32)

    z_ref[...] = jnp.dot(h_scr[...], w_ref[...], preferred_element_type=F32).astype(z_ref.dtype)


def _in_proj(xp, xs, g, w_main, w_ab):
    n_p, n_s = xp.shape[0], xs.shape[0]
    n = n_p + n_s
    tm, tn = _row_tile(n_p, n_s, cap=1024), 2048
    cols = w_main.shape[1]
    return pl.pallas_call(
        functools.partial(_in_proj_kernel, n_p=n_p),
        grid=(n // tm, cols // tn),
        in_specs=_group_specs(tm, D_MODEL, n_p) + [
            pl.BlockSpec((1, D_MODEL), lambda i, j: (0, 0)),
            pl.BlockSpec((D_MODEL, tn), lambda i, j: (0, j)),
            pl.BlockSpec((D_MODEL, LANES), lambda i, j: (0, 0)),
        ],
        out_specs=[
            pl.BlockSpec((tm, tn), lambda i, j: (i, j)),
            pl.BlockSpec((tm, LANES), lambda i, j: (i, 0)),
        ],
        out_shape=[jax.ShapeDtypeStruct((n, cols), BF16), jax.ShapeDtypeStruct((n, LANES), F32)],
        scratch_shapes=[pltpu.VMEM((tm, D_MODEL), BF16)],
        compiler_params=_params("parallel", "arbitrary"),
        name="in_proj",
    )(xp, xs, g, w_main, w_ab)


def _delta_kernel(*refs, C, G, carry, precise):
    (qkv_ref, zg_ref, ab_ref, buf_ref, s0_ref, cw_ref, alog_ref, dtb_ref, ong_ref) = refs[:9]
    refs = refs[9:]
    o_ref, snew_ref, bufnew_ref, xc_scr = refs[:4]
    H, DK, DV = DN_HEADS, DN_DK, DN_DV
    T = G * C
    dotm = _dot3 if precise else _dot
    dotm_tn = _dot3_tn if precise else _dot_tn
    halo = DN_CONV - 1
    base = 8
    cw = cw_ref[...]

    def conv(window):
        y = window(0) * cw[0:1]
        for i in range(1, DN_CONV):
            y = y + window(i) * cw[i:i + 1]
        return y

    if carry:
        s_scr = refs[4]
        c = pl.program_id(1)
        last = pl.num_programs(1) - 1

        @pl.when(c == 0)
        def _():
            xc_scr[base - halo:base, :] = buf_ref[0]
            s_scr[...] = s0_ref[0]

        xb = qkv_ref[...]
        xf = xb.astype(F32)
        xc_scr[base:base + 8, :] = xf[0:8]
        y_head = conv(lambda i: xc_scr[base - halo + i:base - halo + i + 8, :])
        ri = lax.broadcasted_iota(jnp.int32, (halo * T, T), 0)
        ci = lax.broadcasted_iota(jnp.int32, (halo * T, T), 1)
        src_row = (ri & (T - 1)) - (halo - (ri >> (T.bit_length() - 1)))
        shifted = jnp.dot(jnp.where(ci == src_row, 1.0, 0.0).astype(BF16), xb, preferred_element_type=F32)
        y = conv(lambda i: shifted[i * T:(i + 1) * T] if i < halo else xf)
        y = jnp.concatenate([y_head, y[8:]], axis=0)
        tail = xf[T - halo:T]
        xc_scr[base - halo:base, :] = tail

        @pl.when(c == last)
        def _():
            bufnew_ref[0] = tail
    else:
        ys = []
        x_new = qkv_ref[...].astype(F32)
        for g in range(G):
            xc_scr[g, base - halo:base, :] = buf_ref[g]
            xc_scr[g, base:base + C, :] = x_new[g * C:(g + 1) * C, :]
            ys.append(conv(lambda i: xc_scr[g, base - halo + i:base - halo + i + C, :]))
            bufnew_ref[g] = xc_scr[g, base + C - halo:base + C, :]
        y = jnp.concatenate(ys, axis=0) if G > 1 else ys[0]
    qkv = y * _sigmoid(y)

    ab = ab_ref[...]
    g_all = -jnp.exp(alog_ref[...]) * jax.nn.softplus(ab + dtb_ref[...])
    beta_all = _sigmoid(ab)
    shift = C.bit_length() - 1
    rt = lax.broadcasted_iota(jnp.int32, (T, T), 0)
    ct = lax.broadcasted_iota(jnp.int32, (T, T), 1)
    chunk_tril = ((rt >> shift) == (ct >> shift)) & (rt >= ct)
    gcum = _dot3(chunk_tril.astype(F32), g_all)
    gam_all = jnp.exp(gcum)

    row = lax.broadcasted_iota(jnp.int32, (C, C), 0)
    col = lax.broadcasted_iota(jnp.int32, (C, C), 1)
    incl, strict, eye = row >= col, row > col, row == col
    eyef = eye.astype(F32)
    units = [(g, h) for g in range(G) for h in range(H)]
    rows = lambda a, g: a[g * C:(g + 1) * C]

    qn, kn, vv = [], [], []
    for h in range(H):
        q = qkv[:, h * DK:(h + 1) * DK]
        k = qkv[:, QK_W + h * DK:QK_W + (h + 1) * DK]
        qn.append(q * (lax.rsqrt(jnp.sum(q * q, axis=-1, keepdims=True) + EPS) * (DK ** -0.5)))
        kn.append(k * lax.rsqrt(jnp.sum(k * k, axis=-1, keepdims=True) + EPS))
        vv.append(qkv[:, 2 * QK_W + h * DV:2 * QK_W + (h + 1) * DV])
    qb = [q.astype(BF16) for q in qn]
    kb = [k.astype(BF16) for k in kn]

    gc, bc, gl, a_low, m_intra = {}, {}, {}, {}, {}
    for u in units:
        g, h = u
        gc[u] = rows(gcum, g)[:, h:h + 1]
        bc[u] = rows(beta_all, g)[:, H + h:H + h + 1]
        gl[u] = gc[u][C - 1:C, :]
        gr = jnp.sum(jnp.where(eye, gc[u], 0.0), axis=0, keepdims=True)
        br = jnp.sum(jnp.where(eye, bc[u], 0.0), axis=0, keepdims=True)
        db = jnp.where(incl, jnp.exp(gc[u] - gr), 0.0) * br
        k = rows(kb[h], g)
        kq = _dot_nt(jnp.concatenate([k, rows(qb[h], g)], axis=0), k)
        a_low[u] = jnp.where(strict, kq[:C] * db, 0.0)
        m_intra[u] = kq[C:] * db

    t_inv = {u: eyef - a_low[u] for u in units}
    a_pow = {u: dotm(a_low[u], a_low[u]) for u in units}
    n = 2
    while n < C:
        t_inv = {u: t_inv[u] + dotm(t_inv[u], a_pow[u]) for u in units}
        n *= 2
        if n < C:
            a_pow = {u: dotm(a_pow[u], a_pow[u]) for u in units}

    u_base, wq, k_dec = {}, {}, {}
    for u in units:
        g, h = u
        gam = rows(gam_all, g)[:, h:h + 1]
        k = rows(kn[h], g)
        sol = dotm(t_inv[u], jnp.concatenate([rows(vv[h], g), gam * k], axis=1))
        u_base[u] = sol[:, :DV]
        wq[u] = jnp.concatenate([sol[:, DV:], gam * rows(qn[h], g)], axis=0)
        k_dec[u] = k * (bc[u] * jnp.exp(gl[u] - gc[u]))

    state = [s_scr[h] for h in range(H)] if carry else None
    outs = {}
    for g in range(G):
        s_in = state if carry else [s0_ref[g, h] for h in range(H)]
        wqs = [dotm(wq[(g, h)], s_in[h]) for h in range(H)]
        us = [u_base[(g, h)] - wqs[h][:C] for h in range(H)]
        for h in range(H):
            outs[(g, h)] = wqs[h][C:] + dotm(m_intra[(g, h)], us[h])
        s_out = [jnp.exp(gl[(g, h)]) * s_in[h] + dotm_tn(k_dec[(g, h)], us[h]) for h in range(H)]
        if carry:
            state = s_out
        else:
            for h in range(H):
                snew_ref[g, h] = s_out[h]

    for h in range(H):
        o = jnp.concatenate([outs[(g, h)] for g in range(G)], axis=0) if G > 1 else outs[(0, h)]
        zg = zg_ref[:, h * DV:(h + 1) * DV].astype(F32)
        o_ref[:, h * DV:(h + 1) * DV] = (_rms(o, ong_ref[...]) * (zg * _sigmoid(zg))).astype(o_ref.dtype)

    if carry:
        for h in range(H):
            s_scr[h] = state[h]

        @pl.when(c == last)
        def _():
            for h in range(H):
                snew_ref[0, h] = state[h]


def _delta_branch(z, ab, conv_buf, s0, conv_w, alog_p, dtb_p, onorm_g, *, tok0, L, G, precise):
    B = conv_buf.shape[0]
    C = math.gcd(L, DN_CHUNK)
    nc = L // C
    carry = nc > 1
    T = G * C
    blk0 = tok0 // T
    if carry:
        assert nc % G == 0
        grid = (B, nc // G)
        own_blk = lambda b, c: b * (nc // G) + c
        gs = 1
    else:
        assert B % G == 0
        grid = (B // G, 1)
        own_blk = lambda b, c: b
        gs = G
    tok = lambda b, c: (blk0 + own_blk(b, c), 0)
    seq3 = lambda b, c: (b, 0, 0)
    seq4 = lambda b, c: (b, 0, 0, 0)
    const = lambda b, c: (0, 0)
    if carry:
        scratch = [pltpu.VMEM((16, QKV_W), F32), pltpu.VMEM((DN_HEADS, DN_DK, DN_DV), F32)]
    else:
        scratch = [pltpu.VMEM((G, 8 + C, QKV_W), F32)]
    return pl.pallas_call(
        functools.partial(_delta_kernel, C=C, G=G, carry=carry, precise=precise),
        grid=grid,
        in_specs=[
            pl.BlockSpec((T, QKV_W), tok),
            pl.BlockSpec((T, V_W), lambda b, c: (blk0 + own_blk(b, c), QKV_W // V_W)),
            pl.BlockSpec((T, LANES), tok),
            pl.BlockSpec((gs, DN_CONV - 1, QKV_W), seq3),
            pl.BlockSpec((gs, DN_HEADS, DN_DK, DN_DV), seq4),
            pl.BlockSpec((DN_CONV, QKV_W), const),
            pl.BlockSpec((1, LANES), const),
            pl.BlockSpec((1, LANES), const),
            pl.BlockSpec((1, DN_DV), const),
        ],
        out_specs=[
            pl.BlockSpec((T, V_W), lambda b, c: (own_blk(b, c), 0)),
            pl.BlockSpec((gs, DN_HEADS, DN_DK, DN_DV), seq4),
            pl.BlockSpec((gs, DN_CONV - 1, QKV_W), seq3),
        ],
        out_shape=[
            jax.ShapeDtypeStruct((B * L, V_W), BF16),
            jax.ShapeDtypeStruct(s0.shape, F32),
            jax.ShapeDtypeStruct(conv_buf.shape, F32),
        ],
        scratch_shapes=scratch,
        compiler_params=_params("parallel", "arbitrary"),
        name=f"delta_rule_c{C}",
    )(z, z, ab, conv_buf, s0, conv_w, alog_p, dtb_p, onorm_g)


def _post_mix_kernel(op_ref, os_ref, gu_ref, gv_ref, ma_ref, mb_ref, xp_ref, xs_ref, lng_ref, lnb_ref,
                     mix_ref, bias_ref, wa_ref, wb_ref, wo_ref, g2_ref, rwh_ref, rwl_ref, rb_ref, tri_ref,
                     x1_ref, h2_ref, gate_ref, idx_ref, v_ref, cnt_ref, us_scr, cnt_scr, *, n_p):
    u = _gelu(gu_ref[...].astype(F32))
    a = _gelu(gv_ref[...].astype(F32))
    ac = a - jnp.mean(a, axis=-1, keepdims=True)
    v = ac * lax.rsqrt(jnp.mean(ac * ac, axis=-1, keepdims=True) + EPS) * lng_ref[...] + lnb_ref[...]
    v_ref[...] = v
    vb = v.astype(BF16)
    gw = GM_WIDTH // GM_GROUPS
    for c in range(u.shape[0] // GM_CHUNK):
        rs = slice(c * GM_CHUNK, (c + 1) * GM_CHUNK)
        for g in range(GM_GROUPS):
            sl = slice(g * gw, (g + 1) * gw)
            s = jnp.dot(mix_ref[0, g], vb[rs, sl], preferred_element_type=F32) + bias_ref[0, :, sl]
            us_scr[rs, sl] = (u[rs, sl] * s).astype(BF16)

    ya = jnp.dot(_group_pick(op_ref, os_ref, n_p).astype(BF16), wa_ref[...], preferred_element_type=F32)
    yb = jnp.dot(us_scr[...], wb_ref[...], preferred_element_type=F32)
    mixed = _sigmoid(ma_ref[...].astype(F32)) * ya + _sigmoid(mb_ref[...].astype(F32)) * yb
    x1 = _group_pick(xp_ref, xs_ref, n_p) + jnp.dot(mixed.astype(BF16), wo_ref[...], preferred_element_type=F32)
    x1_ref[...] = x1
    h2 = _rms(x1, g2_ref[...])
    h2_ref[...] = _pack_bf16_pairs(h2)
    hh, hl = _split(h2)
    d = functools.partial(jnp.dot, preferred_element_type=F32)
    logits = d(hh, rwh_ref[...]) + (d(hh, rwl_ref[...]) + d(hl, rwh_ref[...])) + rb_ref[...]
    lane = lax.broadcasted_iota(jnp.int32, logits.shape, 1).astype(F32)
    vals, idxs = [], []
    for _ in range(TOP_K):
        m = jnp.max(logits, axis=-1, keepdims=True)
        i = jnp.min(jnp.where(logits == m, lane, float(LANES)), axis=-1, keepdims=True)
        vals.append(m)
        idxs.append(i)
        logits = jnp.where(lane == i, -jnp.inf, logits)
    es = [jnp.exp(v - vals[0]) for v in vals]
    tot = es[0]
    for e in es[1:]:
        tot = tot + e

    @pl.when(pl.program_id(0) == 0)
    def _():
        cnt_scr[...] = jnp.zeros(cnt_scr.shape, F32)

    onehot = (lane == idxs[0]).astype(F32)
    for k in range(1, TOP_K):
        onehot = onehot + (lane == idxs[k]).astype(F32)
    before = jnp.dot(tri_ref[...], onehot.astype(BF16), preferred_element_type=F32) + cnt_scr[...]
    cnt_scr[...] = cnt_scr[...] + jnp.sum(onehot, axis=0, keepdims=True)
    cnt_ref[...] = cnt_scr[...]

    gates = jnp.zeros(logits.shape, F32)
    route = jnp.zeros(logits.shape, F32)
    for k in range(TOP_K):
        rank = jnp.sum(jnp.where(lane == idxs[k], before, 0.0), axis=-1, keepdims=True)
        gates = jnp.where(lane == float(k), es[k] / tot, gates)
        route = jnp.where(lane == float(k), idxs[k], route)
        route = jnp.where(lane == float(TOP_K + k), rank, route)
    gate_ref[...] = gates
    idx_ref[...] = route.astype(jnp.int32)


def _post_mix(o_p, o_s, z, xp, xs, ln_g, ln_b, mix, bias, wa, wb, wo, g2, rwh, rwl, rb):
    n_p, n_s = xp.shape[0], xs.shape[0]
    n = n_p + n_s
    tm = _row_tile(n_p, n_s)
    pt = n_p // tm
    tri = jnp.tril(jnp.ones((tm, tm), BF16), k=-1)
    grp = lambda i: jnp.where(i < pt, 0, 1)
    once = pl.Buffered(1)
    tok = pl.BlockSpec((tm, D_MODEL), lambda i: (i, 0))
    zcol = lambda c: pl.BlockSpec((tm, D_MODEL), lambda i: (i, c))
    full = lambda r, c: pl.BlockSpec((r, c), lambda i: (0, 0), pipeline_mode=once)
    narrow = pl.BlockSpec((tm, LANES), lambda i: (i, 0))
    op_spec, os_spec = _group_specs(tm, V_W, n_p)
    xp_spec, xs_spec = _group_specs(tm, D_MODEL, n_p)
    return pl.pallas_call(
        functools.partial(_post_mix_kernel, n_p=n_p),
        grid=(n // tm,),
        in_specs=[
            op_spec, os_spec, zcol(4), zcol(5), zcol(6), zcol(7), xp_spec, xs_spec,
            full(1, GM_WIDTH), full(1, GM_WIDTH),
            pl.BlockSpec((1, GM_GROUPS, GM_CHUNK, GM_CHUNK), lambda i: (grp(i), 0, 0, 0)),
            pl.BlockSpec((1, GM_CHUNK, GM_WIDTH), lambda i: (grp(i), 0, 0)),
            full(V_W, D_MODEL), full(GM_WIDTH, D_MODEL), full(D_MODEL, D_MODEL), full(1, D_MODEL),
            full(D_MODEL, LANES), full(D_MODEL, LANES), full(1, LANES), full(tm, tm),
        ],
        out_specs=[tok, pl.BlockSpec((tm, D_MODEL // 2), lambda i: (i, 0)), narrow, narrow,
                   pl.BlockSpec((tm, GM_WIDTH), lambda i: (jnp.maximum(i - pt, 0), 0)),
                   pl.BlockSpec((1, LANES), lambda i: (0, 0))],
        out_shape=[
            jax.ShapeDtypeStruct((n, D_MODEL), F32),
            jax.ShapeDtypeStruct((n, D_MODEL // 2), jnp.int32),
            jax.ShapeDtypeStruct((n, LANES), F32),
            jax.ShapeDtypeStruct((n, LANES), jnp.int32),
            jax.ShapeDtypeStruct((n_s, GM_WIDTH), F32),
            jax.ShapeDtypeStruct((1, LANES), F32),
        ],
        scratch_shapes=[pltpu.VMEM((tm, GM_WIDTH), BF16), pltpu.VMEM((1, LANES), F32)],
        compiler_params=_params("arbitrary"),
        name="chunk_mlp_post_mix_router",
    )(o_p, o_s, z, z, z, z, xp, xs, ln_g, ln_b, mix, bias, wa, wb, wo, g2, rwh, rwl, rb, tri)


def _moe_kernel(be_ref, first_ref, valid_ref, x_ref, w1_ref, b1_ref, w2_ref, b2_ref, y_ref,
                w1_scr, w2_scr):
    b = pl.program_id(0)

    @pl.when(first_ref[b] == 1)
    def _():
        w1_scr[...] = w1_ref[0].astype(BF16)
        w2_scr[...] = w2_ref[0].astype(BF16)

    @pl.when(valid_ref[b] == 1)
    def _():
        x = jnp.concatenate(_unpack_bf16_pairs(x_ref[...]), axis=1).astype(BF16)
        hid = jnp.dot(x, w1_scr[...], preferred_element_type=F32) + b1_ref[0]
        gate = jnp.minimum(hid[:, :D_EXPERT], SWIGLU_LIMIT)
        up = jnp.clip(hid[:, D_EXPERT:], -SWIGLU_LIMIT, SWIGLU_LIMIT)
        act = gate * _sigmoid(SWIGLU_ALPHA * gate) * (up + 1.0)
        y = jnp.dot(act.astype(BF16), w2_scr[...], preferred_element_type=F32) + b2_ref[0]
        y_ref[...] = _pack_bf16_pairs(y)

    @pl.when(valid_ref[b] == 0)
    def _():
        y_ref[...] = jnp.zeros(y_ref.shape, jnp.int32)


def _moe_experts(block_e, first, valid, xb, w1, b1, w2, b2):
    rows = xb.shape[0]
    nb = rows // MOE_ROWS
    return pl.pallas_call(
        _moe_kernel,
        grid_spec=pltpu.PrefetchScalarGridSpec(
            num_scalar_prefetch=3,
            grid=(nb,),
            in_specs=[
                pl.BlockSpec((MOE_ROWS, D_MODEL // 2), lambda b, be, fi, va: (b, 0)),
                pl.BlockSpec((1, D_MODEL, 2 * D_EXPERT), lambda b, be, fi, va: (be[b], 0, 0)),
                pl.BlockSpec((1, 1, 2 * D_EXPERT), lambda b, be, fi, va: (be[b], 0, 0)),
                pl.BlockSpec((1, D_EXPERT, D_MODEL), lambda b, be, fi, va: (be[b], 0, 0)),
                pl.BlockSpec((1, 1, D_MODEL), lambda b, be, fi, va: (be[b], 0, 0)),
            ],
            out_specs=pl.BlockSpec((MOE_ROWS, D_MODEL // 2), lambda b, be, fi, va: (b, 0)),
            scratch_shapes=[
                pltpu.VMEM((D_MODEL, 2 * D_EXPERT), BF16),
                pltpu.VMEM((D_EXPERT, D_MODEL), BF16),
            ],
        ),
        out_shape=jax.ShapeDtypeStruct((rows, D_MODEL // 2), jnp.int32),
        compiler_params=_params("arbitrary"),
        name="moe_experts",
    )(block_e, first, valid, xb, w1, b1, w2, b2)


def _moe_dispatch(idx, rank, counts, n):
    experts = jnp.arange(N_EXPERTS, dtype=jnp.int32)
    padded = (counts + MOE_ROWS - 1) // MOE_ROWS * MOE_ROWS
    pad_end = jnp.cumsum(padded)
    pad_start = pad_end - padded
    start_of = jnp.sum(jnp.where(idx[:, :, None] == experts, pad_start, 0), axis=-1)
    dest = start_of + rank
    nb = -(-n * TOP_K // MOE_ROWS) + N_EXPERTS
    rows = nb * MOE_ROWS
    starts = jnp.arange(nb, dtype=jnp.int32) * MOE_ROWS
    valid = (starts < pad_end[-1]).astype(jnp.int32)
    owner = lambda r: jnp.minimum(jnp.sum((pad_end[None, :] <= r[:, None]).astype(jnp.int32), axis=1),
                                  N_EXPERTS - 1)
    last_e = owner(pad_end[-1:] - 1)[0]
    block_e = jnp.where(valid == 1, owner(starts), last_e).astype(jnp.int32)
    first = jnp.concatenate([jnp.ones((1,), jnp.int32),
                             (block_e[1:] != block_e[:-1]).astype(jnp.int32)])
    return dest.T, rows, block_e, first, valid


def _sc_copy_rows(src, src_idx, dst_idx, out_rows):
    m = src_idx.shape[1]
    d = src.shape[1]
    mesh = plsc.VectorSubcoreMesh(core_axis_name="core", subcore_axis_name="subcore",
                                  num_cores=SC_CORES, num_subcores=SC_SUBCORES)

    @pl.kernel(out_type=jax.ShapeDtypeStruct((out_rows, d), src.dtype), mesh=mesh,
               scratch_types=[pltpu.VMEM((SC_CHUNK, d), src.dtype)])
    def copy_rows(src_hbm, si_hbm, di_hbm, out_hbm, buf):
        def body(si_vmem, di_vmem):
            for c in range(SC_WINDOW // SC_CHUNK):
                sl = pl.ds(c * SC_CHUNK, SC_CHUNK)
                pltpu.sync_copy(src_hbm.at[si_vmem.at[0, sl]], buf)
                pltpu.sync_copy(buf, out_hbm.at[di_vmem.at[0, sl]])

        pltpu.emit_pipeline(
            body,
            grid=(m // SC_WINDOW,),
            in_specs=[pl.BlockSpec((1, SC_WINDOW), lambda i: (0, i)),
                      pl.BlockSpec((1, SC_WINDOW), lambda i: (0, i))],
            out_specs=[],
            core_axis_name=("core", "subcore"),
            dimension_semantics=(pltpu.PARALLEL,),
        )(si_hbm, di_hbm)

    return copy_rows(src, src_idx, dst_idx)


def _tail_kernel(x1_ref, yk_ref, gate_ref, pp_ref, ps_ref, g3_ref, wg_ref, wp_ref, gf_ref,
                 yp_ref, ys_ref, *, n_p):
    gates = gate_ref[...]
    lo, hi = _unpack_bf16_pairs(yk_ref[0])
    lo, hi = lo * gates[:, 0:1], hi * gates[:, 0:1]
    for k in range(1, TOP_K):
        lo_k, hi_k = _unpack_bf16_pairs(yk_ref[k])
        lo, hi = lo + lo_k * gates[:, k:k + 1], hi + hi_k * gates[:, k:k + 1]
    moe = jnp.concatenate([lo, hi], axis=1)
    x2 = x1_ref[...] + moe
    h3 = _rms(x2, g3_ref[...])
    gate = _sigmoid(jnp.dot(h3.astype(BF16), wg_ref[...], preferred_element_type=F32))
    pe = jnp.dot(_group_pick(pp_ref, ps_ref, n_p).astype(BF16), wp_ref[...], preferred_element_type=F32)
    y = _rms(x2 + gate * pe, gf_ref[...])
    in_prompt = pl.program_id(0) < n_p // yp_ref.shape[0]

    @pl.when(in_prompt)
    def _():
        yp_ref[...] = y

    @pl.when(jnp.logical_not(in_prompt))
    def _():
        ys_ref[...] = y


def _tail(x1, yk, gates, pp, ps, g3, wg, wp, gf):
    n_p, n_s = pp.shape[0], ps.shape[0]
    n = n_p + n_s
    tm = _row_tile(n_p, n_s)
    tok = pl.BlockSpec((tm, D_MODEL), lambda i: (i, 0))
    full = lambda r, c: pl.BlockSpec((r, c), lambda i: (0, 0))
    return pl.pallas_call(
        functools.partial(_tail_kernel, n_p=n_p),
        grid=(n // tm,),
        in_specs=[tok, pl.BlockSpec((TOP_K, tm, D_MODEL // 2), lambda i: (0, i, 0)),
                  pl.BlockSpec((tm, LANES), lambda i: (i, 0)),
                  *_group_specs(tm, PLE_DIM, n_p),
                  full(1, D_MODEL), full(D_MODEL, D_MODEL), full(PLE_DIM, D_MODEL), full(1, D_MODEL)],
        out_specs=_group_specs(tm, D_MODEL, n_p),
        out_shape=[jax.ShapeDtypeStruct((n_p, D_MODEL), F32), jax.ShapeDtypeStruct((n_s, D_MODEL), F32)],
        compiler_params=_params("arbitrary"),
        name="ple_final_norm",
    )(x1, yk, gates, pp, ps, g3, wg, wp, gf)


def _lane_pad(v, offset, fill=0.0):
    out = jnp.full((1, LANES), fill, F32)
    return out.at[0, offset:offset + v.shape[0]].set(v.astype(F32))


def kernel(x_prompt, x_sample, state_delta, state_conv, p_prompt, p_sample, norm1_g, w_in, conv_w, a_log, dt_bias, dn_norm_g, w_proj_a, gm_ln_g, gm_ln_b, gm_ws, gm_bs, w_proj_b, w_out, norm2_g, router_w, router_b, moe_w1, moe_b1, moe_w2, moe_b2, norm3_g, ple_w, ple_gate_w, final_norm_g):
    bp, lp, d = x_prompt.shape
    bs, ls, _ = x_sample.shape
    depth = w_in.shape[0]
    assert depth == 1 and d == D_MODEL
    assert lp % GM_CHUNK == 0 and GM_CHUNK % ls == 0 and ls >= DN_CONV - 1
    n_p, n_s = bp * lp, bs * ls
    n = n_p + n_s
    i = 0

    xp, xs = x_prompt.reshape(n_p, d), x_sample.reshape(n_s, d)

    ab0 = QKV_W
    w = w_in[i]
    w_main = jnp.concatenate([w[:, :ab0], w[:, ab0 + 2 * DN_HEADS:]], axis=1).astype(BF16)
    w_ab = jnp.pad(w[:, ab0:ab0 + 2 * DN_HEADS], ((0, 0), (0, LANES - 2 * DN_HEADS))).astype(BF16)
    row2 = lambda v: v.reshape(1, -1).astype(F32)

    z, ab = _in_proj(xp, xs, row2(norm1_g[i]), w_main, w_ab)

    alog_p = _lane_pad(a_log[i], 0)
    dtb_p = _lane_pad(dt_bias[i], 0)
    cw = conv_w[i].astype(F32)
    ong = row2(dn_norm_g[i])
    zero_s = jnp.zeros((bp, DN_HEADS, DN_DK, DN_DV), F32)
    zero_buf = jnp.zeros((bp, DN_CONV - 1, QKV_W), F32)
    o_p, sd_p, sc_p = _delta_branch(z, ab, zero_buf, zero_s, cw, alog_p, dtb_p, ong,
                                    tok0=0, L=lp, G=2, precise=False)
    o_s, sd_s, sc_s = _delta_branch(z, ab, state_conv[i], state_delta[i], cw, alog_p, dtb_p, ong,
                                    tok0=n_p, L=ls, G=2, precise=False)

    t = GM_CHUNK
    tri = jnp.tril(jnp.ones((t, t), bool))
    ws = gm_ws[i]
    mix_p = jnp.where(tri, ws, 0.0)
    small = jnp.where(tri[:ls, :ls], ws[:, :ls, :ls], 0.0)
    mix_s = jnp.einsum('ab,gts->gatbs', jnp.eye(t // ls, dtype=F32), small).reshape(GM_GROUPS, t, t)
    mix = jnp.stack([mix_p, mix_s]).astype(BF16)
    gw = GM_WIDTH // GM_GROUPS
    bias_p = jnp.repeat(gm_bs[i].T, gw, axis=1)
    bias_s = jnp.tile(bias_p[:ls], (t // ls, 1))
    bias = jnp.stack([bias_p, bias_s]).astype(F32)
    rw = jnp.pad(router_w[i].astype(F32), ((0, 0), (0, LANES - N_EXPERTS)))
    rwh = rw.astype(BF16)
    rwl = (rw - rwh.astype(F32)).astype(BF16)
    rb = _lane_pad(router_b[i], 0, fill=-jnp.inf)
    x1, h2, gates, route, v_s, counts = _post_mix(
        o_p, o_s, z, xp, xs, row2(gm_ln_g[i]), row2(gm_ln_b[i]), mix, bias, w_proj_a[i].astype(BF16),
        w_proj_b[i].astype(BF16), w_out[i].astype(BF16), row2(norm2_g[i]), rwh, rwl, rb)
    dest_t, rows, block_e, first, valid = _moe_dispatch(
        route[:, :TOP_K], route[:, TOP_K:2 * TOP_K], counts[0, :N_EXPERTS].astype(jnp.int32), n)
    flat_dest = dest_t.reshape(1, TOP_K * n)
    tok_ids = jnp.tile(jnp.arange(n, dtype=jnp.int32), TOP_K).reshape(1, TOP_K * n)
    xb = _sc_copy_rows(h2, tok_ids, flat_dest, rows)
    yb = _moe_experts(block_e, first, valid, xb, moe_w1[i], moe_b1[i][:, None, :],
                      moe_w2[i], moe_b2[i][:, None, :])
    slots = jnp.arange(TOP_K * n, dtype=jnp.int32).reshape(1, TOP_K * n)
    yk = _sc_copy_rows(yb, flat_dest, slots, TOP_K * n).reshape(TOP_K, n, d // 2)

    y_p, y_s = _tail(x1, yk, gates, p_prompt[i].reshape(n_p, PLE_DIM), p_sample[i].reshape(n_s, PLE_DIM),
                     row2(norm3_g[i]), ple_gate_w[i].astype(BF16), ple_w[i].astype(BF16), row2(final_norm_g))

    return (y_p.reshape(bp, lp, d), y_s.reshape(bs, ls, d),
            sd_p[None], sc_p[None], sd_s[None], sc_s[None], v_s.reshape(1, bs, ls, GM_WIDTH))
```

```python
import functools
import math

import jax
import jax.numpy as jnp
from jax import lax
from jax.experimental import pallas as pl
from jax.experimental.pallas import tpu as pltpu
from jax.experimental.pallas import tpu_sc as plsc

F32 = jnp.float32
BF16 = jnp.bfloat16

D_MODEL = 1024
DN_HEADS = 8
DN_DK = 128
DN_DV = 128
DN_CONV = 4
DN_CHUNK = 64
GM_WIDTH = 1024
GM_GROUPS = 8
GM_CHUNK = 128
N_EXPERTS = 32
TOP_K = 4
D_EXPERT = 1024
SWIGLU_LIMIT = 7.0
SWIGLU_ALPHA = 1.702
PLE_DIM = 256
EPS = 1e-6
QK_W = DN_HEADS * DN_DK
V_W = DN_HEADS * DN_DV
QKV_W = 2 * QK_W + V_W

LANES = 128
MOE_ROWS = 512
VMEM_LIMIT = 56 << 20
SC_CORES, SC_SUBCORES = 2, 16
SC_WINDOW = 128
SC_TOKENS = SC_WINDOW // TOP_K
SC_CHUNK = 64


def _params(*sem):
    return pltpu.CompilerParams(dimension_semantics=sem, vmem_limit_bytes=VMEM_LIMIT)


def _dot(a, b):
    return jnp.dot(a.astype(BF16), b.astype(BF16), preferred_element_type=F32)


def _split(a):
    hi = a.astype(BF16)
    return hi, (a - hi.astype(F32)).astype(BF16)


def _dot3(a, b):
    ah, al = _split(a)
    bh, bl = _split(b)
    d = functools.partial(jnp.dot, preferred_element_type=F32)
    return d(ah, bh) + (d(ah, bl) + d(al, bh))


def _dot_nt(a, b):
    return lax.dot_general(a.astype(BF16), b.astype(BF16), (((1,), (1,)), ((), ())),
                           preferred_element_type=F32)


def _dot_tn(a, b):
    return lax.dot_general(a.astype(BF16), b.astype(BF16), (((0,), (0,)), ((), ())),
                           preferred_element_type=F32)


def _dot3_tn(a, b):
    ah, al = _split(a)
    bh, bl = _split(b)
    d = functools.partial(lax.dot_general, dimension_numbers=(((0,), (0,)), ((), ())),
                          preferred_element_type=F32)
    return d(ah, bh) + (d(ah, bl) + d(al, bh))


def _sigmoid(x):
    return 0.5 * jnp.tanh(0.5 * x) + 0.5


def _pack_bf16_pairs(x):
    bits = lax.bitcast_convert_type(x.astype(BF16).astype(F32), jnp.uint32)
    half = x.shape[1] // 2
    return lax.bitcast_convert_type((bits[:, :half] >> 16) | bits[:, half:], jnp.int32)


def _unpack_bf16_pairs(w):
    bits = lax.bitcast_convert_type(w, jnp.uint32)
    return (lax.bitcast_convert_type(bits << 16, F32),
            lax.bitcast_convert_type(bits & jnp.uint32(0xFFFF0000), F32))


def _rms(x, g):
    return x * lax.rsqrt(jnp.mean(x * x, axis=-1, keepdims=True) + EPS) * g


def _gelu(x):
    return 0.5 * x * (1.0 + lax.erf(x * (1.0 / math.sqrt(2.0))))


def _row_tile(n_p, n_s, cap=512):
    for t in (1024, 512, 256, 128):
        if t > cap:
            continue
        if n_p % t == 0 and n_s % t == 0:
            return t
    raise ValueError(f"token counts {n_p}, {n_s} must be multiples of 128")


def _group_specs(tm, width, n_p):
    pt = n_p // tm
    return [pl.BlockSpec((tm, width), lambda i, *_: (jnp.minimum(i, pt - 1), 0)),
            pl.BlockSpec((tm, width), lambda i, *_: (jnp.maximum(i - pt, 0), 0))]


def _group_pick(prompt_ref, sample_ref, n_p):
    pt = n_p // prompt_ref.shape[0]
    return jnp.where(pl.program_id(0) < pt, prompt_ref[...], sample_ref[...])


def _in_proj_kernel(xp_ref, xs_ref, g_ref, w_ref, wab_ref, z_ref, ab_ref, h_scr, *, n_p):
    @pl.when(pl.program_id(1) == 0)
    def _():
        hb = _rms(_group_pick(xp_ref, xs_ref, n_p), g_ref[...]).astype(BF16)
        h_scr[...] = hb
        ab_ref[...] = jnp.dot(hb, wab_ref[...], preferred_element_type=F32)

    z_ref[...] = jnp.dot(h_scr[...], w_ref[...], preferred_element_type=F32).astype(z_ref.dtype)


def _in_proj(xp, xs, g, w_main, w_ab):
    n_p, n_s = xp.shape[0], xs.shape[0]
    n = n_p + n_s
    tm, tn = _row_tile(n_p, n_s, cap=1024), 2048
    cols = w_main.shape[1]
    return pl.pallas_call(
        functools.partial(_in_proj_kernel, n_p=n_p),
        grid=(n // tm, cols // tn),
        in_specs=_group_specs(tm, D_MODEL, n_p) + [
            pl.BlockSpec((1, D_MODEL), lambda i, j: (0, 0)),
            pl.BlockSpec((D_MODEL, tn), lambda i, j: (0, j)),
            pl.BlockSpec((D_MODEL, LANES), lambda i, j: (0, 0)),
        ],
        out_specs=[
            pl.BlockSpec((tm, tn), lambda i, j: (i, j)),
            pl.BlockSpec((tm, LANES), lambda i, j: (i, 0)),
        ],
        out_shape=[jax.ShapeDtypeStruct((n, cols), BF16), jax.ShapeDtypeStruct((n, LANES), F32)],
        scratch_shapes=[pltpu.VMEM((tm, D_MODEL), BF16)],
        compiler_params=_params("parallel", "arbitrary"),
        name="in_proj",
    )(xp, xs, g, w_main, w_ab)


def _delta_kernel(*refs, C, G, carry, precise):
    (qkv_ref, zg_ref, ab_ref, buf_ref, s0_ref, cw_ref, alog_ref, dtb_ref, ong_ref) = refs[:9]
    refs = refs[9:]
    o_ref, snew_ref, bufnew_ref, xc_scr = refs[:4]
    H, DK, DV = DN_HEADS, DN_DK, DN_DV
    T = G * C
    dotm = _dot3 if precise else _dot
    dotm_tn = _dot3_tn if precise else _dot_tn
    halo = DN_CONV - 1
    base = 8
    cw = cw_ref[...]

    def conv(window):
        y = window(0) * cw[0:1]
        for i in range(1, DN_CONV):
            y = y + window(i) * cw[i:i + 1]
        return y

    if carry:
        s_scr = refs[4]
        c = pl.program_id(1)
        last = pl.num_programs(1) - 1

        @pl.when(c == 0)
        def _():
            xc_scr[base - halo:base, :] = buf_ref[0]
            s_scr[...] = s0_ref[0]

        xb = qkv_ref[...]
        xf = xb.astype(F32)
        xc_scr[base:base + 8, :] = xf[0:8]
        y_head = conv(lambda i: xc_scr[base - halo + i:base - halo + i + 8, :])
        ri = lax.broadcasted_iota(jnp.int32, (halo * T, T), 0)
        ci = lax.broadcasted_iota(jnp.int32, (halo * T, T), 1)
        src_row = (ri & (T - 1)) - (halo - (ri >> (T.bit_length() - 1)))
        shifted = jnp.dot(jnp.where(ci == src_row, 1.0, 0.0).astype(BF16), xb, preferred_element_type=F32)
        y = conv(lambda i: shifted[i * T:(i + 1) * T] if i < halo else xf)
        y = jnp.concatenate([y_head, y[8:]], axis=0)
        tail = xf[T - halo:T]
        xc_scr[base - halo:base, :] = tail

        @pl.when(c == last)
        def _():
            bufnew_ref[0] = tail
    else:
        ys = []
        x_new = qkv_ref[...].astype(F32)
        for g in range(G):
            xc_scr[g, base - halo:base, :] = buf_ref[g]
            xc_scr[g, base:base + C, :] = x_new[g * C:(g + 1) * C, :]
            ys.append(conv(lambda i: xc_scr[g, base - halo + i:base - halo + i + C, :]))
            bufnew_ref[g] = xc_scr[g, base + C - halo:base + C, :]
        y = jnp.concatenate(ys, axis=0) if G > 1 else ys[0]
    qkv = y * _sigmoid(y)

    ab = ab_ref[...]
    g_all = -jnp.exp(alog_ref[...]) * jax.nn.softplus(ab + dtb_ref[...])
    beta_all = _sigmoid(ab)
    shift = C.bit_length() - 1
    rt = lax.broadcasted_iota(jnp.int32, (T, T), 0)
    ct = lax.broadcasted_iota(jnp.int32, (T, T), 1)
    chunk_tril = ((rt >> shift) == (ct >> shift)) & (rt >= ct)
    gcum = _dot3(chunk_tril.astype(F32), g_all)
    gam_all = jnp.exp(gcum)

    row = lax.broadcasted_iota(jnp.int32, (C, C), 0)
    col = lax.broadcasted_iota(jnp.int32, (C, C), 1)
    incl, strict, eye = row >= col, row > col, row == col
    eyef = eye.astype(F32)
    units = [(g, h) for g in range(G) for h in range(H)]
    rows = lambda a, g: a[g * C:(g + 1) * C]

    qn, kn, vv = [], [], []
    for h in range(H):
        q = qkv[:, h * DK:(h + 1) * DK]
        k = qkv[:, QK_W + h * DK:QK_W + (h + 1) * DK]
        qn.append(q * (lax.rsqrt(jnp.sum(q * q, axis=-1, keepdims=True) + EPS) * (DK ** -0.5)))
        kn.append(k * lax.rsqrt(jnp.sum(k * k, axis=-1, keepdims=True) + EPS))
        vv.append(qkv[:, 2 * QK_W + h * DV:2 * QK_W + (h + 1) * DV])
    qb = [q.astype(BF16) for q in qn]
    kb = [k.astype(BF16) for k in kn]

    gc, bc, gl, a_low, m_intra = {}, {}, {}, {}, {}
    for u in units:
        g, h = u
        gc[u] = rows(gcum, g)[:, h:h + 1]
        bc[u] = rows(beta_all, g)[:, H + h:H + h + 1]
        gl[u] = gc[u][C - 1:C, :]
        gr = jnp.sum(jnp.where(eye, gc[u], 0.0), axis=0, keepdims=True)
        br = jnp.sum(jnp.where(eye, bc[u], 0.0), axis=0, keepdims=True)
        db = jnp.where(incl, jnp.exp(gc[u] - gr), 0.0) * br
        k = rows(kb[h], g)
        kq = _dot_nt(jnp.concatenate([k, rows(qb[h], g)], axis=0), k)
        a_low[u] = jnp.where(strict, kq[:C] * db, 0.0)
        m_intra[u] = kq[C:] * db

    t_inv = {u: eyef - a_low[u] for u in units}
    a_pow = {u: dotm(a_low[u], a_low[u]) for u in units}
    n = 2
    while n < C:
        t_inv = {u: t_inv[u] + dotm(t_inv[u], a_pow[u]) for u in units}
        n *= 2
        if n < C:
            a_pow = {u: dotm(a_pow[u], a_pow[u]) for u in units}

    u_base, wq, k_dec = {}, {}, {}
    for u in units:
        g, h = u
        gam = rows(gam_all, g)[:, h:h + 1]
        k = rows(kn[h], g)
        sol = dotm(t_inv[u], jnp.concatenate([rows(vv[h], g), gam * k], axis=1))
        u_base[u] = sol[:, :DV]
        wq[u] = jnp.concatenate([sol[:, DV:], gam * rows(qn[h], g)], axis=0)
        k_dec[u] = k * (bc[u] * jnp.exp(gl[u] - gc[u]))

    state = [s_scr[h] for h in range(H)] if carry else None
    outs = {}
    for g in range(G):
        s_in = state if carry else [s0_ref[g, h] for h in range(H)]
        wqs = [dotm(wq[(g, h)], s_in[h]) for h in range(H)]
        us = [u_base[(g, h)] - wqs[h][:C] for h in range(H)]
        for h in range(H):
            outs[(g, h)] = wqs[h][C:] + dotm(m_intra[(g, h)], us[h])
        s_out = [jnp.exp(gl[(g, h)]) * s_in[h] + dotm_tn(k_dec[(g, h)], us[h]) for h in range(H)]
        if carry:
            state = s_out
        else:
            for h in range(H):
                snew_ref[g, h] = s_out[h]

    for h in range(H):
        o = jnp.concatenate([outs[(g, h)] for g in range(G)], axis=0) if G > 1 else outs[(0, h)]
        zg = zg_ref[:, h * DV:(h + 1) * DV].astype(F32)
        o_ref[:, h * DV:(h + 1) * DV] = (_rms(o, ong_ref[...]) * (zg * _sigmoid(zg))).astype(o_ref.dtype)

    if carry:
        for h in range(H):
            s_scr[h] = state[h]

        @pl.when(c == last)
        def _():
            for h in range(H):
                snew_ref[0, h] = state[h]


def _delta_branch(z, ab, conv_buf, s0, conv_w, alog_p, dtb_p, onorm_g, *, tok0, L, G, precise):
    B = conv_buf.shape[0]
    C = math.gcd(L, DN_CHUNK)
    nc = L // C
    carry = nc > 1
    T = G * C
    blk0 = tok0 // T
    if carry:
        assert nc % G == 0
        grid = (B, nc // G)
        own_blk = lambda b, c: b * (nc // G) + c
        gs = 1
    else:
        assert B % G == 0
        grid = (B // G, 1)
        own_blk = lambda b, c: b
        gs = G
    tok = lambda b, c: (blk0 + own_blk(b, c), 0)
    seq3 = lambda b, c: (b, 0, 0)
    seq4 = lambda b, c: (b, 0, 0, 0)
    const = lambda b, c: (0, 0)
    if carry:
        scratch = [pltpu.VMEM((16, QKV_W), F32), pltpu.VMEM((DN_HEADS, DN_DK, DN_DV), F32)]
    else:
        scratch = [pltpu.VMEM((G, 8 + C, QKV_W), F32)]
    return pl.pallas_call(
        functools.partial(_delta_kernel, C=C, G=G, carry=carry, precise=precise),
        grid=grid,
        in_specs=[
            pl.BlockSpec((T, QKV_W), tok),
            pl.BlockSpec((T, V_W), lambda b, c: (blk0 + own_blk(b, c), QKV_W // V_W)),
            pl.BlockSpec((T, LANES), tok),
            pl.BlockSpec((gs, DN_CONV - 1, QKV_W), seq3),
            pl.BlockSpec((gs, DN_HEADS, DN_DK, DN_DV), seq4),
            pl.BlockSpec((DN_CONV, QKV_W), const),
            pl.BlockSpec((1, LANES), const),
            pl.BlockSpec((1, LANES), const),
            pl.BlockSpec((1, DN_DV), const),
        ],
        out_specs=[
            pl.BlockSpec((T, V_W), lambda b, c: (own_blk(b, c), 0)),
            pl.BlockSpec((gs, DN_HEADS, DN_DK, DN_DV), seq4),
            pl.BlockSpec((gs, DN_CONV - 1, QKV_W), seq3),
        ],
        out_shape=[
            jax.ShapeDtypeStruct((B * L, V_W), BF16),
            jax.ShapeDtypeStruct(s0.shape, F32),
            jax.ShapeDtypeStruct(conv_buf.shape, F32),
        ],
        scratch_shapes=scratch,
        compiler_params=_params("parallel", "arbitrary"),
        name=f"delta_rule_c{C}",
    )(z, z, ab, conv_buf, s0, conv_w, alog_p, dtb_p, onorm_g)


def _post_mix_kernel(op_ref, os_ref, gu_ref, gv_ref, ma_ref, mb_ref, xp_ref, xs_ref, lng_ref, lnb_ref,
                     mix_ref, bias_ref, wa_ref, wb_ref, wo_ref, g2_ref, rwh_ref, rwl_ref, rb_ref, tri_ref,
                     x1_ref, h2_ref, gate_ref, idx_ref, v_ref, cnt_ref, us_scr, cnt_scr, *, n_p):
    u = _gelu(gu_ref[...].astype(F32))
    a = _gelu(gv_ref[...].astype(F32))
    ac = a - jnp.mean(a, axis=-1, keepdims=True)
    v = ac * lax.rsqrt(jnp.mean(ac * ac, axis=-1, keepdims=True) + EPS) * lng_ref[...] + lnb_ref[...]
    v_ref[...] = v
    vb = v.astype(BF16)
    gw = GM_WIDTH // GM_GROUPS
    for c in range(u.shape[0] // GM_CHUNK):
        rs = slice(c * GM_CHUNK, (c + 1) * GM_CHUNK)
        for g in range(GM_GROUPS):
            sl = slice(g * gw, (g + 1) * gw)
            s = jnp.dot(mix_ref[0, g], vb[rs, sl], preferred_element_type=F32) + bias_ref[0, :, sl]
            us_scr[rs, sl] = (u[rs, sl] * s).astype(BF16)

    ya = jnp.dot(_group_pick(op_ref, os_ref, n_p).astype(BF16), wa_ref[...], preferred_element_type=F32)
    yb = jnp.dot(us_scr[...], wb_ref[...], preferred_element_type=F32)
    mixed = _sigmoid(ma_ref[...].astype(F32)) * ya + _sigmoid(mb_ref[...].astype(F32)) * yb
    x1 = _group_pick(xp_ref, xs_ref, n_p) + jnp.dot(mixed.astype(BF16), wo_ref[...], preferred_element_type=F32)
    x1_ref[...] = x1
    h2 = _rms(x1, g2_ref[...])
    h2_ref[...] = _pack_bf16_pairs(h2)
    hh, hl = _split(h2)
    d = functools.partial(jnp.dot, preferred_element_type=F32)
    logits = d(hh, rwh_ref[...]) + (d(hh, rwl_ref[...]) + d(hl, rwh_ref[...])) + rb_ref[...]
    lane = lax.broadcasted_iota(jnp.int32, logits.shape, 1).astype(F32)
    vals, idxs = [], []
    for _ in range(TOP_K):
        m = jnp.max(logits, axis=-1, keepdims=True)
        i = jnp.min(jnp.where(logits == m, lane, float(LANES)), axis=-1, keepdims=True)
        vals.append(m)
        idxs.append(i)
        logits = jnp.where(lane == i, -jnp.inf, logits)
    es = [jnp.exp(v - vals[0]) for v in vals]
    tot = es[0]
    for e in es[1:]:
        tot = tot + e

    @pl.when(pl.program_id(0) == 0)
    def _():
        cnt_scr[...] = jnp.zeros(cnt_scr.shape, F32)

    onehot = (lane == idxs[0]).astype(F32)
    for k in range(1, TOP_K):
        onehot = onehot + (lane == idxs[k]).astype(F32)
    before = jnp.dot(tri_ref[...], onehot.astype(BF16), preferred_element_type=F32) + cnt_scr[...]
    cnt_scr[...] = cnt_scr[...] + jnp.sum(onehot, axis=0, keepdims=True)
    cnt_ref[...] = cnt_scr[...]

    gates = jnp.zeros(logits.shape, F32)
    route = jnp.zeros(logits.shape, F32)
    for k in range(TOP_K):
        rank = jnp.sum(jnp.where(lane == idxs[k], before, 0.0), axis=-1, keepdims=True)
        gates = jnp.where(lane == float(k), es[k] / tot, gates)
        route = jnp.where(lane == float(k), idxs[k], route)
        route = jnp.where(lane == float(TOP_K + k), rank, route)
    gate_ref[...] = gates
    idx_ref[...] = route.astype(jnp.int32)


def _post_mix(o_p, o_s, z, xp, xs, ln_g, ln_b, mix, bias, wa, wb, wo, g2, rwh, rwl, rb):
    n_p, n_s = xp.shape[0], xs.shape[0]
    n = n_p + n_s
    tm = _row_tile(n_p, n_s)
    pt = n_p // tm
    tri = jnp.tril(jnp.ones((tm, tm), BF16), k=-1)
    grp = lambda i: jnp.where(i < pt, 0, 1)
    once = pl.Buffered(1)
    tok = pl.BlockSpec((tm, D_MODEL), lambda i: (i, 0))
    zcol = lambda c: pl.BlockSpec((tm, D_MODEL), lambda i: (i, c))
    full = lambda r, c: pl.BlockSpec((r, c), lambda i: (0, 0), pipeline_mode=once)
    narrow = pl.BlockSpec((tm, LANES), lambda i: (i, 0))
    op_spec, os_spec = _group_specs(tm, V_W, n_p)
    xp_spec, xs_spec = _group_specs(tm, D_MODEL, n_p)
    return pl.pallas_call(
        functools.partial(_post_mix_kernel, n_p=n_p),
        grid=(n // tm,),
        in_specs=[
            op_spec, os_spec, zcol(4), zcol(5), zcol(6), zcol(7), xp_spec, xs_spec,
            full(1, GM_WIDTH), full(1, GM_WIDTH),
            pl.BlockSpec((1, GM_GROUPS, GM_CHUNK, GM_CHUNK), lambda i: (grp(i), 0, 0, 0)),
            pl.BlockSpec((1, GM_CHUNK, GM_WIDTH), lambda i: (grp(i), 0, 0)),
            full(V_W, D_MODEL), full(GM_WIDTH, D_MODEL), full(D_MODEL, D_MODEL), full(1, D_MODEL),
            full(D_MODEL, LANES), full(D_MODEL, LANES), full(1, LANES), full(tm, tm),
        ],
        out_specs=[tok, pl.BlockSpec((tm, D_MODEL // 2), lambda i: (i, 0)), narrow, narrow,
                   pl.BlockSpec((tm, GM_WIDTH), lambda i: (jnp.maximum(i - pt, 0), 0)),
                   pl.BlockSpec((1, LANES), lambda i: (0, 0))],
        out_shape=[
            jax.ShapeDtypeStruct((n, D_MODEL), F32),
            jax.ShapeDtypeStruct((n, D_MODEL // 2), jnp.int32),
            jax.ShapeDtypeStruct((n, LANES), F32),
            jax.ShapeDtypeStruct((n, LANES), jnp.int32),
            jax.ShapeDtypeStruct((n_s, GM_WIDTH), F32),
            jax.ShapeDtypeStruct((1, LANES), F32),
        ],
        scratch_shapes=[pltpu.VMEM((tm, GM_WIDTH), BF16), pltpu.VMEM((1, LANES), F32)],
        compiler_params=_params("arbitrary"),
        name="chunk_mlp_post_mix_router",
    )(o_p, o_s, z, z, z, z, xp, xs, ln_g, ln_b, mix, bias, wa, wb, wo, g2, rwh, rwl, rb, tri)


def _moe_kernel(be_ref, first_ref, valid_ref, x_ref, w1_ref, b1_ref, w2_ref, b2_ref, y_ref,
                w1_scr, w2_scr):
    b = pl.program_id(0)

    @pl.when(first_ref[b] == 1)
    def _():
        w1_scr[...] = w1_ref[0].astype(BF16)
        w2_scr[...] = w2_ref[0].astype(BF16)

    @pl.when(valid_ref[b] == 1)
    def _():
        x = jnp.concatenate(_unpack_bf16_pairs(x_ref[...]), axis=1).astype(BF16)
        hid = jnp.dot(x, w1_scr[...], preferred_element_type=F32) + b1_ref[0]
        gate = jnp.minimum(hid[:, :D_EXPERT], SWIGLU_LIMIT)
        up = jnp.clip(hid[:, D_EXPERT:], -SWIGLU_LIMIT, SWIGLU_LIMIT)
        act = gate * _sigmoid(SWIGLU_ALPHA * gate) * (up + 1.0)
        y = jnp.dot(act.astype(BF16), w2_scr[...], preferred_element_type=F32) + b2_ref[0]
        y_ref[...] = _pack_bf16_pairs(y)

    @pl.when(valid_ref[b] == 0)
    def _():
        y_ref[...] = jnp.zeros(y_ref.shape, jnp.int32)


def _moe_experts(block_e, first, valid, xb, w1, b1, w2, b2):
    rows = xb.shape[0]
    nb = rows // MOE_ROWS
    return pl.pallas_call(
        _moe_kernel,
        grid_spec=pltpu.PrefetchScalarGridSpec(
            num_scalar_prefetch=3,
            grid=(nb,),
            in_specs=[
                pl.BlockSpec((MOE_ROWS, D_MODEL // 2), lambda b, be, fi, va: (b, 0)),
                pl.BlockSpec((1, D_MODEL, 2 * D_EXPERT), lambda b, be, fi, va: (be[b], 0, 0)),
                pl.BlockSpec((1, 1, 2 * D_EXPERT), lambda b, be, fi, va: (be[b], 0, 0)),
                pl.BlockSpec((1, D_EXPERT, D_MODEL), lambda b, be, fi, va: (be[b], 0, 0)),
                pl.BlockSpec((1, 1, D_MODEL), lambda b, be, fi, va: (be[b], 0, 0)),
            ],
            out_specs=pl.BlockSpec((MOE_ROWS, D_MODEL // 2), lambda b, be, fi, va: (b, 0)),
            scratch_shapes=[
                pltpu.VMEM((D_MODEL, 2 * D_EXPERT), BF16),
                pltpu.VMEM((D_EXPERT, D_MODEL), BF16),
            ],
        ),
        out_shape=jax.ShapeDtypeStruct((rows, D_MODEL // 2), jnp.int32),
        compiler_params=_params("arbitrary"),
        name="moe_experts",
    )(block_e, first, valid, xb, w1, b1, w2, b2)


def _moe_dispatch(idx, rank, counts, n):
    experts = jnp.arange(N_EXPERTS, dtype=jnp.int32)
    padded = (counts + MOE_ROWS - 1) // MOE_ROWS * MOE_ROWS
    pad_end = jnp.cumsum(padded)
    pad_start = pad_end - padded
    start_of = jnp.sum(jnp.where(idx[:, :, None] == experts, pad_start, 0), axis=-1)
    dest = start_of + rank
    nb = -(-n * TOP_K // MOE_ROWS) + N_EXPERTS
    rows = nb * MOE_ROWS
    starts = jnp.arange(nb, dtype=jnp.int32) * MOE_ROWS
    valid = (starts < pad_end[-1]).astype(jnp.int32)
    owner = lambda r: jnp.minimum(jnp.sum((pad_end[None, :] <= r[:, None]).astype(jnp.int32), axis=1),
                                  N_EXPERTS - 1)
    last_e = owner(pad_end[-1:] - 1)[0]
    block_e = jnp.where(valid == 1, owner(starts), last_e).astype(jnp.int32)
    first = jnp.concatenate([jnp.ones((1,), jnp.int32),
                             (block_e[1:] != block_e[:-1]).astype(jnp.int32)])
    return dest.T, rows, block_e, first, valid


def _sc_scatter_rows(src, dest_t, out_rows):
    n, d = src.shape
    nk = dest_t.shape[0]
    assert nk * SC_TOKENS == SC_WINDOW
    idx = dest_t.reshape(nk, n // SC_TOKENS, SC_TOKENS).transpose(1, 0, 2).reshape(n // SC_TOKENS, SC_WINDOW)
    mesh = plsc.VectorSubcoreMesh(core_axis_name="core", subcore_axis_name="subcore",
                                  num_cores=SC_CORES, num_subcores=SC_SUBCORES)

    @pl.kernel(out_type=jax.ShapeDtypeStruct((out_rows, d), src.dtype), mesh=mesh, scratch_types=[])
    def scatter_rows(src_hbm, di_hbm, out_hbm):
        def body(x_vmem, di_vmem):
            for k in range(nk):
                pltpu.sync_copy(x_vmem, out_hbm.at[di_vmem.at[0, pl.ds(k * SC_TOKENS, SC_TOKENS)]])

        pltpu.emit_pipeline(
            body,
            grid=(n // SC_TOKENS,),
            in_specs=[pl.BlockSpec((SC_TOKENS, d), lambda i: (i, 0)),
                      pl.BlockSpec((1, SC_WINDOW), lambda i: (i, 0))],
            out_specs=[],
            core_axis_name=("core", "subcore"),
            dimension_semantics=(pltpu.PARALLEL,),
        )(src_hbm, di_hbm)

    return scatter_rows(src, idx)


def _sc_gather_rows(src, idx):
    m = idx.shape[0]
    d = src.shape[1]
    idx_rows = jnp.pad(idx.reshape(m // SC_CHUNK, SC_CHUNK), ((0, 0), (0, SC_WINDOW - SC_CHUNK)))
    mesh = plsc.VectorSubcoreMesh(core_axis_name="core", subcore_axis_name="subcore",
                                  num_cores=SC_CORES, num_subcores=SC_SUBCORES)

    @pl.kernel(out_type=jax.ShapeDtypeStruct((m, d), src.dtype), mesh=mesh, scratch_types=[])
    def gather_rows(src_hbm, si_hbm, out_hbm):
        def body(si_vmem, o_vmem):
            pltpu.sync_copy(src_hbm.at[si_vmem.at[0, pl.ds(0, SC_CHUNK)]], o_vmem)

        pltpu.emit_pipeline(
            body,
            grid=(m // SC_CHUNK,),
            in_specs=[pl.BlockSpec((1, SC_WINDOW), lambda i: (i, 0))],
            out_specs=[pl.BlockSpec((SC_CHUNK, d), lambda i: (i, 0))],
            core_axis_name=("core", "subcore"),
            dimension_semantics=(pltpu.PARALLEL,),
        )(si_hbm, out_hbm)

    return gather_rows(src, idx_rows)


def _tail_kernel(x1_ref, yk_ref, gate_ref, pp_ref, ps_ref, g3_ref, wg_ref, wp_ref, gf_ref,
                 yp_ref, ys_ref, *, n_p):
    gates = gate_ref[...]
    lo, hi = _unpack_bf16_pairs(yk_ref[0])
    lo, hi = lo * gates[:, 0:1], hi * gates[:, 0:1]
    for k in range(1, TOP_K):
        lo_k, hi_k = _unpack_bf16_pairs(yk_ref[k])
        lo, hi = lo + lo_k * gates[:, k:k + 1], hi + hi_k * gates[:, k:k + 1]
    moe = jnp.concatenate([lo, hi], axis=1)
    x2 = x1_ref[...] + moe
    h3 = _rms(x2, g3_ref[...])
    gate = _sigmoid(jnp.dot(h3.astype(BF16), wg_ref[...], preferred_element_type=F32))
    pe = jnp.dot(_group_pick(pp_ref, ps_ref, n_p).astype(BF16), wp_ref[...], preferred_element_type=F32)
    y = _rms(x2 + gate * pe, gf_ref[...])
    in_prompt = pl.program_id(0) < n_p // yp_ref.shape[0]

    @pl.when(in_prompt)
    def _():
        yp_ref[...] = y

    @pl.when(jnp.logical_not(in_prompt))
    def _():
        ys_ref[...] = y


def _tail(x1, yk, gates, pp, ps, g3, wg, wp, gf):
    n_p, n_s = pp.shape[0], ps.shape[0]
    n = n_p + n_s
    tm = _row_tile(n_p, n_s)
    tok = pl.BlockSpec((tm, D_MODEL), lambda i: (i, 0))
    full = lambda r, c: pl.BlockSpec((r, c), lambda i: (0, 0))
    return pl.pallas_call(
        functools.partial(_tail_kernel, n_p=n_p),
        grid=(n // tm,),
        in_specs=[tok, pl.BlockSpec((TOP_K, tm, D_MODEL // 2), lambda i: (0, i, 0)),
                  pl.BlockSpec((tm, LANES), lambda i: (i, 0)),
                  *_group_specs(tm, PLE_DIM, n_p),
                  full(1, D_MODEL), full(D_MODEL, D_MODEL), full(PLE_DIM, D_MODEL), full(1, D_MODEL)],
        out_specs=_group_specs(tm, D_MODEL, n_p),
        out_shape=[jax.ShapeDtypeStruct((n_p, D_MODEL), F32), jax.ShapeDtypeStruct((n_s, D_MODEL), F32)],
        compiler_params=_params("arbitrary"),
        name="ple_final_norm",
    )(x1, yk, gates, pp, ps, g3, wg, wp, gf)


def _lane_pad(v, offset, fill=0.0):
    out = jnp.full((1, LANES), fill, F32)
    return out.at[0, offset:offset + v.shape[0]].set(v.astype(F32))


def kernel(x_prompt, x_sample, state_delta, state_conv, p_prompt, p_sample, norm1_g, w_in, conv_w, a_log, dt_bias, dn_norm_g, w_proj_a, gm_ln_g, gm_ln_b, gm_ws, gm_bs, w_proj_b, w_out, norm2_g, router_w, router_b, moe_w1, moe_b1, moe_w2, moe_b2, norm3_g, ple_w, ple_gate_w, final_norm_g):
    bp, lp, d = x_prompt.shape
    bs, ls, _ = x_sample.shape
    depth = w_in.shape[0]
    assert depth == 1 and d == D_MODEL
    assert lp % GM_CHUNK == 0 and GM_CHUNK % ls == 0 and ls >= DN_CONV - 1
    n_p, n_s = bp * lp, bs * ls
    n = n_p + n_s
    i = 0

    xp, xs = x_prompt.reshape(n_p, d), x_sample.reshape(n_s, d)

    ab0 = QKV_W
    w = w_in[i]
    w_main = jnp.concatenate([w[:, :ab0], w[:, ab0 + 2 * DN_HEADS:]], axis=1).astype(BF16)
    w_ab = jnp.pad(w[:, ab0:ab0 + 2 * DN_HEADS], ((0, 0), (0, LANES - 2 * DN_HEADS))).astype(BF16)
    row2 = lambda v: v.reshape(1, -1).astype(F32)

    z, ab = _in_proj(xp, xs, row2(norm1_g[i]), w_main, w_ab)

    alog_p = _lane_pad(a_log[i], 0)
    dtb_p = _lane_pad(dt_bias[i], 0)
    cw = conv_w[i].astype(F32)
    ong = row2(dn_norm_g[i])
    zero_s = jnp.zeros((bp, DN_HEADS, DN_DK, DN_DV), F32)
    zero_buf = jnp.zeros((bp, DN_CONV - 1, QKV_W), F32)
    o_p, sd_p, sc_p = _delta_branch(z, ab, zero_buf, zero_s, cw, alog_p, dtb_p, ong,
                                    tok0=0, L=lp, G=4, precise=False)
    o_s, sd_s, sc_s = _delta_branch(z, ab, state_conv[i], state_delta[i], cw, alog_p, dtb_p, ong,
                                    tok0=n_p, L=ls, G=8, precise=False)

    t = GM_CHUNK
    tri = jnp.tril(jnp.ones((t, t), bool))
    ws = gm_ws[i]
    mix_p = jnp.where(tri, ws, 0.0)
    small = jnp.where(tri[:ls, :ls], ws[:, :ls, :ls], 0.0)
    mix_s = jnp.einsum('ab,gts->gatbs', jnp.eye(t // ls, dtype=F32), small).reshape(GM_GROUPS, t, t)
    mix = jnp.stack([mix_p, mix_s]).astype(BF16)
    gw = GM_WIDTH // GM_GROUPS
    bias_p = jnp.repeat(gm_bs[i].T, gw, axis=1)
    bias_s = jnp.tile(bias_p[:ls], (t // ls, 1))
    bias = jnp.stack([bias_p, bias_s]).astype(F32)
    rw = jnp.pad(router_w[i].astype(F32), ((0, 0), (0, LANES - N_EXPERTS)))
    rwh = rw.astype(BF16)
    rwl = (rw - rwh.astype(F32)).astype(BF16)
    rb = _lane_pad(router_b[i], 0, fill=-jnp.inf)
    x1, h2, gates, route, v_s, counts = _post_mix(
        o_p, o_s, z, xp, xs, row2(gm_ln_g[i]), row2(gm_ln_b[i]), mix, bias, w_proj_a[i].astype(BF16),
        w_proj_b[i].astype(BF16), w_out[i].astype(BF16), row2(norm2_g[i]), rwh, rwl, rb)
    dest_t, rows, block_e, first, valid = _moe_dispatch(
        route[:, :TOP_K], route[:, TOP_K:2 * TOP_K], counts[0, :N_EXPERTS].astype(jnp.int32), n)
    xb = _sc_scatter_rows(h2, dest_t, rows)
    yb = _moe_experts(block_e, first, valid, xb, moe_w1[i], moe_b1[i][:, None, :],
                      moe_w2[i], moe_b2[i][:, None, :])
    yk = _sc_gather_rows(yb, dest_t.reshape(TOP_K * n)).reshape(TOP_K, n, d // 2)

    y_p, y_s = _tail(x1, yk, gates, p_prompt[i].reshape(n_p, PLE_DIM), p_sample[i].reshape(n_s, PLE_DIM),
                     row2(norm3_g[i]), ple_gate_w[i].astype(BF16), ple_w[i].astype(BF16), row2(final_norm_g))

    return (y_p.reshape(bp, lp, d), y_s.reshape(bs, ls, d),
            sd_p[None], sc_p[None], sd_s[None], sc_s[None], v_s.reshape(1, bs, ls, GM_WIDTH))
```

```python
import functools
import math

import jax
import jax.numpy as jnp
from jax import lax
from jax.experimental import pallas as pl
from jax.experimental.pallas import tpu as pltpu
from jax.experimental.pallas import tpu_sc as plsc

F32 = jnp.float32
BF16 = jnp.bfloat16

D_MODEL = 1024
DN_HEADS = 8
DN_DK = 128
DN_DV = 128
DN_CONV = 4
DN_CHUNK = 64
GM_WIDTH = 1024
GM_GROUPS = 8
GM_CHUNK = 128
N_EXPERTS = 32
TOP_K = 4
D_EXPERT = 1024
SWIGLU_LIMIT = 7.0
SWIGLU_ALPHA = 1.702
PLE_DIM = 256
EPS = 1e-6
QK_W = DN_HEADS * DN_DK
V_W = DN_HEADS * DN_DV
QKV_W = 2 * QK_W + V_W

LANES = 128
MOE_ROWS = 512
VMEM_LIMIT = 56 << 20
SC_CORES, SC_SUBCORES = 2, 16
SC_WINDOW = 128
SC_TOKENS = SC_WINDOW // TOP_K
SC_CHUNK = 64


def _params(*sem):
    return pltpu.CompilerParams(dimension_semantics=sem, vmem_limit_bytes=VMEM_LIMIT)


def _dot(a, b):
    return jnp.dot(a.astype(BF16), b.astype(BF16), preferred_element_type=F32)


def _split(a):
    hi = a.astype(BF16)
    return hi, (a - hi.astype(F32)).astype(BF16)


def _dot3(a, b):
    ah, al = _split(a)
    bh, bl = _split(b)
    d = functools.partial(jnp.dot, preferred_element_type=F32)
    return d(ah, bh) + (d(ah, bl) + d(al, bh))


def _dot_nt(a, b):
    return lax.dot_general(a.astype(BF16), b.astype(BF16), (((1,), (1,)), ((), ())),
                           preferred_element_type=F32)


def _dot_tn(a, b):
    return lax.dot_general(a.astype(BF16), b.astype(BF16), (((0,), (0,)), ((), ())),
                           preferred_element_type=F32)


def _dot3_tn(a, b):
    ah, al = _split(a)
    bh, bl = _split(b)
    d = functools.partial(lax.dot_general, dimension_numbers=(((0,), (0,)), ((), ())),
                          preferred_element_type=F32)
    return d(ah, bh) + (d(ah, bl) + d(al, bh))


def _sigmoid(x):
    return 0.5 * jnp.tanh(0.5 * x) + 0.5


def _pack_bf16_pairs(x):
    bits = lax.bitcast_convert_type(x.astype(BF16).astype(F32), jnp.uint32)
    half = x.shape[1] // 2
    return lax.bitcast_convert_type((bits[:, :half] >> 16) | bits[:, half:], jnp.int32)


def _unpack_bf16_pairs(w):
    bits = lax.bitcast_convert_type(w, jnp.uint32)
    return (lax.bitcast_convert_type(bits << 16, F32),
            lax.bitcast_convert_type(bits & jnp.uint32(0xFFFF0000), F32))


def _rms(x, g):
    return x * lax.rsqrt(jnp.mean(x * x, axis=-1, keepdims=True) + EPS) * g


def _gelu(x):
    return 0.5 * x * (1.0 + lax.erf(x * (1.0 / math.sqrt(2.0))))


def _row_tile(n_p, n_s, cap=512):
    for t in (1024, 512, 256, 128):
        if t > cap:
            continue
        if n_p % t == 0 and n_s % t == 0:
            return t
    raise ValueError(f"token counts {n_p}, {n_s} must be multiples of 128")


def _group_specs(tm, width, n_p):
    pt = n_p // tm
    return [pl.BlockSpec((tm, width), lambda i, *_: (jnp.minimum(i, pt - 1), 0)),
            pl.BlockSpec((tm, width), lambda i, *_: (jnp.maximum(i - pt, 0), 0))]


def _group_pick(prompt_ref, sample_ref, n_p):
    pt = n_p // prompt_ref.shape[0]
    return jnp.where(pl.program_id(0) < pt, prompt_ref[...], sample_ref[...])


def _in_proj_kernel(xp_ref, xs_ref, g_ref, w_ref, wab_ref, z_ref, ab_ref, h_scr, *, n_p):
    @pl.when(pl.program_id(1) == 0)
    def _():
        hb = _rms(_group_pick(xp_ref, xs_ref, n_p), g_ref[...]).astype(BF16)
        h_scr[...] = hb
        ab_ref[...] = jnp.dot(hb, wab_ref[...], preferred_element_type=F32)

    z_ref[...] = jnp.dot(h_scr[...], w_ref[...], preferred_element_type=F32).astype(z_ref.dtype)


def _in_proj(xp, xs, g, w_main, w_ab):
    n_p, n_s = xp.shape[0], xs.shape[0]
    n = n_p + n_s
    tm, tn = _row_tile(n_p, n_s, cap=1024), 2048
    cols = w_main.shape[1]
    return pl.pallas_call(
        functools.partial(_in_proj_kernel, n_p=n_p),
        grid=(n // tm, cols // tn),
        in_specs=_group_specs(tm, D_MODEL, n_p) + [
            pl.BlockSpec((1, D_MODEL), lambda i, j: (0, 0)),
            pl.BlockSpec((D_MODEL, tn), lambda i, j: (0, j)),
            pl.BlockSpec((D_MODEL, LANES), lambda i, j: (0, 0)),
        ],
        out_specs=[
            pl.BlockSpec((tm, tn), lambda i, j: (i, j)),
            pl.BlockSpec((tm, LANES), lambda i, j: (i, 0)),
        ],
        out_shape=[jax.ShapeDtypeStruct((n, cols), BF16), jax.ShapeDtypeStruct((n, LANES), F32)],
        scratch_shapes=[pltpu.VMEM((tm, D_MODEL), BF16)],
        compiler_params=_params("parallel", "arbitrary"),
        name="in_proj",
    )(xp, xs, g, w_main, w_ab)


def _delta_kernel(*refs, C, G, carry, precise):
    (qkv_ref, zg_ref, ab_ref, buf_ref, s0_ref, cw_ref, alog_ref, dtb_ref, ong_ref) = refs[:9]
    refs = refs[9:]
    o_ref, snew_ref, bufnew_ref, xc_scr = refs[:4]
    H, DK, DV = DN_HEADS, DN_DK, DN_DV
    T = G * C
    dotm = _dot3 if precise else _dot
    dotm_tn = _dot3_tn if precise else _dot_tn
    halo = DN_CONV - 1
    base = 8
    cw = cw_ref[...]

    def conv(window):
        y = window(0) * cw[0:1]
        for i in range(1, DN_CONV):
            y = y + window(i) * cw[i:i + 1]
        return y

    if carry:
        s_scr = refs[4]
        c = pl.program_id(1)
        last = pl.num_programs(1) - 1

        @pl.when(c == 0)
        def _():
            xc_scr[base - halo:base, :] = buf_ref[0]
            s_scr[...] = s0_ref[0]

        xb = qkv_ref[...]
        xf = xb.astype(F32)
        xc_scr[base:base + 8, :] = xf[0:8]
        y_head = conv(lambda i: xc_scr[base - halo + i:base - halo + i + 8, :])
        ri = lax.broadcasted_iota(jnp.int32, (halo * T, T), 0)
        ci = lax.broadcasted_iota(jnp.int32, (halo * T, T), 1)
        src_row = (ri & (T - 1)) - (halo - (ri >> (T.bit_length() - 1)))
        shifted = jnp.dot(jnp.where(ci == src_row, 1.0, 0.0).astype(BF16), xb, preferred_element_type=F32)
        y = conv(lambda i: shifted[i * T:(i + 1) * T] if i < halo else xf)
        y = jnp.concatenate([y_head, y[8:]], axis=0)
        tail = xf[T - halo:T]
        xc_scr[base - halo:base, :] = tail

        @pl.when(c == last)
        def _():
            bufnew_ref[0] = tail
    else:
        ys = []
        x_new = qkv_ref[...].astype(F32)
        for g in range(G):
            xc_scr[g, base - halo:base, :] = buf_ref[g]
            xc_scr[g, base:base + C, :] = x_new[g * C:(g + 1) * C, :]
            ys.append(conv(lambda i: xc_scr[g, base - halo + i:base - halo + i + C, :]))
            bufnew_ref[g] = xc_scr[g, base + C - halo:base + C, :]
        y = jnp.concatenate(ys, axis=0) if G > 1 else ys[0]
    qkv = y * _sigmoid(y)

    ab = ab_ref[...]
    g_all = -jnp.exp(alog_ref[...]) * jax.nn.softplus(ab + dtb_ref[...])
    beta_all = _sigmoid(ab)
    shift = C.bit_length() - 1
    rt = lax.broadcasted_iota(jnp.int32, (T, T), 0)
    ct = lax.broadcasted_iota(jnp.int32, (T, T), 1)
    chunk_tril = ((rt >> shift) == (ct >> shift)) & (rt >= ct)
    gcum = _dot3(chunk_tril.astype(F32), g_all)
    gam_all = jnp.exp(gcum)

    row = lax.broadcasted_iota(jnp.int32, (C, C), 0)
    col = lax.broadcasted_iota(jnp.int32, (C, C), 1)
    incl, strict, eye = row >= col, row > col, row == col
    eyef = eye.astype(F32)
    units = [(g, h) for g in range(G) for h in range(H)]
    rows = lambda a, g: a[g * C:(g + 1) * C]

    qn, kn, vv = [], [], []
    for h in range(H):
        q = qkv[:, h * DK:(h + 1) * DK]
        k = qkv[:, QK_W + h * DK:QK_W + (h + 1) * DK]
        qn.append(q * (lax.rsqrt(jnp.sum(q * q, axis=-1, keepdims=True) + EPS) * (DK ** -0.5)))
        kn.append(k * lax.rsqrt(jnp.sum(k * k, axis=-1, keepdims=True) + EPS))
        vv.append(qkv[:, 2 * QK_W + h * DV:2 * QK_W + (h + 1) * DV])
    qb = [q.astype(BF16) for q in qn]
    kb = [k.astype(BF16) for k in kn]

    gc, bc, gl, a_low, m_intra = {}, {}, {}, {}, {}
    for u in units:
        g, h = u
        gc[u] = rows(gcum, g)[:, h:h + 1]
        bc[u] = rows(beta_all, g)[:, H + h:H + h + 1]
        gl[u] = gc[u][C - 1:C, :]
        gr = jnp.sum(jnp.where(eye, gc[u], 0.0), axis=0, keepdims=True)
        br = jnp.sum(jnp.where(eye, bc[u], 0.0), axis=0, keepdims=True)
        db = jnp.where(incl, jnp.exp(gc[u] - gr), 0.0) * br
        k = rows(kb[h], g)
        kq = _dot_nt(jnp.concatenate([k, rows(qb[h], g)], axis=0), k)
        a_low[u] = jnp.where(strict, kq[:C] * db, 0.0)
        m_intra[u] = kq[C:] * db

    t_inv = {u: eyef - a_low[u] for u in units}
    a_pow = {u: dotm(a_low[u], a_low[u]) for u in units}
    n = 2
    while n < C:
        t_inv = {u: t_inv[u] + dotm(t_inv[u], a_pow[u]) for u in units}
        n *= 2
        if n < C:
            a_pow = {u: dotm(a_pow[u], a_pow[u]) for u in units}

    u_base, wq, k_dec = {}, {}, {}
    for u in units:
        g, h = u
        gam = rows(gam_all, g)[:, h:h + 1]
        k = rows(kn[h], g)
        sol = dotm(t_inv[u], jnp.concatenate([rows(vv[h], g), gam * k], axis=1))
        u_base[u] = sol[:, :DV]
        wq[u] = jnp.concatenate([sol[:, DV:], gam * rows(qn[h], g)], axis=0)
        k_dec[u] = k * (bc[u] * jnp.exp(gl[u] - gc[u]))

    state = [s_scr[h] for h in range(H)] if carry else None
    outs = {}
    for g in range(G):
        s_in = state if carry else [s0_ref[g, h] for h in range(H)]
        wqs = [dotm(wq[(g, h)], s_in[h]) for h in range(H)]
        us = [u_base[(g, h)] - wqs[h][:C] for h in range(H)]
        for h in range(H):
            outs[(g, h)] = wqs[h][C:] + dotm(m_intra[(g, h)], us[h])
        s_out = [jnp.exp(gl[(g, h)]) * s_in[h] + dotm_tn(k_dec[(g, h)], us[h]) for h in range(H)]
        if carry:
            state = s_out
        else:
            for h in range(H):
                snew_ref[g, h] = s_out[h]

    for h in range(H):
        o = jnp.concatenate([outs[(g, h)] for g in range(G)], axis=0) if G > 1 else outs[(0, h)]
        zg = zg_ref[:, h * DV:(h + 1) * DV].astype(F32)
        o_ref[:, h * DV:(h + 1) * DV] = (_rms(o, ong_ref[...]) * (zg * _sigmoid(zg))).astype(o_ref.dtype)

    if carry:
        for h in range(H):
            s_scr[h] = state[h]

        @pl.when(c == last)
        def _():
            for h in range(H):
                snew_ref[0, h] = state[h]


def _delta_branch(z, ab, conv_buf, s0, conv_w, alog_p, dtb_p, onorm_g, *, tok0, L, G, precise):
    B = conv_buf.shape[0]
    C = math.gcd(L, DN_CHUNK)
    nc = L // C
    carry = nc > 1
    T = G * C
    blk0 = tok0 // T
    if carry:
        assert nc % G == 0
        grid = (B, nc // G)
        own_blk = lambda b, c: b * (nc // G) + c
        gs = 1
    else:
        assert B % G == 0
        grid = (B // G, 1)
        own_blk = lambda b, c: b
        gs = G
    tok = lambda b, c: (blk0 + own_blk(b, c), 0)
    seq3 = lambda b, c: (b, 0, 0)
    seq4 = lambda b, c: (b, 0, 0, 0)
    const = lambda b, c: (0, 0)
    if carry:
        scratch = [pltpu.VMEM((16, QKV_W), F32), pltpu.VMEM((DN_HEADS, DN_DK, DN_DV), F32)]
    else:
        scratch = [pltpu.VMEM((G, 8 + C, QKV_W), F32)]
    return pl.pallas_call(
        functools.partial(_delta_kernel, C=C, G=G, carry=carry, precise=precise),
        grid=grid,
        in_specs=[
            pl.BlockSpec((T, QKV_W), tok),
            pl.BlockSpec((T, V_W), lambda b, c: (blk0 + own_blk(b, c), QKV_W // V_W)),
            pl.BlockSpec((T, LANES), tok),
            pl.BlockSpec((gs, DN_CONV - 1, QKV_W), seq3),
            pl.BlockSpec((gs, DN_HEADS, DN_DK, DN_DV), seq4),
            pl.BlockSpec((DN_CONV, QKV_W), const),
            pl.BlockSpec((1, LANES), const),
            pl.BlockSpec((1, LANES), const),
            pl.BlockSpec((1, DN_DV), const),
        ],
        out_specs=[
            pl.BlockSpec((T, V_W), lambda b, c: (own_blk(b, c), 0)),
            pl.BlockSpec((gs, DN_HEADS, DN_DK, DN_DV), seq4),
            pl.BlockSpec((gs, DN_CONV - 1, QKV_W), seq3),
        ],
        out_shape=[
            jax.ShapeDtypeStruct((B * L, V_W), BF16),
            jax.ShapeDtypeStruct(s0.shape, F32),
            jax.ShapeDtypeStruct(conv_buf.shape, F32),
        ],
        scratch_shapes=scratch,
        compiler_params=_params("parallel", "arbitrary"),
        name=f"delta_rule_c{C}",
    )(z, z, ab, conv_buf, s0, conv_w, alog_p, dtb_p, onorm_g)


def _post_mix_kernel(op_ref, os_ref, gu_ref, gv_ref, ma_ref, mb_ref, xp_ref, xs_ref, lng_ref, lnb_ref,
                     mix_ref, bias_ref, wa_ref, wb_ref, wo_ref, g2_ref, rwh_ref, rwl_ref, rb_ref, tri_ref,
                     x1_ref, h2_ref, gate_ref, idx_ref, v_ref, cnt_ref, us_scr, cnt_scr, *, n_p):
    u = _gelu(gu_ref[...].astype(F32))
    a = _gelu(gv_ref[...].astype(F32))
    ac = a - jnp.mean(a, axis=-1, keepdims=True)
    v = ac * lax.rsqrt(jnp.mean(ac * ac, axis=-1, keepdims=True) + EPS) * lng_ref[...] + lnb_ref[...]
    v_ref[...] = v
    vb = v.astype(BF16)
    gw = GM_WIDTH // GM_GROUPS
    for c in range(u.shape[0] // GM_CHUNK):
        rs = slice(c * GM_CHUNK, (c + 1) * GM_CHUNK)
        for g in range(GM_GROUPS):
            sl = slice(g * gw, (g + 1) * gw)
            s = jnp.dot(mix_ref[0, g], vb[rs, sl], preferred_element_type=F32) + bias_ref[0, :, sl]
            us_scr[rs, sl] = (u[rs, sl] * s).astype(BF16)

    ya = jnp.dot(_group_pick(op_ref, os_ref, n_p).astype(BF16), wa_ref[...], preferred_element_type=F32)
    yb = jnp.dot(us_scr[...], wb_ref[...], preferred_element_type=F32)
    mixed = _sigmoid(ma_ref[...].astype(F32)) * ya + _sigmoid(mb_ref[...].astype(F32)) * yb
    x1 = _group_pick(xp_ref, xs_ref, n_p) + jnp.dot(mixed.astype(BF16), wo_ref[...], preferred_element_type=F32)
    x1_ref[...] = x1
    h2 = _rms(x1, g2_ref[...])
    h2_ref[...] = _pack_bf16_pairs(h2)
    hh, hl = _split(h2)
    d = functools.partial(jnp.dot, preferred_element_type=F32)
    logits = d(hh, rwh_ref[...]) + (d(hh, rwl_ref[...]) + d(hl, rwh_ref[...])) + rb_ref[...]
    lane = lax.broadcasted_iota(jnp.int32, logits.shape, 1).astype(F32)
    vals, idxs = [], []
    for _ in range(TOP_K):
        m = jnp.max(logits, axis=-1, keepdims=True)
        i = jnp.min(jnp.where(logits == m, lane, float(LANES)), axis=-1, keepdims=True)
        vals.append(m)
        idxs.append(i)
        logits = jnp.where(lane == i, -jnp.inf, logits)
    es = [jnp.exp(v - vals[0]) for v in vals]
    tot = es[0]
    for e in es[1:]:
        tot = tot + e

    @pl.when(pl.program_id(0) == 0)
    def _():
        cnt_scr[...] = jnp.zeros(cnt_scr.shape, F32)

    onehot = (lane == idxs[0]).astype(F32)
    for k in range(1, TOP_K):
        onehot = onehot + (lane == idxs[k]).astype(F32)
    before = jnp.dot(tri_ref[...], onehot.astype(BF16), preferred_element_type=F32) + cnt_scr[...]
    cnt_scr[...] = cnt_scr[...] + jnp.sum(onehot, axis=0, keepdims=True)
    cnt_ref[...] = cnt_scr[...]

    gates = jnp.zeros(logits.shape, F32)
    route = jnp.zeros(logits.shape, F32)
    for k in range(TOP_K):
        rank = jnp.sum(jnp.where(lane == idxs[k], before, 0.0), axis=-1, keepdims=True)
        gates = jnp.where(lane == float(k), es[k] / tot, gates)
        route = jnp.where(lane == float(k), idxs[k], route)
        route = jnp.where(lane == float(TOP_K + k), rank, route)
    gate_ref[...] = gates
    idx_ref[...] = route.astype(jnp.int32)


def _post_mix(o_p, o_s, z, xp, xs, ln_g, ln_b, mix, bias, wa, wb, wo, g2, rwh, rwl, rb):
    n_p, n_s = xp.shape[0], xs.shape[0]
    n = n_p + n_s
    tm = _row_tile(n_p, n_s)
    pt = n_p // tm
    tri = jnp.tril(jnp.ones((tm, tm), BF16), k=-1)
    grp = lambda i: jnp.where(i < pt, 0, 1)
    once = pl.Buffered(1)
    tok = pl.BlockSpec((tm, D_MODEL), lambda i: (i, 0))
    zcol = lambda c: pl.BlockSpec((tm, D_MODEL), lambda i: (i, c))
    full = lambda r, c: pl.BlockSpec((r, c), lambda i: (0, 0), pipeline_mode=once)
    narrow = pl.BlockSpec((tm, LANES), lambda i: (i, 0))
    op_spec, os_spec = _group_specs(tm, V_W, n_p)
    xp_spec, xs_spec = _group_specs(tm, D_MODEL, n_p)
    return pl.pallas_call(
        functools.partial(_post_mix_kernel, n_p=n_p),
        grid=(n // tm,),
        in_specs=[
            op_spec, os_spec, zcol(4), zcol(5), zcol(6), zcol(7), xp_spec, xs_spec,
            full(1, GM_WIDTH), full(1, GM_WIDTH),
            pl.BlockSpec((1, GM_GROUPS, GM_CHUNK, GM_CHUNK), lambda i: (grp(i), 0, 0, 0)),
            pl.BlockSpec((1, GM_CHUNK, GM_WIDTH), lambda i: (grp(i), 0, 0)),
            full(V_W, D_MODEL), full(GM_WIDTH, D_MODEL), full(D_MODEL, D_MODEL), full(1, D_MODEL),
            full(D_MODEL, LANES), full(D_MODEL, LANES), full(1, LANES), full(tm, tm),
        ],
        out_specs=[tok, pl.BlockSpec((tm, D_MODEL // 2), lambda i: (i, 0)), narrow, narrow,
                   pl.BlockSpec((tm, GM_WIDTH), lambda i: (jnp.maximum(i - pt, 0), 0)),
                   pl.BlockSpec((1, LANES), lambda i: (0, 0))],
        out_shape=[
            jax.ShapeDtypeStruct((n, D_MODEL), F32),
            jax.ShapeDtypeStruct((n, D_MODEL // 2), jnp.int32),
            jax.ShapeDtypeStruct((n, LANES), F32),
            jax.ShapeDtypeStruct((n, LANES), jnp.int32),
            jax.ShapeDtypeStruct((n_s, GM_WIDTH), F32),
            jax.ShapeDtypeStruct((1, LANES), F32),
        ],
        scratch_shapes=[pltpu.VMEM((tm, GM_WIDTH), BF16), pltpu.VMEM((1, LANES), F32)],
        compiler_params=_params("arbitrary"),
        name="chunk_mlp_post_mix_router",
    )(o_p, o_s, z, z, z, z, xp, xs, ln_g, ln_b, mix, bias, wa, wb, wo, g2, rwh, rwl, rb, tri)


def _moe_kernel(be_ref, first_ref, valid_ref, next_ref, x_ref, w1_hbm, b1_ref, w2_hbm, b2_ref, y_ref,
                w1_stage, w2_stage, w1_scr, w2_scr, sems):
    b = pl.program_id(0)

    def weight_copies(e):
        return (pltpu.make_async_copy(w1_hbm.at[e], w1_stage, sems.at[0]),
                pltpu.make_async_copy(w2_hbm.at[e], w2_stage, sems.at[1]))

    @pl.when(b == 0)
    def _():
        for c in weight_copies(be_ref[0]):
            c.start()

    @pl.when(first_ref[b] == 1)
    def _():
        for c in weight_copies(be_ref[b]):
            c.wait()
        w1_scr[...] = w1_stage[...].astype(BF16)
        w2_scr[...] = w2_stage[...].astype(BF16)

        @pl.when(next_ref[b] >= 0)
        def _():
            for c in weight_copies(next_ref[b]):
                c.start()

    @pl.when(valid_ref[b] == 1)
    def _():
        x = jnp.concatenate(_unpack_bf16_pairs(x_ref[...]), axis=1).astype(BF16)
        hid = jnp.dot(x, w1_scr[...], preferred_element_type=F32) + b1_ref[0]
        gate = jnp.minimum(hid[:, :D_EXPERT], SWIGLU_LIMIT)
        up = jnp.clip(hid[:, D_EXPERT:], -SWIGLU_LIMIT, SWIGLU_LIMIT)
        act = gate * _sigmoid(SWIGLU_ALPHA * gate) * (up + 1.0)
        y = jnp.dot(act.astype(BF16), w2_scr[...], preferred_element_type=F32) + b2_ref[0]
        y_ref[...] = _pack_bf16_pairs(y)

    @pl.when(valid_ref[b] == 0)
    def _():
        y_ref[...] = jnp.zeros(y_ref.shape, jnp.int32)


def _moe_experts(block_e, first, valid, next_e, xb, w1, b1, w2, b2):
    rows = xb.shape[0]
    nb = rows // MOE_ROWS
    smem4 = lambda f: (lambda b, be, fi, va, ne: f(b, be))
    return pl.pallas_call(
        _moe_kernel,
        grid_spec=pltpu.PrefetchScalarGridSpec(
            num_scalar_prefetch=4,
            grid=(nb,),
            in_specs=[
                pl.BlockSpec((MOE_ROWS, D_MODEL // 2), smem4(lambda b, be: (b, 0))),
                pl.BlockSpec(memory_space=pl.ANY),
                pl.BlockSpec((1, 1, 2 * D_EXPERT), smem4(lambda b, be: (be[b], 0, 0))),
                pl.BlockSpec(memory_space=pl.ANY),
                pl.BlockSpec((1, 1, D_MODEL), smem4(lambda b, be: (be[b], 0, 0))),
            ],
            out_specs=pl.BlockSpec((MOE_ROWS, D_MODEL // 2), smem4(lambda b, be: (b, 0))),
            scratch_shapes=[
                pltpu.VMEM((D_MODEL, 2 * D_EXPERT), F32),
                pltpu.VMEM((D_EXPERT, D_MODEL), F32),
                pltpu.VMEM((D_MODEL, 2 * D_EXPERT), BF16),
                pltpu.VMEM((D_EXPERT, D_MODEL), BF16),
                pltpu.SemaphoreType.DMA((2,)),
            ],
        ),
        out_shape=jax.ShapeDtypeStruct((rows, D_MODEL // 2), jnp.int32),
        compiler_params=_params("arbitrary"),
        name="moe_experts",
    )(block_e, first, valid, next_e, xb, w1, b1, w2, b2)


def _moe_dispatch(idx, rank, counts, n):
    experts = jnp.arange(N_EXPERTS, dtype=jnp.int32)
    padded = (counts + MOE_ROWS - 1) // MOE_ROWS * MOE_ROWS
    pad_end = jnp.cumsum(padded)
    pad_start = pad_end - padded
    start_of = jnp.sum(jnp.where(idx[:, :, None] == experts, pad_start, 0), axis=-1)
    dest = start_of + rank
    nb = -(-n * TOP_K // MOE_ROWS) + N_EXPERTS
    rows = nb * MOE_ROWS
    starts = jnp.arange(nb, dtype=jnp.int32) * MOE_ROWS
    valid = (starts < pad_end[-1]).astype(jnp.int32)
    owner = lambda r: jnp.minimum(jnp.sum((pad_end[None, :] <= r[:, None]).astype(jnp.int32), axis=1),
                                  N_EXPERTS - 1)
    last_e = owner(pad_end[-1:] - 1)[0]
    block_e = jnp.where(valid == 1, owner(starts), last_e).astype(jnp.int32)
    first = jnp.concatenate([jnp.ones((1,), jnp.int32),
                             (block_e[1:] != block_e[:-1]).astype(jnp.int32)])
    blk = jnp.arange(nb, dtype=jnp.int32)
    later_first = (blk[None, :] > blk[:, None]) & (first[None, :] == 1)
    next_pos = jnp.min(jnp.where(later_first, blk[None, :], nb), axis=1)
    next_e = jnp.where(next_pos < nb, block_e[jnp.minimum(next_pos, nb - 1)], -1).astype(jnp.int32)
    return dest.T, rows, block_e, first, valid, next_e


def _sc_scatter_rows(src, dest_t, out_rows):
    n, d = src.shape
    nk = dest_t.shape[0]
    assert nk * SC_TOKENS == SC_WINDOW
    idx = dest_t.reshape(nk, n // SC_TOKENS, SC_TOKENS).transpose(1, 0, 2).reshape(n // SC_TOKENS, SC_WINDOW)
    mesh = plsc.VectorSubcoreMesh(core_axis_name="core", subcore_axis_name="subcore",
                                  num_cores=SC_CORES, num_subcores=SC_SUBCORES)

    @pl.kernel(out_type=jax.ShapeDtypeStruct((out_rows, d), src.dtype), mesh=mesh, scratch_types=[])
    def scatter_rows(src_hbm, di_hbm, out_hbm):
        def body(x_vmem, di_vmem):
            for k in range(nk):
                pltpu.sync_copy(x_vmem, out_hbm.at[di_vmem.at[0, pl.ds(k * SC_TOKENS, SC_TOKENS)]])

        pltpu.emit_pipeline(
            body,
            grid=(n // SC_TOKENS,),
            in_specs=[pl.BlockSpec((SC_TOKENS, d), lambda i: (i, 0)),
                      pl.BlockSpec((1, SC_WINDOW), lambda i: (i, 0))],
            out_specs=[],
            core_axis_name=("core", "subcore"),
            dimension_semantics=(pltpu.PARALLEL,),
        )(src_hbm, di_hbm)

    return scatter_rows(src, idx)


def _sc_gather_rows(src, idx):
    m = idx.shape[0]
    d = src.shape[1]
    idx_rows = jnp.pad(idx.reshape(m // SC_CHUNK, SC_CHUNK), ((0, 0), (0, SC_WINDOW - SC_CHUNK)))
    mesh = plsc.VectorSubcoreMesh(core_axis_name="core", subcore_axis_name="subcore",
                                  num_cores=SC_CORES, num_subcores=SC_SUBCORES)

    @pl.kernel(out_type=jax.ShapeDtypeStruct((m, d), src.dtype), mesh=mesh, scratch_types=[])
    def gather_rows(src_hbm, si_hbm, out_hbm):
        def body(si_vmem, o_vmem):
            pltpu.sync_copy(src_hbm.at[si_vmem.at[0, pl.ds(0, SC_CHUNK)]], o_vmem)

        pltpu.emit_pipeline(
            body,
            grid=(m // SC_CHUNK,),
            in_specs=[pl.BlockSpec((1, SC_WINDOW), lambda i: (i, 0))],
            out_specs=[pl.BlockSpec((SC_CHUNK, d), lambda i: (i, 0))],
            core_axis_name=("core", "subcore"),
            dimension_semantics=(pltpu.PARALLEL,),
        )(si_hbm, out_hbm)

    return gather_rows(src, idx_rows)


def _tail_kernel(x1_ref, yk_ref, gate_ref, pp_ref, ps_ref, g3_ref, wg_ref, wp_ref, gf_ref,
                 yp_ref, ys_ref, *, n_p):
    gates = gate_ref[...]
    lo, hi = _unpack_bf16_pairs(yk_ref[0])
    lo, hi = lo * gates[:, 0:1], hi * gates[:, 0:1]
    for k in range(1, TOP_K):
        lo_k, hi_k = _unpack_bf16_pairs(yk_ref[k])
        lo, hi = lo + lo_k * gates[:, k:k + 1], hi + hi_k * gates[:, k:k + 1]
    moe = jnp.concatenate([lo, hi], axis=1)
    x2 = x1_ref[...] + moe
    h3 = _rms(x2, g3_ref[...])
    gate = _sigmoid(jnp.dot(h3.astype(BF16), wg_ref[...], preferred_element_type=F32))
    pe = jnp.dot(_group_pick(pp_ref, ps_ref, n_p).astype(BF16), wp_ref[...], preferred_element_type=F32)
    y = _rms(x2 + gate * pe, gf_ref[...])
    in_prompt = pl.program_id(0) < n_p // yp_ref.shape[0]

    @pl.when(in_prompt)
    def _():
        yp_ref[...] = y

    @pl.when(jnp.logical_not(in_prompt))
    def _():
        ys_ref[...] = y


def _tail(x1, yk, gates, pp, ps, g3, wg, wp, gf):
    n_p, n_s = pp.shape[0], ps.shape[0]
    n = n_p + n_s
    tm = _row_tile(n_p, n_s)
    tok = pl.BlockSpec((tm, D_MODEL), lambda i: (i, 0))
    full = lambda r, c: pl.BlockSpec((r, c), lambda i: (0, 0))
    return pl.pallas_call(
        functools.partial(_tail_kernel, n_p=n_p),
        grid=(n // tm,),
        in_specs=[tok, pl.BlockSpec((TOP_K, tm, D_MODEL // 2), lambda i: (0, i, 0)),
                  pl.BlockSpec((tm, LANES), lambda i: (i, 0)),
                  *_group_specs(tm, PLE_DIM, n_p),
                  full(1, D_MODEL), full(D_MODEL, D_MODEL), full(PLE_DIM, D_MODEL), full(1, D_MODEL)],
        out_specs=_group_specs(tm, D_MODEL, n_p),
        out_shape=[jax.ShapeDtypeStruct((n_p, D_MODEL), F32), jax.ShapeDtypeStruct((n_s, D_MODEL), F32)],
        compiler_params=_params("arbitrary"),
        name="ple_final_norm",
    )(x1, yk, gates, pp, ps, g3, wg, wp, gf)


def _lane_pad(v, offset, fill=0.0):
    out = jnp.full((1, LANES), fill, F32)
    return out.at[0, offset:offset + v.shape[0]].set(v.astype(F32))


def kernel(x_prompt, x_sample, state_delta, state_conv, p_prompt, p_sample, norm1_g, w_in, conv_w, a_log, dt_bias, dn_norm_g, w_proj_a, gm_ln_g, gm_ln_b, gm_ws, gm_bs, w_proj_b, w_out, norm2_g, router_w, router_b, moe_w1, moe_b1, moe_w2, moe_b2, norm3_g, ple_w, ple_gate_w, final_norm_g):
    bp, lp, d = x_prompt.shape
    bs, ls, _ = x_sample.shape
    depth = w_in.shape[0]
    assert depth == 1 and d == D_MODEL
    assert lp % GM_CHUNK == 0 and GM_CHUNK % ls == 0 and ls >= DN_CONV - 1
    n_p, n_s = bp * lp, bs * ls
    n = n_p + n_s
    i = 0

    xp, xs = x_prompt.reshape(n_p, d), x_sample.reshape(n_s, d)

    ab0 = QKV_W
    w = w_in[i]
    w_main = jnp.concatenate([w[:, :ab0], w[:, ab0 + 2 * DN_HEADS:]], axis=1).astype(BF16)
    w_ab = jnp.pad(w[:, ab0:ab0 + 2 * DN_HEADS], ((0, 0), (0, LANES - 2 * DN_HEADS))).astype(BF16)
    row2 = lambda v: v.reshape(1, -1).astype(F32)

    z, ab = _in_proj(xp, xs, row2(norm1_g[i]), w_main, w_ab)

    alog_p = _lane_pad(a_log[i], 0)
    dtb_p = _lane_pad(dt_bias[i], 0)
    cw = conv_w[i].astype(F32)
    ong = row2(dn_norm_g[i])
    zero_s = jnp.zeros((bp, DN_HEADS, DN_DK, DN_DV), F32)
    zero_buf = jnp.zeros((bp, DN_CONV - 1, QKV_W), F32)
    o_p, sd_p, sc_p = _delta_branch(z, ab, zero_buf, zero_s, cw, alog_p, dtb_p, ong,
                                    tok0=0, L=lp, G=4, precise=False)
    o_s, sd_s, sc_s = _delta_branch(z, ab, state_conv[i], state_delta[i], cw, alog_p, dtb_p, ong,
                                    tok0=n_p, L=ls, G=8, precise=False)

    t = GM_CHUNK
    tri = jnp.tril(jnp.ones((t, t), bool))
    ws = gm_ws[i]
    mix_p = jnp.where(tri, ws, 0.0)
    small = jnp.where(tri[:ls, :ls], ws[:, :ls, :ls], 0.0)
    mix_s = jnp.einsum('ab,gts->gatbs', jnp.eye(t // ls, dtype=F32), small).reshape(GM_GROUPS, t, t)
    mix = jnp.stack([mix_p, mix_s]).astype(BF16)
    gw = GM_WIDTH // GM_GROUPS
    bias_p = jnp.repeat(gm_bs[i].T, gw, axis=1)
    bias_s = jnp.tile(bias_p[:ls], (t // ls, 1))
    bias = jnp.stack([bias_p, bias_s]).astype(F32)
    rw = jnp.pad(router_w[i].astype(F32), ((0, 0), (0, LANES - N_EXPERTS)))
    rwh = rw.astype(BF16)
    rwl = (rw - rwh.astype(F32)).astype(BF16)
    rb = _lane_pad(router_b[i], 0, fill=-jnp.inf)
    x1, h2, gates, route, v_s, counts = _post_mix(
        o_p, o_s, z, xp, xs, row2(gm_ln_g[i]), row2(gm_ln_b[i]), mix, bias, w_proj_a[i].astype(BF16),
        w_proj_b[i].astype(BF16), w_out[i].astype(BF16), row2(norm2_g[i]), rwh, rwl, rb)
    dest_t, rows, block_e, first, valid, next_e = _moe_dispatch(
        route[:, :TOP_K], route[:, TOP_K:2 * TOP_K], counts[0, :N_EXPERTS].astype(jnp.int32), n)
    xb = _sc_scatter_rows(h2, dest_t, rows)
    yb = _moe_experts(block_e, first, valid, next_e, xb, moe_w1[i], moe_b1[i][:, None, :],
                      moe_w2[i], moe_b2[i][:, None, :])
    yk = _sc_gather_rows(yb, dest_t.reshape(TOP_K * n)).reshape(TOP_K, n, d // 2)

    y_p, y_s = _tail(x1, yk, gates, p_prompt[i].reshape(n_p, PLE_DIM), p_sample[i].reshape(n_s, PLE_DIM),
                     row2(norm3_g[i]), ple_gate_w[i].astype(BF16), ple_w[i].astype(BF16), row2(final_norm_g))

    return (y_p.reshape(bp, lp, d), y_s.reshape(bs, ls, d),
            sd_p[None], sc_p[None], sd_s[None], sc_s[None], v_s.reshape(1, bs, ls, GM_WIDTH))
```

```python
import functools
import math

import jax
import jax.numpy as jnp
from jax import lax
from jax.experimental import pallas as pl
from jax.experimental.pallas import tpu as pltpu
from jax.experimental.pallas import tpu_sc as plsc

F32 = jnp.float32
BF16 = jnp.bfloat16

D_MODEL = 1024
DN_HEADS = 8
DN_DK = 128
DN_DV = 128
DN_CONV = 4
DN_CHUNK = 64
GM_WIDTH = 1024
GM_GROUPS = 8
GM_CHUNK = 128
N_EXPERTS = 32
TOP_K = 4
D_EXPERT = 1024
SWIGLU_LIMIT = 7.0
SWIGLU_ALPHA = 1.702
PLE_DIM = 256
EPS = 1e-6
QK_W = DN_HEADS * DN_DK
V_W = DN_HEADS * DN_DV
QKV_W = 2 * QK_W + V_W

LANES = 128
MOE_ROWS = 512
VMEM_LIMIT = 56 << 20
SC_CORES, SC_SUBCORES = 2, 16
SC_WINDOW = 128
SC_TOKENS = SC_WINDOW // TOP_K
SC_CHUNK = 64


def _params(*sem):
    return pltpu.CompilerParams(dimension_semantics=sem, vmem_limit_bytes=VMEM_LIMIT)


def _dot(a, b):
    return jnp.dot(a.astype(BF16), b.astype(BF16), preferred_element_type=F32)


def _split(a):
    hi = a.astype(BF16)
    return hi, (a - hi.astype(F32)).astype(BF16)


def _dot3(a, b):
    ah, al = _split(a)
    bh, bl = _split(b)
    d = functools.partial(jnp.dot, preferred_element_type=F32)
    return d(ah, bh) + (d(ah, bl) + d(al, bh))


def _dot_nt(a, b):
    return lax.dot_general(a.astype(BF16), b.astype(BF16), (((1,), (1,)), ((), ())),
                           preferred_element_type=F32)


def _dot_tn(a, b):
    return lax.dot_general(a.astype(BF16), b.astype(BF16), (((0,), (0,)), ((), ())),
                           preferred_element_type=F32)


def _dot3_tn(a, b):
    ah, al = _split(a)
    bh, bl = _split(b)
    d = functools.partial(lax.dot_general, dimension_numbers=(((0,), (0,)), ((), ())),
                          preferred_element_type=F32)
    return d(ah, bh) + (d(ah, bl) + d(al, bh))


def _sigmoid(x):
    return 0.5 * jnp.tanh(0.5 * x) + 0.5


def _pack_bf16_pairs(x):
    bits = lax.bitcast_convert_type(x.astype(BF16).astype(F32), jnp.uint32)
    half = x.shape[1] // 2
    return lax.bitcast_convert_type((bits[:, :half] >> 16) | bits[:, half:], jnp.int32)


def _unpack_bf16_pairs(w):
    bits = lax.bitcast_convert_type(w, jnp.uint32)
    return (lax.bitcast_convert_type(bits << 16, F32),
            lax.bitcast_convert_type(bits & jnp.uint32(0xFFFF0000), F32))


def _rms(x, g):
    return x * lax.rsqrt(jnp.mean(x * x, axis=-1, keepdims=True) + EPS) * g


def _gelu(x):
    return 0.5 * x * (1.0 + lax.erf(x * (1.0 / math.sqrt(2.0))))


def _row_tile(n_p, n_s, cap=512):
    for t in (1024, 512, 256, 128):
        if t > cap:
            continue
        if n_p % t == 0 and n_s % t == 0:
            return t
    raise ValueError(f"token counts {n_p}, {n_s} must be multiples of 128")


def _group_specs(tm, width, n_p, tile0=0):
    pt = n_p // tm
    return [pl.BlockSpec((tm, width), lambda i, *_: (jnp.minimum(i + tile0, pt - 1), 0)),
            pl.BlockSpec((tm, width), lambda i, *_: (jnp.maximum(i + tile0 - pt, 0), 0))]


def _group_pick(prompt_ref, sample_ref, n_p, tile0=0):
    pt = n_p // prompt_ref.shape[0]
    return jnp.where(pl.program_id(0) + tile0 < pt, prompt_ref[...], sample_ref[...])


def _in_proj_kernel(xp_ref, xs_ref, g_ref, w_ref, wab_ref, z_ref, ab_ref, h_scr, *, n_p):
    @pl.when(pl.program_id(1) == 0)
    def _():
        hb = _rms(_group_pick(xp_ref, xs_ref, n_p), g_ref[...]).astype(BF16)
        h_scr[...] = hb
        ab_ref[...] = jnp.dot(hb, wab_ref[...], preferred_element_type=F32)

    z_ref[...] = jnp.dot(h_scr[...], w_ref[...], preferred_element_type=F32).astype(z_ref.dtype)


def _in_proj(xp, xs, g, w_main, w_ab):
    n_p, n_s = xp.shape[0], xs.shape[0]
    n = n_p + n_s
    tm, tn = _row_tile(n_p, n_s, cap=1024), 2048
    cols = w_main.shape[1]
    return pl.pallas_call(
        functools.partial(_in_proj_kernel, n_p=n_p),
        grid=(n // tm, cols // tn),
        in_specs=_group_specs(tm, D_MODEL, n_p) + [
            pl.BlockSpec((1, D_MODEL), lambda i, j: (0, 0)),
            pl.BlockSpec((D_MODEL, tn), lambda i, j: (0, j)),
            pl.BlockSpec((D_MODEL, LANES), lambda i, j: (0, 0)),
        ],
        out_specs=[
            pl.BlockSpec((tm, tn), lambda i, j: (i, j)),
            pl.BlockSpec((tm, LANES), lambda i, j: (i, 0)),
        ],
        out_shape=[jax.ShapeDtypeStruct((n, cols), BF16), jax.ShapeDtypeStruct((n, LANES), F32)],
        scratch_shapes=[pltpu.VMEM((tm, D_MODEL), BF16)],
        compiler_params=_params("parallel", "arbitrary"),
        name="in_proj",
    )(xp, xs, g, w_main, w_ab)


def _delta_kernel(*refs, C, G, carry, precise):
    (qkv_ref, zg_ref, ab_ref, buf_ref, s0_ref, cw_ref, alog_ref, dtb_ref, ong_ref) = refs[:9]
    refs = refs[9:]
    o_ref, snew_ref, bufnew_ref, xc_scr = refs[:4]
    H, DK, DV = DN_HEADS, DN_DK, DN_DV
    T = G * C
    dotm = _dot3 if precise else _dot
    dotm_tn = _dot3_tn if precise else _dot_tn
    halo = DN_CONV - 1
    base = 8
    cw = cw_ref[...]

    def conv(window):
        y = window(0) * cw[0:1]
        for i in range(1, DN_CONV):
            y = y + window(i) * cw[i:i + 1]
        return y

    if carry:
        s_scr = refs[4]
        c = pl.program_id(1)
        last = pl.num_programs(1) - 1

        @pl.when(c == 0)
        def _():
            xc_scr[base - halo:base, :] = buf_ref[0]
            s_scr[...] = s0_ref[0]

        xb = qkv_ref[...]
        xf = xb.astype(F32)
        xc_scr[base:base + 8, :] = xf[0:8]
        y_head = conv(lambda i: xc_scr[base - halo + i:base - halo + i + 8, :])
        ri = lax.broadcasted_iota(jnp.int32, (halo * T, T), 0)
        ci = lax.broadcasted_iota(jnp.int32, (halo * T, T), 1)
        src_row = (ri & (T - 1)) - (halo - (ri >> (T.bit_length() - 1)))
        shifted = jnp.dot(jnp.where(ci == src_row, 1.0, 0.0).astype(BF16), xb, preferred_element_type=F32)
        y = conv(lambda i: shifted[i * T:(i + 1) * T] if i < halo else xf)
        y = jnp.concatenate([y_head, y[8:]], axis=0)
        tail = xf[T - halo:T]
        xc_scr[base - halo:base, :] = tail

        @pl.when(c == last)
        def _():
            bufnew_ref[0] = tail
    else:
        ys = []
        x_new = qkv_ref[...].astype(F32)
        for g in range(G):
            xc_scr[g, base - halo:base, :] = buf_ref[g]
            xc_scr[g, base:base + C, :] = x_new[g * C:(g + 1) * C, :]
            ys.append(conv(lambda i: xc_scr[g, base - halo + i:base - halo + i + C, :]))
            bufnew_ref[g] = xc_scr[g, base + C - halo:base + C, :]
        y = jnp.concatenate(ys, axis=0) if G > 1 else ys[0]
    qkv = y * _sigmoid(y)

    ab = ab_ref[...]
    g_all = -jnp.exp(alog_ref[...]) * jax.nn.softplus(ab + dtb_ref[...])
    beta_all = _sigmoid(ab)
    shift = C.bit_length() - 1
    rt = lax.broadcasted_iota(jnp.int32, (T, T), 0)
    ct = lax.broadcasted_iota(jnp.int32, (T, T), 1)
    chunk_tril = ((rt >> shift) == (ct >> shift)) & (rt >= ct)
    gcum = _dot3(chunk_tril.astype(F32), g_all)
    gam_all = jnp.exp(gcum)

    row = lax.broadcasted_iota(jnp.int32, (C, C), 0)
    col = lax.broadcasted_iota(jnp.int32, (C, C), 1)
    incl, strict, eye = row >= col, row > col, row == col
    eyef = eye.astype(F32)
    units = [(g, h) for g in range(G) for h in range(H)]
    rows = lambda a, g: a[g * C:(g + 1) * C]

    qn, kn, vv = [], [], []
    for h in range(H):
        q = qkv[:, h * DK:(h + 1) * DK]
        k = qkv[:, QK_W + h * DK:QK_W + (h + 1) * DK]
        qn.append(q * (lax.rsqrt(jnp.sum(q * q, axis=-1, keepdims=True) + EPS) * (DK ** -0.5)))
        kn.append(k * lax.rsqrt(jnp.sum(k * k, axis=-1, keepdims=True) + EPS))
        vv.append(qkv[:, 2 * QK_W + h * DV:2 * QK_W + (h + 1) * DV])
    qb = [q.astype(BF16) for q in qn]
    kb = [k.astype(BF16) for k in kn]

    gc, bc, gl, a_low, m_intra = {}, {}, {}, {}, {}
    for u in units:
        g, h = u
        gc[u] = rows(gcum, g)[:, h:h + 1]
        bc[u] = rows(beta_all, g)[:, H + h:H + h + 1]
        gl[u] = gc[u][C - 1:C, :]
        gr = jnp.sum(jnp.where(eye, gc[u], 0.0), axis=0, keepdims=True)
        br = jnp.sum(jnp.where(eye, bc[u], 0.0), axis=0, keepdims=True)
        db = jnp.where(incl, jnp.exp(gc[u] - gr), 0.0) * br
        k = rows(kb[h], g)
        kq = _dot_nt(jnp.concatenate([k, rows(qb[h], g)], axis=0), k)
        a_low[u] = jnp.where(strict, kq[:C] * db, 0.0)
        m_intra[u] = kq[C:] * db

    t_inv = {u: eyef - a_low[u] for u in units}
    a_pow = {u: dotm(a_low[u], a_low[u]) for u in units}
    n = 2
    while n < C:
        t_inv = {u: t_inv[u] + dotm(t_inv[u], a_pow[u]) for u in units}
        n *= 2
        if n < C:
            a_pow = {u: dotm(a_pow[u], a_pow[u]) for u in units}

    u_base, wq, k_dec = {}, {}, {}
    for u in units:
        g, h = u
        gam = rows(gam_all, g)[:, h:h + 1]
        k = rows(kn[h], g)
        sol = dotm(t_inv[u], jnp.concatenate([rows(vv[h], g), gam * k], axis=1))
        u_base[u] = sol[:, :DV]
        wq[u] = jnp.concatenate([sol[:, DV:], gam * rows(qn[h], g)], axis=0)
        k_dec[u] = k * (bc[u] * jnp.exp(gl[u] - gc[u]))

    state = [s_scr[h] for h in range(H)] if carry else None
    outs = {}
    for g in range(G):
        s_in = state if carry else [s0_ref[g, h] for h in range(H)]
        wqs = [dotm(wq[(g, h)], s_in[h]) for h in range(H)]
        us = [u_base[(g, h)] - wqs[h][:C] for h in range(H)]
        for h in range(H):
            outs[(g, h)] = wqs[h][C:] + dotm(m_intra[(g, h)], us[h])
        s_out = [jnp.exp(gl[(g, h)]) * s_in[h] + dotm_tn(k_dec[(g, h)], us[h]) for h in range(H)]
        if carry:
            state = s_out
        else:
            for h in range(H):
                snew_ref[g, h] = s_out[h]

    for h in range(H):
        o = jnp.concatenate([outs[(g, h)] for g in range(G)], axis=0) if G > 1 else outs[(0, h)]
        zg = zg_ref[:, h * DV:(h + 1) * DV].astype(F32)
        o_ref[:, h * DV:(h + 1) * DV] = (_rms(o, ong_ref[...]) * (zg * _sigmoid(zg))).astype(o_ref.dtype)

    if carry:
        for h in range(H):
            s_scr[h] = state[h]

        @pl.when(c == last)
        def _():
            for h in range(H):
                snew_ref[0, h] = state[h]


def _delta_branch(z, ab, conv_buf, s0, conv_w, alog_p, dtb_p, onorm_g, *, tok0, L, G, precise):
    B = conv_buf.shape[0]
    C = math.gcd(L, DN_CHUNK)
    nc = L // C
    carry = nc > 1
    T = G * C
    blk0 = tok0 // T
    if carry:
        assert nc % G == 0
        grid = (B, nc // G)
        own_blk = lambda b, c: b * (nc // G) + c
        gs = 1
    else:
        assert B % G == 0
        grid = (B // G, 1)
        own_blk = lambda b, c: b
        gs = G
    tok = lambda b, c: (blk0 + own_blk(b, c), 0)
    seq3 = lambda b, c: (b, 0, 0)
    seq4 = lambda b, c: (b, 0, 0, 0)
    const = lambda b, c: (0, 0)
    if carry:
        scratch = [pltpu.VMEM((16, QKV_W), F32), pltpu.VMEM((DN_HEADS, DN_DK, DN_DV), F32)]
    else:
        scratch = [pltpu.VMEM((G, 8 + C, QKV_W), F32)]
    return pl.pallas_call(
        functools.partial(_delta_kernel, C=C, G=G, carry=carry, precise=precise),
        grid=grid,
        in_specs=[
            pl.BlockSpec((T, QKV_W), tok),
            pl.BlockSpec((T, V_W), lambda b, c: (blk0 + own_blk(b, c), QKV_W // V_W)),
            pl.BlockSpec((T, LANES), tok),
            pl.BlockSpec((gs, DN_CONV - 1, QKV_W), seq3),
            pl.BlockSpec((gs, DN_HEADS, DN_DK, DN_DV), seq4),
            pl.BlockSpec((DN_CONV, QKV_W), const),
            pl.BlockSpec((1, LANES), const),
            pl.BlockSpec((1, LANES), const),
            pl.BlockSpec((1, DN_DV), const),
        ],
        out_specs=[
            pl.BlockSpec((T, V_W), lambda b, c: (own_blk(b, c), 0)),
            pl.BlockSpec((gs, DN_HEADS, DN_DK, DN_DV), seq4),
            pl.BlockSpec((gs, DN_CONV - 1, QKV_W), seq3),
        ],
        out_shape=[
            jax.ShapeDtypeStruct((B * L, V_W), BF16),
            jax.ShapeDtypeStruct(s0.shape, F32),
            jax.ShapeDtypeStruct(conv_buf.shape, F32),
        ],
        scratch_shapes=scratch,
        compiler_params=_params("parallel", "arbitrary"),
        name=f"delta_rule_c{C}",
    )(z, z, ab, conv_buf, s0, conv_w, alog_p, dtb_p, onorm_g)


def _post_mix_kernel(op_ref, os_ref, gu_ref, gv_ref, ma_ref, mb_ref, xp_ref, xs_ref, lng_ref, lnb_ref,
                     mix_ref, bias_ref, wa_ref, wb_ref, wo_ref, g2_ref, rwh_ref, rwl_ref, rb_ref, tri_ref,
                     x1_ref, h2_ref, gate_ref, idx_ref, v_ref, cnt_ref, us_scr, cnt_scr, *, n_p):
    u = _gelu(gu_ref[...].astype(F32))
    a = _gelu(gv_ref[...].astype(F32))
    ac = a - jnp.mean(a, axis=-1, keepdims=True)
    v = ac * lax.rsqrt(jnp.mean(ac * ac, axis=-1, keepdims=True) + EPS) * lng_ref[...] + lnb_ref[...]
    v_ref[...] = v
    vb = v.astype(BF16)
    gw = GM_WIDTH // GM_GROUPS
    for c in range(u.shape[0] // GM_CHUNK):
        rs = slice(c * GM_CHUNK, (c + 1) * GM_CHUNK)
        for g in range(GM_GROUPS):
            sl = slice(g * gw, (g + 1) * gw)
            s = jnp.dot(mix_ref[0, g], vb[rs, sl], preferred_element_type=F32) + bias_ref[0, :, sl]
            us_scr[rs, sl] = (u[rs, sl] * s).astype(BF16)

    ya = jnp.dot(_group_pick(op_ref, os_ref, n_p).astype(BF16), wa_ref[...], preferred_element_type=F32)
    yb = jnp.dot(us_scr[...], wb_ref[...], preferred_element_type=F32)
    mixed = _sigmoid(ma_ref[...].astype(F32)) * ya + _sigmoid(mb_ref[...].astype(F32)) * yb
    x1 = _group_pick(xp_ref, xs_ref, n_p) + jnp.dot(mixed.astype(BF16), wo_ref[...], preferred_element_type=F32)
    x1_ref[...] = x1
    h2 = _rms(x1, g2_ref[...])
    h2_ref[...] = _pack_bf16_pairs(h2)
    hh, hl = _split(h2)
    d = functools.partial(jnp.dot, preferred_element_type=F32)
    logits = d(hh, rwh_ref[...]) + (d(hh, rwl_ref[...]) + d(hl, rwh_ref[...])) + rb_ref[...]
    lane = lax.broadcasted_iota(jnp.int32, logits.shape, 1).astype(F32)
    vals, idxs = [], []
    for _ in range(TOP_K):
        m = jnp.max(logits, axis=-1, keepdims=True)
        i = jnp.min(jnp.where(logits == m, lane, float(LANES)), axis=-1, keepdims=True)
        vals.append(m)
        idxs.append(i)
        logits = jnp.where(lane == i, -jnp.inf, logits)
    es = [jnp.exp(v - vals[0]) for v in vals]
    tot = es[0]
    for e in es[1:]:
        tot = tot + e

    @pl.when(pl.program_id(0) == 0)
    def _():
        cnt_scr[...] = jnp.zeros(cnt_scr.shape, F32)

    onehot = (lane == idxs[0]).astype(F32)
    for k in range(1, TOP_K):
        onehot = onehot + (lane == idxs[k]).astype(F32)
    before = jnp.dot(tri_ref[...], onehot.astype(BF16), preferred_element_type=F32) + cnt_scr[...]
    cnt_scr[...] = cnt_scr[...] + jnp.sum(onehot, axis=0, keepdims=True)
    cnt_ref[...] = cnt_scr[...]

    gates = jnp.zeros(logits.shape, F32)
    route = jnp.zeros(logits.shape, F32)
    for k in range(TOP_K):
        rank = jnp.sum(jnp.where(lane == idxs[k], before, 0.0), axis=-1, keepdims=True)
        gates = jnp.where(lane == float(k), es[k] / tot, gates)
        route = jnp.where(lane == float(k), idxs[k], route)
        route = jnp.where(lane == float(TOP_K + k), rank, route)
    gate_ref[...] = gates
    idx_ref[...] = route.astype(jnp.int32)


def _post_mix(o_p, o_s, z, xp, xs, ln_g, ln_b, mix, bias, wa, wb, wo, g2, rwh, rwl, rb):
    n_p, n_s = xp.shape[0], xs.shape[0]
    n = n_p + n_s
    tm = _row_tile(n_p, n_s)
    pt = n_p // tm
    tri = jnp.tril(jnp.ones((tm, tm), BF16), k=-1)
    grp = lambda i: jnp.where(i < pt, 0, 1)
    once = pl.Buffered(1)
    tok = pl.BlockSpec((tm, D_MODEL), lambda i: (i, 0))
    zcol = lambda c: pl.BlockSpec((tm, D_MODEL), lambda i: (i, c))
    full = lambda r, c: pl.BlockSpec((r, c), lambda i: (0, 0), pipeline_mode=once)
    narrow = pl.BlockSpec((tm, LANES), lambda i: (i, 0))
    op_spec, os_spec = _group_specs(tm, V_W, n_p)
    xp_spec, xs_spec = _group_specs(tm, D_MODEL, n_p)
    return pl.pallas_call(
        functools.partial(_post_mix_kernel, n_p=n_p),
        grid=(n // tm,),
        in_specs=[
            op_spec, os_spec, zcol(4), zcol(5), zcol(6), zcol(7), xp_spec, xs_spec,
            full(1, GM_WIDTH), full(1, GM_WIDTH),
            pl.BlockSpec((1, GM_GROUPS, GM_CHUNK, GM_CHUNK), lambda i: (grp(i), 0, 0, 0)),
            pl.BlockSpec((1, GM_CHUNK, GM_WIDTH), lambda i: (grp(i), 0, 0)),
            full(V_W, D_MODEL), full(GM_WIDTH, D_MODEL), full(D_MODEL, D_MODEL), full(1, D_MODEL),
            full(D_MODEL, LANES), full(D_MODEL, LANES), full(1, LANES), full(tm, tm),
        ],
        out_specs=[tok, pl.BlockSpec((tm, D_MODEL // 2), lambda i: (i, 0)), narrow, narrow,
                   pl.BlockSpec((tm, GM_WIDTH), lambda i: (jnp.maximum(i - pt, 0), 0)),
                   pl.BlockSpec((1, LANES), lambda i: (0, 0))],
        out_shape=[
            jax.ShapeDtypeStruct((n, D_MODEL), F32),
            jax.ShapeDtypeStruct((n, D_MODEL // 2), jnp.int32),
            jax.ShapeDtypeStruct((n, LANES), F32),
            jax.ShapeDtypeStruct((n, LANES), jnp.int32),
            jax.ShapeDtypeStruct((n_s, GM_WIDTH), F32),
            jax.ShapeDtypeStruct((1, LANES), F32),
        ],
        scratch_shapes=[pltpu.VMEM((tm, GM_WIDTH), BF16), pltpu.VMEM((1, LANES), F32)],
        compiler_params=_params("arbitrary"),
        name="chunk_mlp_post_mix_router",
    )(o_p, o_s, z, z, z, z, xp, xs, ln_g, ln_b, mix, bias, wa, wb, wo, g2, rwh, rwl, rb, tri)


def _moe_kernel(be_ref, first_ref, valid_ref, next_ref, slot_ref, x_ref, w1_hbm, b1_ref, w2_hbm, b2_ref, y_ref,
                w1_stage, w2_stage, sems):
    b = pl.program_id(0)
    slot = slot_ref[b]

    def weight_copies(e, s):
        return (pltpu.make_async_copy(w1_hbm.at[e], w1_stage.at[s], sems.at[0, s]),
                pltpu.make_async_copy(w2_hbm.at[e], w2_stage.at[s], sems.at[1, s]))

    @pl.when(b == 0)
    def _():
        for c in weight_copies(be_ref[0], slot):
            c.start()

    @pl.when(first_ref[b] == 1)
    def _():
        for c in weight_copies(be_ref[b], slot):
            c.wait()

        @pl.when(next_ref[b] >= 0)
        def _():
            for c in weight_copies(next_ref[b], 1 - slot):
                c.start()

    @pl.when(valid_ref[b] == 1)
    def _():
        x = jnp.concatenate(_unpack_bf16_pairs(x_ref[...]), axis=1).astype(BF16)
        hid = jnp.dot(x, w1_stage[slot].astype(BF16), preferred_element_type=F32) + b1_ref[0]
        gate = jnp.minimum(hid[:, :D_EXPERT], SWIGLU_LIMIT)
        up = jnp.clip(hid[:, D_EXPERT:], -SWIGLU_LIMIT, SWIGLU_LIMIT)
        act = gate * _sigmoid(SWIGLU_ALPHA * gate) * (up + 1.0)
        y = jnp.dot(act.astype(BF16), w2_stage[slot].astype(BF16), preferred_element_type=F32) + b2_ref[0]
        y_ref[...] = _pack_bf16_pairs(y)

    @pl.when(valid_ref[b] == 0)
    def _():
        y_ref[...] = jnp.zeros(y_ref.shape, jnp.int32)


def _moe_experts(block_e, first, valid, next_e, slot, xb, w1, b1, w2, b2):
    rows = xb.shape[0]
    nb = rows // MOE_ROWS
    smem4 = lambda f: (lambda b, be, fi, va, ne, sl: f(b, be))
    return pl.pallas_call(
        _moe_kernel,
        grid_spec=pltpu.PrefetchScalarGridSpec(
            num_scalar_prefetch=5,
            grid=(nb,),
            in_specs=[
                pl.BlockSpec((MOE_ROWS, D_MODEL // 2), smem4(lambda b, be: (b, 0))),
                pl.BlockSpec(memory_space=pl.ANY),
                pl.BlockSpec((1, 1, 2 * D_EXPERT), smem4(lambda b, be: (be[b], 0, 0))),
                pl.BlockSpec(memory_space=pl.ANY),
                pl.BlockSpec((1, 1, D_MODEL), smem4(lambda b, be: (be[b], 0, 0))),
            ],
            out_specs=pl.BlockSpec((MOE_ROWS, D_MODEL // 2), smem4(lambda b, be: (b, 0))),
            scratch_shapes=[
                pltpu.VMEM((2, D_MODEL, 2 * D_EXPERT), F32),
                pltpu.VMEM((2, D_EXPERT, D_MODEL), F32),
                pltpu.SemaphoreType.DMA((2, 2)),
            ],
        ),
        out_shape=jax.ShapeDtypeStruct((rows, D_MODEL // 2), jnp.int32),
        compiler_params=_params("arbitrary"),
        name="moe_experts",
    )(block_e, first, valid, next_e, slot, xb, w1, b1, w2, b2)


def _moe_dispatch(idx, rank, counts, n):
    experts = jnp.arange(N_EXPERTS, dtype=jnp.int32)
    padded = (counts + MOE_ROWS - 1) // MOE_ROWS * MOE_ROWS
    pad_end = jnp.cumsum(padded)
    pad_start = pad_end - padded
    start_of = jnp.sum(jnp.where(idx[:, :, None] == experts, pad_start, 0), axis=-1)
    dest = start_of + rank
    nb = -(-n * TOP_K // MOE_ROWS) + N_EXPERTS
    rows = nb * MOE_ROWS
    starts = jnp.arange(nb, dtype=jnp.int32) * MOE_ROWS
    valid = (starts < pad_end[-1]).astype(jnp.int32)
    owner = lambda r: jnp.minimum(jnp.sum((pad_end[None, :] <= r[:, None]).astype(jnp.int32), axis=1),
                                  N_EXPERTS - 1)
    last_e = owner(pad_end[-1:] - 1)[0]
    block_e = jnp.where(valid == 1, owner(starts), last_e).astype(jnp.int32)
    first = jnp.concatenate([jnp.ones((1,), jnp.int32),
                             (block_e[1:] != block_e[:-1]).astype(jnp.int32)])
    blk = jnp.arange(nb, dtype=jnp.int32)
    later_first = (blk[None, :] > blk[:, None]) & (first[None, :] == 1)
    next_pos = jnp.min(jnp.where(later_first, blk[None, :], nb), axis=1)
    next_e = jnp.where(next_pos < nb, block_e[jnp.minimum(next_pos, nb - 1)], -1).astype(jnp.int32)
    slot = (jnp.cumsum(first) - 1) % 2
    return dest.T, rows, block_e, first, valid, next_e, slot.astype(jnp.int32)


def _sc_scatter_rows(src, dest_t, out_rows):
    n, d = src.shape
    nk = dest_t.shape[0]
    assert nk * SC_TOKENS == SC_WINDOW
    idx = dest_t.reshape(nk, n // SC_TOKENS, SC_TOKENS).transpose(1, 0, 2).reshape(n // SC_TOKENS, SC_WINDOW)
    mesh = plsc.VectorSubcoreMesh(core_axis_name="core", subcore_axis_name="subcore",
                                  num_cores=SC_CORES, num_subcores=SC_SUBCORES)

    @pl.kernel(out_type=jax.ShapeDtypeStruct((out_rows, d), src.dtype), mesh=mesh, scratch_types=[])
    def scatter_rows(src_hbm, di_hbm, out_hbm):
        def body(x_vmem, di_vmem):
            for k in range(nk):
                pltpu.sync_copy(x_vmem, out_hbm.at[di_vmem.at[0, pl.ds(k * SC_TOKENS, SC_TOKENS)]])

        pltpu.emit_pipeline(
            body,
            grid=(n // SC_TOKENS,),
            in_specs=[pl.BlockSpec((SC_TOKENS, d), lambda i: (i, 0)),
                      pl.BlockSpec((1, SC_WINDOW), lambda i: (i, 0))],
            out_specs=[],
            core_axis_name=("core", "subcore"),
            dimension_semantics=(pltpu.PARALLEL,),
        )(src_hbm, di_hbm)

    return scatter_rows(src, idx)


def _sc_gather_rows(src, idx):
    m = idx.shape[0]
    d = src.shape[1]
    idx_rows = jnp.pad(idx.reshape(m // SC_CHUNK, SC_CHUNK), ((0, 0), (0, SC_WINDOW - SC_CHUNK)))
    mesh = plsc.VectorSubcoreMesh(core_axis_name="core", subcore_axis_name="subcore",
                                  num_cores=SC_CORES, num_subcores=SC_SUBCORES)

    @pl.kernel(out_type=jax.ShapeDtypeStruct((m, d), src.dtype), mesh=mesh, scratch_types=[])
    def gather_rows(src_hbm, si_hbm, out_hbm):
        def body(si_vmem, o_vmem):
            pltpu.sync_copy(src_hbm.at[si_vmem.at[0, pl.ds(0, SC_CHUNK)]], o_vmem)

        pltpu.emit_pipeline(
            body,
            grid=(m // SC_CHUNK,),
            in_specs=[pl.BlockSpec((1, SC_WINDOW), lambda i: (i, 0))],
            out_specs=[pl.BlockSpec((SC_CHUNK, d), lambda i: (i, 0))],
            core_axis_name=("core", "subcore"),
            dimension_semantics=(pltpu.PARALLEL,),
        )(si_hbm, out_hbm)

    return gather_rows(src, idx_rows)


def _tail_kernel(*refs, n_p, tile0, aliased):
    x1_ref, yk_ref, gate_ref, pp_ref, ps_ref, g3_ref, wg_ref, wp_ref, gf_ref = refs[:9]
    yp_ref, ys_ref = refs[9 + (1 if aliased else 0):]
    gates = gate_ref[...]
    lo, hi = _unpack_bf16_pairs(yk_ref[0])
    lo, hi = lo * gates[:, 0:1], hi * gates[:, 0:1]
    for k in range(1, TOP_K):
        lo_k, hi_k = _unpack_bf16_pairs(yk_ref[k])
        lo, hi = lo + lo_k * gates[:, k:k + 1], hi + hi_k * gates[:, k:k + 1]
    moe = jnp.concatenate([lo, hi], axis=1)
    x2 = x1_ref[...] + moe
    h3 = _rms(x2, g3_ref[...])
    gate = _sigmoid(jnp.dot(h3.astype(BF16), wg_ref[...], preferred_element_type=F32))
    pe = jnp.dot(_group_pick(pp_ref, ps_ref, n_p, tile0).astype(BF16), wp_ref[...],
                 preferred_element_type=F32)
    y = _rms(x2 + gate * pe, gf_ref[...])
    in_prompt = pl.program_id(0) + tile0 < n_p // yp_ref.shape[0]

    @pl.when(in_prompt)
    def _():
        yp_ref[...] = y

    @pl.when(jnp.logical_not(in_prompt))
    def _():
        ys_ref[...] = y


def _tail(x1, yk, gates, pp, ps, g3, wg, wp, gf, *, tile0=0, yp_prev=None):
    n_p, n_s = pp.shape[0], ps.shape[0]
    tm = _row_tile(n_p, n_s)
    aliased = yp_prev is not None
    tok = lambda w: pl.BlockSpec((tm, w), lambda i: (i + tile0, 0))
    full = lambda r, c: pl.BlockSpec((r, c), lambda i: (0, 0))
    in_specs = [tok(D_MODEL), pl.BlockSpec((TOP_K, tm, D_MODEL // 2), lambda i: (0, i, 0)), tok(LANES),
                *_group_specs(tm, PLE_DIM, n_p, tile0),
                full(1, D_MODEL), full(D_MODEL, D_MODEL), full(PLE_DIM, D_MODEL), full(1, D_MODEL)]
    args = [x1, yk, gates, pp, ps, g3, wg, wp, gf]
    if aliased:
        in_specs.append(pl.BlockSpec(memory_space=pl.ANY))
        args.append(yp_prev)
    return pl.pallas_call(
        functools.partial(_tail_kernel, n_p=n_p, tile0=tile0, aliased=aliased),
        grid=(yk.shape[1] // tm,),
        in_specs=in_specs,
        out_specs=_group_specs(tm, D_MODEL, n_p, tile0),
        out_shape=[jax.ShapeDtypeStruct((n_p, D_MODEL), F32), jax.ShapeDtypeStruct((n_s, D_MODEL), F32)],
        input_output_aliases={len(args) - 1: 0} if aliased else {},
        compiler_params=_params("arbitrary"),
        name="ple_final_norm",
    )(*args)


def _lane_pad(v, offset, fill=0.0):
    out = jnp.full((1, LANES), fill, F32)
    return out.at[0, offset:offset + v.shape[0]].set(v.astype(F32))


def kernel(x_prompt, x_sample, state_delta, state_conv, p_prompt, p_sample, norm1_g, w_in, conv_w, a_log, dt_bias, dn_norm_g, w_proj_a, gm_ln_g, gm_ln_b, gm_ws, gm_bs, w_proj_b, w_out, norm2_g, router_w, router_b, moe_w1, moe_b1, moe_w2, moe_b2, norm3_g, ple_w, ple_gate_w, final_norm_g):
    bp, lp, d = x_prompt.shape
    bs, ls, _ = x_sample.shape
    depth = w_in.shape[0]
    assert depth == 1 and d == D_MODEL
    assert lp % GM_CHUNK == 0 and GM_CHUNK % ls == 0 and ls >= DN_CONV - 1
    n_p, n_s = bp * lp, bs * ls
    n = n_p + n_s
    i = 0

    xp, xs = x_prompt.reshape(n_p, d), x_sample.reshape(n_s, d)

    ab0 = QKV_W
    w = w_in[i]
    w_main = jnp.concatenate([w[:, :ab0], w[:, ab0 + 2 * DN_HEADS:]], axis=1).astype(BF16)
    w_ab = jnp.pad(w[:, ab0:ab0 + 2 * DN_HEADS], ((0, 0), (0, LANES - 2 * DN_HEADS))).astype(BF16)
    row2 = lambda v: v.reshape(1, -1).astype(F32)

    z, ab = _in_proj(xp, xs, row2(norm1_g[i]), w_main, w_ab)

    alog_p = _lane_pad(a_log[i], 0)
    dtb_p = _lane_pad(dt_bias[i], 0)
    cw = conv_w[i].astype(F32)
    ong = row2(dn_norm_g[i])
    zero_s = jnp.zeros((bp, DN_HEADS, DN_DK, DN_DV), F32)
    zero_buf = jnp.zeros((bp, DN_CONV - 1, QKV_W), F32)
    o_p, sd_p, sc_p = _delta_branch(z, ab, zero_buf, zero_s, cw, alog_p, dtb_p, ong,
                                    tok0=0, L=lp, G=4, precise=False)
    o_s, sd_s, sc_s = _delta_branch(z, ab, state_conv[i], state_delta[i], cw, alog_p, dtb_p, ong,
                                    tok0=n_p, L=ls, G=8, precise=False)

    t = GM_CHUNK
    tri = jnp.tril(jnp.ones((t, t), bool))
    ws = gm_ws[i]
    mix_p = jnp.where(tri, ws, 0.0)
    small = jnp.where(tri[:ls, :ls], ws[:, :ls, :ls], 0.0)
    mix_s = jnp.einsum('ab,gts->gatbs', jnp.eye(t // ls, dtype=F32), small).reshape(GM_GROUPS, t, t)
    mix = jnp.stack([mix_p, mix_s]).astype(BF16)
    gw = GM_WIDTH // GM_GROUPS
    bias_p = jnp.repeat(gm_bs[i].T, gw, axis=1)
    bias_s = jnp.tile(bias_p[:ls], (t // ls, 1))
    bias = jnp.stack([bias_p, bias_s]).astype(F32)
    rw = jnp.pad(router_w[i].astype(F32), ((0, 0), (0, LANES - N_EXPERTS)))
    rwh = rw.astype(BF16)
    rwl = (rw - rwh.astype(F32)).astype(BF16)
    rb = _lane_pad(router_b[i], 0, fill=-jnp.inf)
    x1, h2, gates, route, v_s, counts = _post_mix(
        o_p, o_s, z, xp, xs, row2(gm_ln_g[i]), row2(gm_ln_b[i]), mix, bias, w_proj_a[i].astype(BF16),
        w_proj_b[i].astype(BF16), w_out[i].astype(BF16), row2(norm2_g[i]), rwh, rwl, rb)
    dest_t, rows, block_e, first, valid, next_e, slot = _moe_dispatch(
        route[:, :TOP_K], route[:, TOP_K:2 * TOP_K], counts[0, :N_EXPERTS].astype(jnp.int32), n)
    xb = _sc_scatter_rows(h2, dest_t, rows)
    yb = _moe_experts(block_e, first, valid, next_e, slot, xb, moe_w1[i], moe_b1[i][:, None, :],
                      moe_w2[i], moe_b2[i][:, None, :])
    tm = _row_tile(n_p, n_s)
    half = (n // tm + 1) // 2 * tm
    tail_w = (row2(norm3_g[i]), ple_gate_w[i].astype(BF16), ple_w[i].astype(BF16), row2(final_norm_g))
    pp, ps = p_prompt[i].reshape(n_p, PLE_DIM), p_sample[i].reshape(n_s, PLE_DIM)
    yk_a = _sc_gather_rows(yb, dest_t[:, :half].reshape(-1)).reshape(TOP_K, half, d // 2)
    yk_b = _sc_gather_rows(yb, dest_t[:, half:].reshape(-1)).reshape(TOP_K, n - half, d // 2)
    y_p, _ = _tail(x1, yk_a, gates, pp, ps, *tail_w)
    y_p, y_s = _tail(x1, yk_b, gates, pp, ps, *tail_w, tile0=half // tm, yp_prev=y_p)

    return (y_p.reshape(bp, lp, d), y_s.reshape(bs, ls, d),
            sd_p[None], sc_p[None], sd_s[None], sc_s[None], v_s.reshape(1, bs, ls, GM_WIDTH))
```

```python
import functools
import math

import jax
import jax.numpy as jnp
from jax import lax
from jax.experimental import pallas as pl
from jax.experimental.pallas import tpu as pltpu
from jax.experimental.pallas import tpu_sc as plsc

F32 = jnp.float32
BF16 = jnp.bfloat16

D_MODEL = 1024
DN_HEADS = 8
DN_DK = 128
DN_DV = 128
DN_CONV = 4
DN_CHUNK = 64
GM_WIDTH = 1024
GM_GROUPS = 8
GM_CHUNK = 128
N_EXPERTS = 32
TOP_K = 4
D_EXPERT = 1024
SWIGLU_LIMIT = 7.0
SWIGLU_ALPHA = 1.702
PLE_DIM = 256
EPS = 1e-6
QK_W = DN_HEADS * DN_DK
V_W = DN_HEADS * DN_DV
QKV_W = 2 * QK_W + V_W

LANES = 128
MOE_ROWS = 512
VMEM_LIMIT = 56 << 20
SC_CORES, SC_SUBCORES = 2, 16
SC_WINDOW = 128
SC_TOKENS = SC_WINDOW // TOP_K
SC_CHUNK = 64


def _params(*sem):
    return pltpu.CompilerParams(dimension_semantics=sem, vmem_limit_bytes=VMEM_LIMIT)


def _dot(a, b):
    return jnp.dot(a.astype(BF16), b.astype(BF16), preferred_element_type=F32)


def _split(a):
    hi = a.astype(BF16)
    return hi, (a - hi.astype(F32)).astype(BF16)


def _dot3(a, b):
    ah, al = _split(a)
    bh, bl = _split(b)
    d = functools.partial(jnp.dot, preferred_element_type=F32)
    return d(ah, bh) + (d(ah, bl) + d(al, bh))


def _dot_nt(a, b):
    return lax.dot_general(a.astype(BF16), b.astype(BF16), (((1,), (1,)), ((), ())),
                           preferred_element_type=F32)


def _dot_tn(a, b):
    return lax.dot_general(a.astype(BF16), b.astype(BF16), (((0,), (0,)), ((), ())),
                           preferred_element_type=F32)


def _dot3_tn(a, b):
    ah, al = _split(a)
    bh, bl = _split(b)
    d = functools.partial(lax.dot_general, dimension_numbers=(((0,), (0,)), ((), ())),
                          preferred_element_type=F32)
    return d(ah, bh) + (d(ah, bl) + d(al, bh))


def _sigmoid(x):
    return 0.5 * jnp.tanh(0.5 * x) + 0.5


def _pack_bf16_pairs(x):
    bits = lax.bitcast_convert_type(x.astype(BF16).astype(F32), jnp.uint32)
    half = x.shape[1] // 2
    return lax.bitcast_convert_type((bits[:, :half] >> 16) | bits[:, half:], jnp.int32)


def _unpack_bf16_pairs(w):
    bits = lax.bitcast_convert_type(w, jnp.uint32)
    return (lax.bitcast_convert_type(bits << 16, F32),
            lax.bitcast_convert_type(bits & jnp.uint32(0xFFFF0000), F32))


def _rms(x, g):
    return x * lax.rsqrt(jnp.mean(x * x, axis=-1, keepdims=True) + EPS) * g


def _gelu(x):
    return 0.5 * x * (1.0 + lax.erf(x * (1.0 / math.sqrt(2.0))))


def _row_tile(n_p, n_s, cap=512):
    for t in (1024, 512, 256, 128):
        if t > cap:
            continue
        if n_p % t == 0 and n_s % t == 0:
            return t
    raise ValueError(f"token counts {n_p}, {n_s} must be multiples of 128")


def _group_specs(tm, width, n_p):
    pt = n_p // tm
    return [pl.BlockSpec((tm, width), lambda i, *_: (jnp.minimum(i, pt - 1), 0)),
            pl.BlockSpec((tm, width), lambda i, *_: (jnp.maximum(i - pt, 0), 0))]


def _group_pick(prompt_ref, sample_ref, n_p):
    pt = n_p // prompt_ref.shape[0]
    return jnp.where(pl.program_id(0) < pt, prompt_ref[...], sample_ref[...])


def _in_proj_kernel(xp_ref, xs_ref, g_ref, w_ref, wab_ref, z_ref, ab_ref, h_scr, *, n_p):
    @pl.when(pl.program_id(1) == 0)
    def _():
        hb = _rms(_group_pick(xp_ref, xs_ref, n_p), g_ref[...]).astype(BF16)
        h_scr[...] = hb
        ab_ref[...] = jnp.dot(hb, wab_ref[...], preferred_element_type=F32)

    z_ref[...] = jnp.dot(h_scr[...], w_ref[...], preferred_element_type=F32).astype(z_ref.dtype)


def _in_proj(xp, xs, g, w_main, w_ab):
    n_p, n_s = xp.shape[0], xs.shape[0]
    n = n_p + n_s
    tm, tn = _row_tile(n_p, n_s, cap=1024), 2048
    cols = w_main.shape[1]
    return pl.pallas_call(
        functools.partial(_in_proj_kernel, n_p=n_p),
        grid=(n // tm, cols // tn),
        in_specs=_group_specs(tm, D_MODEL, n_p) + [
            pl.BlockSpec((1, D_MODEL), lambda i, j: (0, 0)),
            pl.BlockSpec((D_MODEL, tn), lambda i, j: (0, j)),
            pl.BlockSpec((D_MODEL, LANES), lambda i, j: (0, 0)),
        ],
        out_specs=[
            pl.BlockSpec((tm, tn), lambda i, j: (i, j)),
            pl.BlockSpec((tm, LANES), lambda i, j: (i, 0)),
        ],
        out_shape=[jax.ShapeDtypeStruct((n, cols), BF16), jax.ShapeDtypeStruct((n, LANES), F32)],
        scratch_shapes=[pltpu.VMEM((tm, D_MODEL), BF16)],
        compiler_params=_params("parallel", "arbitrary"),
        name="in_proj",
    )(xp, xs, g, w_main, w_ab)


def _delta_kernel(*refs, C, G, carry, precise):
    (qkv_ref, zg_ref, ab_ref, buf_ref, s0_ref, cw_ref, alog_ref, dtb_ref, ong_ref) = refs[:9]
    refs = refs[9:]
    o_ref, snew_ref, bufnew_ref, xc_scr = refs[:4]
    H, DK, DV = DN_HEADS, DN_DK, DN_DV
    T = G * C
    dotm = _dot3 if precise else _dot
    dotm_tn = _dot3_tn if precise else _dot_tn
    halo = DN_CONV - 1
    base = 8
    cw = cw_ref[...]

    def conv(window):
        y = window(0) * cw[0:1]
        for i in range(1, DN_CONV):
            y = y + window(i) * cw[i:i + 1]
        return y

    if carry:
        s_scr = refs[4]
        c = pl.program_id(1)
        last = pl.num_programs(1) - 1

        @pl.when(c == 0)
        def _():
            xc_scr[base - halo:base, :] = buf_ref[0]
            s_scr[...] = s0_ref[0]

        xb = qkv_ref[...]
        xf = xb.astype(F32)
        xc_scr[base:base + 8, :] = xf[0:8]
        y_head = conv(lambda i: xc_scr[base - halo + i:base - halo + i + 8, :])
        ri = lax.broadcasted_iota(jnp.int32, (halo * T, T), 0)
        ci = lax.broadcasted_iota(jnp.int32, (halo * T, T), 1)
        src_row = (ri & (T - 1)) - (halo - (ri >> (T.bit_length() - 1)))
        shifted = jnp.dot(jnp.where(ci == src_row, 1.0, 0.0).astype(BF16), xb, preferred_element_type=F32)
        y = conv(lambda i: shifted[i * T:(i + 1) * T] if i < halo else xf)
        y = jnp.concatenate([y_head, y[8:]], axis=0)
        tail = xf[T - halo:T]
        xc_scr[base - halo:base, :] = tail

        @pl.when(c == last)
        def _():
            bufnew_ref[0] = tail
    else:
        ys = []
        x_new = qkv_ref[...].astype(F32)
        for g in range(G):
            xc_scr[g, base - halo:base, :] = buf_ref[g]
            xc_scr[g, base:base + C, :] = x_new[g * C:(g + 1) * C, :]
            ys.append(conv(lambda i: xc_scr[g, base - halo + i:base - halo + i + C, :]))
            bufnew_ref[g] = xc_scr[g, base + C - halo:base + C, :]
        y = jnp.concatenate(ys, axis=0) if G > 1 else ys[0]
    qkv = y * _sigmoid(y)

    ab = ab_ref[...]
    g_all = -jnp.exp(alog_ref[...]) * jax.nn.softplus(ab + dtb_ref[...])
    beta_all = _sigmoid(ab)
    shift = C.bit_length() - 1
    rt = lax.broadcasted_iota(jnp.int32, (T, T), 0)
    ct = lax.broadcasted_iota(jnp.int32, (T, T), 1)
    chunk_tril = ((rt >> shift) == (ct >> shift)) & (rt >= ct)
    gcum = _dot3(chunk_tril.astype(F32), g_all)
    gam_all = jnp.exp(gcum)

    row = lax.broadcasted_iota(jnp.int32, (C, C), 0)
    col = lax.broadcasted_iota(jnp.int32, (C, C), 1)
    incl, strict, eye = row >= col, row > col, row == col
    eyef = eye.astype(F32)
    units = [(g, h) for g in range(G) for h in range(H)]
    rows = lambda a, g: a[g * C:(g + 1) * C]

    qn, kn, vv = [], [], []
    for h in range(H):
        q = qkv[:, h * DK:(h + 1) * DK]
        k = qkv[:, QK_W + h * DK:QK_W + (h + 1) * DK]
        qn.append(q * (lax.rsqrt(jnp.sum(q * q, axis=-1, keepdims=True) + EPS) * (DK ** -0.5)))
        kn.append(k * lax.rsqrt(jnp.sum(k * k, axis=-1, keepdims=True) + EPS))
        vv.append(qkv[:, 2 * QK_W + h * DV:2 * QK_W + (h + 1) * DV])
    qb = [q.astype(BF16) for q in qn]
    kb = [k.astype(BF16) for k in kn]

    gc, bc, gl, a_low, m_intra = {}, {}, {}, {}, {}
    for u in units:
        g, h = u
        gc[u] = rows(gcum, g)[:, h:h + 1]
        bc[u] = rows(beta_all, g)[:, H + h:H + h + 1]
        gl[u] = gc[u][C - 1:C, :]
        gr = jnp.sum(jnp.where(eye, gc[u], 0.0), axis=0, keepdims=True)
        br = jnp.sum(jnp.where(eye, bc[u], 0.0), axis=0, keepdims=True)
        db = jnp.where(incl, jnp.exp(gc[u] - gr), 0.0) * br
        k = rows(kb[h], g)
        kq = _dot_nt(jnp.concatenate([k, rows(qb[h], g)], axis=0), k)
        a_low[u] = jnp.where(strict, kq[:C] * db, 0.0)
        m_intra[u] = kq[C:] * db

    t_inv = {u: eyef - a_low[u] for u in units}
    a_pow = {u: dotm(a_low[u], a_low[u]) for u in units}
    n = 2
    while n < C:
        t_inv = {u: t_inv[u] + dotm(t_inv[u], a_pow[u]) for u in units}
        n *= 2
        if n < C:
            a_pow = {u: dotm(a_pow[u], a_pow[u]) for u in units}

    u_base, wq, k_dec = {}, {}, {}
    for u in units:
        g, h = u
        gam = rows(gam_all, g)[:, h:h + 1]
        k = rows(kn[h], g)
        sol = dotm(t_inv[u], jnp.concatenate([rows(vv[h], g), gam * k], axis=1))
        u_base[u] = sol[:, :DV]
        wq[u] = jnp.concatenate([sol[:, DV:], gam * rows(qn[h], g)], axis=0)
        k_dec[u] = k * (bc[u] * jnp.exp(gl[u] - gc[u]))

    state = [s_scr[h] for h in range(H)] if carry else None
    outs = {}
    for g in range(G):
        s_in = state if carry else [s0_ref[g, h] for h in range(H)]
        wqs = [dotm(wq[(g, h)], s_in[h]) for h in range(H)]
        us = [u_base[(g, h)] - wqs[h][:C] for h in range(H)]
        for h in range(H):
            outs[(g, h)] = wqs[h][C:] + dotm(m_intra[(g, h)], us[h])
        s_out = [jnp.exp(gl[(g, h)]) * s_in[h] + dotm_tn(k_dec[(g, h)], us[h]) for h in range(H)]
        if carry:
            state = s_out
        else:
            for h in range(H):
                snew_ref[g, h] = s_out[h]

    for h in range(H):
        o = jnp.concatenate([outs[(g, h)] for g in range(G)], axis=0) if G > 1 else outs[(0, h)]
        zg = zg_ref[:, h * DV:(h + 1) * DV].astype(F32)
        o_ref[:, h * DV:(h + 1) * DV] = (_rms(o, ong_ref[...]) * (zg * _sigmoid(zg))).astype(o_ref.dtype)

    if carry:
        for h in range(H):
            s_scr[h] = state[h]

        @pl.when(c == last)
        def _():
            for h in range(H):
                snew_ref[0, h] = state[h]


def _delta_branch(z, ab, conv_buf, s0, conv_w, alog_p, dtb_p, onorm_g, *, tok0, L, G, precise):
    B = conv_buf.shape[0]
    C = math.gcd(L, DN_CHUNK)
    nc = L // C
    carry = nc > 1
    T = G * C
    blk0 = tok0 // T
    if carry:
        assert nc % G == 0
        grid = (B, nc // G)
        own_blk = lambda b, c: b * (nc // G) + c
        gs = 1
    else:
        assert B % G == 0
        grid = (B // G, 1)
        own_blk = lambda b, c: b
        gs = G
    tok = lambda b, c: (blk0 + own_blk(b, c), 0)
    seq3 = lambda b, c: (b, 0, 0)
    seq4 = lambda b, c: (b, 0, 0, 0)
    const = lambda b, c: (0, 0)
    if carry:
        scratch = [pltpu.VMEM((16, QKV_W), F32), pltpu.VMEM((DN_HEADS, DN_DK, DN_DV), F32)]
    else:
        scratch = [pltpu.VMEM((G, 8 + C, QKV_W), F32)]
    return pl.pallas_call(
        functools.partial(_delta_kernel, C=C, G=G, carry=carry, precise=precise),
        grid=grid,
        in_specs=[
            pl.BlockSpec((T, QKV_W), tok),
            pl.BlockSpec((T, V_W), lambda b, c: (blk0 + own_blk(b, c), QKV_W // V_W)),
            pl.BlockSpec((T, LANES), tok),
            pl.BlockSpec((gs, DN_CONV - 1, QKV_W), seq3),
            pl.BlockSpec((gs, DN_HEADS, DN_DK, DN_DV), seq4),
            pl.BlockSpec((DN_CONV, QKV_W), const),
            pl.BlockSpec((1, LANES), const),
            pl.BlockSpec((1, LANES), const),
            pl.BlockSpec((1, DN_DV), const),
        ],
        out_specs=[
            pl.BlockSpec((T, V_W), lambda b, c: (own_blk(b, c), 0)),
            pl.BlockSpec((gs, DN_HEADS, DN_DK, DN_DV), seq4),
            pl.BlockSpec((gs, DN_CONV - 1, QKV_W), seq3),
        ],
        out_shape=[
            jax.ShapeDtypeStruct((B * L, V_W), BF16),
            jax.ShapeDtypeStruct(s0.shape, F32),
            jax.ShapeDtypeStruct(conv_buf.shape, F32),
        ],
        scratch_shapes=scratch,
        compiler_params=_params("parallel", "arbitrary"),
        name=f"delta_rule_c{C}",
    )(z, z, ab, conv_buf, s0, conv_w, alog_p, dtb_p, onorm_g)


def _post_mix_kernel(op_ref, os_ref, gu_ref, gv_ref, ma_ref, mb_ref, xp_ref, xs_ref, lng_ref, lnb_ref,
                     mix_ref, bias_ref, wa_ref, wb_ref, wo_ref, g2_ref, rwh_ref, rwl_ref, rb_ref, tri_ref,
                     x1_ref, h2_ref, gate_ref, idx_ref, v_ref, cnt_ref, us_scr, cnt_scr, *, n_p):
    u = _gelu(gu_ref[...].astype(F32))
    a = _gelu(gv_ref[...].astype(F32))
    ac = a - jnp.mean(a, axis=-1, keepdims=True)
    v = ac * lax.rsqrt(jnp.mean(ac * ac, axis=-1, keepdims=True) + EPS) * lng_ref[...] + lnb_ref[...]
    v_ref[...] = v
    vb = v.astype(BF16)
    gw = GM_WIDTH // GM_GROUPS
    for c in range(u.shape[0] // GM_CHUNK):
        rs = slice(c * GM_CHUNK, (c + 1) * GM_CHUNK)
        for g in range(GM_GROUPS):
            sl = slice(g * gw, (g + 1) * gw)
            s = jnp.dot(mix_ref[0, g], vb[rs, sl], preferred_element_type=F32) + bias_ref[0, :, sl]
            us_scr[rs, sl] = (u[rs, sl] * s).astype(BF16)

    ya = jnp.dot(_group_pick(op_ref, os_ref, n_p).astype(BF16), wa_ref[...], preferred_element_type=F32)
    yb = jnp.dot(us_scr[...], wb_ref[...], preferred_element_type=F32)
    mixed = _sigmoid(ma_ref[...].astype(F32)) * ya + _sigmoid(mb_ref[...].astype(F32)) * yb
    x1 = _group_pick(xp_ref, xs_ref, n_p) + jnp.dot(mixed.astype(BF16), wo_ref[...], preferred_element_type=F32)
    x1_ref[...] = x1
    h2 = _rms(x1, g2_ref[...])
    h2_ref[...] = _pack_bf16_pairs(h2)
    hh, hl = _split(h2)
    d = functools.partial(jnp.dot, preferred_element_type=F32)
    logits = d(hh, rwh_ref[...]) + (d(hh, rwl_ref[...]) + d(hl, rwh_ref[...])) + rb_ref[...]
    lane = lax.broadcasted_iota(jnp.int32, logits.shape, 1).astype(F32)
    vals, idxs = [], []
    for _ in range(TOP_K):
        m = jnp.max(logits, axis=-1, keepdims=True)
        i = jnp.min(jnp.where(logits == m, lane, float(LANES)), axis=-1, keepdims=True)
        vals.append(m)
        idxs.append(i)
        logits = jnp.where(lane == i, -jnp.inf, logits)
    es = [jnp.exp(v - vals[0]) for v in vals]
    tot = es[0]
    for e in es[1:]:
        tot = tot + e

    @pl.when(pl.program_id(0) == 0)
    def _():
        cnt_scr[...] = jnp.zeros(cnt_scr.shape, F32)

    onehot = (lane == idxs[0]).astype(F32)
    for k in range(1, TOP_K):
        onehot = onehot + (lane == idxs[k]).astype(F32)
    before = jnp.dot(tri_ref[...], onehot.astype(BF16), preferred_element_type=F32) + cnt_scr[...]
    cnt_scr[...] = cnt_scr[...] + jnp.sum(onehot, axis=0, keepdims=True)
    cnt_ref[...] = cnt_scr[...]

    gates = jnp.zeros(logits.shape, F32)
    route = jnp.zeros(logits.shape, F32)
    for k in range(TOP_K):
        rank = jnp.sum(jnp.where(lane == idxs[k], before, 0.0), axis=-1, keepdims=True)
        gates = jnp.where(lane == float(k), es[k] / tot, gates)
        route = jnp.where(lane == float(k), idxs[k], route)
        route = jnp.where(lane == float(TOP_K + k), rank, route)
    gate_ref[...] = gates
    idx_ref[...] = route.T[:2 * TOP_K].astype(jnp.int32)


def _post_mix(o_p, o_s, z, xp, xs, ln_g, ln_b, mix, bias, wa, wb, wo, g2, rwh, rwl, rb):
    n_p, n_s = xp.shape[0], xs.shape[0]
    n = n_p + n_s
    tm = _row_tile(n_p, n_s)
    pt = n_p // tm
    tri = jnp.tril(jnp.ones((tm, tm), BF16), k=-1)
    grp = lambda i: jnp.where(i < pt, 0, 1)
    once = pl.Buffered(1)
    tok = pl.BlockSpec((tm, D_MODEL), lambda i: (i, 0))
    zcol = lambda c: pl.BlockSpec((tm, D_MODEL), lambda i: (i, c))
    full = lambda r, c: pl.BlockSpec((r, c), lambda i: (0, 0), pipeline_mode=once)
    narrow = pl.BlockSpec((tm, LANES), lambda i: (i, 0))
    op_spec, os_spec = _group_specs(tm, V_W, n_p)
    xp_spec, xs_spec = _group_specs(tm, D_MODEL, n_p)
    return pl.pallas_call(
        functools.partial(_post_mix_kernel, n_p=n_p),
        grid=(n // tm,),
        in_specs=[
            op_spec, os_spec, zcol(4), zcol(5), zcol(6), zcol(7), xp_spec, xs_spec,
            full(1, GM_WIDTH), full(1, GM_WIDTH),
            pl.BlockSpec((1, GM_GROUPS, GM_CHUNK, GM_CHUNK), lambda i: (grp(i), 0, 0, 0)),
            pl.BlockSpec((1, GM_CHUNK, GM_WIDTH), lambda i: (grp(i), 0, 0)),
            full(V_W, D_MODEL), full(GM_WIDTH, D_MODEL), full(D_MODEL, D_MODEL), full(1, D_MODEL),
            full(D_MODEL, LANES), full(D_MODEL, LANES), full(1, LANES), full(tm, tm),
        ],
        out_specs=[tok, pl.BlockSpec((tm, D_MODEL // 2), lambda i: (i, 0)), narrow,
                   pl.BlockSpec((2 * TOP_K, tm), lambda i: (0, i)),
                   pl.BlockSpec((tm, GM_WIDTH), lambda i: (jnp.maximum(i - pt, 0), 0)),
                   pl.BlockSpec((1, LANES), lambda i: (0, 0))],
        out_shape=[
            jax.ShapeDtypeStruct((n, D_MODEL), F32),
            jax.ShapeDtypeStruct((n, D_MODEL // 2), jnp.int32),
            jax.ShapeDtypeStruct((n, LANES), F32),
            jax.ShapeDtypeStruct((2 * TOP_K, n), jnp.int32),
            jax.ShapeDtypeStruct((n_s, GM_WIDTH), F32),
            jax.ShapeDtypeStruct((1, LANES), F32),
        ],
        scratch_shapes=[pltpu.VMEM((tm, GM_WIDTH), BF16), pltpu.VMEM((1, LANES), F32)],
        compiler_params=_params("arbitrary"),
        name="chunk_mlp_post_mix_router",
    )(o_p, o_s, z, z, z, z, xp, xs, ln_g, ln_b, mix, bias, wa, wb, wo, g2, rwh, rwl, rb, tri)


def _moe_kernel(be_ref, first_ref, valid_ref, next_ref, slot_ref, x_ref, w1_hbm, b1_ref, w2_hbm, b2_ref, y_ref,
                w1_stage, w2_stage, sems):
    b = pl.program_id(0)
    slot = slot_ref[b]

    def weight_copies(e, s):
        return (pltpu.make_async_copy(w1_hbm.at[e], w1_stage.at[s], sems.at[0, s]),
                pltpu.make_async_copy(w2_hbm.at[e], w2_stage.at[s], sems.at[1, s]))

    @pl.when(b == 0)
    def _():
        for c in weight_copies(be_ref[0], slot):
            c.start()

    @pl.when(first_ref[b] == 1)
    def _():
        for c in weight_copies(be_ref[b], slot):
            c.wait()

        @pl.when(next_ref[b] >= 0)
        def _():
            for c in weight_copies(next_ref[b], 1 - slot):
                c.start()

    @pl.when(valid_ref[b] == 1)
    def _():
        x = jnp.concatenate(_unpack_bf16_pairs(x_ref[...]), axis=1).astype(BF16)
        hid = jnp.dot(x, w1_stage[slot].astype(BF16), preferred_element_type=F32) + b1_ref[0]
        gate = jnp.minimum(hid[:, :D_EXPERT], SWIGLU_LIMIT)
        up = jnp.clip(hid[:, D_EXPERT:], -SWIGLU_LIMIT, SWIGLU_LIMIT)
        act = gate * _sigmoid(SWIGLU_ALPHA * gate) * (up + 1.0)
        y = jnp.dot(act.astype(BF16), w2_stage[slot].astype(BF16), preferred_element_type=F32) + b2_ref[0]
        y_ref[...] = _pack_bf16_pairs(y)

    @pl.when(valid_ref[b] == 0)
    def _():
        y_ref[...] = jnp.zeros(y_ref.shape, jnp.int32)


def _moe_experts(block_e, first, valid, next_e, slot, xb, w1, b1, w2, b2):
    rows = xb.shape[0]
    nb = rows // MOE_ROWS
    smem4 = lambda f: (lambda b, be, fi, va, ne, sl: f(b, be))
    return pl.pallas_call(
        _moe_kernel,
        grid_spec=pltpu.PrefetchScalarGridSpec(
            num_scalar_prefetch=5,
            grid=(nb,),
            in_specs=[
                pl.BlockSpec((MOE_ROWS, D_MODEL // 2), smem4(lambda b, be: (b, 0))),
                pl.BlockSpec(memory_space=pl.ANY),
                pl.BlockSpec((1, 1, 2 * D_EXPERT), smem4(lambda b, be: (be[b], 0, 0))),
                pl.BlockSpec(memory_space=pl.ANY),
                pl.BlockSpec((1, 1, D_MODEL), smem4(lambda b, be: (be[b], 0, 0))),
            ],
            out_specs=pl.BlockSpec((MOE_ROWS, D_MODEL // 2), smem4(lambda b, be: (b, 0))),
            scratch_shapes=[
                pltpu.VMEM((2, D_MODEL, 2 * D_EXPERT), F32),
                pltpu.VMEM((2, D_EXPERT, D_MODEL), F32),
                pltpu.SemaphoreType.DMA((2, 2)),
            ],
        ),
        out_shape=jax.ShapeDtypeStruct((rows, D_MODEL // 2), jnp.int32),
        compiler_params=_params("arbitrary"),
        name="moe_experts",
    )(block_e, first, valid, next_e, slot, xb, w1, b1, w2, b2)


def _moe_dispatch(idx, rank, counts, n):
    experts = jnp.arange(N_EXPERTS, dtype=jnp.int32)
    padded = (counts + MOE_ROWS - 1) // MOE_ROWS * MOE_ROWS
    pad_end = jnp.cumsum(padded)
    pad_start = pad_end - padded
    start_of = jnp.sum(jnp.where(idx[None] == experts[:, None, None], pad_start[:, None, None], 0), axis=0)
    dest_t = start_of + rank
    nb = -(-n * TOP_K // MOE_ROWS) + N_EXPERTS
    rows = nb * MOE_ROWS
    starts = jnp.arange(nb, dtype=jnp.int32) * MOE_ROWS
    valid = (starts < pad_end[-1]).astype(jnp.int32)
    owner = lambda r: jnp.minimum(jnp.sum((pad_end[None, :] <= r[:, None]).astype(jnp.int32), axis=1),
                                  N_EXPERTS - 1)
    last_e = owner(pad_end[-1:] - 1)[0]
    block_e = jnp.where(valid == 1, owner(starts), last_e).astype(jnp.int32)
    first = jnp.concatenate([jnp.ones((1,), jnp.int32),
                             (block_e[1:] != block_e[:-1]).astype(jnp.int32)])
    blk = jnp.arange(nb, dtype=jnp.int32)
    later_first = (blk[None, :] > blk[:, None]) & (first[None, :] == 1)
    next_pos = jnp.min(jnp.where(later_first, blk[None, :], nb), axis=1)
    next_e = jnp.where(next_pos < nb, block_e[jnp.minimum(next_pos, nb - 1)], -1).astype(jnp.int32)
    slot = (jnp.cumsum(first) - 1) % 2
    return dest_t, rows, block_e, first, valid, next_e, slot.astype(jnp.int32)


def _sc_scatter_rows(src, dest_t, out_rows):
    n, d = src.shape
    nk = dest_t.shape[0]
    assert nk * SC_TOKENS == SC_WINDOW
    idx = dest_t.reshape(nk, n // SC_TOKENS, SC_TOKENS).transpose(1, 0, 2).reshape(n // SC_TOKENS, SC_WINDOW)
    mesh = plsc.VectorSubcoreMesh(core_axis_name="core", subcore_axis_name="subcore",
                                  num_cores=SC_CORES, num_subcores=SC_SUBCORES)

    @pl.kernel(out_type=jax.ShapeDtypeStruct((out_rows, d), src.dtype), mesh=mesh, scratch_types=[])
    def scatter_rows(src_hbm, di_hbm, out_hbm):
        def body(x_vmem, di_vmem):
            for k in range(nk):
                pltpu.sync_copy(x_vmem, out_hbm.at[di_vmem.at[0, pl.ds(k * SC_TOKENS, SC_TOKENS)]])

        pltpu.emit_pipeline(
            body,
            grid=(n // SC_TOKENS,),
            in_specs=[pl.BlockSpec((SC_TOKENS, d), lambda i: (i, 0)),
                      pl.BlockSpec((1, SC_WINDOW), lambda i: (i, 0))],
            out_specs=[],
            core_axis_name=("core", "subcore"),
            dimension_semantics=(pltpu.PARALLEL,),
        )(src_hbm, di_hbm)

    return scatter_rows(src, idx)


def _sc_gather_rows(src, idx):
    m = idx.shape[0]
    d = src.shape[1]
    idx_rows = jnp.pad(idx.reshape(m // SC_CHUNK, SC_CHUNK), ((0, 0), (0, SC_WINDOW - SC_CHUNK)))
    mesh = plsc.VectorSubcoreMesh(core_axis_name="core", subcore_axis_name="subcore",
                                  num_cores=SC_CORES, num_subcores=SC_SUBCORES)

    @pl.kernel(out_type=jax.ShapeDtypeStruct((m, d), src.dtype), mesh=mesh, scratch_types=[])
    def gather_rows(src_hbm, si_hbm, out_hbm):
        def body(si_vmem, o_vmem):
            pltpu.sync_copy(src_hbm.at[si_vmem.at[0, pl.ds(0, SC_CHUNK)]], o_vmem)

        pltpu.emit_pipeline(
            body,
            grid=(m // SC_CHUNK,),
            in_specs=[pl.BlockSpec((1, SC_WINDOW), lambda i: (i, 0))],
            out_specs=[pl.BlockSpec((SC_CHUNK, d), lambda i: (i, 0))],
            core_axis_name=("core", "subcore"),
            dimension_semantics=(pltpu.PARALLEL,),
        )(si_hbm, out_hbm)

    return gather_rows(src, idx_rows)


def _tail_kernel(x1_ref, yk_ref, gate_ref, pp_ref, ps_ref, g3_ref, wg_ref, wp_ref, gf_ref,
                 yp_ref, ys_ref, *, n_p):
    gates = gate_ref[...]
    lo, hi = _unpack_bf16_pairs(yk_ref[0])
    lo, hi = lo * gates[:, 0:1], hi * gates[:, 0:1]
    for k in range(1, TOP_K):
        lo_k, hi_k = _unpack_bf16_pairs(yk_ref[k])
        lo, hi = lo + lo_k * gates[:, k:k + 1], hi + hi_k * gates[:, k:k + 1]
    moe = jnp.concatenate([lo, hi], axis=1)
    x2 = x1_ref[...] + moe
    h3 = _rms(x2, g3_ref[...])
    gate = _sigmoid(jnp.dot(h3.astype(BF16), wg_ref[...], preferred_element_type=F32))
    pe = jnp.dot(_group_pick(pp_ref, ps_ref, n_p).astype(BF16), wp_ref[...], preferred_element_type=F32)
    y = _rms(x2 + gate * pe, gf_ref[...])
    in_prompt = pl.program_id(0) < n_p // yp_ref.shape[0]

    @pl.when(in_prompt)
    def _():
        yp_ref[...] = y

    @pl.when(jnp.logical_not(in_prompt))
    def _():
        ys_ref[...] = y


def _tail(x1, yk, gates, pp, ps, g3, wg, wp, gf):
    n_p, n_s = pp.shape[0], ps.shape[0]
    n = n_p + n_s
    tm = _row_tile(n_p, n_s)
    tok = pl.BlockSpec((tm, D_MODEL), lambda i: (i, 0))
    full = lambda r, c: pl.BlockSpec((r, c), lambda i: (0, 0))
    return pl.pallas_call(
        functools.partial(_tail_kernel, n_p=n_p),
        grid=(n // tm,),
        in_specs=[tok, pl.BlockSpec((TOP_K, tm, D_MODEL // 2), lambda i: (0, i, 0)),
                  pl.BlockSpec((tm, LANES), lambda i: (i, 0)),
                  *_group_specs(tm, PLE_DIM, n_p),
                  full(1, D_MODEL), full(D_MODEL, D_MODEL), full(PLE_DIM, D_MODEL), full(1, D_MODEL)],
        out_specs=_group_specs(tm, D_MODEL, n_p),
        out_shape=[jax.ShapeDtypeStruct((n_p, D_MODEL), F32), jax.ShapeDtypeStruct((n_s, D_MODEL), F32)],
        compiler_params=_params("arbitrary"),
        name="ple_final_norm",
    )(x1, yk, gates, pp, ps, g3, wg, wp, gf)


def _lane_pad(v, offset, fill=0.0):
    out = jnp.full((1, LANES), fill, F32)
    return out.at[0, offset:offset + v.shape[0]].set(v.astype(F32))


def kernel(x_prompt, x_sample, state_delta, state_conv, p_prompt, p_sample, norm1_g, w_in, conv_w, a_log, dt_bias, dn_norm_g, w_proj_a, gm_ln_g, gm_ln_b, gm_ws, gm_bs, w_proj_b, w_out, norm2_g, router_w, router_b, moe_w1, moe_b1, moe_w2, moe_b2, norm3_g, ple_w, ple_gate_w, final_norm_g):
    bp, lp, d = x_prompt.shape
    bs, ls, _ = x_sample.shape
    depth = w_in.shape[0]
    assert depth == 1 and d == D_MODEL
    assert lp % GM_CHUNK == 0 and GM_CHUNK % ls == 0 and ls >= DN_CONV - 1
    n_p, n_s = bp * lp, bs * ls
    n = n_p + n_s
    i = 0

    xp, xs = x_prompt.reshape(n_p, d), x_sample.reshape(n_s, d)

    ab0 = QKV_W
    w = w_in[i]
    w_main = jnp.concatenate([w[:, :ab0], w[:, ab0 + 2 * DN_HEADS:]], axis=1).astype(BF16)
    w_ab = jnp.pad(w[:, ab0:ab0 + 2 * DN_HEADS], ((0, 0), (0, LANES - 2 * DN_HEADS))).astype(BF16)
    row2 = lambda v: v.reshape(1, -1).astype(F32)

    z, ab = _in_proj(xp, xs, row2(norm1_g[i]), w_main, w_ab)

    alog_p = _lane_pad(a_log[i], 0)
    dtb_p = _lane_pad(dt_bias[i], 0)
    cw = conv_w[i].astype(F32)
    ong = row2(dn_norm_g[i])
    zero_s = jnp.zeros((bp, DN_HEADS, DN_DK, DN_DV), F32)
    zero_buf = jnp.zeros((bp, DN_CONV - 1, QKV_W), F32)
    o_p, sd_p, sc_p = _delta_branch(z, ab, zero_buf, zero_s, cw, alog_p, dtb_p, ong,
                                    tok0=0, L=lp, G=4, precise=False)
    o_s, sd_s, sc_s = _delta_branch(z, ab, state_conv[i], state_delta[i], cw, alog_p, dtb_p, ong,
                                    tok0=n_p, L=ls, G=8, precise=False)

    t = GM_CHUNK
    tri = jnp.tril(jnp.ones((t, t), bool))
    ws = gm_ws[i]
    mix_p = jnp.where(tri, ws, 0.0)
    small = jnp.where(tri[:ls, :ls], ws[:, :ls, :ls], 0.0)
    mix_s = jnp.einsum('ab,gts->gatbs', jnp.eye(t // ls, dtype=F32), small).reshape(GM_GROUPS, t, t)
    mix = jnp.stack([mix_p, mix_s]).astype(BF16)
    gw = GM_WIDTH // GM_GROUPS
    bias_p = jnp.repeat(gm_bs[i].T, gw, axis=1)
    bias_s = jnp.tile(bias_p[:ls], (t // ls, 1))
    bias = jnp.stack([bias_p, bias_s]).astype(F32)
    rw = jnp.pad(router_w[i].astype(F32), ((0, 0), (0, LANES - N_EXPERTS)))
    rwh = rw.astype(BF16)
    rwl = (rw - rwh.astype(F32)).astype(BF16)
    rb = _lane_pad(router_b[i], 0, fill=-jnp.inf)
    x1, h2, gates, route, v_s, counts = _post_mix(
        o_p, o_s, z, xp, xs, row2(gm_ln_g[i]), row2(gm_ln_b[i]), mix, bias, w_proj_a[i].astype(BF16),
        w_proj_b[i].astype(BF16), w_out[i].astype(BF16), row2(norm2_g[i]), rwh, rwl, rb)
    dest_t, rows, block_e, first, valid, next_e, slot = _moe_dispatch(
        route[:TOP_K], route[TOP_K:], counts[0, :N_EXPERTS].astype(jnp.int32), n)
    xb = _sc_scatter_rows(h2, dest_t, rows)
    yb = _moe_experts(block_e, first, valid, next_e, slot, xb, moe_w1[i], moe_b1[i][:, None, :],
                      moe_w2[i], moe_b2[i][:, None, :])
    yk = _sc_gather_rows(yb, dest_t.reshape(TOP_K * n)).reshape(TOP_K, n, d // 2)

    y_p, y_s = _tail(x1, yk, gates, p_prompt[i].reshape(n_p, PLE_DIM), p_sample[i].reshape(n_s, PLE_DIM),
                     row2(norm3_g[i]), ple_gate_w[i].astype(BF16), ple_w[i].astype(BF16), row2(final_norm_g))

    return (y_p.reshape(bp, lp, d), y_s.reshape(bs, ls, d),
            sd_p[None], sc_p[None], sd_s[None], sc_s[None], v_s.reshape(1, bs, ls, GM_WIDTH))
```

```python
import functools
import math

import jax
import jax.numpy as jnp
from jax import lax
from jax.experimental import pallas as pl
from jax.experimental.pallas import tpu as pltpu
from jax.experimental.pallas import tpu_sc as plsc

F32 = jnp.float32
BF16 = jnp.bfloat16

D_MODEL = 1024
DN_HEADS = 8
DN_DK = 128
DN_DV = 128
DN_CONV = 4
DN_CHUNK = 64
GM_WIDTH = 1024
GM_GROUPS = 8
GM_CHUNK = 128
N_EXPERTS = 32
TOP_K = 4
D_EXPERT = 1024
SWIGLU_LIMIT = 7.0
SWIGLU_ALPHA = 1.702
PLE_DIM = 256
EPS = 1e-6
QK_W = DN_HEADS * DN_DK
V_W = DN_HEADS * DN_DV
QKV_W = 2 * QK_W + V_W

LANES = 128
MOE_ROWS = 512
VMEM_LIMIT = 56 << 20
SC_CORES, SC_SUBCORES = 2, 16
SC_WINDOW = 128
SC_TOKENS = SC_WINDOW // TOP_K
SC_CHUNK = 64


def _params(*sem):
    return pltpu.CompilerParams(dimension_semantics=sem, vmem_limit_bytes=VMEM_LIMIT)


def _dot(a, b):
    return jnp.dot(a.astype(BF16), b.astype(BF16), preferred_element_type=F32)


def _split(a):
    hi = a.astype(BF16)
    return hi, (a - hi.astype(F32)).astype(BF16)


def _dot3(a, b):
    ah, al = _split(a)
    bh, bl = _split(b)
    d = functools.partial(jnp.dot, preferred_element_type=F32)
    return d(ah, bh) + (d(ah, bl) + d(al, bh))


def _dot_nt(a, b):
    return lax.dot_general(a.astype(BF16), b.astype(BF16), (((1,), (1,)), ((), ())),
                           preferred_element_type=F32)


def _dot_tn(a, b):
    return lax.dot_general(a.astype(BF16), b.astype(BF16), (((0,), (0,)), ((), ())),
                           preferred_element_type=F32)


def _dot3_tn(a, b):
    ah, al = _split(a)
    bh, bl = _split(b)
    d = functools.partial(lax.dot_general, dimension_numbers=(((0,), (0,)), ((), ())),
                          preferred_element_type=F32)
    return d(ah, bh) + (d(ah, bl) + d(al, bh))


def _sigmoid(x):
    return 0.5 * jnp.tanh(0.5 * x) + 0.5


def _pack_bf16_pairs(x):
    bits = lax.bitcast_convert_type(x.astype(BF16).astype(F32), jnp.uint32)
    half = x.shape[1] // 2
    return lax.bitcast_convert_type((bits[:, :half] >> 16) | bits[:, half:], jnp.int32)


def _unpack_bf16_pairs(w):
    bits = lax.bitcast_convert_type(w, jnp.uint32)
    return (lax.bitcast_convert_type(bits << 16, F32),
            lax.bitcast_convert_type(bits & jnp.uint32(0xFFFF0000), F32))


def _rms(x, g):
    return x * lax.rsqrt(jnp.mean(x * x, axis=-1, keepdims=True) + EPS) * g


def _gelu(x):
    return 0.5 * x * (1.0 + lax.erf(x * (1.0 / math.sqrt(2.0))))


def _row_tile(n_p, n_s, cap=512):
    for t in (1024, 512, 256, 128):
        if t > cap:
            continue
        if n_p % t == 0 and n_s % t == 0:
            return t
    raise ValueError(f"token counts {n_p}, {n_s} must be multiples of 128")


def _group_specs(tm, width, n_p):
    pt = n_p // tm
    return [pl.BlockSpec((tm, width), lambda i, *_: (jnp.minimum(i, pt - 1), 0)),
            pl.BlockSpec((tm, width), lambda i, *_: (jnp.maximum(i - pt, 0), 0))]


def _group_pick(prompt_ref, sample_ref, n_p):
    pt = n_p // prompt_ref.shape[0]
    return jnp.where(pl.program_id(0) < pt, prompt_ref[...], sample_ref[...])


def _in_proj_kernel(xp_ref, xs_ref, g_ref, w_ref, wab_ref, z_ref, ab_ref, h_scr, *, n_p):
    @pl.when(pl.program_id(1) == 0)
    def _():
        hb = _rms(_group_pick(xp_ref, xs_ref, n_p), g_ref[...]).astype(BF16)
        h_scr[...] = hb
        ab_ref[...] = jnp.dot(hb, wab_ref[...], preferred_element_type=F32)

    z_ref[...] = jnp.dot(h_scr[...], w_ref[...], preferred_element_type=F32).astype(z_ref.dtype)


def _in_proj(xp, xs, g, w_main, w_ab):
    n_p, n_s = xp.shape[0], xs.shape[0]
    n = n_p + n_s
    tm, tn = _row_tile(n_p, n_s, cap=1024), 2048
    cols = w_main.shape[1]
    return pl.pallas_call(
        functools.partial(_in_proj_kernel, n_p=n_p),
        grid=(n // tm, cols // tn),
        in_specs=_group_specs(tm, D_MODEL, n_p) + [
            pl.BlockSpec((1, D_MODEL), lambda i, j: (0, 0)),
            pl.BlockSpec((D_MODEL, tn), lambda i, j: (0, j)),
            pl.BlockSpec((D_MODEL, LANES), lambda i, j: (0, 0)),
        ],
        out_specs=[
            pl.BlockSpec((tm, tn), lambda i, j: (i, j)),
            pl.BlockSpec((tm, LANES), lambda i, j: (i, 0)),
        ],
        out_shape=[jax.ShapeDtypeStruct((n, cols), BF16), jax.ShapeDtypeStruct((n, LANES), F32)],
        scratch_shapes=[pltpu.VMEM((tm, D_MODEL), BF16)],
        compiler_params=_params("parallel", "arbitrary"),
        name="in_proj",
    )(xp, xs, g, w_main, w_ab)


def _delta_kernel(*refs, C, G, carry, precise):
    (qkv_ref, zg_ref, ab_ref, buf_ref, s0_ref, cw_ref, alog_ref, dtb_ref, ong_ref) = refs[:9]
    refs = refs[9:]
    o_ref, snew_ref, bufnew_ref, xc_scr = refs[:4]
    H, DK, DV = DN_HEADS, DN_DK, DN_DV
    T = G * C
    dotm = _dot3 if precise else _dot
    dotm_tn = _dot3_tn if precise else _dot_tn
    halo = DN_CONV - 1
    base = 8
    cw = cw_ref[...]

    def conv(window):
        y = window(0) * cw[0:1]
        for i in range(1, DN_CONV):
            y = y + window(i) * cw[i:i + 1]
        return y

    if carry:
        s_scr = refs[4]
        c = pl.program_id(1)
        last = pl.num_programs(1) - 1

        @pl.when(c == 0)
        def _():
            xc_scr[base - halo:base, :] = buf_ref[0]
            s_scr[...] = s0_ref[0]

        xb = qkv_ref[...]
        xf = xb.astype(F32)
        xc_scr[base:base + 8, :] = xf[0:8]
        y_head = conv(lambda i: xc_scr[base - halo + i:base - halo + i + 8, :])
        ri = lax.broadcasted_iota(jnp.int32, (halo * T, T), 0)
        ci = lax.broadcasted_iota(jnp.int32, (halo * T, T), 1)
        src_row = (ri & (T - 1)) - (halo - (ri >> (T.bit_length() - 1)))
        shifted = jnp.dot(jnp.where(ci == src_row, 1.0, 0.0).astype(BF16), xb, preferred_element_type=F32)
        y = conv(lambda i: shifted[i * T:(i + 1) * T] if i < halo else xf)
        y = jnp.concatenate([y_head, y[8:]], axis=0)
        tail = xf[T - halo:T]
        xc_scr[base - halo:base, :] = tail

        @pl.when(c == last)
        def _():
            bufnew_ref[0] = tail
    else:
        ys = []
        x_new = qkv_ref[...].astype(F32)
        for g in range(G):
            xc_scr[g, base - halo:base, :] = buf_ref[g]
            xc_scr[g, base:base + C, :] = x_new[g * C:(g + 1) * C, :]
            ys.append(conv(lambda i: xc_scr[g, base - halo + i:base - halo + i + C, :]))
            bufnew_ref[g] = xc_scr[g, base + C - halo:base + C, :]
        y = jnp.concatenate(ys, axis=0) if G > 1 else ys[0]
    qkv = y * _sigmoid(y)

    ab = ab_ref[...]
    g_all = -jnp.exp(alog_ref[...]) * jax.nn.softplus(ab + dtb_ref[...])
    beta_all = _sigmoid(ab)
    shift = C.bit_length() - 1
    rt = lax.broadcasted_iota(jnp.int32, (T, T), 0)
    ct = lax.broadcasted_iota(jnp.int32, (T, T), 1)
    chunk_tril = ((rt >> shift) == (ct >> shift)) & (rt >= ct)
    gcum = _dot3(chunk_tril.astype(F32), g_all)
    gam_all = jnp.exp(gcum)
    wide = C >= LANES // 2
    if wide:
        gcum_t, beta_t = gcum.T, beta_all.T

    row = lax.broadcasted_iota(jnp.int32, (C, C), 0)
    col = lax.broadcasted_iota(jnp.int32, (C, C), 1)
    incl, strict, eye = row >= col, row > col, row == col
    eyef = eye.astype(F32)
    units = [(g, h) for g in range(G) for h in range(H)]
    rows = lambda a, g: a[g * C:(g + 1) * C]

    qn, kn, vv = [], [], []
    for h in range(H):
        q = qkv[:, h * DK:(h + 1) * DK]
        k = qkv[:, QK_W + h * DK:QK_W + (h + 1) * DK]
        qn.append(q * (lax.rsqrt(jnp.sum(q * q, axis=-1, keepdims=True) + EPS) * (DK ** -0.5)))
        kn.append(k * lax.rsqrt(jnp.sum(k * k, axis=-1, keepdims=True) + EPS))
        vv.append(qkv[:, 2 * QK_W + h * DV:2 * QK_W + (h + 1) * DV])
    qb = [q.astype(BF16) for q in qn]
    kb = [k.astype(BF16) for k in kn]

    gc, bc, gl, a_low, m_intra = {}, {}, {}, {}, {}
    for u in units:
        g, h = u
        gc[u] = rows(gcum, g)[:, h:h + 1]
        bc[u] = rows(beta_all, g)[:, H + h:H + h + 1]
        gl[u] = gc[u][C - 1:C, :]
        if wide:
            gr = gcum_t[h:h + 1, g * C:(g + 1) * C]
            br = beta_t[H + h:H + h + 1, g * C:(g + 1) * C]
        else:
            gr = jnp.sum(jnp.where(eye, gc[u], 0.0), axis=0, keepdims=True)
            br = jnp.sum(jnp.where(eye, bc[u], 0.0), axis=0, keepdims=True)
        db = jnp.where(incl, jnp.exp(gc[u] - gr), 0.0) * br
        k = rows(kb[h], g)
        kq = _dot_nt(jnp.concatenate([k, rows(qb[h], g)], axis=0), k)
        a_low[u] = jnp.where(strict, kq[:C] * db, 0.0)
        m_intra[u] = kq[C:] * db

    t_inv = {u: eyef - a_low[u] for u in units}
    a_pow = {u: dotm(a_low[u], a_low[u]) for u in units}
    n = 2
    while n < C:
        t_inv = {u: t_inv[u] + dotm(t_inv[u], a_pow[u]) for u in units}
        n *= 2
        if n < C:
            a_pow = {u: dotm(a_pow[u], a_pow[u]) for u in units}

    u_base, wq, k_dec = {}, {}, {}
    for u in units:
        g, h = u
        gam = rows(gam_all, g)[:, h:h + 1]
        k = rows(kn[h], g)
        sol = dotm(t_inv[u], jnp.concatenate([rows(vv[h], g), gam * k], axis=1))
        u_base[u] = sol[:, :DV]
        wq[u] = jnp.concatenate([sol[:, DV:], gam * rows(qn[h], g)], axis=0)
        k_dec[u] = k * (bc[u] * jnp.exp(gl[u] - gc[u]))

    state = [s_scr[h] for h in range(H)] if carry else None
    outs = {}
    for g in range(G):
        s_in = state if carry else [s0_ref[g, h] for h in range(H)]
        wqs = [dotm(wq[(g, h)], s_in[h]) for h in range(H)]
        us = [u_base[(g, h)] - wqs[h][:C] for h in range(H)]
        for h in range(H):
            outs[(g, h)] = wqs[h][C:] + dotm(m_intra[(g, h)], us[h])
        s_out = [jnp.exp(gl[(g, h)]) * s_in[h] + dotm_tn(k_dec[(g, h)], us[h]) for h in range(H)]
        if carry:
            state = s_out
        else:
            for h in range(H):
                snew_ref[g, h] = s_out[h]

    for h in range(H):
        o = jnp.concatenate([outs[(g, h)] for g in range(G)], axis=0) if G > 1 else outs[(0, h)]
        zg = zg_ref[:, h * DV:(h + 1) * DV].astype(F32)
        o_ref[:, h * DV:(h + 1) * DV] = (_rms(o, ong_ref[...]) * (zg * _sigmoid(zg))).astype(o_ref.dtype)

    if carry:
        for h in range(H):
            s_scr[h] = state[h]

        @pl.when(c == last)
        def _():
            for h in range(H):
                snew_ref[0, h] = state[h]


def _delta_branch(z, ab, conv_buf, s0, conv_w, alog_p, dtb_p, onorm_g, *, tok0, L, G, precise):
    B = conv_buf.shape[0]
    C = math.gcd(L, DN_CHUNK)
    nc = L // C
    carry = nc > 1
    T = G * C
    blk0 = tok0 // T
    if carry:
        assert nc % G == 0
        grid = (B, nc // G)
        own_blk = lambda b, c: b * (nc // G) + c
        gs = 1
    else:
        assert B % G == 0
        grid = (B // G, 1)
        own_blk = lambda b, c: b
        gs = G
    tok = lambda b, c: (blk0 + own_blk(b, c), 0)
    seq3 = lambda b, c: (b, 0, 0)
    seq4 = lambda b, c: (b, 0, 0, 0)
    const = lambda b, c: (0, 0)
    if carry:
        scratch = [pltpu.VMEM((16, QKV_W), F32), pltpu.VMEM((DN_HEADS, DN_DK, DN_DV), F32)]
    else:
        scratch = [pltpu.VMEM((G, 8 + C, QKV_W), F32)]
    return pl.pallas_call(
        functools.partial(_delta_kernel, C=C, G=G, carry=carry, precise=precise),
        grid=grid,
        in_specs=[
            pl.BlockSpec((T, QKV_W), tok),
            pl.BlockSpec((T, V_W), lambda b, c: (blk0 + own_blk(b, c), QKV_W // V_W)),
            pl.BlockSpec((T, LANES), tok),
            pl.BlockSpec((gs, DN_CONV - 1, QKV_W), seq3),
            pl.BlockSpec((gs, DN_HEADS, DN_DK, DN_DV), seq4),
            pl.BlockSpec((DN_CONV, QKV_W), const),
            pl.BlockSpec((1, LANES), const),
            pl.BlockSpec((1, LANES), const),
            pl.BlockSpec((1, DN_DV), const),
        ],
        out_specs=[
            pl.BlockSpec((T, V_W), lambda b, c: (own_blk(b, c), 0)),
            pl.BlockSpec((gs, DN_HEADS, DN_DK, DN_DV), seq4),
            pl.BlockSpec((gs, DN_CONV - 1, QKV_W), seq3),
        ],
        out_shape=[
            jax.ShapeDtypeStruct((B * L, V_W), BF16),
            jax.ShapeDtypeStruct(s0.shape, F32),
            jax.ShapeDtypeStruct(conv_buf.shape, F32),
        ],
        scratch_shapes=scratch,
        compiler_params=_params("parallel", "arbitrary"),
        name=f"delta_rule_c{C}",
    )(z, z, ab, conv_buf, s0, conv_w, alog_p, dtb_p, onorm_g)


def _post_mix_kernel(op_ref, os_ref, gu_ref, gv_ref, ma_ref, mb_ref, xp_ref, xs_ref, lng_ref, lnb_ref,
                     mix_ref, bias_ref, wa_ref, wb_ref, wo_ref, g2_ref, rwh_ref, rwl_ref, rb_ref, tri_ref,
                     x1_ref, h2_ref, gate_ref, idx_ref, v_ref, cnt_ref, us_scr, cnt_scr, *, n_p):
    u = _gelu(gu_ref[...].astype(F32))
    a = _gelu(gv_ref[...].astype(F32))
    ac = a - jnp.mean(a, axis=-1, keepdims=True)
    v = ac * lax.rsqrt(jnp.mean(ac * ac, axis=-1, keepdims=True) + EPS) * lng_ref[...] + lnb_ref[...]
    v_ref[...] = v
    vb = v.astype(BF16)
    gw = GM_WIDTH // GM_GROUPS
    for c in range(u.shape[0] // GM_CHUNK):
        rs = slice(c * GM_CHUNK, (c + 1) * GM_CHUNK)
        for g in range(GM_GROUPS):
            sl = slice(g * gw, (g + 1) * gw)
            s = jnp.dot(mix_ref[0, g], vb[rs, sl], preferred_element_type=F32) + bias_ref[0, :, sl]
            us_scr[rs, sl] = (u[rs, sl] * s).astype(BF16)

    ya = jnp.dot(_group_pick(op_ref, os_ref, n_p).astype(BF16), wa_ref[...], preferred_element_type=F32)
    yb = jnp.dot(us_scr[...], wb_ref[...], preferred_element_type=F32)
    mixed = _sigmoid(ma_ref[...].astype(F32)) * ya + _sigmoid(mb_ref[...].astype(F32)) * yb
    x1 = _group_pick(xp_ref, xs_ref, n_p) + jnp.dot(mixed.astype(BF16), wo_ref[...], preferred_element_type=F32)
    x1_ref[...] = x1
    h2 = _rms(x1, g2_ref[...])
    h2_ref[...] = _pack_bf16_pairs(h2)
    hh, hl = _split(h2)
    d = functools.partial(jnp.dot, preferred_element_type=F32)
    logits = d(hh, rwh_ref[...]) + (d(hh, rwl_ref[...]) + d(hl, rwh_ref[...])) + rb_ref[...]
    lane = lax.broadcasted_iota(jnp.int32, logits.shape, 1).astype(F32)
    vals, idxs = [], []
    for _ in range(TOP_K):
        m = jnp.max(logits, axis=-1, keepdims=True)
        i = jnp.min(jnp.where(logits == m, lane, float(LANES)), axis=-1, keepdims=True)
        vals.append(m)
        idxs.append(i)
        logits = jnp.where(lane == i, -jnp.inf, logits)
    es = [jnp.exp(v - vals[0]) for v in vals]
    tot = es[0]
    for e in es[1:]:
        tot = tot + e

    @pl.when(pl.program_id(0) == 0)
    def _():
        cnt_scr[...] = jnp.zeros(cnt_scr.shape, F32)

    onehot = (lane == idxs[0]).astype(F32)
    for k in range(1, TOP_K):
        onehot = onehot + (lane == idxs[k]).astype(F32)
    before = jnp.dot(tri_ref[...], onehot.astype(BF16), preferred_element_type=F32) + cnt_scr[...]
    cnt_scr[...] = cnt_scr[...] + jnp.sum(onehot, axis=0, keepdims=True)
    cnt_ref[...] = cnt_scr[...]

    gates = jnp.zeros(logits.shape, F32)
    route = jnp.zeros(logits.shape, F32)
    for k in range(TOP_K):
        rank = jnp.sum(jnp.where(lane == idxs[k], before, 0.0), axis=-1, keepdims=True)
        gates = jnp.where(lane == float(k), es[k] / tot, gates)
        route = jnp.where(lane == float(k), idxs[k], route)
        route = jnp.where(lane == float(TOP_K + k), rank, route)
    gate_ref[...] = gates
    idx_ref[...] = route.T[:2 * TOP_K].astype(jnp.int32)


def _post_mix(o_p, o_s, z, xp, xs, ln_g, ln_b, mix, bias, wa, wb, wo, g2, rwh, rwl, rb):
    n_p, n_s = xp.shape[0], xs.shape[0]
    n = n_p + n_s
    tm = _row_tile(n_p, n_s)
    pt = n_p // tm
    tri = jnp.tril(jnp.ones((tm, tm), BF16), k=-1)
    grp = lambda i: jnp.where(i < pt, 0, 1)
    once = pl.Buffered(1)
    tok = pl.BlockSpec((tm, D_MODEL), lambda i: (i, 0))
    zcol = lambda c: pl.BlockSpec((tm, D_MODEL), lambda i: (i, c))
    full = lambda r, c: pl.BlockSpec((r, c), lambda i: (0, 0), pipeline_mode=once)
    narrow = pl.BlockSpec((tm, LANES), lambda i: (i, 0))
    op_spec, os_spec = _group_specs(tm, V_W, n_p)
    xp_spec, xs_spec = _group_specs(tm, D_MODEL, n_p)
    return pl.pallas_call(
        functools.partial(_post_mix_kernel, n_p=n_p),
        grid=(n // tm,),
        in_specs=[
            op_spec, os_spec, zcol(4), zcol(5), zcol(6), zcol(7), xp_spec, xs_spec,
            full(1, GM_WIDTH), full(1, GM_WIDTH),
            pl.BlockSpec((1, GM_GROUPS, GM_CHUNK, GM_CHUNK), lambda i: (grp(i), 0, 0, 0)),
            pl.BlockSpec((1, GM_CHUNK, GM_WIDTH), lambda i: (grp(i), 0, 0)),
            full(V_W, D_MODEL), full(GM_WIDTH, D_MODEL), full(D_MODEL, D_MODEL), full(1, D_MODEL),
            full(D_MODEL, LANES), full(D_MODEL, LANES), full(1, LANES), full(tm, tm),
        ],
        out_specs=[tok, pl.BlockSpec((tm, D_MODEL // 2), lambda i: (i, 0)), narrow,
                   pl.BlockSpec((2 * TOP_K, tm), lambda i: (0, i)),
                   pl.BlockSpec((tm, GM_WIDTH), lambda i: (jnp.maximum(i - pt, 0), 0)),
                   pl.BlockSpec((1, LANES), lambda i: (0, 0))],
        out_shape=[
            jax.ShapeDtypeStruct((n, D_MODEL), F32),
            jax.ShapeDtypeStruct((n, D_MODEL // 2), jnp.int32),
            jax.ShapeDtypeStruct((n, LANES), F32),
            jax.ShapeDtypeStruct((2 * TOP_K, n), jnp.int32),
            jax.ShapeDtypeStruct((n_s, GM_WIDTH), F32),
            jax.ShapeDtypeStruct((1, LANES), F32),
        ],
        scratch_shapes=[pltpu.VMEM((tm, GM_WIDTH), BF16), pltpu.VMEM((1, LANES), F32)],
        compiler_params=_params("arbitrary"),
        name="chunk_mlp_post_mix_router",
    )(o_p, o_s, z, z, z, z, xp, xs, ln_g, ln_b, mix, bias, wa, wb, wo, g2, rwh, rwl, rb, tri)


def _moe_kernel(be_ref, first_ref, valid_ref, next_ref, slot_ref, x_ref, w1_hbm, b1_ref, w2_hbm, b2_ref, y_ref,
                w1_stage, w2_stage, sems):
    b = pl.program_id(0)
    slot = slot_ref[b]

    def weight_copies(e, s):
        return (pltpu.make_async_copy(w1_hbm.at[e], w1_stage.at[s], sems.at[0, s]),
                pltpu.make_async_copy(w2_hbm.at[e], w2_stage.at[s], sems.at[1, s]))

    @pl.when(b == 0)
    def _():
        for c in weight_copies(be_ref[0], slot):
            c.start()

    @pl.when(first_ref[b] == 1)
    def _():
        for c in weight_copies(be_ref[b], slot):
            c.wait()

        @pl.when(next_ref[b] >= 0)
        def _():
            for c in weight_copies(next_ref[b], 1 - slot):
                c.start()

    @pl.when(valid_ref[b] == 1)
    def _():
        x = jnp.concatenate(_unpack_bf16_pairs(x_ref[...]), axis=1).astype(BF16)
        hid = jnp.dot(x, w1_stage[slot].astype(BF16), preferred_element_type=F32) + b1_ref[0]
        gate = jnp.minimum(hid[:, :D_EXPERT], SWIGLU_LIMIT)
        up = jnp.clip(hid[:, D_EXPERT:], -SWIGLU_LIMIT, SWIGLU_LIMIT)
        act = gate * _sigmoid(SWIGLU_ALPHA * gate) * (up + 1.0)
        y = jnp.dot(act.astype(BF16), w2_stage[slot].astype(BF16), preferred_element_type=F32) + b2_ref[0]
        y_ref[...] = _pack_bf16_pairs(y)

    @pl.when(valid_ref[b] == 0)
    def _():
        y_ref[...] = jnp.zeros(y_ref.shape, jnp.int32)


def _moe_experts(block_e, first, valid, next_e, slot, xb, w1, b1, w2, b2):
    rows = xb.shape[0]
    nb = rows // MOE_ROWS
    smem4 = lambda f: (lambda b, be, fi, va, ne, sl: f(b, be))
    return pl.pallas_call(
        _moe_kernel,
        grid_spec=pltpu.PrefetchScalarGridSpec(
            num_scalar_prefetch=5,
            grid=(nb,),
            in_specs=[
                pl.BlockSpec((MOE_ROWS, D_MODEL // 2), smem4(lambda b, be: (b, 0))),
                pl.BlockSpec(memory_space=pl.ANY),
                pl.BlockSpec((1, 1, 2 * D_EXPERT), smem4(lambda b, be: (be[b], 0, 0))),
                pl.BlockSpec(memory_space=pl.ANY),
                pl.BlockSpec((1, 1, D_MODEL), smem4(lambda b, be: (be[b], 0, 0))),
            ],
            out_specs=pl.BlockSpec((MOE_ROWS, D_MODEL // 2), smem4(lambda b, be: (b, 0))),
            scratch_shapes=[
                pltpu.VMEM((2, D_MODEL, 2 * D_EXPERT), F32),
                pltpu.VMEM((2, D_EXPERT, D_MODEL), F32),
                pltpu.SemaphoreType.DMA((2, 2)),
            ],
        ),
        out_shape=jax.ShapeDtypeStruct((rows, D_MODEL // 2), jnp.int32),
        compiler_params=_params("arbitrary"),
        name="moe_experts",
    )(block_e, first, valid, next_e, slot, xb, w1, b1, w2, b2)


def _moe_dispatch(idx, rank, counts, n):
    experts = jnp.arange(N_EXPERTS, dtype=jnp.int32)
    padded = (counts + MOE_ROWS - 1) // MOE_ROWS * MOE_ROWS
    pad_end = jnp.cumsum(padded)
    pad_start = pad_end - padded
    start_of = jnp.sum(jnp.where(idx[None] == experts[:, None, None], pad_start[:, None, None], 0), axis=0)
    dest_t = start_of + rank
    nb = -(-n * TOP_K // MOE_ROWS) + N_EXPERTS
    rows = nb * MOE_ROWS
    starts = jnp.arange(nb, dtype=jnp.int32) * MOE_ROWS
    valid = (starts < pad_end[-1]).astype(jnp.int32)
    owner = lambda r: jnp.minimum(jnp.sum((pad_end[None, :] <= r[:, None]).astype(jnp.int32), axis=1),
                                  N_EXPERTS - 1)
    last_e = owner(pad_end[-1:] - 1)[0]
    block_e = jnp.where(valid == 1, owner(starts), last_e).astype(jnp.int32)
    first = jnp.concatenate([jnp.ones((1,), jnp.int32),
                             (block_e[1:] != block_e[:-1]).astype(jnp.int32)])
    blk = jnp.arange(nb, dtype=jnp.int32)
    later_first = (blk[None, :] > blk[:, None]) & (first[None, :] == 1)
    next_pos = jnp.min(jnp.where(later_first, blk[None, :], nb), axis=1)
    next_e = jnp.where(next_pos < nb, block_e[jnp.minimum(next_pos, nb - 1)], -1).astype(jnp.int32)
    slot = (jnp.cumsum(first) - 1) % 2
    return dest_t, rows, block_e, first, valid, next_e, slot.astype(jnp.int32)


def _sc_scatter_rows(src, dest_t, out_rows):
    n, d = src.shape
    nk = dest_t.shape[0]
    assert nk * SC_TOKENS == SC_WINDOW
    idx = dest_t.reshape(nk, n // SC_TOKENS, SC_TOKENS).transpose(1, 0, 2).reshape(n // SC_TOKENS, SC_WINDOW)
    mesh = plsc.VectorSubcoreMesh(core_axis_name="core", subcore_axis_name="subcore",
                                  num_cores=SC_CORES, num_subcores=SC_SUBCORES)

    @pl.kernel(out_type=jax.ShapeDtypeStruct((out_rows, d), src.dtype), mesh=mesh, scratch_types=[])
    def scatter_rows(src_hbm, di_hbm, out_hbm):
        def body(x_vmem, di_vmem):
            for k in range(nk):
                pltpu.sync_copy(x_vmem, out_hbm.at[di_vmem.at[0, pl.ds(k * SC_TOKENS, SC_TOKENS)]])

        pltpu.emit_pipeline(
            body,
            grid=(n // SC_TOKENS,),
            in_specs=[pl.BlockSpec((SC_TOKENS, d), lambda i: (i, 0)),
                      pl.BlockSpec((1, SC_WINDOW), lambda i: (i, 0))],
            out_specs=[],
            core_axis_name=("core", "subcore"),
            dimension_semantics=(pltpu.PARALLEL,),
        )(src_hbm, di_hbm)

    return scatter_rows(src, idx)


def _sc_gather_rows(src, idx):
    m = idx.shape[0]
    d = src.shape[1]
    idx_rows = jnp.pad(idx.reshape(m // SC_CHUNK, SC_CHUNK), ((0, 0), (0, SC_WINDOW - SC_CHUNK)))
    mesh = plsc.VectorSubcoreMesh(core_axis_name="core", subcore_axis_name="subcore",
                                  num_cores=SC_CORES, num_subcores=SC_SUBCORES)

    @pl.kernel(out_type=jax.ShapeDtypeStruct((m, d), src.dtype), mesh=mesh, scratch_types=[])
    def gather_rows(src_hbm, si_hbm, out_hbm):
        def body(si_vmem, o_vmem):
            pltpu.sync_copy(src_hbm.at[si_vmem.at[0, pl.ds(0, SC_CHUNK)]], o_vmem)

        pltpu.emit_pipeline(
            body,
            grid=(m // SC_CHUNK,),
            in_specs=[pl.BlockSpec((1, SC_WINDOW), lambda i: (i, 0))],
            out_specs=[pl.BlockSpec((SC_CHUNK, d), lambda i: (i, 0))],
            core_axis_name=("core", "subcore"),
            dimension_semantics=(pltpu.PARALLEL,),
        )(si_hbm, out_hbm)

    return gather_rows(src, idx_rows)


def _tail_kernel(x1_ref, yk_ref, gate_ref, pp_ref, ps_ref, g3_ref, wg_ref, wp_ref, gf_ref,
                 yp_ref, ys_ref, *, n_p):
    gates = gate_ref[...]
    lo, hi = _unpack_bf16_pairs(yk_ref[0])
    lo, hi = lo * gates[:, 0:1], hi * gates[:, 0:1]
    for k in range(1, TOP_K):
        lo_k, hi_k = _unpack_bf16_pairs(yk_ref[k])
        lo, hi = lo + lo_k * gates[:, k:k + 1], hi + hi_k * gates[:, k:k + 1]
    moe = jnp.concatenate([lo, hi], axis=1)
    x2 = x1_ref[...] + moe
    h3 = _rms(x2, g3_ref[...])
    gate = _sigmoid(jnp.dot(h3.astype(BF16), wg_ref[...], preferred_element_type=F32))
    pe = jnp.dot(_group_pick(pp_ref, ps_ref, n_p).astype(BF16), wp_ref[...], preferred_element_type=F32)
    y = _rms(x2 + gate * pe, gf_ref[...])
    in_prompt = pl.program_id(0) < n_p // yp_ref.shape[0]

    @pl.when(in_prompt)
    def _():
        yp_ref[...] = y

    @pl.when(jnp.logical_not(in_prompt))
    def _():
        ys_ref[...] = y


def _tail(x1, yk, gates, pp, ps, g3, wg, wp, gf):
    n_p, n_s = pp.shape[0], ps.shape[0]
    n = n_p + n_s
    tm = _row_tile(n_p, n_s)
    tok = pl.BlockSpec((tm, D_MODEL), lambda i: (i, 0))
    full = lambda r, c: pl.BlockSpec((r, c), lambda i: (0, 0))
    return pl.pallas_call(
        functools.partial(_tail_kernel, n_p=n_p),
        grid=(n // tm,),
        in_specs=[tok, pl.BlockSpec((TOP_K, tm, D_MODEL // 2), lambda i: (0, i, 0)),
                  pl.BlockSpec((tm, LANES), lambda i: (i, 0)),
                  *_group_specs(tm, PLE_DIM, n_p),
                  full(1, D_MODEL), full(D_MODEL, D_MODEL), full(PLE_DIM, D_MODEL), full(1, D_MODEL)],
        out_specs=_group_specs(tm, D_MODEL, n_p),
        out_shape=[jax.ShapeDtypeStruct((n_p, D_MODEL), F32), jax.ShapeDtypeStruct((n_s, D_MODEL), F32)],
        compiler_params=_params("arbitrary"),
        name="ple_final_norm",
    )(x1, yk, gates, pp, ps, g3, wg, wp, gf)


def _lane_pad(v, offset, fill=0.0):
    out = jnp.full((1, LANES), fill, F32)
    return out.at[0, offset:offset + v.shape[0]].set(v.astype(F32))


def kernel(x_prompt, x_sample, state_delta, state_conv, p_prompt, p_sample, norm1_g, w_in, conv_w, a_log, dt_bias, dn_norm_g, w_proj_a, gm_ln_g, gm_ln_b, gm_ws, gm_bs, w_proj_b, w_out, norm2_g, router_w, router_b, moe_w1, moe_b1, moe_w2, moe_b2, norm3_g, ple_w, ple_gate_w, final_norm_g):
    bp, lp, d = x_prompt.shape
    bs, ls, _ = x_sample.shape
    depth = w_in.shape[0]
    assert depth == 1 and d == D_MODEL
    assert lp % GM_CHUNK == 0 and GM_CHUNK % ls == 0 and ls >= DN_CONV - 1
    n_p, n_s = bp * lp, bs * ls
    n = n_p + n_s
    i = 0

    xp, xs = x_prompt.reshape(n_p, d), x_sample.reshape(n_s, d)

    ab0 = QKV_W
    w = w_in[i]
    w_main = jnp.concatenate([w[:, :ab0], w[:, ab0 + 2 * DN_HEADS:]], axis=1).astype(BF16)
    w_ab = jnp.pad(w[:, ab0:ab0 + 2 * DN_HEADS], ((0, 0), (0, LANES - 2 * DN_HEADS))).astype(BF16)
    row2 = lambda v: v.reshape(1, -1).astype(F32)

    z, ab = _in_proj(xp, xs, row2(norm1_g[i]), w_main, w_ab)

    alog_p = _lane_pad(a_log[i], 0)
    dtb_p = _lane_pad(dt_bias[i], 0)
    cw = conv_w[i].astype(F32)
    ong = row2(dn_norm_g[i])
    zero_s = jnp.zeros((bp, DN_HEADS, DN_DK, DN_DV), F32)
    zero_buf = jnp.zeros((bp, DN_CONV - 1, QKV_W), F32)
    o_p, sd_p, sc_p = _delta_branch(z, ab, zero_buf, zero_s, cw, alog_p, dtb_p, ong,
                                    tok0=0, L=lp, G=4, precise=False)
    o_s, sd_s, sc_s = _delta_branch(z, ab, state_conv[i], state_delta[i], cw, alog_p, dtb_p, ong,
                                    tok0=n_p, L=ls, G=8, precise=False)

    t = GM_CHUNK
    tri = jnp.tril(jnp.ones((t, t), bool))
    ws = gm_ws[i]
    mix_p = jnp.where(tri, ws, 0.0)
    small = jnp.where(tri[:ls, :ls], ws[:, :ls, :ls], 0.0)
    mix_s = jnp.einsum('ab,gts->gatbs', jnp.eye(t // ls, dtype=F32), small).reshape(GM_GROUPS, t, t)
    mix = jnp.stack([mix_p, mix_s]).astype(BF16)
    gw = GM_WIDTH // GM_GROUPS
    bias_p = jnp.repeat(gm_bs[i].T, gw, axis=1)
    bias_s = jnp.tile(bias_p[:ls], (t // ls, 1))
    bias = jnp.stack([bias_p, bias_s]).astype(F32)
    rw = jnp.pad(router_w[i].astype(F32), ((0, 0), (0, LANES - N_EXPERTS)))
    rwh = rw.astype(BF16)
    rwl = (rw - rwh.astype(F32)).astype(BF16)
    rb = _lane_pad(router_b[i], 0, fill=-jnp.inf)
    x1, h2, gates, route, v_s, counts = _post_mix(
        o_p, o_s, z, xp, xs, row2(gm_ln_g[i]), row2(gm_ln_b[i]), mix, bias, w_proj_a[i].astype(BF16),
        w_proj_b[i].astype(BF16), w_out[i].astype(BF16), row2(norm2_g[i]), rwh, rwl, rb)
    dest_t, rows, block_e, first, valid, next_e, slot = _moe_dispatch(
        route[:TOP_K], route[TOP_K:], counts[0, :N_EXPERTS].astype(jnp.int32), n)
    xb = _sc_scatter_rows(h2, dest_t, rows)
    yb = _moe_experts(block_e, first, valid, next_e, slot, xb, moe_w1[i], moe_b1[i][:, None, :],
                      moe_w2[i], moe_b2[i][:, None, :])
    yk = _sc_gather_rows(yb, dest_t.reshape(TOP_K * n)).reshape(TOP_K, n, d // 2)

    y_p, y_s = _tail(x1, yk, gates, p_prompt[i].reshape(n_p, PLE_DIM), p_sample[i].reshape(n_s, PLE_DIM),
                     row2(norm3_g[i]), ple_gate_w[i].astype(BF16), ple_w[i].astype(BF16), row2(final_norm_g))

    return (y_p.reshape(bp, lp, d), y_s.reshape(bs, ls, d),
            sd_p[None], sc_p[None], sd_s[None], sc_s[None], v_s.reshape(1, bs, ls, GM_WIDTH))
```

```python
import functools
import math

import jax
import jax.numpy as jnp
from jax import lax
from jax.experimental import pallas as pl
from jax.experimental.pallas import tpu as pltpu
from jax.experimental.pallas import tpu_sc as plsc

F32 = jnp.float32
BF16 = jnp.bfloat16

D_MODEL = 1024
DN_HEADS = 8
DN_DK = 128
DN_DV = 128
DN_CONV = 4
DN_CHUNK = 64
GM_WIDTH = 1024
GM_GROUPS = 8
GM_CHUNK = 128
N_EXPERTS = 32
TOP_K = 4
D_EXPERT = 1024
SWIGLU_LIMIT = 7.0
SWIGLU_ALPHA = 1.702
PLE_DIM = 256
EPS = 1e-6
QK_W = DN_HEADS * DN_DK
V_W = DN_HEADS * DN_DV
QKV_W = 2 * QK_W + V_W

LANES = 128
SUBLANES = 8
MOE_ROWS = 512
VMEM_LIMIT = 56 << 20
SC_CORES, SC_SUBCORES = 2, 16
SC_WINDOW = 128
SC_TOKENS = SC_WINDOW // TOP_K
SC_CHUNK = 64


def _params(*sem):
    return pltpu.CompilerParams(dimension_semantics=sem, vmem_limit_bytes=VMEM_LIMIT)


def _dot(a, b):
    return jnp.dot(a.astype(BF16), b.astype(BF16), preferred_element_type=F32)


def _split(a):
    hi = a.astype(BF16)
    return hi, (a - hi.astype(F32)).astype(BF16)


def _dot3(a, b):
    ah, al = _split(a)
    bh, bl = _split(b)
    d = functools.partial(jnp.dot, preferred_element_type=F32)
    return d(ah, bh) + (d(ah, bl) + d(al, bh))


def _dot_nt(a, b):
    return lax.dot_general(a.astype(BF16), b.astype(BF16), (((1,), (1,)), ((), ())),
                           preferred_element_type=F32)


def _dot_tn(a, b):
    return lax.dot_general(a.astype(BF16), b.astype(BF16), (((0,), (0,)), ((), ())),
                           preferred_element_type=F32)


def _dot3_tn(a, b):
    ah, al = _split(a)
    bh, bl = _split(b)
    d = functools.partial(lax.dot_general, dimension_numbers=(((0,), (0,)), ((), ())),
                          preferred_element_type=F32)
    return d(ah, bh) + (d(ah, bl) + d(al, bh))


def _sigmoid(x):
    return 0.5 * jnp.tanh(0.5 * x) + 0.5


def _pack_bf16_pairs(x):
    bits = lax.bitcast_convert_type(x.astype(BF16).astype(F32), jnp.uint32)
    half = x.shape[1] // 2
    return lax.bitcast_convert_type((bits[:, :half] >> 16) | bits[:, half:], jnp.int32)


def _unpack_bf16_pairs(w):
    bits = lax.bitcast_convert_type(w, jnp.uint32)
    return (lax.bitcast_convert_type(bits << 16, F32),
            lax.bitcast_convert_type(bits & jnp.uint32(0xFFFF0000), F32))


def _rms(x, g):
    return x * lax.rsqrt(jnp.mean(x * x, axis=-1, keepdims=True) + EPS) * g


def _gelu(x):
    return 0.5 * x * (1.0 + lax.erf(x * (1.0 / math.sqrt(2.0))))


def _row_tile(n_p, n_s, cap=512):
    for t in (1024, 512, 256, 128):
        if t > cap:
            continue
        if n_p % t == 0 and n_s % t == 0:
            return t
    raise ValueError(f"token counts {n_p}, {n_s} must be multiples of 128")


def _group_specs(tm, width, n_p):
    pt = n_p // tm
    return [pl.BlockSpec((tm, width), lambda i, *_: (jnp.minimum(i, pt - 1), 0)),
            pl.BlockSpec((tm, width), lambda i, *_: (jnp.maximum(i - pt, 0), 0))]


def _group_pick(prompt_ref, sample_ref, n_p):
    pt = n_p // prompt_ref.shape[0]
    return jnp.where(pl.program_id(0) < pt, prompt_ref[...], sample_ref[...])


def _in_proj_kernel(xp_ref, xs_ref, g_ref, w_ref, wab_ref, z_ref, ab_ref, h_scr, *, n_p):
    @pl.when(pl.program_id(1) == 0)
    def _():
        hb = _rms(_group_pick(xp_ref, xs_ref, n_p), g_ref[...]).astype(BF16)
        h_scr[...] = hb
        ab_ref[...] = jnp.dot(hb, wab_ref[...], preferred_element_type=F32)

    z_ref[...] = jnp.dot(h_scr[...], w_ref[...], preferred_element_type=F32).astype(z_ref.dtype)


def _in_proj(xp, xs, g, w_main, w_ab):
    n_p, n_s = xp.shape[0], xs.shape[0]
    n = n_p + n_s
    tm, tn = _row_tile(n_p, n_s, cap=1024), 2048
    cols = w_main.shape[1]
    return pl.pallas_call(
        functools.partial(_in_proj_kernel, n_p=n_p),
        grid=(n // tm, cols // tn),
        in_specs=_group_specs(tm, D_MODEL, n_p) + [
            pl.BlockSpec((1, D_MODEL), lambda i, j: (0, 0)),
            pl.BlockSpec((D_MODEL, tn), lambda i, j: (0, j)),
            pl.BlockSpec((D_MODEL, LANES), lambda i, j: (0, 0)),
        ],
        out_specs=[
            pl.BlockSpec((tm, tn), lambda i, j: (i, j)),
            pl.BlockSpec((tm, LANES), lambda i, j: (i, 0)),
        ],
        out_shape=[jax.ShapeDtypeStruct((n, cols), BF16), jax.ShapeDtypeStruct((n, LANES), F32)],
        scratch_shapes=[pltpu.VMEM((tm, D_MODEL), BF16)],
        compiler_params=_params("parallel", "arbitrary"),
        name="in_proj",
    )(xp, xs, g, w_main, w_ab)


def _delta_kernel(*refs, C, G, carry, precise):
    (qkv_ref, zg_ref, ab_ref, buf_ref, s0_ref, cw_ref, alog_ref, dtb_ref, ong_ref) = refs[:9]
    refs = refs[9:]
    o_ref, snew_ref, bufnew_ref, xc_scr = refs[:4]
    H, DK, DV = DN_HEADS, DN_DK, DN_DV
    T = G * C
    dotm = _dot3 if precise else _dot
    dotm_tn = _dot3_tn if precise else _dot_tn
    halo = DN_CONV - 1
    base = SUBLANES
    cw = cw_ref[...]

    def conv(window):
        y = window(0) * cw[0:1]
        for i in range(1, DN_CONV):
            y = y + window(i) * cw[i:i + 1]
        return y

    if carry:
        s_scr = refs[4]
        c = pl.program_id(1)
        last = pl.num_programs(1) - 1

        @pl.when(c == 0)
        def _():
            xc_scr[base - halo:base, :] = buf_ref[0]
            s_scr[...] = s0_ref[0]

        xb = qkv_ref[...]
        xf = xb.astype(F32)
        xc_scr[base:base + SUBLANES, :] = xf[0:SUBLANES]
        y_head = conv(lambda i: xc_scr[base - halo + i:base - halo + i + SUBLANES, :])
        ri = lax.broadcasted_iota(jnp.int32, (halo * T, T), 0)
        ci = lax.broadcasted_iota(jnp.int32, (halo * T, T), 1)
        src_row = (ri & (T - 1)) - (halo - (ri >> (T.bit_length() - 1)))
        shifted = jnp.dot(jnp.where(ci == src_row, 1.0, 0.0).astype(BF16), xb, preferred_element_type=F32)
        y = conv(lambda i: shifted[i * T:(i + 1) * T] if i < halo else xf)
        y = jnp.concatenate([y_head, y[SUBLANES:]], axis=0)
        tail = xf[T - halo:T]
        xc_scr[base - halo:base, :] = tail

        @pl.when(c == last)
        def _():
            bufnew_ref[0] = tail
    else:
        ys = []
        x_new = qkv_ref[...].astype(F32)
        for g in range(G):
            xc_scr[g, base - halo:base, :] = buf_ref[g]
            xc_scr[g, base:base + C, :] = x_new[g * C:(g + 1) * C, :]
            ys.append(conv(lambda i: xc_scr[g, base - halo + i:base - halo + i + C, :]))
            bufnew_ref[g] = xc_scr[g, base + C - halo:base + C, :]
        y = jnp.concatenate(ys, axis=0) if G > 1 else ys[0]
    qkv = y * _sigmoid(y)

    ab = ab_ref[...]
    g_all = -jnp.exp(alog_ref[...]) * jax.nn.softplus(ab + dtb_ref[...])
    beta_all = _sigmoid(ab)
    shift = C.bit_length() - 1
    rt = lax.broadcasted_iota(jnp.int32, (T, T), 0)
    ct = lax.broadcasted_iota(jnp.int32, (T, T), 1)
    chunk_tril = ((rt >> shift) == (ct >> shift)) & (rt >= ct)
    gcum = _dot3(chunk_tril.astype(F32), g_all)
    gam_all = jnp.exp(gcum)
    wide = C >= LANES // 2
    if wide:
        gcum_t, beta_t = gcum.T, beta_all.T

    row = lax.broadcasted_iota(jnp.int32, (C, C), 0)
    col = lax.broadcasted_iota(jnp.int32, (C, C), 1)
    incl, strict, eye = row >= col, row > col, row == col
    eyef = eye.astype(F32)
    units = [(g, h) for g in range(G) for h in range(H)]
    rows = lambda a, g: a[g * C:(g + 1) * C]

    qn, kn, vv = [], [], []
    for h in range(H):
        q = qkv[:, h * DK:(h + 1) * DK]
        k = qkv[:, QK_W + h * DK:QK_W + (h + 1) * DK]
        qn.append(q * (lax.rsqrt(jnp.sum(q * q, axis=-1, keepdims=True) + EPS) * (DK ** -0.5)))
        kn.append(k * lax.rsqrt(jnp.sum(k * k, axis=-1, keepdims=True) + EPS))
        vv.append(qkv[:, 2 * QK_W + h * DV:2 * QK_W + (h + 1) * DV])
    qb = [q.astype(BF16) for q in qn]
    kb = [k.astype(BF16) for k in kn]

    gc, bc, gl, a_low, m_intra = {}, {}, {}, {}, {}
    for u in units:
        g, h = u
        gc[u] = rows(gcum, g)[:, h:h + 1]
        bc[u] = rows(beta_all, g)[:, H + h:H + h + 1]
        gl[u] = gc[u][C - 1:C, :]
        if wide:
            gr = gcum_t[h:h + 1, g * C:(g + 1) * C]
            br = beta_t[H + h:H + h + 1, g * C:(g + 1) * C]
        else:
            gr = jnp.sum(jnp.where(eye, gc[u], 0.0), axis=0, keepdims=True)
            br = jnp.sum(jnp.where(eye, bc[u], 0.0), axis=0, keepdims=True)
        db = jnp.where(incl, jnp.exp(gc[u] - gr), 0.0) * br
        k = rows(kb[h], g)
        kq = _dot_nt(jnp.concatenate([k, rows(qb[h], g)], axis=0), k)
        a_low[u] = jnp.where(strict, kq[:C] * db, 0.0)
        m_intra[u] = kq[C:] * db

    t_inv = {u: eyef - a_low[u] for u in units}
    a_pow = {u: dotm(a_low[u], a_low[u]) for u in units}
    n = 2
    while n < C:
        t_inv = {u: t_inv[u] + dotm(t_inv[u], a_pow[u]) for u in units}
        n *= 2
        if n < C:
            a_pow = {u: dotm(a_pow[u], a_pow[u]) for u in units}

    u_base, wq, k_dec = {}, {}, {}
    for u in units:
        g, h = u
        gam = rows(gam_all, g)[:, h:h + 1]
        k = rows(kn[h], g)
        sol = dotm(t_inv[u], jnp.concatenate([rows(vv[h], g), gam * k], axis=1))
        u_base[u] = sol[:, :DV]
        wq[u] = jnp.concatenate([sol[:, DV:], gam * rows(qn[h], g)], axis=0)
        k_dec[u] = k * (bc[u] * jnp.exp(gl[u] - gc[u]))

    state = [s_scr[h] for h in range(H)] if carry else None
    outs = {}
    for g in range(G):
        s_in = state if carry else [s0_ref[g, h] for h in range(H)]
        wqs = [dotm(wq[(g, h)], s_in[h]) for h in range(H)]
        us = [u_base[(g, h)] - wqs[h][:C] for h in range(H)]
        for h in range(H):
            outs[(g, h)] = wqs[h][C:] + dotm(m_intra[(g, h)], us[h])
        s_out = [jnp.exp(gl[(g, h)]) * s_in[h] + dotm_tn(k_dec[(g, h)], us[h]) for h in range(H)]
        if carry:
            state = s_out
        else:
            for h in range(H):
                snew_ref[g, h] = s_out[h]

    for h in range(H):
        o = jnp.concatenate([outs[(g, h)] for g in range(G)], axis=0) if G > 1 else outs[(0, h)]
        zg = zg_ref[:, h * DV:(h + 1) * DV].astype(F32)
        o_ref[:, h * DV:(h + 1) * DV] = (_rms(o, ong_ref[...]) * (zg * _sigmoid(zg))).astype(o_ref.dtype)

    if carry:
        for h in range(H):
            s_scr[h] = state[h]

        @pl.when(c == last)
        def _():
            for h in range(H):
                snew_ref[0, h] = state[h]


def _delta_branch(z, ab, conv_buf, s0, conv_w, alog_p, dtb_p, onorm_g, *, tok0, L, G, precise):
    B = conv_buf.shape[0]
    C = math.gcd(L, DN_CHUNK)
    nc = L // C
    carry = nc > 1
    T = G * C
    blk0 = tok0 // T
    if carry:
        assert nc % G == 0
        grid = (B, nc // G)
        own_blk = lambda b, c: b * (nc // G) + c
        gs = 1
    else:
        assert B % G == 0
        grid = (B // G, 1)
        own_blk = lambda b, c: b
        gs = G
    tok = lambda b, c: (blk0 + own_blk(b, c), 0)
    seq3 = lambda b, c: (b, 0, 0)
    seq4 = lambda b, c: (b, 0, 0, 0)
    const = lambda b, c: (0, 0)
    if carry:
        scratch = [pltpu.VMEM((2 * SUBLANES, QKV_W), F32), pltpu.VMEM((DN_HEADS, DN_DK, DN_DV), F32)]
    else:
        scratch = [pltpu.VMEM((G, SUBLANES + C, QKV_W), F32)]
    return pl.pallas_call(
        functools.partial(_delta_kernel, C=C, G=G, carry=carry, precise=precise),
        grid=grid,
        in_specs=[
            pl.BlockSpec((T, QKV_W), tok),
            pl.BlockSpec((T, V_W), lambda b, c: (blk0 + own_blk(b, c), QKV_W // V_W)),
            pl.BlockSpec((T, LANES), tok),
            pl.BlockSpec((gs, DN_CONV - 1, QKV_W), seq3),
            pl.BlockSpec((gs, DN_HEADS, DN_DK, DN_DV), seq4),
            pl.BlockSpec((DN_CONV, QKV_W), const),
            pl.BlockSpec((1, LANES), const),
            pl.BlockSpec((1, LANES), const),
            pl.BlockSpec((1, DN_DV), const),
        ],
        out_specs=[
            pl.BlockSpec((T, V_W), lambda b, c: (own_blk(b, c), 0)),
            pl.BlockSpec((gs, DN_HEADS, DN_DK, DN_DV), seq4),
            pl.BlockSpec((gs, DN_CONV - 1, QKV_W), seq3),
        ],
        out_shape=[
            jax.ShapeDtypeStruct((B * L, V_W), BF16),
            jax.ShapeDtypeStruct(s0.shape, F32),
            jax.ShapeDtypeStruct(conv_buf.shape, F32),
        ],
        scratch_shapes=scratch,
        compiler_params=_params("parallel", "arbitrary"),
        name=f"delta_rule_c{C}",
    )(z, z, ab, conv_buf, s0, conv_w, alog_p, dtb_p, onorm_g)


def _post_mix_kernel(op_ref, os_ref, gu_ref, gv_ref, ma_ref, mb_ref, xp_ref, xs_ref, lng_ref, lnb_ref,
                     mix_ref, bias_ref, wa_ref, wb_ref, wo_ref, g2_ref, rwh_ref, rwl_ref, rb_ref, tri_ref,
                     x1_ref, h2_ref, gate_ref, idx_ref, v_ref, cnt_ref, us_scr, cnt_scr, *, n_p):
    u = _gelu(gu_ref[...].astype(F32))
    a = _gelu(gv_ref[...].astype(F32))
    ac = a - jnp.mean(a, axis=-1, keepdims=True)
    v = ac * lax.rsqrt(jnp.mean(ac * ac, axis=-1, keepdims=True) + EPS) * lng_ref[...] + lnb_ref[...]
    v_ref[...] = v
    vb = v.astype(BF16)
    gw = GM_WIDTH // GM_GROUPS
    for c in range(u.shape[0] // GM_CHUNK):
        rs = slice(c * GM_CHUNK, (c + 1) * GM_CHUNK)
        for g in range(GM_GROUPS):
            sl = slice(g * gw, (g + 1) * gw)
            s = jnp.dot(mix_ref[0, g], vb[rs, sl], preferred_element_type=F32) + bias_ref[0, :, sl]
            us_scr[rs, sl] = (u[rs, sl] * s).astype(BF16)

    ya = jnp.dot(_group_pick(op_ref, os_ref, n_p).astype(BF16), wa_ref[...], preferred_element_type=F32)
    yb = jnp.dot(us_scr[...], wb_ref[...], preferred_element_type=F32)
    mixed = _sigmoid(ma_ref[...].astype(F32)) * ya + _sigmoid(mb_ref[...].astype(F32)) * yb
    x1 = _group_pick(xp_ref, xs_ref, n_p) + jnp.dot(mixed.astype(BF16), wo_ref[...], preferred_element_type=F32)
    x1_ref[...] = x1
    h2 = _rms(x1, g2_ref[...])
    h2_ref[...] = _pack_bf16_pairs(h2)
    hh, hl = _split(h2)
    d = functools.partial(jnp.dot, preferred_element_type=F32)
    logits = d(hh, rwh_ref[...]) + (d(hh, rwl_ref[...]) + d(hl, rwh_ref[...])) + rb_ref[...]
    lane = lax.broadcasted_iota(jnp.int32, logits.shape, 1).astype(F32)
    vals, idxs = [], []
    for _ in range(TOP_K):
        m = jnp.max(logits, axis=-1, keepdims=True)
        i = jnp.min(jnp.where(logits == m, lane, float(LANES)), axis=-1, keepdims=True)
        vals.append(m)
        idxs.append(i)
        logits = jnp.where(lane == i, -jnp.inf, logits)
    es = [jnp.exp(v - vals[0]) for v in vals]
    tot = es[0]
    for e in es[1:]:
        tot = tot + e

    @pl.when(pl.program_id(0) == 0)
    def _():
        cnt_scr[...] = jnp.zeros(cnt_scr.shape, F32)

    onehot = (lane == idxs[0]).astype(F32)
    for k in range(1, TOP_K):
        onehot = onehot + (lane == idxs[k]).astype(F32)
    before = jnp.dot(tri_ref[...], onehot.astype(BF16), preferred_element_type=F32) + cnt_scr[...]
    cnt_scr[...] = cnt_scr[...] + jnp.sum(onehot, axis=0, keepdims=True)
    cnt_ref[...] = cnt_scr[...]

    gates = jnp.zeros(logits.shape, F32)
    route = jnp.zeros(logits.shape, F32)
    for k in range(TOP_K):
        rank = jnp.sum(jnp.where(lane == idxs[k], before, 0.0), axis=-1, keepdims=True)
        gates = jnp.where(lane == float(k), es[k] / tot, gates)
        route = jnp.where(lane == float(k), idxs[k], route)
        route = jnp.where(lane == float(TOP_K + k), rank, route)
    gate_ref[...] = gates
    idx_ref[...] = route.T[:2 * TOP_K].astype(jnp.int32)


def _post_mix(o_p, o_s, z, xp, xs, ln_g, ln_b, mix, bias, wa, wb, wo, g2, rwh, rwl, rb):
    n_p, n_s = xp.shape[0], xs.shape[0]
    n = n_p + n_s
    tm = _row_tile(n_p, n_s)
    pt = n_p // tm
    tri = jnp.tril(jnp.ones((tm, tm), BF16), k=-1)
    grp = lambda i: jnp.where(i < pt, 0, 1)
    once = pl.Buffered(1)
    tok = pl.BlockSpec((tm, D_MODEL), lambda i: (i, 0))
    zcol = lambda c: pl.BlockSpec((tm, D_MODEL), lambda i: (i, c))
    full = lambda r, c: pl.BlockSpec((r, c), lambda i: (0, 0), pipeline_mode=once)
    narrow = pl.BlockSpec((tm, LANES), lambda i: (i, 0))
    op_spec, os_spec = _group_specs(tm, V_W, n_p)
    xp_spec, xs_spec = _group_specs(tm, D_MODEL, n_p)
    return pl.pallas_call(
        functools.partial(_post_mix_kernel, n_p=n_p),
        grid=(n // tm,),
        in_specs=[
            op_spec, os_spec, zcol(4), zcol(5), zcol(6), zcol(7), xp_spec, xs_spec,
            full(1, GM_WIDTH), full(1, GM_WIDTH),
            pl.BlockSpec((1, GM_GROUPS, GM_CHUNK, GM_CHUNK), lambda i: (grp(i), 0, 0, 0)),
            pl.BlockSpec((1, GM_CHUNK, GM_WIDTH), lambda i: (grp(i), 0, 0)),
            full(V_W, D_MODEL), full(GM_WIDTH, D_MODEL), full(D_MODEL, D_MODEL), full(1, D_MODEL),
            full(D_MODEL, LANES), full(D_MODEL, LANES), full(1, LANES), full(tm, tm),
        ],
        out_specs=[tok, pl.BlockSpec((tm, D_MODEL // 2), lambda i: (i, 0)), narrow,
                   pl.BlockSpec((2 * TOP_K, tm), lambda i: (0, i)),
                   pl.BlockSpec((tm, GM_WIDTH), lambda i: (jnp.maximum(i - pt, 0), 0)),
                   pl.BlockSpec((1, LANES), lambda i: (0, 0))],
        out_shape=[
            jax.ShapeDtypeStruct((n, D_MODEL), F32),
            jax.ShapeDtypeStruct((n, D_MODEL // 2), jnp.int32),
            jax.ShapeDtypeStruct((n, LANES), F32),
            jax.ShapeDtypeStruct((2 * TOP_K, n), jnp.int32),
            jax.ShapeDtypeStruct((n_s, GM_WIDTH), F32),
            jax.ShapeDtypeStruct((1, LANES), F32),
        ],
        scratch_shapes=[pltpu.VMEM((tm, GM_WIDTH), BF16), pltpu.VMEM((1, LANES), F32)],
        compiler_params=_params("arbitrary"),
        name="chunk_mlp_post_mix_router",
    )(o_p, o_s, z, z, z, z, xp, xs, ln_g, ln_b, mix, bias, wa, wb, wo, g2, rwh, rwl, rb, tri)


def _moe_kernel(be_ref, first_ref, valid_ref, next_ref, slot_ref, x_ref, w1_hbm, b1_ref, w2_hbm, b2_ref, y_ref,
                w1_stage, w2_stage, sems):
    b = pl.program_id(0)
    slot = slot_ref[b]

    def weight_copies(e, s):
        return (pltpu.make_async_copy(w1_hbm.at[e], w1_stage.at[s], sems.at[0, s]),
                pltpu.make_async_copy(w2_hbm.at[e], w2_stage.at[s], sems.at[1, s]))

    @pl.when(b == 0)
    def _():
        for c in weight_copies(be_ref[0], slot):
            c.start()

    @pl.when(first_ref[b] == 1)
    def _():
        for c in weight_copies(be_ref[b], slot):
            c.wait()

        @pl.when(next_ref[b] >= 0)
        def _():
            for c in weight_copies(next_ref[b], 1 - slot):
                c.start()

    @pl.when(valid_ref[b] == 1)
    def _():
        x = jnp.concatenate(_unpack_bf16_pairs(x_ref[...]), axis=1).astype(BF16)
        hid = jnp.dot(x, w1_stage[slot].astype(BF16), preferred_element_type=F32) + b1_ref[0]
        gate = jnp.minimum(hid[:, :D_EXPERT], SWIGLU_LIMIT)
        up = jnp.clip(hid[:, D_EXPERT:], -SWIGLU_LIMIT, SWIGLU_LIMIT)
        act = gate * _sigmoid(SWIGLU_ALPHA * gate) * (up + 1.0)
        y = jnp.dot(act.astype(BF16), w2_stage[slot].astype(BF16), preferred_element_type=F32) + b2_ref[0]
        y_ref[...] = _pack_bf16_pairs(y)

    @pl.when(valid_ref[b] == 0)
    def _():
        y_ref[...] = jnp.zeros(y_ref.shape, jnp.int32)


def _moe_experts(block_e, first, valid, next_e, slot, xb, w1, b1, w2, b2):
    rows = xb.shape[0]
    nb = rows // MOE_ROWS
    smem4 = lambda f: (lambda b, be, fi, va, ne, sl: f(b, be))
    return pl.pallas_call(
        _moe_kernel,
        grid_spec=pltpu.PrefetchScalarGridSpec(
            num_scalar_prefetch=5,
            grid=(nb,),
            in_specs=[
                pl.BlockSpec((MOE_ROWS, D_MODEL // 2), smem4(lambda b, be: (b, 0))),
                pl.BlockSpec(memory_space=pl.ANY),
                pl.BlockSpec((1, 1, 2 * D_EXPERT), smem4(lambda b, be: (be[b], 0, 0))),
                pl.BlockSpec(memory_space=pl.ANY),
                pl.BlockSpec((1, 1, D_MODEL), smem4(lambda b, be: (be[b], 0, 0))),
            ],
            out_specs=pl.BlockSpec((MOE_ROWS, D_MODEL // 2), smem4(lambda b, be: (b, 0))),
            scratch_shapes=[
                pltpu.VMEM((2, D_MODEL, 2 * D_EXPERT), F32),
                pltpu.VMEM((2, D_EXPERT, D_MODEL), F32),
                pltpu.SemaphoreType.DMA((2, 2)),
            ],
        ),
        out_shape=jax.ShapeDtypeStruct((rows, D_MODEL // 2), jnp.int32),
        compiler_params=_params("arbitrary"),
        name="moe_experts",
    )(block_e, first, valid, next_e, slot, xb, w1, b1, w2, b2)


def _moe_dispatch(idx, rank, counts, n):
    experts = jnp.arange(N_EXPERTS, dtype=jnp.int32)
    padded = (counts + MOE_ROWS - 1) // MOE_ROWS * MOE_ROWS
    pad_end = jnp.cumsum(padded)
    pad_start = pad_end - padded
    start_of = jnp.sum(jnp.where(idx[None] == experts[:, None, None], pad_start[:, None, None], 0), axis=0)
    dest_t = start_of + rank
    nb = -(-n * TOP_K // MOE_ROWS) + N_EXPERTS
    rows = nb * MOE_ROWS
    starts = jnp.arange(nb, dtype=jnp.int32) * MOE_ROWS
    valid = (starts < pad_end[-1]).astype(jnp.int32)
    owner = lambda r: jnp.minimum(jnp.sum((pad_end[None, :] <= r[:, None]).astype(jnp.int32), axis=1),
                                  N_EXPERTS - 1)
    last_e = owner(pad_end[-1:] - 1)[0]
    block_e = jnp.where(valid == 1, owner(starts), last_e).astype(jnp.int32)
    first = jnp.concatenate([jnp.ones((1,), jnp.int32),
                             (block_e[1:] != block_e[:-1]).astype(jnp.int32)])
    blk = jnp.arange(nb, dtype=jnp.int32)
    later_first = (blk[None, :] > blk[:, None]) & (first[None, :] == 1)
    next_pos = jnp.min(jnp.where(later_first, blk[None, :], nb), axis=1)
    next_e = jnp.where(next_pos < nb, block_e[jnp.minimum(next_pos, nb - 1)], -1).astype(jnp.int32)
    slot = (jnp.cumsum(first) - 1) % 2
    return dest_t, rows, block_e, first, valid, next_e, slot.astype(jnp.int32)


def _sc_scatter_rows(src, dest_t, out_rows):
    n, d = src.shape
    nk = dest_t.shape[0]
    assert nk * SC_TOKENS == SC_WINDOW
    idx = dest_t.reshape(nk, n // SC_TOKENS, SC_TOKENS).transpose(1, 0, 2).reshape(n // SC_TOKENS, SC_WINDOW)
    mesh = plsc.VectorSubcoreMesh(core_axis_name="core", subcore_axis_name="subcore",
                                  num_cores=SC_CORES, num_subcores=SC_SUBCORES)

    @pl.kernel(out_type=jax.ShapeDtypeStruct((out_rows, d), src.dtype), mesh=mesh, scratch_types=[])
    def scatter_rows(src_hbm, di_hbm, out_hbm):
        def body(x_vmem, di_vmem):
            for k in range(nk):
                pltpu.sync_copy(x_vmem, out_hbm.at[di_vmem.at[0, pl.ds(k * SC_TOKENS, SC_TOKENS)]])

        pltpu.emit_pipeline(
            body,
            grid=(n // SC_TOKENS,),
            in_specs=[pl.BlockSpec((SC_TOKENS, d), lambda i: (i, 0)),
                      pl.BlockSpec((1, SC_WINDOW), lambda i: (i, 0))],
            out_specs=[],
            core_axis_name=("core", "subcore"),
            dimension_semantics=(pltpu.PARALLEL,),
        )(src_hbm, di_hbm)

    return scatter_rows(src, idx)


def _sc_gather_rows(src, idx):
    m = idx.shape[0]
    d = src.shape[1]
    idx_rows = jnp.pad(idx.reshape(m // SC_CHUNK, SC_CHUNK), ((0, 0), (0, SC_WINDOW - SC_CHUNK)))
    mesh = plsc.VectorSubcoreMesh(core_axis_name="core", subcore_axis_name="subcore",
                                  num_cores=SC_CORES, num_subcores=SC_SUBCORES)

    @pl.kernel(out_type=jax.ShapeDtypeStruct((m, d), src.dtype), mesh=mesh, scratch_types=[])
    def gather_rows(src_hbm, si_hbm, out_hbm):
        def body(si_vmem, o_vmem):
            pltpu.sync_copy(src_hbm.at[si_vmem.at[0, pl.ds(0, SC_CHUNK)]], o_vmem)

        pltpu.emit_pipeline(
            body,
            grid=(m // SC_CHUNK,),
            in_specs=[pl.BlockSpec((1, SC_WINDOW), lambda i: (i, 0))],
            out_specs=[pl.BlockSpec((SC_CHUNK, d), lambda i: (i, 0))],
            core_axis_name=("core", "subcore"),
            dimension_semantics=(pltpu.PARALLEL,),
        )(si_hbm, out_hbm)

    return gather_rows(src, idx_rows)


def _tail_kernel(x1_ref, yk_ref, gate_ref, pp_ref, ps_ref, g3_ref, wg_ref, wp_ref, gf_ref,
                 yp_ref, ys_ref, *, n_p):
    gates = gate_ref[...]
    lo, hi = _unpack_bf16_pairs(yk_ref[0])
    lo, hi = lo * gates[:, 0:1], hi * gates[:, 0:1]
    for k in range(1, TOP_K):
        lo_k, hi_k = _unpack_bf16_pairs(yk_ref[k])
        lo, hi = lo + lo_k * gates[:, k:k + 1], hi + hi_k * gates[:, k:k + 1]
    moe = jnp.concatenate([lo, hi], axis=1)
    x2 = x1_ref[...] + moe
    h3 = _rms(x2, g3_ref[...])
    gate = _sigmoid(jnp.dot(h3.astype(BF16), wg_ref[...], preferred_element_type=F32))
    pe = jnp.dot(_group_pick(pp_ref, ps_ref, n_p).astype(BF16), wp_ref[...], preferred_element_type=F32)
    y = _rms(x2 + gate * pe, gf_ref[...])
    in_prompt = pl.program_id(0) < n_p // yp_ref.shape[0]

    @pl.when(in_prompt)
    def _():
        yp_ref[...] = y

    @pl.when(jnp.logical_not(in_prompt))
    def _():
        ys_ref[...] = y


def _tail(x1, yk, gates, pp, ps, g3, wg, wp, gf):
    n_p, n_s = pp.shape[0], ps.shape[0]
    n = n_p + n_s
    tm = _row_tile(n_p, n_s)
    tok = pl.BlockSpec((tm, D_MODEL), lambda i: (i, 0))
    full = lambda r, c: pl.BlockSpec((r, c), lambda i: (0, 0))
    return pl.pallas_call(
        functools.partial(_tail_kernel, n_p=n_p),
        grid=(n // tm,),
        in_specs=[tok, pl.BlockSpec((TOP_K, tm, D_MODEL // 2), lambda i: (0, i, 0)),
                  pl.BlockSpec((tm, LANES), lambda i: (i, 0)),
                  *_group_specs(tm, PLE_DIM, n_p),
                  full(1, D_MODEL), full(D_MODEL, D_MODEL), full(PLE_DIM, D_MODEL), full(1, D_MODEL)],
        out_specs=_group_specs(tm, D_MODEL, n_p),
        out_shape=[jax.ShapeDtypeStruct((n_p, D_MODEL), F32), jax.ShapeDtypeStruct((n_s, D_MODEL), F32)],
        compiler_params=_params("arbitrary"),
        name="ple_final_norm",
    )(x1, yk, gates, pp, ps, g3, wg, wp, gf)


def _lane_pad(v, offset, fill=0.0):
    out = jnp.full((1, LANES), fill, F32)
    return out.at[0, offset:offset + v.shape[0]].set(v.astype(F32))


def kernel(x_prompt, x_sample, state_delta, state_conv, p_prompt, p_sample, norm1_g, w_in, conv_w, a_log, dt_bias, dn_norm_g, w_proj_a, gm_ln_g, gm_ln_b, gm_ws, gm_bs, w_proj_b, w_out, norm2_g, router_w, router_b, moe_w1, moe_b1, moe_w2, moe_b2, norm3_g, ple_w, ple_gate_w, final_norm_g):
    bp, lp, d = x_prompt.shape
    bs, ls, _ = x_sample.shape
    depth = w_in.shape[0]
    assert depth == 1 and d == D_MODEL
    assert lp % GM_CHUNK == 0 and GM_CHUNK % ls == 0 and ls >= DN_CONV - 1
    n_p, n_s = bp * lp, bs * ls
    n = n_p + n_s
    i = 0

    xp, xs = x_prompt.reshape(n_p, d), x_sample.reshape(n_s, d)

    ab0 = QKV_W
    w = w_in[i]
    w_main = jnp.concatenate([w[:, :ab0], w[:, ab0 + 2 * DN_HEADS:]], axis=1).astype(BF16)
    w_ab = jnp.pad(w[:, ab0:ab0 + 2 * DN_HEADS], ((0, 0), (0, LANES - 2 * DN_HEADS))).astype(BF16)
    row2 = lambda v: v.reshape(1, -1).astype(F32)

    z, ab = _in_proj(xp, xs, row2(norm1_g[i]), w_main, w_ab)

    alog_p = _lane_pad(a_log[i], 0)
    dtb_p = _lane_pad(dt_bias[i], 0)
    cw = conv_w[i].astype(F32)
    ong = row2(dn_norm_g[i])
    zero_s = jnp.zeros((bp, DN_HEADS, DN_DK, DN_DV), F32)
    zero_buf = jnp.zeros((bp, DN_CONV - 1, QKV_W), F32)
    o_p, sd_p, sc_p = _delta_branch(z, ab, zero_buf, zero_s, cw, alog_p, dtb_p, ong,
                                    tok0=0, L=lp, G=4, precise=False)
    o_s, sd_s, sc_s = _delta_branch(z, ab, state_conv[i], state_delta[i], cw, alog_p, dtb_p, ong,
                                    tok0=n_p, L=ls, G=8, precise=False)

    t = GM_CHUNK
    tri = jnp.tril(jnp.ones((t, t), bool))
    ws = gm_ws[i]
    mix_p = jnp.where(tri, ws, 0.0)
    small = jnp.where(tri[:ls, :ls], ws[:, :ls, :ls], 0.0)
    mix_s = jnp.einsum('ab,gts->gatbs', jnp.eye(t // ls, dtype=F32), small).reshape(GM_GROUPS, t, t)
    mix = jnp.stack([mix_p, mix_s]).astype(BF16)
    gw = GM_WIDTH // GM_GROUPS
    bias_p = jnp.repeat(gm_bs[i].T, gw, axis=1)
    bias_s = jnp.tile(bias_p[:ls], (t // ls, 1))
    bias = jnp.stack([bias_p, bias_s]).astype(F32)
    rw = jnp.pad(router_w[i].astype(F32), ((0, 0), (0, LANES - N_EXPERTS)))
    rwh = rw.astype(BF16)
    rwl = (rw - rwh.astype(F32)).astype(BF16)
    rb = _lane_pad(router_b[i], 0, fill=-jnp.inf)
    x1, h2, gates, route, v_s, counts = _post_mix(
        o_p, o_s, z, xp, xs, row2(gm_ln_g[i]), row2(gm_ln_b[i]), mix, bias, w_proj_a[i].astype(BF16),
        w_proj_b[i].astype(BF16), w_out[i].astype(BF16), row2(norm2_g[i]), rwh, rwl, rb)
    dest_t, rows, block_e, first, valid, next_e, slot = _moe_dispatch(
        route[:TOP_K], route[TOP_K:], counts[0, :N_EXPERTS].astype(jnp.int32), n)
    xb = _sc_scatter_rows(h2, dest_t, rows)
    yb = _moe_experts(block_e, first, valid, next_e, slot, xb, moe_w1[i], moe_b1[i][:, None, :],
                      moe_w2[i], moe_b2[i][:, None, :])
    yk = _sc_gather_rows(yb, dest_t.reshape(TOP_K * n)).reshape(TOP_K, n, d // 2)

    y_p, y_s = _tail(x1, yk, gates, p_prompt[i].reshape(n_p, PLE_DIM), p_sample[i].reshape(n_s, PLE_DIM),
                     row2(norm3_g[i]), ple_gate_w[i].astype(BF16), ple_w[i].astype(BF16), row2(final_norm_g))

    return (y_p.reshape(bp, lp, d), y_s.reshape(bs, ls, d),
            sd_p[None], sc_p[None], sd_s[None], sc_s[None], v_s.reshape(1, bs, ls, GM_WIDTH))
```

```python
import functools
import math

import jax
import jax.numpy as jnp
from jax import lax
from jax.experimental import pallas as pl
from jax.experimental.pallas import tpu as pltpu
from jax.experimental.pallas import tpu_sc as plsc

F32 = jnp.float32
BF16 = jnp.bfloat16

D_MODEL = 1024
DN_HEADS = 8
DN_DK = 128
DN_DV = 128
DN_CONV = 4
DN_CHUNK = 64
GM_WIDTH = 1024
GM_GROUPS = 8
GM_CHUNK = 128
N_EXPERTS = 32
TOP_K = 4
D_EXPERT = 1024
SWIGLU_LIMIT = 7.0
SWIGLU_ALPHA = 1.702
PLE_DIM = 256
EPS = 1e-6
QK_W = DN_HEADS * DN_DK
V_W = DN_HEADS * DN_DV
QKV_W = 2 * QK_W + V_W

LANES = 128
SUBLANES = 8
TAIL_PARTS = 4
MOE_ROWS = 512
VMEM_LIMIT = 56 << 20
SC_CORES, SC_SUBCORES = 2, 16
SC_WINDOW = 128
SC_TOKENS = SC_WINDOW // TOP_K
SC_CHUNK = 64


def _params(*sem):
    return pltpu.CompilerParams(dimension_semantics=sem, vmem_limit_bytes=VMEM_LIMIT)


def _dot(a, b):
    return jnp.dot(a.astype(BF16), b.astype(BF16), preferred_element_type=F32)


def _split(a):
    hi = a.astype(BF16)
    return hi, (a - hi.astype(F32)).astype(BF16)


def _dot3(a, b):
    ah, al = _split(a)
    bh, bl = _split(b)
    d = functools.partial(jnp.dot, preferred_element_type=F32)
    return d(ah, bh) + (d(ah, bl) + d(al, bh))


def _dot_nt(a, b):
    return lax.dot_general(a.astype(BF16), b.astype(BF16), (((1,), (1,)), ((), ())),
                           preferred_element_type=F32)


def _dot_tn(a, b):
    return lax.dot_general(a.astype(BF16), b.astype(BF16), (((0,), (0,)), ((), ())),
                           preferred_element_type=F32)


def _dot3_tn(a, b):
    ah, al = _split(a)
    bh, bl = _split(b)
    d = functools.partial(lax.dot_general, dimension_numbers=(((0,), (0,)), ((), ())),
                          preferred_element_type=F32)
    return d(ah, bh) + (d(ah, bl) + d(al, bh))


def _sigmoid(x):
    return 0.5 * jnp.tanh(0.5 * x) + 0.5


def _pack_bf16_pairs(x):
    bits = lax.bitcast_convert_type(x.astype(BF16).astype(F32), jnp.uint32)
    half = x.shape[1] // 2
    return lax.bitcast_convert_type((bits[:, :half] >> 16) | bits[:, half:], jnp.int32)


def _unpack_bf16_pairs(w):
    bits = lax.bitcast_convert_type(w, jnp.uint32)
    return (lax.bitcast_convert_type(bits << 16, F32),
            lax.bitcast_convert_type(bits & jnp.uint32(0xFFFF0000), F32))


def _rms(x, g):
    return x * lax.rsqrt(jnp.mean(x * x, axis=-1, keepdims=True) + EPS) * g


def _gelu(x):
    return 0.5 * x * (1.0 + lax.erf(x * (1.0 / math.sqrt(2.0))))


def _row_tile(n_p, n_s, cap=512):
    for t in (1024, 512, 256, 128):
        if t > cap:
            continue
        if n_p % t == 0 and n_s % t == 0:
            return t
    raise ValueError(f"token counts {n_p}, {n_s} must be multiples of 128")


def _group_specs(tm, width, n_p):
    pt = n_p // tm
    return [pl.BlockSpec((tm, width), lambda i, *_: (jnp.minimum(i, pt - 1), 0)),
            pl.BlockSpec((tm, width), lambda i, *_: (jnp.maximum(i - pt, 0), 0))]


def _group_pick(prompt_ref, sample_ref, n_p):
    pt = n_p // prompt_ref.shape[0]
    return jnp.where(pl.program_id(0) < pt, prompt_ref[...], sample_ref[...])


def _in_proj_kernel(xp_ref, xs_ref, g_ref, w_ref, wab_ref, z_ref, ab_ref, h_scr, *, n_p):
    @pl.when(pl.program_id(1) == 0)
    def _():
        hb = _rms(_group_pick(xp_ref, xs_ref, n_p), g_ref[...]).astype(BF16)
        h_scr[...] = hb
        ab_ref[...] = jnp.dot(hb, wab_ref[...], preferred_element_type=F32)

    z_ref[...] = jnp.dot(h_scr[...], w_ref[...], preferred_element_type=F32).astype(z_ref.dtype)


def _in_proj(xp, xs, g, w_main, w_ab):
    n_p, n_s = xp.shape[0], xs.shape[0]
    n = n_p + n_s
    tm, tn = _row_tile(n_p, n_s, cap=1024), 2048
    cols = w_main.shape[1]
    return pl.pallas_call(
        functools.partial(_in_proj_kernel, n_p=n_p),
        grid=(n // tm, cols // tn),
        in_specs=_group_specs(tm, D_MODEL, n_p) + [
            pl.BlockSpec((1, D_MODEL), lambda i, j: (0, 0)),
            pl.BlockSpec((D_MODEL, tn), lambda i, j: (0, j)),
            pl.BlockSpec((D_MODEL, LANES), lambda i, j: (0, 0)),
        ],
        out_specs=[
            pl.BlockSpec((tm, tn), lambda i, j: (i, j)),
            pl.BlockSpec((tm, LANES), lambda i, j: (i, 0)),
        ],
        out_shape=[jax.ShapeDtypeStruct((n, cols), BF16), jax.ShapeDtypeStruct((n, LANES), F32)],
        scratch_shapes=[pltpu.VMEM((tm, D_MODEL), BF16)],
        compiler_params=_params("parallel", "arbitrary"),
        name="in_proj",
    )(xp, xs, g, w_main, w_ab)


def _delta_kernel(*refs, C, G, carry, precise):
    (qkv_ref, zg_ref, ab_ref, buf_ref, s0_ref, cw_ref, alog_ref, dtb_ref, ong_ref) = refs[:9]
    refs = refs[9:]
    o_ref, snew_ref, bufnew_ref, xc_scr = refs[:4]
    H, DK, DV = DN_HEADS, DN_DK, DN_DV
    T = G * C
    dotm = _dot3 if precise else _dot
    dotm_tn = _dot3_tn if precise else _dot_tn
    halo = DN_CONV - 1
    base = SUBLANES
    cw = cw_ref[...]

    def conv(window):
        y = window(0) * cw[0:1]
        for i in range(1, DN_CONV):
            y = y + window(i) * cw[i:i + 1]
        return y

    if carry:
        s_scr = refs[4]
        c = pl.program_id(1)
        last = pl.num_programs(1) - 1

        @pl.when(c == 0)
        def _():
            xc_scr[base - halo:base, :] = buf_ref[0]
            s_scr[...] = s0_ref[0]

        xb = qkv_ref[...]
        xf = xb.astype(F32)
        xc_scr[base:base + SUBLANES, :] = xf[0:SUBLANES]
        y_head = conv(lambda i: xc_scr[base - halo + i:base - halo + i + SUBLANES, :])
        ri = lax.broadcasted_iota(jnp.int32, (halo * T, T), 0)
        ci = lax.broadcasted_iota(jnp.int32, (halo * T, T), 1)
        src_row = (ri & (T - 1)) - (halo - (ri >> (T.bit_length() - 1)))
        shifted = jnp.dot(jnp.where(ci == src_row, 1.0, 0.0).astype(BF16), xb, preferred_element_type=F32)
        y = conv(lambda i: shifted[i * T:(i + 1) * T] if i < halo else xf)
        y = jnp.concatenate([y_head, y[SUBLANES:]], axis=0)
        tail = xf[T - halo:T]
        xc_scr[base - halo:base, :] = tail

        @pl.when(c == last)
        def _():
            bufnew_ref[0] = tail
    else:
        ys = []
        x_new = qkv_ref[...].astype(F32)
        for g in range(G):
            xc_scr[g, base - halo:base, :] = buf_ref[g]
            xc_scr[g, base:base + C, :] = x_new[g * C:(g + 1) * C, :]
            ys.append(conv(lambda i: xc_scr[g, base - halo + i:base - halo + i + C, :]))
            bufnew_ref[g] = xc_scr[g, base + C - halo:base + C, :]
        y = jnp.concatenate(ys, axis=0) if G > 1 else ys[0]
    qkv = y * _sigmoid(y)

    ab = ab_ref[...]
    g_all = -jnp.exp(alog_ref[...]) * jax.nn.softplus(ab + dtb_ref[...])
    beta_all = _sigmoid(ab)
    shift = C.bit_length() - 1
    rt = lax.broadcasted_iota(jnp.int32, (T, T), 0)
    ct = lax.broadcasted_iota(jnp.int32, (T, T), 1)
    chunk_tril = ((rt >> shift) == (ct >> shift)) & (rt >= ct)
    gcum = _dot3(chunk_tril.astype(F32), g_all)
    gam_all = jnp.exp(gcum)
    wide = C >= LANES // 2
    if wide:
        gcum_t, beta_t = gcum.T, beta_all.T

    row = lax.broadcasted_iota(jnp.int32, (C, C), 0)
    col = lax.broadcasted_iota(jnp.int32, (C, C), 1)
    incl, strict, eye = row >= col, row > col, row == col
    eyef = eye.astype(F32)
    units = [(g, h) for g in range(G) for h in range(H)]
    rows = lambda a, g: a[g * C:(g + 1) * C]

    qn, kn, vv = [], [], []
    for h in range(H):
        q = qkv[:, h * DK:(h + 1) * DK]
        k = qkv[:, QK_W + h * DK:QK_W + (h + 1) * DK]
        qn.append(q * (lax.rsqrt(jnp.sum(q * q, axis=-1, keepdims=True) + EPS) * (DK ** -0.5)))
        kn.append(k * lax.rsqrt(jnp.sum(k * k, axis=-1, keepdims=True) + EPS))
        vv.append(qkv[:, 2 * QK_W + h * DV:2 * QK_W + (h + 1) * DV])
    qb = [q.astype(BF16) for q in qn]
    kb = [k.astype(BF16) for k in kn]

    gc, bc, gl, a_low, m_intra = {}, {}, {}, {}, {}
    for u in units:
        g, h = u
        gc[u] = rows(gcum, g)[:, h:h + 1]
        bc[u] = rows(beta_all, g)[:, H + h:H + h + 1]
        gl[u] = gc[u][C - 1:C, :]
        if wide:
            gr = gcum_t[h:h + 1, g * C:(g + 1) * C]
            br = beta_t[H + h:H + h + 1, g * C:(g + 1) * C]
        else:
            gr = jnp.sum(jnp.where(eye, gc[u], 0.0), axis=0, keepdims=True)
            br = jnp.sum(jnp.where(eye, bc[u], 0.0), axis=0, keepdims=True)
        db = jnp.where(incl, jnp.exp(gc[u] - gr), 0.0) * br
        k = rows(kb[h], g)
        kq = _dot_nt(jnp.concatenate([k, rows(qb[h], g)], axis=0), k)
        a_low[u] = jnp.where(strict, kq[:C] * db, 0.0)
        m_intra[u] = kq[C:] * db

    t_inv = {u: eyef - a_low[u] for u in units}
    a_pow = {u: dotm(a_low[u], a_low[u]) for u in units}
    n = 2
    while n < C:
        t_inv = {u: t_inv[u] + dotm(t_inv[u], a_pow[u]) for u in units}
        n *= 2
        if n < C:
            a_pow = {u: dotm(a_pow[u], a_pow[u]) for u in units}

    u_base, wq, k_dec = {}, {}, {}
    for u in units:
        g, h = u
        gam = rows(gam_all, g)[:, h:h + 1]
        k = rows(kn[h], g)
        sol = dotm(t_inv[u], jnp.concatenate([rows(vv[h], g), gam * k], axis=1))
        u_base[u] = sol[:, :DV]
        wq[u] = jnp.concatenate([sol[:, DV:], gam * rows(qn[h], g)], axis=0)
        k_dec[u] = k * (bc[u] * jnp.exp(gl[u] - gc[u]))

    state = [s_scr[h] for h in range(H)] if carry else None
    outs = {}
    for g in range(G):
        s_in = state if carry else [s0_ref[g, h] for h in range(H)]
        wqs = [dotm(wq[(g, h)], s_in[h]) for h in range(H)]
        us = [u_base[(g, h)] - wqs[h][:C] for h in range(H)]
        for h in range(H):
            outs[(g, h)] = wqs[h][C:] + dotm(m_intra[(g, h)], us[h])
        s_out = [jnp.exp(gl[(g, h)]) * s_in[h] + dotm_tn(k_dec[(g, h)], us[h]) for h in range(H)]
        if carry:
            state = s_out
        else:
            for h in range(H):
                snew_ref[g, h] = s_out[h]

    for h in range(H):
        o = jnp.concatenate([outs[(g, h)] for g in range(G)], axis=0) if G > 1 else outs[(0, h)]
        zg = zg_ref[:, h * DV:(h + 1) * DV].astype(F32)
        o_ref[:, h * DV:(h + 1) * DV] = (_rms(o, ong_ref[...]) * (zg * _sigmoid(zg))).astype(o_ref.dtype)

    if carry:
        for h in range(H):
            s_scr[h] = state[h]

        @pl.when(c == last)
        def _():
            for h in range(H):
                snew_ref[0, h] = state[h]


def _delta_branch(z, ab, conv_buf, s0, conv_w, alog_p, dtb_p, onorm_g, *, tok0, L, G, precise):
    B = conv_buf.shape[0]
    C = math.gcd(L, DN_CHUNK)
    nc = L // C
    carry = nc > 1
    T = G * C
    blk0 = tok0 // T
    if carry:
        assert nc % G == 0
        grid = (B, nc // G)
        own_blk = lambda b, c: b * (nc // G) + c
        gs = 1
    else:
        assert B % G == 0
        grid = (B // G, 1)
        own_blk = lambda b, c: b
        gs = G
    tok = lambda b, c: (blk0 + own_blk(b, c), 0)
    seq3 = lambda b, c: (b, 0, 0)
    seq4 = lambda b, c: (b, 0, 0, 0)
    const = lambda b, c: (0, 0)
    if carry:
        scratch = [pltpu.VMEM((2 * SUBLANES, QKV_W), F32), pltpu.VMEM((DN_HEADS, DN_DK, DN_DV), F32)]
    else:
        scratch = [pltpu.VMEM((G, SUBLANES + C, QKV_W), F32)]
    return pl.pallas_call(
        functools.partial(_delta_kernel, C=C, G=G, carry=carry, precise=precise),
        grid=grid,
        in_specs=[
            pl.BlockSpec((T, QKV_W), tok),
            pl.BlockSpec((T, V_W), lambda b, c: (blk0 + own_blk(b, c), QKV_W // V_W)),
            pl.BlockSpec((T, LANES), tok),
            pl.BlockSpec((gs, DN_CONV - 1, QKV_W), seq3),
            pl.BlockSpec((gs, DN_HEADS, DN_DK, DN_DV), seq4),
            pl.BlockSpec((DN_CONV, QKV_W), const),
            pl.BlockSpec((1, LANES), const),
            pl.BlockSpec((1, LANES), const),
            pl.BlockSpec((1, DN_DV), const),
        ],
        out_specs=[
            pl.BlockSpec((T, V_W), lambda b, c: (own_blk(b, c), 0)),
            pl.BlockSpec((gs, DN_HEADS, DN_DK, DN_DV), seq4),
            pl.BlockSpec((gs, DN_CONV - 1, QKV_W), seq3),
        ],
        out_shape=[
            jax.ShapeDtypeStruct((B * L, V_W), BF16),
            jax.ShapeDtypeStruct(s0.shape, F32),
            jax.ShapeDtypeStruct(conv_buf.shape, F32),
        ],
        scratch_shapes=scratch,
        compiler_params=_params("parallel", "arbitrary"),
        name=f"delta_rule_c{C}",
    )(z, z, ab, conv_buf, s0, conv_w, alog_p, dtb_p, onorm_g)


def _post_mix_kernel(op_ref, os_ref, gu_ref, gv_ref, ma_ref, mb_ref, xp_ref, xs_ref, lng_ref, lnb_ref,
                     mix_ref, bias_ref, wa_ref, wb_ref, wo_ref, g2_ref, rwh_ref, rwl_ref, rb_ref, tri_ref,
                     x1_ref, h2_ref, gate_ref, idx_ref, v_ref, cnt_ref, us_scr, cnt_scr, *, n_p):
    u = _gelu(gu_ref[...].astype(F32))
    a = _gelu(gv_ref[...].astype(F32))
    ac = a - jnp.mean(a, axis=-1, keepdims=True)
    v = ac * lax.rsqrt(jnp.mean(ac * ac, axis=-1, keepdims=True) + EPS) * lng_ref[...] + lnb_ref[...]
    v_ref[...] = v
    vb = v.astype(BF16)
    gw = GM_WIDTH // GM_GROUPS
    for c in range(u.shape[0] // GM_CHUNK):
        rs = slice(c * GM_CHUNK, (c + 1) * GM_CHUNK)
        for g in range(GM_GROUPS):
            sl = slice(g * gw, (g + 1) * gw)
            s = jnp.dot(mix_ref[0, g], vb[rs, sl], preferred_element_type=F32) + bias_ref[0, :, sl]
            us_scr[rs, sl] = (u[rs, sl] * s).astype(BF16)

    ya = jnp.dot(_group_pick(op_ref, os_ref, n_p).astype(BF16), wa_ref[...], preferred_element_type=F32)
    yb = jnp.dot(us_scr[...], wb_ref[...], preferred_element_type=F32)
    mixed = _sigmoid(ma_ref[...].astype(F32)) * ya + _sigmoid(mb_ref[...].astype(F32)) * yb
    x1 = _group_pick(xp_ref, xs_ref, n_p) + jnp.dot(mixed.astype(BF16), wo_ref[...], preferred_element_type=F32)
    x1_ref[...] = x1
    h2 = _rms(x1, g2_ref[...])
    h2_ref[...] = _pack_bf16_pairs(h2)
    hh, hl = _split(h2)
    d = functools.partial(jnp.dot, preferred_element_type=F32)
    logits = d(hh, rwh_ref[...]) + (d(hh, rwl_ref[...]) + d(hl, rwh_ref[...])) + rb_ref[...]
    lane = lax.broadcasted_iota(jnp.int32, logits.shape, 1).astype(F32)
    vals, idxs = [], []
    for _ in range(TOP_K):
        m = jnp.max(logits, axis=-1, keepdims=True)
        i = jnp.min(jnp.where(logits == m, lane, float(LANES)), axis=-1, keepdims=True)
        vals.append(m)
        idxs.append(i)
        logits = jnp.where(lane == i, -jnp.inf, logits)
    es = [jnp.exp(v - vals[0]) for v in vals]
    tot = es[0]
    for e in es[1:]:
        tot = tot + e

    @pl.when(pl.program_id(0) == 0)
    def _():
        cnt_scr[...] = jnp.zeros(cnt_scr.shape, F32)

    onehot = (lane == idxs[0]).astype(F32)
    for k in range(1, TOP_K):
        onehot = onehot + (lane == idxs[k]).astype(F32)
    before = jnp.dot(tri_ref[...], onehot.astype(BF16), preferred_element_type=F32) + cnt_scr[...]
    cnt_scr[...] = cnt_scr[...] + jnp.sum(onehot, axis=0, keepdims=True)
    cnt_ref[...] = cnt_scr[...]

    gates = jnp.zeros(logits.shape, F32)
    route = jnp.zeros(logits.shape, F32)
    for k in range(TOP_K):
        rank = jnp.sum(jnp.where(lane == idxs[k], before, 0.0), axis=-1, keepdims=True)
        gates = jnp.where(lane == float(k), es[k] / tot, gates)
        route = jnp.where(lane == float(k), idxs[k], route)
        route = jnp.where(lane == float(TOP_K + k), rank, route)
    gate_ref[...] = gates
    idx_ref[...] = route.T[:2 * TOP_K].astype(jnp.int32)


def _post_mix(o_p, o_s, z, xp, xs, ln_g, ln_b, mix, bias, wa, wb, wo, g2, rwh, rwl, rb):
    n_p, n_s = xp.shape[0], xs.shape[0]
    n = n_p + n_s
    tm = _row_tile(n_p, n_s)
    pt = n_p // tm
    tri = jnp.tril(jnp.ones((tm, tm), BF16), k=-1)
    grp = lambda i: jnp.where(i < pt, 0, 1)
    once = pl.Buffered(1)
    tok = pl.BlockSpec((tm, D_MODEL), lambda i: (i, 0))
    zcol = lambda c: pl.BlockSpec((tm, D_MODEL), lambda i: (i, c))
    full = lambda r, c: pl.BlockSpec((r, c), lambda i: (0, 0), pipeline_mode=once)
    narrow = pl.BlockSpec((tm, LANES), lambda i: (i, 0))
    op_spec, os_spec = _group_specs(tm, V_W, n_p)
    xp_spec, xs_spec = _group_specs(tm, D_MODEL, n_p)
    return pl.pallas_call(
        functools.partial(_post_mix_kernel, n_p=n_p),
        grid=(n // tm,),
        in_specs=[
            op_spec, os_spec, zcol(4), zcol(5), zcol(6), zcol(7), xp_spec, xs_spec,
            full(1, GM_WIDTH), full(1, GM_WIDTH),
            pl.BlockSpec((1, GM_GROUPS, GM_CHUNK, GM_CHUNK), lambda i: (grp(i), 0, 0, 0)),
            pl.BlockSpec((1, GM_CHUNK, GM_WIDTH), lambda i: (grp(i), 0, 0)),
            full(V_W, D_MODEL), full(GM_WIDTH, D_MODEL), full(D_MODEL, D_MODEL), full(1, D_MODEL),
            full(D_MODEL, LANES), full(D_MODEL, LANES), full(1, LANES), full(tm, tm),
        ],
        out_specs=[tok, pl.BlockSpec((tm, D_MODEL // 2), lambda i: (i, 0)), narrow,
                   pl.BlockSpec((2 * TOP_K, tm), lambda i: (0, i)),
                   pl.BlockSpec((tm, GM_WIDTH), lambda i: (jnp.maximum(i - pt, 0), 0)),
                   pl.BlockSpec((1, LANES), lambda i: (0, 0))],
        out_shape=[
            jax.ShapeDtypeStruct((n, D_MODEL), F32),
            jax.ShapeDtypeStruct((n, D_MODEL // 2), jnp.int32),
            jax.ShapeDtypeStruct((n, LANES), F32),
            jax.ShapeDtypeStruct((2 * TOP_K, n), jnp.int32),
            jax.ShapeDtypeStruct((n_s, GM_WIDTH), F32),
            jax.ShapeDtypeStruct((1, LANES), F32),
        ],
        scratch_shapes=[pltpu.VMEM((tm, GM_WIDTH), BF16), pltpu.VMEM((1, LANES), F32)],
        compiler_params=_params("arbitrary"),
        name="chunk_mlp_post_mix_router",
    )(o_p, o_s, z, z, z, z, xp, xs, ln_g, ln_b, mix, bias, wa, wb, wo, g2, rwh, rwl, rb, tri)


def _moe_kernel(be_ref, first_ref, valid_ref, next_ref, slot_ref, x_ref, w1_hbm, b1_ref, w2_hbm, b2_ref, y_ref,
                w1_stage, w2_stage, sems):
    b = pl.program_id(0)
    slot = slot_ref[b]

    def weight_copies(e, s):
        return (pltpu.make_async_copy(w1_hbm.at[e], w1_stage.at[s], sems.at[0, s]),
                pltpu.make_async_copy(w2_hbm.at[e], w2_stage.at[s], sems.at[1, s]))

    @pl.when(b == 0)
    def _():
        for c in weight_copies(be_ref[0], slot):
            c.start()

    @pl.when(first_ref[b] == 1)
    def _():
        for c in weight_copies(be_ref[b], slot):
            c.wait()

        @pl.when(next_ref[b] >= 0)
        def _():
            for c in weight_copies(next_ref[b], 1 - slot):
                c.start()

    @pl.when(valid_ref[b] == 1)
    def _():
        x = jnp.concatenate(_unpack_bf16_pairs(x_ref[...]), axis=1).astype(BF16)
        hid = jnp.dot(x, w1_stage[slot].astype(BF16), preferred_element_type=F32) + b1_ref[0]
        gate = jnp.minimum(hid[:, :D_EXPERT], SWIGLU_LIMIT)
        up = jnp.clip(hid[:, D_EXPERT:], -SWIGLU_LIMIT, SWIGLU_LIMIT)
        act = gate * _sigmoid(SWIGLU_ALPHA * gate) * (up + 1.0)
        y = jnp.dot(act.astype(BF16), w2_stage[slot].astype(BF16), preferred_element_type=F32) + b2_ref[0]
        y_ref[...] = _pack_bf16_pairs(y)

    @pl.when(valid_ref[b] == 0)
    def _():
        y_ref[...] = jnp.zeros(y_ref.shape, jnp.int32)


def _moe_experts(block_e, first, valid, next_e, slot, xb, w1, b1, w2, b2):
    rows = xb.shape[0]
    nb = rows // MOE_ROWS
    smem4 = lambda f: (lambda b, be, fi, va, ne, sl: f(b, be))
    return pl.pallas_call(
        _moe_kernel,
        grid_spec=pltpu.PrefetchScalarGridSpec(
            num_scalar_prefetch=5,
            grid=(nb,),
            in_specs=[
                pl.BlockSpec((MOE_ROWS, D_MODEL // 2), smem4(lambda b, be: (b, 0))),
                pl.BlockSpec(memory_space=pl.ANY),
                pl.BlockSpec((1, 1, 2 * D_EXPERT), smem4(lambda b, be: (be[b], 0, 0))),
                pl.BlockSpec(memory_space=pl.ANY),
                pl.BlockSpec((1, 1, D_MODEL), smem4(lambda b, be: (be[b], 0, 0))),
            ],
            out_specs=pl.BlockSpec((MOE_ROWS, D_MODEL // 2), smem4(lambda b, be: (b, 0))),
            scratch_shapes=[
                pltpu.VMEM((2, D_MODEL, 2 * D_EXPERT), F32),
                pltpu.VMEM((2, D_EXPERT, D_MODEL), F32),
                pltpu.SemaphoreType.DMA((2, 2)),
            ],
        ),
        out_shape=jax.ShapeDtypeStruct((rows, D_MODEL // 2), jnp.int32),
        compiler_params=_params("arbitrary"),
        name="moe_experts",
    )(block_e, first, valid, next_e, slot, xb, w1, b1, w2, b2)


def _moe_dispatch(idx, rank, counts, n):
    experts = jnp.arange(N_EXPERTS, dtype=jnp.int32)
    padded = (counts + MOE_ROWS - 1) // MOE_ROWS * MOE_ROWS
    pad_end = jnp.cumsum(padded)
    pad_start = pad_end - padded
    start_of = jnp.sum(jnp.where(idx[None] == experts[:, None, None], pad_start[:, None, None], 0), axis=0)
    dest_t = start_of + rank
    nb = -(-n * TOP_K // MOE_ROWS) + N_EXPERTS
    rows = nb * MOE_ROWS
    starts = jnp.arange(nb, dtype=jnp.int32) * MOE_ROWS
    valid = (starts < pad_end[-1]).astype(jnp.int32)
    owner = lambda r: jnp.minimum(jnp.sum((pad_end[None, :] <= r[:, None]).astype(jnp.int32), axis=1),
                                  N_EXPERTS - 1)
    last_e = owner(pad_end[-1:] - 1)[0]
    block_e = jnp.where(valid == 1, owner(starts), last_e).astype(jnp.int32)
    first = jnp.concatenate([jnp.ones((1,), jnp.int32),
                             (block_e[1:] != block_e[:-1]).astype(jnp.int32)])
    blk = jnp.arange(nb, dtype=jnp.int32)
    later_first = (blk[None, :] > blk[:, None]) & (first[None, :] == 1)
    next_pos = jnp.min(jnp.where(later_first, blk[None, :], nb), axis=1)
    next_e = jnp.where(next_pos < nb, block_e[jnp.minimum(next_pos, nb - 1)], -1).astype(jnp.int32)
    slot = (jnp.cumsum(first) - 1) % 2
    return dest_t, rows, block_e, first, valid, next_e, slot.astype(jnp.int32)


def _sc_scatter_rows(src, dest_t, out_rows):
    n, d = src.shape
    nk = dest_t.shape[0]
    assert nk * SC_TOKENS == SC_WINDOW
    idx = dest_t.reshape(nk, n // SC_TOKENS, SC_TOKENS).transpose(1, 0, 2).reshape(n // SC_TOKENS, SC_WINDOW)
    mesh = plsc.VectorSubcoreMesh(core_axis_name="core", subcore_axis_name="subcore",
                                  num_cores=SC_CORES, num_subcores=SC_SUBCORES)

    @pl.kernel(out_type=jax.ShapeDtypeStruct((out_rows, d), src.dtype), mesh=mesh, scratch_types=[])
    def scatter_rows(src_hbm, di_hbm, out_hbm):
        def body(x_vmem, di_vmem):
            for k in range(nk):
                pltpu.sync_copy(x_vmem, out_hbm.at[di_vmem.at[0, pl.ds(k * SC_TOKENS, SC_TOKENS)]])

        pltpu.emit_pipeline(
            body,
            grid=(n // SC_TOKENS,),
            in_specs=[pl.BlockSpec((SC_TOKENS, d), lambda i: (i, 0)),
                      pl.BlockSpec((1, SC_WINDOW), lambda i: (i, 0))],
            out_specs=[],
            core_axis_name=("core", "subcore"),
            dimension_semantics=(pltpu.PARALLEL,),
        )(src_hbm, di_hbm)

    return scatter_rows(src, idx)


def _sc_gather_rows(src, idx):
    m = idx.shape[0]
    d = src.shape[1]
    idx_rows = jnp.pad(idx.reshape(m // SC_CHUNK, SC_CHUNK), ((0, 0), (0, SC_WINDOW - SC_CHUNK)))
    mesh = plsc.VectorSubcoreMesh(core_axis_name="core", subcore_axis_name="subcore",
                                  num_cores=SC_CORES, num_subcores=SC_SUBCORES)

    @pl.kernel(out_type=jax.ShapeDtypeStruct((m, d), src.dtype), mesh=mesh, scratch_types=[])
    def gather_rows(src_hbm, si_hbm, out_hbm):
        def body(si_vmem, o_vmem):
            pltpu.sync_copy(src_hbm.at[si_vmem.at[0, pl.ds(0, SC_CHUNK)]], o_vmem)

        pltpu.emit_pipeline(
            body,
            grid=(m // SC_CHUNK,),
            in_specs=[pl.BlockSpec((1, SC_WINDOW), lambda i: (i, 0))],
            out_specs=[pl.BlockSpec((SC_CHUNK, d), lambda i: (i, 0))],
            core_axis_name=("core", "subcore"),
            dimension_semantics=(pltpu.PARALLEL,),
        )(si_hbm, out_hbm)

    return gather_rows(src, idx_rows)


def _tail_kernel(x1_ref, yk_ref, gate_ref, pp_ref, ps_ref, g3_ref, wg_ref, wp_ref, gf_ref,
                 yp_ref, ys_ref, *, n_p):
    tm = x1_ref.shape[0]
    parts = [slice(p * tm // TAIL_PARTS, (p + 1) * tm // TAIL_PARTS) for p in range(TAIL_PARTS)]
    in_prompt = pl.program_id(0) < n_p // tm
    d = functools.partial(jnp.dot, preferred_element_type=F32)

    def combine(r):
        gates = gate_ref[r, :]
        lo, hi = _unpack_bf16_pairs(yk_ref[0, r, :])
        lo, hi = lo * gates[:, 0:1], hi * gates[:, 0:1]
        for k in range(1, TOP_K):
            lo_k, hi_k = _unpack_bf16_pairs(yk_ref[k, r, :])
            lo, hi = lo + lo_k * gates[:, k:k + 1], hi + hi_k * gates[:, k:k + 1]
        return x1_ref[r, :] + jnp.concatenate([lo, hi], axis=1)

    x2 = [combine(r) for r in parts]
    h3 = [_rms(x, g3_ref[...]) for x in x2]
    gate = [_sigmoid(d(h.astype(BF16), wg_ref[...])) for h in h3]
    pe = [d(jnp.where(in_prompt, pp_ref[r, :], ps_ref[r, :]).astype(BF16), wp_ref[...]) for r in parts]
    y = [_rms(x + g * p, gf_ref[...]) for x, g, p in zip(x2, gate, pe)]

    @pl.when(in_prompt)
    def _():
        for r, yy in zip(parts, y):
            yp_ref[r, :] = yy

    @pl.when(jnp.logical_not(in_prompt))
    def _():
        for r, yy in zip(parts, y):
            ys_ref[r, :] = yy


def _tail(x1, yk, gates, pp, ps, g3, wg, wp, gf):
    n_p, n_s = pp.shape[0], ps.shape[0]
    n = n_p + n_s
    tm = _row_tile(n_p, n_s)
    tok = pl.BlockSpec((tm, D_MODEL), lambda i: (i, 0))
    full = lambda r, c: pl.BlockSpec((r, c), lambda i: (0, 0))
    return pl.pallas_call(
        functools.partial(_tail_kernel, n_p=n_p),
        grid=(n // tm,),
        in_specs=[tok, pl.BlockSpec((TOP_K, tm, D_MODEL // 2), lambda i: (0, i, 0)),
                  pl.BlockSpec((tm, LANES), lambda i: (i, 0)),
                  *_group_specs(tm, PLE_DIM, n_p),
                  full(1, D_MODEL), full(D_MODEL, D_MODEL), full(PLE_DIM, D_MODEL), full(1, D_MODEL)],
        out_specs=_group_specs(tm, D_MODEL, n_p),
        out_shape=[jax.ShapeDtypeStruct((n_p, D_MODEL), F32), jax.ShapeDtypeStruct((n_s, D_MODEL), F32)],
        compiler_params=_params("arbitrary"),
        name="ple_final_norm",
    )(x1, yk, gates, pp, ps, g3, wg, wp, gf)


def _lane_pad(v, offset, fill=0.0):
    out = jnp.full((1, LANES), fill, F32)
    return out.at[0, offset:offset + v.shape[0]].set(v.astype(F32))


def kernel(x_prompt, x_sample, state_delta, state_conv, p_prompt, p_sample, norm1_g, w_in, conv_w, a_log, dt_bias, dn_norm_g, w_proj_a, gm_ln_g, gm_ln_b, gm_ws, gm_bs, w_proj_b, w_out, norm2_g, router_w, router_b, moe_w1, moe_b1, moe_w2, moe_b2, norm3_g, ple_w, ple_gate_w, final_norm_g):
    bp, lp, d = x_prompt.shape
    bs, ls, _ = x_sample.shape
    depth = w_in.shape[0]
    assert depth == 1 and d == D_MODEL
    assert lp % GM_CHUNK == 0 and GM_CHUNK % ls == 0 and ls >= DN_CONV - 1
    n_p, n_s = bp * lp, bs * ls
    n = n_p + n_s
    i = 0

    xp, xs = x_prompt.reshape(n_p, d), x_sample.reshape(n_s, d)

    ab0 = QKV_W
    w = w_in[i]
    w_main = jnp.concatenate([w[:, :ab0], w[:, ab0 + 2 * DN_HEADS:]], axis=1).astype(BF16)
    w_ab = jnp.pad(w[:, ab0:ab0 + 2 * DN_HEADS], ((0, 0), (0, LANES - 2 * DN_HEADS))).astype(BF16)
    row2 = lambda v: v.reshape(1, -1).astype(F32)

    z, ab = _in_proj(xp, xs, row2(norm1_g[i]), w_main, w_ab)

    alog_p = _lane_pad(a_log[i], 0)
    dtb_p = _lane_pad(dt_bias[i], 0)
    cw = conv_w[i].astype(F32)
    ong = row2(dn_norm_g[i])
    zero_s = jnp.zeros((bp, DN_HEADS, DN_DK, DN_DV), F32)
    zero_buf = jnp.zeros((bp, DN_CONV - 1, QKV_W), F32)
    o_p, sd_p, sc_p = _delta_branch(z, ab, zero_buf, zero_s, cw, alog_p, dtb_p, ong,
                                    tok0=0, L=lp, G=4, precise=False)
    o_s, sd_s, sc_s = _delta_branch(z, ab, state_conv[i], state_delta[i], cw, alog_p, dtb_p, ong,
                                    tok0=n_p, L=ls, G=8, precise=False)

    t = GM_CHUNK
    tri = jnp.tril(jnp.ones((t, t), bool))
    ws = gm_ws[i]
    mix_p = jnp.where(tri, ws, 0.0)
    small = jnp.where(tri[:ls, :ls], ws[:, :ls, :ls], 0.0)
    mix_s = jnp.einsum('ab,gts->gatbs', jnp.eye(t // ls, dtype=F32), small).reshape(GM_GROUPS, t, t)
    mix = jnp.stack([mix_p, mix_s]).astype(BF16)
    gw = GM_WIDTH // GM_GROUPS
    bias_p = jnp.repeat(gm_bs[i].T, gw, axis=1)
    bias_s = jnp.tile(bias_p[:ls], (t // ls, 1))
    bias = jnp.stack([bias_p, bias_s]).astype(F32)
    rw = jnp.pad(router_w[i].astype(F32), ((0, 0), (0, LANES - N_EXPERTS)))
    rwh = rw.astype(BF16)
    rwl = (rw - rwh.astype(F32)).astype(BF16)
    rb = _lane_pad(router_b[i], 0, fill=-jnp.inf)
    x1, h2, gates, route, v_s, counts = _post_mix(
        o_p, o_s, z, xp, xs, row2(gm_ln_g[i]), row2(gm_ln_b[i]), mix, bias, w_proj_a[i].astype(BF16),
        w_proj_b[i].astype(BF16), w_out[i].astype(BF16), row2(norm2_g[i]), rwh, rwl, rb)
    dest_t, rows, block_e, first, valid, next_e, slot = _moe_dispatch(
        route[:TOP_K], route[TOP_K:], counts[0, :N_EXPERTS].astype(jnp.int32), n)
    xb = _sc_scatter_rows(h2, dest_t, rows)
    yb = _moe_experts(block_e, first, valid, next_e, slot, xb, moe_w1[i], moe_b1[i][:, None, :],
                      moe_w2[i], moe_b2[i][:, None, :])
    yk = _sc_gather_rows(yb, dest_t.reshape(TOP_K * n)).reshape(TOP_K, n, d // 2)

    y_p, y_s = _tail(x1, yk, gates, p_prompt[i].reshape(n_p, PLE_DIM), p_sample[i].reshape(n_s, PLE_DIM),
                     row2(norm3_g[i]), ple_gate_w[i].astype(BF16), ple_w[i].astype(BF16), row2(final_norm_g))

    return (y_p.reshape(bp, lp, d), y_s.reshape(bs, ls, d),
            sd_p[None], sc_p[None], sd_s[None], sc_s[None], v_s.reshape(1, bs, ls, GM_WIDTH))
```

```python
import functools
import math

import jax
import jax.numpy as jnp
from jax import lax
from jax.experimental import pallas as pl
from jax.experimental.pallas import tpu as pltpu
from jax.experimental.pallas import tpu_sc as plsc

F32 = jnp.float32
BF16 = jnp.bfloat16

D_MODEL = 1024
DN_HEADS = 8
DN_DK = 128
DN_DV = 128
DN_CONV = 4
DN_CHUNK = 64
GM_WIDTH = 1024
GM_GROUPS = 8
GM_CHUNK = 128
N_EXPERTS = 32
TOP_K = 4
D_EXPERT = 1024
SWIGLU_LIMIT = 7.0
SWIGLU_ALPHA = 1.702
PLE_DIM = 256
EPS = 1e-6
QK_W = DN_HEADS * DN_DK
V_W = DN_HEADS * DN_DV
QKV_W = 2 * QK_W + V_W

LANES = 128
SUBLANES = 8
TAIL_PARTS = 4
MOE_ROWS = 512
VMEM_LIMIT = 56 << 20
SC_CORES, SC_SUBCORES = 2, 16
SC_WINDOW = 128
SC_TOKENS = SC_WINDOW // TOP_K
SC_CHUNK = 64


def _params(*sem):
    return pltpu.CompilerParams(dimension_semantics=sem, vmem_limit_bytes=VMEM_LIMIT)


def _dot(a, b):
    return jnp.dot(a.astype(BF16), b.astype(BF16), preferred_element_type=F32)


def _split(a):
    hi = a.astype(BF16)
    return hi, (a - hi.astype(F32)).astype(BF16)


def _dot3(a, b):
    ah, al = _split(a)
    bh, bl = _split(b)
    d = functools.partial(jnp.dot, preferred_element_type=F32)
    return d(ah, bh) + (d(ah, bl) + d(al, bh))


def _dot_nt(a, b):
    return lax.dot_general(a.astype(BF16), b.astype(BF16), (((1,), (1,)), ((), ())),
                           preferred_element_type=F32)


def _dot_tn(a, b):
    return lax.dot_general(a.astype(BF16), b.astype(BF16), (((0,), (0,)), ((), ())),
                           preferred_element_type=F32)


def _dot3_tn(a, b):
    ah, al = _split(a)
    bh, bl = _split(b)
    d = functools.partial(lax.dot_general, dimension_numbers=(((0,), (0,)), ((), ())),
                          preferred_element_type=F32)
    return d(ah, bh) + (d(ah, bl) + d(al, bh))


def _sigmoid(x):
    return 0.5 * jnp.tanh(0.5 * x) + 0.5


def _pack_bf16_pairs(x):
    bits = lax.bitcast_convert_type(x.astype(BF16).astype(F32), jnp.uint32)
    half = x.shape[1] // 2
    return lax.bitcast_convert_type((bits[:, :half] >> 16) | bits[:, half:], jnp.int32)


def _unpack_bf16_pairs(w):
    bits = lax.bitcast_convert_type(w, jnp.uint32)
    return (lax.bitcast_convert_type(bits << 16, F32),
            lax.bitcast_convert_type(bits & jnp.uint32(0xFFFF0000), F32))


def _rms(x, g):
    return x * lax.rsqrt(jnp.mean(x * x, axis=-1, keepdims=True) + EPS) * g


def _gelu(x):
    return 0.5 * x * (1.0 + lax.erf(x * (1.0 / math.sqrt(2.0))))


def _row_tile(n_p, n_s, cap=512):
    for t in (1024, 512, 256, 128):
        if t > cap:
            continue
        if n_p % t == 0 and n_s % t == 0:
            return t
    raise ValueError(f"token counts {n_p}, {n_s} must be multiples of 128")


def _group_specs(tm, width, n_p):
    pt = n_p // tm
    return [pl.BlockSpec((tm, width), lambda i, *_: (jnp.minimum(i, pt - 1), 0)),
            pl.BlockSpec((tm, width), lambda i, *_: (jnp.maximum(i - pt, 0), 0))]


def _group_pick(prompt_ref, sample_ref, n_p):
    pt = n_p // prompt_ref.shape[0]
    return jnp.where(pl.program_id(0) < pt, prompt_ref[...], sample_ref[...])


def _in_proj_kernel(xp_ref, xs_ref, g_ref, w_ref, wab_ref, z_ref, ab_ref, h_scr, *, n_p):
    @pl.when(pl.program_id(1) == 0)
    def _():
        hb = _rms(_group_pick(xp_ref, xs_ref, n_p), g_ref[...]).astype(BF16)
        h_scr[...] = hb
        ab_ref[...] = jnp.dot(hb, wab_ref[...], preferred_element_type=F32)

    z_ref[...] = lax.dot_general(h_scr[...], w_ref[...], (((1,), (1,)), ((), ())),
                                 preferred_element_type=F32).astype(z_ref.dtype)


def _in_proj(xp, xs, g, w_main, w_ab):
    n_p, n_s = xp.shape[0], xs.shape[0]
    n = n_p + n_s
    tm, tn = _row_tile(n_p, n_s, cap=1024), 2048
    cols = w_main.shape[0]
    return pl.pallas_call(
        functools.partial(_in_proj_kernel, n_p=n_p),
        grid=(n // tm, cols // tn),
        in_specs=_group_specs(tm, D_MODEL, n_p) + [
            pl.BlockSpec((1, D_MODEL), lambda i, j: (0, 0)),
            pl.BlockSpec((tn, D_MODEL), lambda i, j: (j, 0)),
            pl.BlockSpec((D_MODEL, LANES), lambda i, j: (0, 0)),
        ],
        out_specs=[
            pl.BlockSpec((tm, tn), lambda i, j: (i, j)),
            pl.BlockSpec((tm, LANES), lambda i, j: (i, 0)),
        ],
        out_shape=[jax.ShapeDtypeStruct((n, cols), BF16), jax.ShapeDtypeStruct((n, LANES), F32)],
        scratch_shapes=[pltpu.VMEM((tm, D_MODEL), BF16)],
        compiler_params=_params("parallel", "arbitrary"),
        name="in_proj",
    )(xp, xs, g, w_main, w_ab)


def _delta_kernel(*refs, C, G, carry, precise):
    (qkv_ref, zg_ref, ab_ref, buf_ref, s0_ref, cw_ref, alog_ref, dtb_ref, ong_ref) = refs[:9]
    refs = refs[9:]
    o_ref, snew_ref, bufnew_ref, xc_scr = refs[:4]
    H, DK, DV = DN_HEADS, DN_DK, DN_DV
    T = G * C
    dotm = _dot3 if precise else _dot
    dotm_tn = _dot3_tn if precise else _dot_tn
    halo = DN_CONV - 1
    base = SUBLANES
    cw = cw_ref[...]

    def conv(window):
        y = window(0) * cw[0:1]
        for i in range(1, DN_CONV):
            y = y + window(i) * cw[i:i + 1]
        return y

    if carry:
        s_scr = refs[4]
        c = pl.program_id(1)
        last = pl.num_programs(1) - 1

        @pl.when(c == 0)
        def _():
            xc_scr[base - halo:base, :] = buf_ref[0]
            s_scr[...] = s0_ref[0]

        xb = qkv_ref[...]
        xf = xb.astype(F32)
        xc_scr[base:base + SUBLANES, :] = xf[0:SUBLANES]
        y_head = conv(lambda i: xc_scr[base - halo + i:base - halo + i + SUBLANES, :])
        ri = lax.broadcasted_iota(jnp.int32, (halo * T, T), 0)
        ci = lax.broadcasted_iota(jnp.int32, (halo * T, T), 1)
        src_row = (ri & (T - 1)) - (halo - (ri >> (T.bit_length() - 1)))
        shifted = jnp.dot(jnp.where(ci == src_row, 1.0, 0.0).astype(BF16), xb, preferred_element_type=F32)
        y = conv(lambda i: shifted[i * T:(i + 1) * T] if i < halo else xf)
        y = jnp.concatenate([y_head, y[SUBLANES:]], axis=0)
        tail = xf[T - halo:T]
        xc_scr[base - halo:base, :] = tail

        @pl.when(c == last)
        def _():
            bufnew_ref[0] = tail
    else:
        ys = []
        x_new = qkv_ref[...].astype(F32)
        for g in range(G):
            xc_scr[g, base - halo:base, :] = buf_ref[g]
            xc_scr[g, base:base + C, :] = x_new[g * C:(g + 1) * C, :]
            ys.append(conv(lambda i: xc_scr[g, base - halo + i:base - halo + i + C, :]))
            bufnew_ref[g] = xc_scr[g, base + C - halo:base + C, :]
        y = jnp.concatenate(ys, axis=0) if G > 1 else ys[0]
    qkv = y * _sigmoid(y)

    ab = ab_ref[...]
    g_all = -jnp.exp(alog_ref[...]) * jax.nn.softplus(ab + dtb_ref[...])
    beta_all = _sigmoid(ab)
    shift = C.bit_length() - 1
    rt = lax.broadcasted_iota(jnp.int32, (T, T), 0)
    ct = lax.broadcasted_iota(jnp.int32, (T, T), 1)
    chunk_tril = ((rt >> shift) == (ct >> shift)) & (rt >= ct)
    gcum = _dot3(chunk_tril.astype(F32), g_all)
    gam_all = jnp.exp(gcum)
    wide = C >= LANES // 2
    if wide:
        gcum_t, beta_t = gcum.T, beta_all.T

    row = lax.broadcasted_iota(jnp.int32, (C, C), 0)
    col = lax.broadcasted_iota(jnp.int32, (C, C), 1)
    incl, strict, eye = row >= col, row > col, row == col
    eyef = eye.astype(F32)
    units = [(g, h) for g in range(G) for h in range(H)]
    rows = lambda a, g: a[g * C:(g + 1) * C]

    qn, kn, vv = [], [], []
    for h in range(H):
        q = qkv[:, h * DK:(h + 1) * DK]
        k = qkv[:, QK_W + h * DK:QK_W + (h + 1) * DK]
        qn.append(q * (lax.rsqrt(jnp.sum(q * q, axis=-1, keepdims=True) + EPS) * (DK ** -0.5)))
        kn.append(k * lax.rsqrt(jnp.sum(k * k, axis=-1, keepdims=True) + EPS))
        vv.append(qkv[:, 2 * QK_W + h * DV:2 * QK_W + (h + 1) * DV])
    qb = [q.astype(BF16) for q in qn]
    kb = [k.astype(BF16) for k in kn]

    gc, bc, gl, a_low, m_intra = {}, {}, {}, {}, {}
    for u in units:
        g, h = u
        gc[u] = rows(gcum, g)[:, h:h + 1]
        bc[u] = rows(beta_all, g)[:, H + h:H + h + 1]
        gl[u] = gc[u][C - 1:C, :]
        if wide:
            gr = gcum_t[h:h + 1, g * C:(g + 1) * C]
            br = beta_t[H + h:H + h + 1, g * C:(g + 1) * C]
        else:
            gr = jnp.sum(jnp.where(eye, gc[u], 0.0), axis=0, keepdims=True)
            br = jnp.sum(jnp.where(eye, bc[u], 0.0), axis=0, keepdims=True)
        db = jnp.where(incl, jnp.exp(gc[u] - gr), 0.0) * br
        k = rows(kb[h], g)
        kq = _dot_nt(jnp.concatenate([k, rows(qb[h], g)], axis=0), k)
        a_low[u] = jnp.where(strict, kq[:C] * db, 0.0)
        m_intra[u] = kq[C:] * db

    t_inv = {u: eyef - a_low[u] for u in units}
    a_pow = {u: dotm(a_low[u], a_low[u]) for u in units}
    n = 2
    while n < C:
        t_inv = {u: t_inv[u] + dotm(t_inv[u], a_pow[u]) for u in units}
        n *= 2
        if n < C:
            a_pow = {u: dotm(a_pow[u], a_pow[u]) for u in units}

    u_base, wq, k_dec = {}, {}, {}
    for u in units:
        g, h = u
        gam = rows(gam_all, g)[:, h:h + 1]
        k = rows(kn[h], g)
        sol = dotm(t_inv[u], jnp.concatenate([rows(vv[h], g), gam * k], axis=1))
        u_base[u] = sol[:, :DV]
        wq[u] = jnp.concatenate([sol[:, DV:], gam * rows(qn[h], g)], axis=0)
        k_dec[u] = k * (bc[u] * jnp.exp(gl[u] - gc[u]))

    state = [s_scr[h] for h in range(H)] if carry else None
    outs = {}
    for g in range(G):
        s_in = state if carry else [s0_ref[g, h] for h in range(H)]
        wqs = [dotm(wq[(g, h)], s_in[h]) for h in range(H)]
        us = [u_base[(g, h)] - wqs[h][:C] for h in range(H)]
        for h in range(H):
            outs[(g, h)] = wqs[h][C:] + dotm(m_intra[(g, h)], us[h])
        s_out = [jnp.exp(gl[(g, h)]) * s_in[h] + dotm_tn(k_dec[(g, h)], us[h]) for h in range(H)]
        if carry:
            state = s_out
        else:
            for h in range(H):
                snew_ref[g, h] = s_out[h]

    for h in range(H):
        o = jnp.concatenate([outs[(g, h)] for g in range(G)], axis=0) if G > 1 else outs[(0, h)]
        zg = zg_ref[:, h * DV:(h + 1) * DV].astype(F32)
        o_ref[:, h * DV:(h + 1) * DV] = (_rms(o, ong_ref[...]) * (zg * _sigmoid(zg))).astype(o_ref.dtype)

    if carry:
        for h in range(H):
            s_scr[h] = state[h]

        @pl.when(c == last)
        def _():
            for h in range(H):
                snew_ref[0, h] = state[h]


def _delta_branch(z, ab, conv_buf, s0, conv_w, alog_p, dtb_p, onorm_g, *, tok0, L, G, precise):
    B = conv_buf.shape[0]
    C = math.gcd(L, DN_CHUNK)
    nc = L // C
    carry = nc > 1
    T = G * C
    blk0 = tok0 // T
    if carry:
        assert nc % G == 0
        grid = (B, nc // G)
        own_blk = lambda b, c: b * (nc // G) + c
        gs = 1
    else:
        assert B % G == 0
        grid = (B // G, 1)
        own_blk = lambda b, c: b
        gs = G
    tok = lambda b, c: (blk0 + own_blk(b, c), 0)
    seq3 = lambda b, c: (b, 0, 0)
    seq4 = lambda b, c: (b, 0, 0, 0)
    const = lambda b, c: (0, 0)
    if carry:
        scratch = [pltpu.VMEM((2 * SUBLANES, QKV_W), F32), pltpu.VMEM((DN_HEADS, DN_DK, DN_DV), F32)]
    else:
        scratch = [pltpu.VMEM((G, SUBLANES + C, QKV_W), F32)]
    return pl.pallas_call(
        functools.partial(_delta_kernel, C=C, G=G, carry=carry, precise=precise),
        grid=grid,
        in_specs=[
            pl.BlockSpec((T, QKV_W), tok),
            pl.BlockSpec((T, V_W), lambda b, c: (blk0 + own_blk(b, c), QKV_W // V_W)),
            pl.BlockSpec((T, LANES), tok),
            pl.BlockSpec((gs, DN_CONV - 1, QKV_W), seq3),
            pl.BlockSpec((gs, DN_HEADS, DN_DK, DN_DV), seq4),
            pl.BlockSpec((DN_CONV, QKV_W), const),
            pl.BlockSpec((1, LANES), const),
            pl.BlockSpec((1, LANES), const),
            pl.BlockSpec((1, DN_DV), const),
        ],
        out_specs=[
            pl.BlockSpec((T, V_W), lambda b, c: (own_blk(b, c), 0)),
            pl.BlockSpec((gs, DN_HEADS, DN_DK, DN_DV), seq4),
            pl.BlockSpec((gs, DN_CONV - 1, QKV_W), seq3),
        ],
        out_shape=[
            jax.ShapeDtypeStruct((B * L, V_W), BF16),
            jax.ShapeDtypeStruct(s0.shape, F32),
            jax.ShapeDtypeStruct(conv_buf.shape, F32),
        ],
        scratch_shapes=scratch,
        compiler_params=_params("parallel", "arbitrary"),
        name=f"delta_rule_c{C}",
    )(z, z, ab, conv_buf, s0, conv_w, alog_p, dtb_p, onorm_g)


def _post_mix_kernel(op_ref, os_ref, gu_ref, gv_ref, ma_ref, mb_ref, xp_ref, xs_ref, lng_ref, lnb_ref,
                     mix_ref, bias_ref, wa_ref, wb_ref, wo_ref, g2_ref, rwh_ref, rwl_ref, rb_ref, tri_ref,
                     x1_ref, h2_ref, gate_ref, idx_ref, v_ref, cnt_ref, us_scr, cnt_scr, *, n_p):
    u = _gelu(gu_ref[...].astype(F32))
    a = _gelu(gv_ref[...].astype(F32))
    ac = a - jnp.mean(a, axis=-1, keepdims=True)
    v = ac * lax.rsqrt(jnp.mean(ac * ac, axis=-1, keepdims=True) + EPS) * lng_ref[...] + lnb_ref[...]
    v_ref[...] = v
    vb = v.astype(BF16)
    gw = GM_WIDTH // GM_GROUPS
    for c in range(u.shape[0] // GM_CHUNK):
        rs = slice(c * GM_CHUNK, (c + 1) * GM_CHUNK)
        for g in range(GM_GROUPS):
            sl = slice(g * gw, (g + 1) * gw)
            s = jnp.dot(mix_ref[0, g], vb[rs, sl], preferred_element_type=F32) + bias_ref[0, :, sl]
            us_scr[rs, sl] = (u[rs, sl] * s).astype(BF16)

    ya = jnp.dot(_group_pick(op_ref, os_ref, n_p).astype(BF16), wa_ref[...], preferred_element_type=F32)
    yb = jnp.dot(us_scr[...], wb_ref[...], preferred_element_type=F32)
    mixed = _sigmoid(ma_ref[...].astype(F32)) * ya + _sigmoid(mb_ref[...].astype(F32)) * yb
    x1 = _group_pick(xp_ref, xs_ref, n_p) + jnp.dot(mixed.astype(BF16), wo_ref[...], preferred_element_type=F32)
    x1_ref[...] = x1
    h2 = _rms(x1, g2_ref[...])
    h2_ref[...] = _pack_bf16_pairs(h2)
    hh, hl = _split(h2)
    d = functools.partial(jnp.dot, preferred_element_type=F32)
    logits = d(hh, rwh_ref[...]) + (d(hh, rwl_ref[...]) + d(hl, rwh_ref[...])) + rb_ref[...]
    lane = lax.broadcasted_iota(jnp.int32, logits.shape, 1).astype(F32)
    vals, idxs = [], []
    for _ in range(TOP_K):
        m = jnp.max(logits, axis=-1, keepdims=True)
        i = jnp.min(jnp.where(logits == m, lane, float(LANES)), axis=-1, keepdims=True)
        vals.append(m)
        idxs.append(i)
        logits = jnp.where(lane == i, -jnp.inf, logits)
    es = [jnp.exp(v - vals[0]) for v in vals]
    tot = es[0]
    for e in es[1:]:
        tot = tot + e

    @pl.when(pl.program_id(0) == 0)
    def _():
        cnt_scr[...] = jnp.zeros(cnt_scr.shape, F32)

    onehot = (lane == idxs[0]).astype(F32)
    for k in range(1, TOP_K):
        onehot = onehot + (lane == idxs[k]).astype(F32)
    before = jnp.dot(tri_ref[...], onehot.astype(BF16), preferred_element_type=F32) + cnt_scr[...]
    cnt_scr[...] = cnt_scr[...] + jnp.sum(onehot, axis=0, keepdims=True)
    cnt_ref[...] = cnt_scr[...]

    gates = jnp.zeros(logits.shape, F32)
    route = jnp.zeros(logits.shape, F32)
    for k in range(TOP_K):
        rank = jnp.sum(jnp.where(lane == idxs[k], before, 0.0), axis=-1, keepdims=True)
        gates = jnp.where(lane == float(k), es[k] / tot, gates)
        route = jnp.where(lane == float(k), idxs[k], route)
        route = jnp.where(lane == float(TOP_K + k), rank, route)
    gate_ref[...] = gates
    idx_ref[...] = route.T[:2 * TOP_K].astype(jnp.int32)


def _post_mix(o_p, o_s, z, xp, xs, ln_g, ln_b, mix, bias, wa, wb, wo, g2, rwh, rwl, rb):
    n_p, n_s = xp.shape[0], xs.shape[0]
    n = n_p + n_s
    tm = _row_tile(n_p, n_s)
    pt = n_p // tm
    tri = jnp.tril(jnp.ones((tm, tm), BF16), k=-1)
    grp = lambda i: jnp.where(i < pt, 0, 1)
    once = pl.Buffered(1)
    tok = pl.BlockSpec((tm, D_MODEL), lambda i: (i, 0))
    zcol = lambda c: pl.BlockSpec((tm, D_MODEL), lambda i: (i, c))
    full = lambda r, c: pl.BlockSpec((r, c), lambda i: (0, 0), pipeline_mode=once)
    narrow = pl.BlockSpec((tm, LANES), lambda i: (i, 0))
    op_spec, os_spec = _group_specs(tm, V_W, n_p)
    xp_spec, xs_spec = _group_specs(tm, D_MODEL, n_p)
    return pl.pallas_call(
        functools.partial(_post_mix_kernel, n_p=n_p),
        grid=(n // tm,),
        in_specs=[
            op_spec, os_spec, zcol(4), zcol(5), zcol(6), zcol(7), xp_spec, xs_spec,
            full(1, GM_WIDTH), full(1, GM_WIDTH),
            pl.BlockSpec((1, GM_GROUPS, GM_CHUNK, GM_CHUNK), lambda i: (grp(i), 0, 0, 0)),
            pl.BlockSpec((1, GM_CHUNK, GM_WIDTH), lambda i: (grp(i), 0, 0)),
            full(V_W, D_MODEL), full(GM_WIDTH, D_MODEL), full(D_MODEL, D_MODEL), full(1, D_MODEL),
            full(D_MODEL, LANES), full(D_MODEL, LANES), full(1, LANES), full(tm, tm),
        ],
        out_specs=[tok, pl.BlockSpec((tm, D_MODEL // 2), lambda i: (i, 0)), narrow,
                   pl.BlockSpec((2 * TOP_K, tm), lambda i: (0, i)),
                   pl.BlockSpec((tm, GM_WIDTH), lambda i: (jnp.maximum(i - pt, 0), 0)),
                   pl.BlockSpec((1, LANES), lambda i: (0, 0))],
        out_shape=[
            jax.ShapeDtypeStruct((n, D_MODEL), F32),
            jax.ShapeDtypeStruct((n, D_MODEL // 2), jnp.int32),
            jax.ShapeDtypeStruct((n, LANES), F32),
            jax.ShapeDtypeStruct((2 * TOP_K, n), jnp.int32),
            jax.ShapeDtypeStruct((n_s, GM_WIDTH), F32),
            jax.ShapeDtypeStruct((1, LANES), F32),
        ],
        scratch_shapes=[pltpu.VMEM((tm, GM_WIDTH), BF16), pltpu.VMEM((1, LANES), F32)],
        compiler_params=_params("arbitrary"),
        name="chunk_mlp_post_mix_router",
    )(o_p, o_s, z, z, z, z, xp, xs, ln_g, ln_b, mix, bias, wa, wb, wo, g2, rwh, rwl, rb, tri)


def _moe_kernel(be_ref, first_ref, valid_ref, next_ref, slot_ref, x_ref, w1_hbm, b1_ref, w2_hbm, b2_ref, y_ref,
                w1_stage, w2_stage, sems):
    b = pl.program_id(0)
    slot = slot_ref[b]

    def weight_copies(e, s):
        return (pltpu.make_async_copy(w1_hbm.at[e], w1_stage.at[s], sems.at[0, s]),
                pltpu.make_async_copy(w2_hbm.at[e], w2_stage.at[s], sems.at[1, s]))

    @pl.when(b == 0)
    def _():
        for c in weight_copies(be_ref[0], slot):
            c.start()

    @pl.when(first_ref[b] == 1)
    def _():
        for c in weight_copies(be_ref[b], slot):
            c.wait()

        @pl.when(next_ref[b] >= 0)
        def _():
            for c in weight_copies(next_ref[b], 1 - slot):
                c.start()

    @pl.when(valid_ref[b] == 1)
    def _():
        x = jnp.concatenate(_unpack_bf16_pairs(x_ref[...]), axis=1).astype(BF16)
        hid = jnp.dot(x, w1_stage[slot].astype(BF16), preferred_element_type=F32) + b1_ref[0]
        gate = jnp.minimum(hid[:, :D_EXPERT], SWIGLU_LIMIT)
        up = jnp.clip(hid[:, D_EXPERT:], -SWIGLU_LIMIT, SWIGLU_LIMIT)
        act = gate * _sigmoid(SWIGLU_ALPHA * gate) * (up + 1.0)
        y = jnp.dot(act.astype(BF16), w2_stage[slot].astype(BF16), preferred_element_type=F32) + b2_ref[0]
        y_ref[...] = _pack_bf16_pairs(y)

    @pl.when(valid_ref[b] == 0)
    def _():
        y_ref[...] = jnp.zeros(y_ref.shape, jnp.int32)


def _moe_experts(block_e, first, valid, next_e, slot, xb, w1, b1, w2, b2):
    rows = xb.shape[0]
    nb = rows // MOE_ROWS
    smem4 = lambda f: (lambda b, be, fi, va, ne, sl: f(b, be))
    return pl.pallas_call(
        _moe_kernel,
        grid_spec=pltpu.PrefetchScalarGridSpec(
            num_scalar_prefetch=5,
            grid=(nb,),
            in_specs=[
                pl.BlockSpec((MOE_ROWS, D_MODEL // 2), smem4(lambda b, be: (b, 0))),
                pl.BlockSpec(memory_space=pl.ANY),
                pl.BlockSpec((1, 1, 2 * D_EXPERT), smem4(lambda b, be: (be[b], 0, 0))),
                pl.BlockSpec(memory_space=pl.ANY),
                pl.BlockSpec((1, 1, D_MODEL), smem4(lambda b, be: (be[b], 0, 0))),
            ],
            out_specs=pl.BlockSpec((MOE_ROWS, D_MODEL // 2), smem4(lambda b, be: (b, 0))),
            scratch_shapes=[
                pltpu.VMEM((2, D_MODEL, 2 * D_EXPERT), F32),
                pltpu.VMEM((2, D_EXPERT, D_MODEL), F32),
                pltpu.SemaphoreType.DMA((2, 2)),
            ],
        ),
        out_shape=jax.ShapeDtypeStruct((rows, D_MODEL // 2), jnp.int32),
        compiler_params=_params("arbitrary"),
        name="moe_experts",
    )(block_e, first, valid, next_e, slot, xb, w1, b1, w2, b2)


def _moe_dispatch(idx, rank, counts, n):
    experts = jnp.arange(N_EXPERTS, dtype=jnp.int32)
    padded = (counts + MOE_ROWS - 1) // MOE_ROWS * MOE_ROWS
    pad_end = jnp.cumsum(padded)
    pad_start = pad_end - padded
    start_of = jnp.sum(jnp.where(idx[None] == experts[:, None, None], pad_start[:, None, None], 0), axis=0)
    dest_t = start_of + rank
    nb = -(-n * TOP_K // MOE_ROWS) + N_EXPERTS
    rows = nb * MOE_ROWS
    starts = jnp.arange(nb, dtype=jnp.int32) * MOE_ROWS
    valid = (starts < pad_end[-1]).astype(jnp.int32)
    owner = lambda r: jnp.minimum(jnp.sum((pad_end[None, :] <= r[:, None]).astype(jnp.int32), axis=1),
                                  N_EXPERTS - 1)
    last_e = owner(pad_end[-1:] - 1)[0]
    block_e = jnp.where(valid == 1, owner(starts), last_e).astype(jnp.int32)
    first = jnp.concatenate([jnp.ones((1,), jnp.int32),
                             (block_e[1:] != block_e[:-1]).astype(jnp.int32)])
    blk = jnp.arange(nb, dtype=jnp.int32)
    later_first = (blk[None, :] > blk[:, None]) & (first[None, :] == 1)
    next_pos = jnp.min(jnp.where(later_first, blk[None, :], nb), axis=1)
    next_e = jnp.where(next_pos < nb, block_e[jnp.minimum(next_pos, nb - 1)], -1).astype(jnp.int32)
    slot = (jnp.cumsum(first) - 1) % 2
    return dest_t, rows, block_e, first, valid, next_e, slot.astype(jnp.int32)


def _sc_scatter_rows(src, dest_t, out_rows):
    n, d = src.shape
    nk = dest_t.shape[0]
    assert nk * SC_TOKENS == SC_WINDOW
    idx = dest_t.reshape(nk, n // SC_TOKENS, SC_TOKENS).transpose(1, 0, 2).reshape(n // SC_TOKENS, SC_WINDOW)
    mesh = plsc.VectorSubcoreMesh(core_axis_name="core", subcore_axis_name="subcore",
                                  num_cores=SC_CORES, num_subcores=SC_SUBCORES)

    @pl.kernel(out_type=jax.ShapeDtypeStruct((out_rows, d), src.dtype), mesh=mesh, scratch_types=[])
    def scatter_rows(src_hbm, di_hbm, out_hbm):
        def body(x_vmem, di_vmem):
            for k in range(nk):
                pltpu.sync_copy(x_vmem, out_hbm.at[di_vmem.at[0, pl.ds(k * SC_TOKENS, SC_TOKENS)]])

        pltpu.emit_pipeline(
            body,
            grid=(n // SC_TOKENS,),
            in_specs=[pl.BlockSpec((SC_TOKENS, d), lambda i: (i, 0)),
                      pl.BlockSpec((1, SC_WINDOW), lambda i: (i, 0))],
            out_specs=[],
            core_axis_name=("core", "subcore"),
            dimension_semantics=(pltpu.PARALLEL,),
        )(src_hbm, di_hbm)

    return scatter_rows(src, idx)


def _sc_gather_rows(src, idx):
    m = idx.shape[0]
    d = src.shape[1]
    idx_rows = jnp.pad(idx.reshape(m // SC_CHUNK, SC_CHUNK), ((0, 0), (0, SC_WINDOW - SC_CHUNK)))
    mesh = plsc.VectorSubcoreMesh(core_axis_name="core", subcore_axis_name="subcore",
                                  num_cores=SC_CORES, num_subcores=SC_SUBCORES)

    @pl.kernel(out_type=jax.ShapeDtypeStruct((m, d), src.dtype), mesh=mesh, scratch_types=[])
    def gather_rows(src_hbm, si_hbm, out_hbm):
        def body(si_vmem, o_vmem):
            pltpu.sync_copy(src_hbm.at[si_vmem.at[0, pl.ds(0, SC_CHUNK)]], o_vmem)

        pltpu.emit_pipeline(
            body,
            grid=(m // SC_CHUNK,),
            in_specs=[pl.BlockSpec((1, SC_WINDOW), lambda i: (i, 0))],
            out_specs=[pl.BlockSpec((SC_CHUNK, d), lambda i: (i, 0))],
            core_axis_name=("core", "subcore"),
            dimension_semantics=(pltpu.PARALLEL,),
        )(si_hbm, out_hbm)

    return gather_rows(src, idx_rows)


def _tail_kernel(x1_ref, yk_ref, gate_ref, pp_ref, ps_ref, g3_ref, wg_ref, wp_ref, gf_ref,
                 yp_ref, ys_ref, *, n_p):
    tm = x1_ref.shape[0]
    parts = [slice(p * tm // TAIL_PARTS, (p + 1) * tm // TAIL_PARTS) for p in range(TAIL_PARTS)]
    in_prompt = pl.program_id(0) < n_p // tm
    d = functools.partial(jnp.dot, preferred_element_type=F32)

    def combine(r):
        gates = gate_ref[r, :]
        lo, hi = _unpack_bf16_pairs(yk_ref[0, r, :])
        lo, hi = lo * gates[:, 0:1], hi * gates[:, 0:1]
        for k in range(1, TOP_K):
            lo_k, hi_k = _unpack_bf16_pairs(yk_ref[k, r, :])
            lo, hi = lo + lo_k * gates[:, k:k + 1], hi + hi_k * gates[:, k:k + 1]
        return x1_ref[r, :] + jnp.concatenate([lo, hi], axis=1)

    x2 = [combine(r) for r in parts]
    h3 = [_rms(x, g3_ref[...]) for x in x2]
    gate = [_sigmoid(d(h.astype(BF16), wg_ref[...])) for h in h3]
    pe = [d(jnp.where(in_prompt, pp_ref[r, :], ps_ref[r, :]).astype(BF16), wp_ref[...]) for r in parts]
    y = [_rms(x + g * p, gf_ref[...]) for x, g, p in zip(x2, gate, pe)]

    @pl.when(in_prompt)
    def _():
        for r, yy in zip(parts, y):
            yp_ref[r, :] = yy

    @pl.when(jnp.logical_not(in_prompt))
    def _():
        for r, yy in zip(parts, y):
            ys_ref[r, :] = yy


def _tail(x1, yk, gates, pp, ps, g3, wg, wp, gf):
    n_p, n_s = pp.shape[0], ps.shape[0]
    n = n_p + n_s
    tm = _row_tile(n_p, n_s)
    tok = pl.BlockSpec((tm, D_MODEL), lambda i: (i, 0))
    full = lambda r, c: pl.BlockSpec((r, c), lambda i: (0, 0))
    return pl.pallas_call(
        functools.partial(_tail_kernel, n_p=n_p),
        grid=(n // tm,),
        in_specs=[tok, pl.BlockSpec((TOP_K, tm, D_MODEL // 2), lambda i: (0, i, 0)),
                  pl.BlockSpec((tm, LANES), lambda i: (i, 0)),
                  *_group_specs(tm, PLE_DIM, n_p),
                  full(1, D_MODEL), full(D_MODEL, D_MODEL), full(PLE_DIM, D_MODEL), full(1, D_MODEL)],
        out_specs=_group_specs(tm, D_MODEL, n_p),
        out_shape=[jax.ShapeDtypeStruct((n_p, D_MODEL), F32), jax.ShapeDtypeStruct((n_s, D_MODEL), F32)],
        compiler_params=_params("arbitrary"),
        name="ple_final_norm",
    )(x1, yk, gates, pp, ps, g3, wg, wp, gf)


def _lane_pad(v, offset, fill=0.0):
    out = jnp.full((1, LANES), fill, F32)
    return out.at[0, offset:offset + v.shape[0]].set(v.astype(F32))


def kernel(x_prompt, x_sample, state_delta, state_conv, p_prompt, p_sample, norm1_g, w_in, conv_w, a_log, dt_bias, dn_norm_g, w_proj_a, gm_ln_g, gm_ln_b, gm_ws, gm_bs, w_proj_b, w_out, norm2_g, router_w, router_b, moe_w1, moe_b1, moe_w2, moe_b2, norm3_g, ple_w, ple_gate_w, final_norm_g):
    bp, lp, d = x_prompt.shape
    bs, ls, _ = x_sample.shape
    depth = w_in.shape[0]
    assert depth == 1 and d == D_MODEL
    assert lp % GM_CHUNK == 0 and GM_CHUNK % ls == 0 and ls >= DN_CONV - 1
    n_p, n_s = bp * lp, bs * ls
    n = n_p + n_s
    i = 0

    xp, xs = x_prompt.reshape(n_p, d), x_sample.reshape(n_s, d)

    ab0 = QKV_W
    w = w_in[i]
    wt = w.T
    w_main = jnp.concatenate([wt[:ab0], wt[ab0 + 2 * DN_HEADS:]], axis=0).astype(BF16)
    w_ab = jnp.pad(w[:, ab0:ab0 + 2 * DN_HEADS], ((0, 0), (0, LANES - 2 * DN_HEADS))).astype(BF16)
    row2 = lambda v: v.reshape(1, -1).astype(F32)

    z, ab = _in_proj(xp, xs, row2(norm1_g[i]), w_main, w_ab)

    alog_p = _lane_pad(a_log[i], 0)
    dtb_p = _lane_pad(dt_bias[i], 0)
    cw = conv_w[i].astype(F32)
    ong = row2(dn_norm_g[i])
    zero_s = jnp.zeros((bp, DN_HEADS, DN_DK, DN_DV), F32)
    zero_buf = jnp.zeros((bp, DN_CONV - 1, QKV_W), F32)
    o_p, sd_p, sc_p = _delta_branch(z, ab, zero_buf, zero_s, cw, alog_p, dtb_p, ong,
                                    tok0=0, L=lp, G=4, precise=False)
    o_s, sd_s, sc_s = _delta_branch(z, ab, state_conv[i], state_delta[i], cw, alog_p, dtb_p, ong,
                                    tok0=n_p, L=ls, G=8, precise=False)

    t = GM_CHUNK
    tri = jnp.tril(jnp.ones((t, t), bool))
    ws = gm_ws[i]
    mix_p = jnp.where(tri, ws, 0.0)
    small = jnp.where(tri[:ls, :ls], ws[:, :ls, :ls], 0.0)
    mix_s = jnp.einsum('ab,gts->gatbs', jnp.eye(t // ls, dtype=F32), small).reshape(GM_GROUPS, t, t)
    mix = jnp.stack([mix_p, mix_s]).astype(BF16)
    gw = GM_WIDTH // GM_GROUPS
    bias_p = jnp.repeat(gm_bs[i].T, gw, axis=1)
    bias_s = jnp.tile(bias_p[:ls], (t // ls, 1))
    bias = jnp.stack([bias_p, bias_s]).astype(F32)
    rw = jnp.pad(router_w[i].astype(F32), ((0, 0), (0, LANES - N_EXPERTS)))
    rwh = rw.astype(BF16)
    rwl = (rw - rwh.astype(F32)).astype(BF16)
    rb = _lane_pad(router_b[i], 0, fill=-jnp.inf)
    x1, h2, gates, route, v_s, counts = _post_mix(
        o_p, o_s, z, xp, xs, row2(gm_ln_g[i]), row2(gm_ln_b[i]), mix, bias, w_proj_a[i].astype(BF16),
        w_proj_b[i].astype(BF16), w_out[i].astype(BF16), row2(norm2_g[i]), rwh, rwl, rb)
    dest_t, rows, block_e, first, valid, next_e, slot = _moe_dispatch(
        route[:TOP_K], route[TOP_K:], counts[0, :N_EXPERTS].astype(jnp.int32), n)
    xb = _sc_scatter_rows(h2, dest_t, rows)
    yb = _moe_experts(block_e, first, valid, next_e, slot, xb, moe_w1[i], moe_b1[i][:, None, :],
                      moe_w2[i], moe_b2[i][:, None, :])
    yk = _sc_gather_rows(yb, dest_t.reshape(TOP_K * n)).reshape(TOP_K, n, d // 2)

    y_p, y_s = _tail(x1, yk, gates, p_prompt[i].reshape(n_p, PLE_DIM), p_sample[i].reshape(n_s, PLE_DIM),
                     row2(norm3_g[i]), ple_gate_w[i].astype(BF16), ple_w[i].astype(BF16), row2(final_norm_g))

    return (y_p.reshape(bp, lp, d), y_s.reshape(bs, ls, d),
            sd_p[None], sc_p[None], sd_s[None], sc_s[None], v_s.reshape(1, bs, ls, GM_WIDTH))
```

```python
import functools
import math

import jax
import jax.numpy as jnp
from jax import lax
from jax.experimental import pallas as pl
from jax.experimental.pallas import tpu as pltpu
from jax.experimental.pallas import tpu_sc as plsc

F32 = jnp.float32
BF16 = jnp.bfloat16

D_MODEL = 1024
DN_HEADS = 8
DN_DK = 128
DN_DV = 128
DN_CONV = 4
DN_CHUNK = 64
GM_WIDTH = 1024
GM_GROUPS = 8
GM_CHUNK = 128
N_EXPERTS = 32
TOP_K = 4
D_EXPERT = 1024
SWIGLU_LIMIT = 7.0
SWIGLU_ALPHA = 1.702
PLE_DIM = 256
EPS = 1e-6
QK_W = DN_HEADS * DN_DK
V_W = DN_HEADS * DN_DV
QKV_W = 2 * QK_W + V_W

LANES = 128
SUBLANES = 8
TAIL_PARTS = 4
MOE_ROWS = 512
VMEM_LIMIT = 56 << 20
SC_CORES, SC_SUBCORES = 2, 16
SC_WINDOW = 128
SC_TOKENS = SC_WINDOW // TOP_K
SC_CHUNK = 64


def _params(*sem):
    return pltpu.CompilerParams(dimension_semantics=sem, vmem_limit_bytes=VMEM_LIMIT)


def _dot(a, b):
    return jnp.dot(a.astype(BF16), b.astype(BF16), preferred_element_type=F32)


def _split(a):
    hi = a.astype(BF16)
    return hi, (a - hi.astype(F32)).astype(BF16)


def _dot3(a, b):
    ah, al = _split(a)
    bh, bl = _split(b)
    d = functools.partial(jnp.dot, preferred_element_type=F32)
    return d(ah, bh) + (d(ah, bl) + d(al, bh))


def _dot_nt(a, b):
    return lax.dot_general(a.astype(BF16), b.astype(BF16), (((1,), (1,)), ((), ())),
                           preferred_element_type=F32)


def _dot_tn(a, b):
    return lax.dot_general(a.astype(BF16), b.astype(BF16), (((0,), (0,)), ((), ())),
                           preferred_element_type=F32)


def _sigmoid(x):
    return 0.5 * jnp.tanh(0.5 * x) + 0.5


def _pack_bf16_pairs(x):
    bits = lax.bitcast_convert_type(x.astype(BF16).astype(F32), jnp.uint32)
    half = x.shape[1] // 2
    return lax.bitcast_convert_type((bits[:, :half] >> 16) | bits[:, half:], jnp.int32)


def _unpack_bf16_pairs(w):
    bits = lax.bitcast_convert_type(w, jnp.uint32)
    return (lax.bitcast_convert_type(bits << 16, F32),
            lax.bitcast_convert_type(bits & jnp.uint32(0xFFFF0000), F32))


def _rms(x, g):
    return x * lax.rsqrt(jnp.mean(x * x, axis=-1, keepdims=True) + EPS) * g


def _gelu(x):
    return 0.5 * x * (1.0 + lax.erf(x * (1.0 / math.sqrt(2.0))))


def _row_tile(n_p, n_s, cap=512):
    for t in (1024, 512, 256, 128):
        if t > cap:
            continue
        if n_p % t == 0 and n_s % t == 0:
            return t
    raise ValueError(f"token counts {n_p}, {n_s} must be multiples of 128")


def _group_specs(tm, width, n_p):
    pt = n_p // tm
    return [pl.BlockSpec((tm, width), lambda i, *_: (jnp.minimum(i, pt - 1), 0)),
            pl.BlockSpec((tm, width), lambda i, *_: (jnp.maximum(i - pt, 0), 0))]


def _group_pick(prompt_ref, sample_ref, n_p):
    pt = n_p // prompt_ref.shape[0]
    return jnp.where(pl.program_id(0) < pt, prompt_ref[...], sample_ref[...])


def _in_proj_kernel(xp_ref, xs_ref, g_ref, w_ref, wab_ref, z_ref, ab_ref, h_scr, *, n_p):
    @pl.when(pl.program_id(1) == 0)
    def _():
        hb = _rms(_group_pick(xp_ref, xs_ref, n_p), g_ref[...]).astype(BF16)
        h_scr[...] = hb
        ab_ref[...] = jnp.dot(hb, wab_ref[...], preferred_element_type=F32)

    z_ref[...] = jnp.dot(h_scr[...], w_ref[...], preferred_element_type=F32).astype(z_ref.dtype)


def _in_proj(xp, xs, g, w_main, w_ab):
    n_p, n_s = xp.shape[0], xs.shape[0]
    n = n_p + n_s
    tm, tn = _row_tile(n_p, n_s, cap=1024), 2048
    cols = w_main.shape[1]
    return pl.pallas_call(
        functools.partial(_in_proj_kernel, n_p=n_p),
        grid=(n // tm, cols // tn),
        in_specs=_group_specs(tm, D_MODEL, n_p) + [
            pl.BlockSpec((1, D_MODEL), lambda i, j: (0, 0)),
            pl.BlockSpec((D_MODEL, tn), lambda i, j: (0, j)),
            pl.BlockSpec((D_MODEL, LANES), lambda i, j: (0, 0)),
        ],
        out_specs=[
            pl.BlockSpec((tm, tn), lambda i, j: (i, j)),
            pl.BlockSpec((tm, LANES), lambda i, j: (i, 0)),
        ],
        out_shape=[jax.ShapeDtypeStruct((n, cols), BF16), jax.ShapeDtypeStruct((n, LANES), F32)],
        scratch_shapes=[pltpu.VMEM((tm, D_MODEL), BF16)],
        compiler_params=_params("parallel", "arbitrary"),
        name="in_proj",
    )(xp, xs, g, w_main, w_ab)


def _delta_kernel(*refs, C, G, carry):
    (qkv_ref, zg_ref, ab_ref, buf_ref, s0_ref, cw_ref, alog_ref, dtb_ref, ong_ref) = refs[:9]
    refs = refs[9:]
    o_ref, snew_ref, bufnew_ref, xc_scr = refs[:4]
    H, DK, DV = DN_HEADS, DN_DK, DN_DV
    T = G * C
    halo = DN_CONV - 1
    base = SUBLANES
    cw = cw_ref[...]

    def conv(window):
        y = window(0) * cw[0:1]
        for i in range(1, DN_CONV):
            y = y + window(i) * cw[i:i + 1]
        return y

    if carry:
        s_scr = refs[4]
        c = pl.program_id(1)
        last = pl.num_programs(1) - 1

        @pl.when(c == 0)
        def _():
            xc_scr[base - halo:base, :] = buf_ref[0]
            s_scr[...] = s0_ref[0]

        xb = qkv_ref[...]
        xf = xb.astype(F32)
        xc_scr[base:base + SUBLANES, :] = xf[0:SUBLANES]
        y_head = conv(lambda i: xc_scr[base - halo + i:base - halo + i + SUBLANES, :])
        ri = lax.broadcasted_iota(jnp.int32, (halo * T, T), 0)
        ci = lax.broadcasted_iota(jnp.int32, (halo * T, T), 1)
        src_row = (ri & (T - 1)) - (halo - (ri >> (T.bit_length() - 1)))
        shifted = jnp.dot(jnp.where(ci == src_row, 1.0, 0.0).astype(BF16), xb, preferred_element_type=F32)
        y = conv(lambda i: shifted[i * T:(i + 1) * T] if i < halo else xf)
        y = jnp.concatenate([y_head, y[SUBLANES:]], axis=0)
        tail = xf[T - halo:T]
        xc_scr[base - halo:base, :] = tail

        @pl.when(c == last)
        def _():
            bufnew_ref[0] = tail
    else:
        ys = []
        x_new = qkv_ref[...].astype(F32)
        for g in range(G):
            xc_scr[g, base - halo:base, :] = buf_ref[g]
            xc_scr[g, base:base + C, :] = x_new[g * C:(g + 1) * C, :]
            ys.append(conv(lambda i: xc_scr[g, base - halo + i:base - halo + i + C, :]))
            bufnew_ref[g] = xc_scr[g, base + C - halo:base + C, :]
        y = jnp.concatenate(ys, axis=0) if G > 1 else ys[0]
    qkv = y * _sigmoid(y)

    ab = ab_ref[...]
    g_all = -jnp.exp(alog_ref[...]) * jax.nn.softplus(ab + dtb_ref[...])
    beta_all = _sigmoid(ab)
    shift = C.bit_length() - 1
    rt = lax.broadcasted_iota(jnp.int32, (T, T), 0)
    ct = lax.broadcasted_iota(jnp.int32, (T, T), 1)
    chunk_tril = ((rt >> shift) == (ct >> shift)) & (rt >= ct)
    gcum = _dot3(chunk_tril.astype(F32), g_all)
    gam_all = jnp.exp(gcum)
    wide = C >= LANES // 2
    if wide:
        gcum_t, beta_t = gcum.T, beta_all.T

    row = lax.broadcasted_iota(jnp.int32, (C, C), 0)
    col = lax.broadcasted_iota(jnp.int32, (C, C), 1)
    incl, strict, eye = row >= col, row > col, row == col
    eyef = eye.astype(F32)
    units = [(g, h) for g in range(G) for h in range(H)]
    rows = lambda a, g: a[g * C:(g + 1) * C]

    qn, kn, vv = [], [], []
    for h in range(H):
        q = qkv[:, h * DK:(h + 1) * DK]
        k = qkv[:, QK_W + h * DK:QK_W + (h + 1) * DK]
        qn.append(q * (lax.rsqrt(jnp.sum(q * q, axis=-1, keepdims=True) + EPS) * (DK ** -0.5)))
        kn.append(k * lax.rsqrt(jnp.sum(k * k, axis=-1, keepdims=True) + EPS))
        vv.append(qkv[:, 2 * QK_W + h * DV:2 * QK_W + (h + 1) * DV])
    qb = [q.astype(BF16) for q in qn]
    kb = [k.astype(BF16) for k in kn]

    gc, bc, gl, a_low, m_intra = {}, {}, {}, {}, {}
    for u in units:
        g, h = u
        gc[u] = rows(gcum, g)[:, h:h + 1]
        bc[u] = rows(beta_all, g)[:, H + h:H + h + 1]
        gl[u] = gc[u][C - 1:C, :]
        if wide:
            gr = gcum_t[h:h + 1, g * C:(g + 1) * C]
            br = beta_t[H + h:H + h + 1, g * C:(g + 1) * C]
        else:
            gr = jnp.sum(jnp.where(eye, gc[u], 0.0), axis=0, keepdims=True)
            br = jnp.sum(jnp.where(eye, bc[u], 0.0), axis=0, keepdims=True)
        db = jnp.where(incl, jnp.exp(gc[u] - gr), 0.0) * br
        k = rows(kb[h], g)
        kq = _dot_nt(jnp.concatenate([k, rows(qb[h], g)], axis=0), k)
        a_low[u] = jnp.where(strict, kq[:C] * db, 0.0)
        m_intra[u] = kq[C:] * db

    t_inv = {u: eyef - a_low[u] for u in units}
    a_pow = {u: _dot(a_low[u], a_low[u]) for u in units}
    n = 2
    while n < C:
        t_inv = {u: t_inv[u] + _dot(t_inv[u], a_pow[u]) for u in units}
        n *= 2
        if n < C:
            a_pow = {u: _dot(a_pow[u], a_pow[u]) for u in units}

    u_base, wq, k_dec = {}, {}, {}
    for u in units:
        g, h = u
        gam = rows(gam_all, g)[:, h:h + 1]
        k = rows(kn[h], g)
        sol = _dot(t_inv[u], jnp.concatenate([rows(vv[h], g), gam * k], axis=1))
        u_base[u] = sol[:, :DV]
        wq[u] = jnp.concatenate([sol[:, DV:], gam * rows(qn[h], g)], axis=0)
        k_dec[u] = k * (bc[u] * jnp.exp(gl[u] - gc[u]))

    state = [s_scr[h] for h in range(H)] if carry else None
    outs = {}
    for g in range(G):
        s_in = state if carry else [s0_ref[g, h] for h in range(H)]
        wqs = [_dot(wq[(g, h)], s_in[h]) for h in range(H)]
        us = [u_base[(g, h)] - wqs[h][:C] for h in range(H)]
        for h in range(H):
            outs[(g, h)] = wqs[h][C:] + _dot(m_intra[(g, h)], us[h])
        s_out = [jnp.exp(gl[(g, h)]) * s_in[h] + _dot_tn(k_dec[(g, h)], us[h]) for h in range(H)]
        if carry:
            state = s_out
        else:
            for h in range(H):
                snew_ref[g, h] = s_out[h]

    for h in range(H):
        o = jnp.concatenate([outs[(g, h)] for g in range(G)], axis=0) if G > 1 else outs[(0, h)]
        zg = zg_ref[:, h * DV:(h + 1) * DV].astype(F32)
        o_ref[:, h * DV:(h + 1) * DV] = (_rms(o, ong_ref[...]) * (zg * _sigmoid(zg))).astype(o_ref.dtype)

    if carry:
        for h in range(H):
            s_scr[h] = state[h]

        @pl.when(c == last)
        def _():
            for h in range(H):
                snew_ref[0, h] = state[h]


def _delta_branch(z, ab, conv_buf, s0, conv_w, alog_p, dtb_p, onorm_g, *, tok0, L, G):
    B = conv_buf.shape[0]
    C = math.gcd(L, DN_CHUNK)
    nc = L // C
    carry = nc > 1
    T = G * C
    blk0 = tok0 // T
    if carry:
        assert nc % G == 0
        grid = (B, nc // G)
        own_blk = lambda b, c: b * (nc // G) + c
        gs = 1
    else:
        assert B % G == 0
        grid = (B // G, 1)
        own_blk = lambda b, c: b
        gs = G
    tok = lambda b, c: (blk0 + own_blk(b, c), 0)
    seq3 = lambda b, c: (b, 0, 0)
    seq4 = lambda b, c: (b, 0, 0, 0)
    const = lambda b, c: (0, 0)
    if carry:
        scratch = [pltpu.VMEM((2 * SUBLANES, QKV_W), F32), pltpu.VMEM((DN_HEADS, DN_DK, DN_DV), F32)]
    else:
        scratch = [pltpu.VMEM((G, SUBLANES + C, QKV_W), F32)]
    return pl.pallas_call(
        functools.partial(_delta_kernel, C=C, G=G, carry=carry),
        grid=grid,
        in_specs=[
            pl.BlockSpec((T, QKV_W), tok),
            pl.BlockSpec((T, V_W), lambda b, c: (blk0 + own_blk(b, c), QKV_W // V_W)),
            pl.BlockSpec((T, LANES), tok),
            pl.BlockSpec((gs, DN_CONV - 1, QKV_W), seq3),
            pl.BlockSpec((gs, DN_HEADS, DN_DK, DN_DV), seq4),
            pl.BlockSpec((DN_CONV, QKV_W), const),
            pl.BlockSpec((1, LANES), const),
            pl.BlockSpec((1, LANES), const),
            pl.BlockSpec((1, DN_DV), const),
        ],
        out_specs=[
            pl.BlockSpec((T, V_W), lambda b, c: (own_blk(b, c), 0)),
            pl.BlockSpec((gs, DN_HEADS, DN_DK, DN_DV), seq4),
            pl.BlockSpec((gs, DN_CONV - 1, QKV_W), seq3),
        ],
        out_shape=[
            jax.ShapeDtypeStruct((B * L, V_W), BF16),
            jax.ShapeDtypeStruct(s0.shape, F32),
            jax.ShapeDtypeStruct(conv_buf.shape, F32),
        ],
        scratch_shapes=scratch,
        compiler_params=_params("parallel", "arbitrary"),
        name=f"delta_rule_c{C}",
    )(z, z, ab, conv_buf, s0, conv_w, alog_p, dtb_p, onorm_g)


def _post_mix_kernel(op_ref, os_ref, gu_ref, gv_ref, ma_ref, mb_ref, xp_ref, xs_ref, lng_ref, lnb_ref,
                     mix_ref, bias_ref, wa_ref, wb_ref, wo_ref, g2_ref, rwh_ref, rwl_ref, rb_ref, tri_ref,
                     x1_ref, h2_ref, gate_ref, idx_ref, v_ref, cnt_ref, us_scr, cnt_scr, *, n_p):
    u = _gelu(gu_ref[...].astype(F32))
    a = _gelu(gv_ref[...].astype(F32))
    ac = a - jnp.mean(a, axis=-1, keepdims=True)
    v = ac * lax.rsqrt(jnp.mean(ac * ac, axis=-1, keepdims=True) + EPS) * lng_ref[...] + lnb_ref[...]
    v_ref[...] = v
    vb = v.astype(BF16)
    gw = GM_WIDTH // GM_GROUPS
    for c in range(u.shape[0] // GM_CHUNK):
        rs = slice(c * GM_CHUNK, (c + 1) * GM_CHUNK)
        for g in range(GM_GROUPS):
            sl = slice(g * gw, (g + 1) * gw)
            s = jnp.dot(mix_ref[0, g], vb[rs, sl], preferred_element_type=F32) + bias_ref[0, :, sl]
            us_scr[rs, sl] = (u[rs, sl] * s).astype(BF16)

    ya = jnp.dot(_group_pick(op_ref, os_ref, n_p).astype(BF16), wa_ref[...], preferred_element_type=F32)
    yb = jnp.dot(us_scr[...], wb_ref[...], preferred_element_type=F32)
    mixed = _sigmoid(ma_ref[...].astype(F32)) * ya + _sigmoid(mb_ref[...].astype(F32)) * yb
    x1 = _group_pick(xp_ref, xs_ref, n_p) + jnp.dot(mixed.astype(BF16), wo_ref[...], preferred_element_type=F32)
    x1_ref[...] = x1
    h2 = _rms(x1, g2_ref[...])
    h2_ref[...] = _pack_bf16_pairs(h2)
    hh, hl = _split(h2)
    d = functools.partial(jnp.dot, preferred_element_type=F32)
    logits = d(hh, rwh_ref[...]) + (d(hh, rwl_ref[...]) + d(hl, rwh_ref[...])) + rb_ref[...]
    lane = lax.broadcasted_iota(jnp.int32, logits.shape, 1).astype(F32)
    vals, idxs = [], []
    for _ in range(TOP_K):
        m = jnp.max(logits, axis=-1, keepdims=True)
        i = jnp.min(jnp.where(logits == m, lane, float(LANES)), axis=-1, keepdims=True)
        vals.append(m)
        idxs.append(i)
        logits = jnp.where(lane == i, -jnp.inf, logits)
    es = [jnp.exp(v - vals[0]) for v in vals]
    tot = es[0]
    for e in es[1:]:
        tot = tot + e

    @pl.when(pl.program_id(0) == 0)
    def _():
        cnt_scr[...] = jnp.zeros(cnt_scr.shape, F32)

    onehot = (lane == idxs[0]).astype(F32)
    for k in range(1, TOP_K):
        onehot = onehot + (lane == idxs[k]).astype(F32)
    before = jnp.dot(tri_ref[...], onehot.astype(BF16), preferred_element_type=F32) + cnt_scr[...]
    cnt_scr[...] = cnt_scr[...] + jnp.sum(onehot, axis=0, keepdims=True)
    cnt_ref[...] = cnt_scr[...]

    gates = jnp.zeros(logits.shape, F32)
    route = jnp.zeros(logits.shape, F32)
    for k in range(TOP_K):
        rank = jnp.sum(jnp.where(lane == idxs[k], before, 0.0), axis=-1, keepdims=True)
        gates = jnp.where(lane == float(k), es[k] / tot, gates)
        route = jnp.where(lane == float(k), idxs[k], route)
        route = jnp.where(lane == float(TOP_K + k), rank, route)
    gate_ref[...] = gates
    idx_ref[...] = route.T[:2 * TOP_K].astype(jnp.int32)


def _post_mix(o_p, o_s, z, xp, xs, ln_g, ln_b, mix, bias, wa, wb, wo, g2, rwh, rwl, rb):
    n_p, n_s = xp.shape[0], xs.shape[0]
    n = n_p + n_s
    tm = _row_tile(n_p, n_s)
    pt = n_p // tm
    tri = jnp.tril(jnp.ones((tm, tm), BF16), k=-1)
    grp = lambda i: jnp.where(i < pt, 0, 1)
    once = pl.Buffered(1)
    tok = pl.BlockSpec((tm, D_MODEL), lambda i: (i, 0))
    zcol = lambda c: pl.BlockSpec((tm, D_MODEL), lambda i: (i, c))
    full = lambda r, c: pl.BlockSpec((r, c), lambda i: (0, 0), pipeline_mode=once)
    narrow = pl.BlockSpec((tm, LANES), lambda i: (i, 0))
    op_spec, os_spec = _group_specs(tm, V_W, n_p)
    xp_spec, xs_spec = _group_specs(tm, D_MODEL, n_p)
    return pl.pallas_call(
        functools.partial(_post_mix_kernel, n_p=n_p),
        grid=(n // tm,),
        in_specs=[
            op_spec, os_spec, zcol(4), zcol(5), zcol(6), zcol(7), xp_spec, xs_spec,
            full(1, GM_WIDTH), full(1, GM_WIDTH),
            pl.BlockSpec((1, GM_GROUPS, GM_CHUNK, GM_CHUNK), lambda i: (grp(i), 0, 0, 0)),
            pl.BlockSpec((1, GM_CHUNK, GM_WIDTH), lambda i: (grp(i), 0, 0)),
            full(V_W, D_MODEL), full(GM_WIDTH, D_MODEL), full(D_MODEL, D_MODEL), full(1, D_MODEL),
            full(D_MODEL, LANES), full(D_MODEL, LANES), full(1, LANES), full(tm, tm),
        ],
        out_specs=[tok, pl.BlockSpec((tm, D_MODEL // 2), lambda i: (i, 0)), narrow,
                   pl.BlockSpec((2 * TOP_K, tm), lambda i: (0, i)),
                   pl.BlockSpec((tm, GM_WIDTH), lambda i: (jnp.maximum(i - pt, 0), 0)),
                   pl.BlockSpec((1, LANES), lambda i: (0, 0))],
        out_shape=[
            jax.ShapeDtypeStruct((n, D_MODEL), F32),
            jax.ShapeDtypeStruct((n, D_MODEL // 2), jnp.int32),
            jax.ShapeDtypeStruct((n, LANES), F32),
            jax.ShapeDtypeStruct((2 * TOP_K, n), jnp.int32),
            jax.ShapeDtypeStruct((n_s, GM_WIDTH), F32),
            jax.ShapeDtypeStruct((1, LANES), F32),
        ],
        scratch_shapes=[pltpu.VMEM((tm, GM_WIDTH), BF16), pltpu.VMEM((1, LANES), F32)],
        compiler_params=_params("arbitrary"),
        name="chunk_mlp_post_mix_router",
    )(o_p, o_s, z, z, z, z, xp, xs, ln_g, ln_b, mix, bias, wa, wb, wo, g2, rwh, rwl, rb, tri)


def _moe_kernel(be_ref, first_ref, valid_ref, next_ref, slot_ref, x_ref, w1_hbm, b1_ref, w2_hbm, b2_ref, y_ref,
                w1_stage, w2_stage, sems):
    b = pl.program_id(0)
    slot = slot_ref[b]

    def weight_copies(e, s):
        return (pltpu.make_async_copy(w1_hbm.at[e], w1_stage.at[s], sems.at[0, s]),
                pltpu.make_async_copy(w2_hbm.at[e], w2_stage.at[s], sems.at[1, s]))

    @pl.when(b == 0)
    def _():
        for c in weight_copies(be_ref[0], slot):
            c.start()

    @pl.when(first_ref[b] == 1)
    def _():
        for c in weight_copies(be_ref[b], slot):
            c.wait()

        @pl.when(next_ref[b] >= 0)
        def _():
            for c in weight_copies(next_ref[b], 1 - slot):
                c.start()

    @pl.when(valid_ref[b] == 1)
    def _():
        x = jnp.concatenate(_unpack_bf16_pairs(x_ref[...]), axis=1).astype(BF16)
        hid = jnp.dot(x, w1_stage[slot].astype(BF16), preferred_element_type=F32) + b1_ref[0]
        gate = jnp.minimum(hid[:, :D_EXPERT], SWIGLU_LIMIT)
        up = jnp.clip(hid[:, D_EXPERT:], -SWIGLU_LIMIT, SWIGLU_LIMIT)
        act = gate * _sigmoid(SWIGLU_ALPHA * gate) * (up + 1.0)
        y = jnp.dot(act.astype(BF16), w2_stage[slot].astype(BF16), preferred_element_type=F32) + b2_ref[0]
        y_ref[...] = _pack_bf16_pairs(y)

    @pl.when(valid_ref[b] == 0)
    def _():
        y_ref[...] = jnp.zeros(y_ref.shape, jnp.int32)


def _moe_experts(block_e, first, valid, next_e, slot, xb, w1, b1, w2, b2):
    rows = xb.shape[0]
    nb = rows // MOE_ROWS
    smem4 = lambda f: (lambda b, be, fi, va, ne, sl: f(b, be))
    return pl.pallas_call(
        _moe_kernel,
        grid_spec=pltpu.PrefetchScalarGridSpec(
            num_scalar_prefetch=5,
            grid=(nb,),
            in_specs=[
                pl.BlockSpec((MOE_ROWS, D_MODEL // 2), smem4(lambda b, be: (b, 0))),
                pl.BlockSpec(memory_space=pl.ANY),
                pl.BlockSpec((1, 1, 2 * D_EXPERT), smem4(lambda b, be: (be[b], 0, 0))),
                pl.BlockSpec(memory_space=pl.ANY),
                pl.BlockSpec((1, 1, D_MODEL), smem4(lambda b, be: (be[b], 0, 0))),
            ],
            out_specs=pl.BlockSpec((MOE_ROWS, D_MODEL // 2), smem4(lambda b, be: (b, 0))),
            scratch_shapes=[
                pltpu.VMEM((2, D_MODEL, 2 * D_EXPERT), F32),
                pltpu.VMEM((2, D_EXPERT, D_MODEL), F32),
                pltpu.SemaphoreType.DMA((2, 2)),
            ],
        ),
        out_shape=jax.ShapeDtypeStruct((rows, D_MODEL // 2), jnp.int32),
        compiler_params=_params("arbitrary"),
        name="moe_experts",
    )(block_e, first, valid, next_e, slot, xb, w1, b1, w2, b2)


def _moe_dispatch(idx, rank, counts, n):
    experts = jnp.arange(N_EXPERTS, dtype=jnp.int32)
    padded = (counts + MOE_ROWS - 1) // MOE_ROWS * MOE_ROWS
    pad_end = jnp.cumsum(padded)
    pad_start = pad_end - padded
    start_of = jnp.sum(jnp.where(idx[None] == experts[:, None, None], pad_start[:, None, None], 0), axis=0)
    dest_t = start_of + rank
    nb = -(-n * TOP_K // MOE_ROWS) + N_EXPERTS
    rows = nb * MOE_ROWS
    starts = jnp.arange(nb, dtype=jnp.int32) * MOE_ROWS
    valid = (starts < pad_end[-1]).astype(jnp.int32)
    owner = lambda r: jnp.minimum(jnp.sum((pad_end[None, :] <= r[:, None]).astype(jnp.int32), axis=1),
                                  N_EXPERTS - 1)
    last_e = owner(pad_end[-1:] - 1)[0]
    block_e = jnp.where(valid == 1, owner(starts), last_e).astype(jnp.int32)
    first = jnp.concatenate([jnp.ones((1,), jnp.int32),
                             (block_e[1:] != block_e[:-1]).astype(jnp.int32)])
    blk = jnp.arange(nb, dtype=jnp.int32)
    later_first = (blk[None, :] > blk[:, None]) & (first[None, :] == 1)
    next_pos = jnp.min(jnp.where(later_first, blk[None, :], nb), axis=1)
    next_e = jnp.where(next_pos < nb, block_e[jnp.minimum(next_pos, nb - 1)], -1).astype(jnp.int32)
    slot = (jnp.cumsum(first) - 1) % 2
    return dest_t, rows, block_e, first, valid, next_e, slot.astype(jnp.int32)


def _sc_scatter_rows(src, dest_t, out_rows):
    n, d = src.shape
    nk = dest_t.shape[0]
    assert nk * SC_TOKENS == SC_WINDOW
    idx = dest_t.reshape(nk, n // SC_TOKENS, SC_TOKENS).transpose(1, 0, 2).reshape(n // SC_TOKENS, SC_WINDOW)
    mesh = plsc.VectorSubcoreMesh(core_axis_name="core", subcore_axis_name="subcore",
                                  num_cores=SC_CORES, num_subcores=SC_SUBCORES)

    @pl.kernel(out_type=jax.ShapeDtypeStruct((out_rows, d), src.dtype), mesh=mesh, scratch_types=[])
    def scatter_rows(src_hbm, di_hbm, out_hbm):
        def body(x_vmem, di_vmem):
            for k in range(nk):
                pltpu.sync_copy(x_vmem, out_hbm.at[di_vmem.at[0, pl.ds(k * SC_TOKENS, SC_TOKENS)]])

        pltpu.emit_pipeline(
            body,
            grid=(n // SC_TOKENS,),
            in_specs=[pl.BlockSpec((SC_TOKENS, d), lambda i: (i, 0)),
                      pl.BlockSpec((1, SC_WINDOW), lambda i: (i, 0))],
            out_specs=[],
            core_axis_name=("core", "subcore"),
            dimension_semantics=(pltpu.PARALLEL,),
        )(src_hbm, di_hbm)

    return scatter_rows(src, idx)


def _sc_gather_rows(src, idx):
    m = idx.shape[0]
    d = src.shape[1]
    idx_rows = jnp.pad(idx.reshape(m // SC_CHUNK, SC_CHUNK), ((0, 0), (0, SC_WINDOW - SC_CHUNK)))
    mesh = plsc.VectorSubcoreMesh(core_axis_name="core", subcore_axis_name="subcore",
                                  num_cores=SC_CORES, num_subcores=SC_SUBCORES)

    @pl.kernel(out_type=jax.ShapeDtypeStruct((m, d), src.dtype), mesh=mesh, scratch_types=[])
    def gather_rows(src_hbm, si_hbm, out_hbm):
        def body(si_vmem, o_vmem):
            pltpu.sync_copy(src_hbm.at[si_vmem.at[0, pl.ds(0, SC_CHUNK)]], o_vmem)

        pltpu.emit_pipeline(
            body,
            grid=(m // SC_CHUNK,),
            in_specs=[pl.BlockSpec((1, SC_WINDOW), lambda i: (i, 0))],
            out_specs=[pl.BlockSpec((SC_CHUNK, d), lambda i: (i, 0))],
            core_axis_name=("core", "subcore"),
            dimension_semantics=(pltpu.PARALLEL,),
        )(si_hbm, out_hbm)

    return gather_rows(src, idx_rows)


def _tail_kernel(x1_ref, yk_ref, gate_ref, pp_ref, ps_ref, g3_ref, wg_ref, wp_ref, gf_ref,
                 yp_ref, ys_ref, *, n_p):
    tm = x1_ref.shape[0]
    parts = [slice(p * tm // TAIL_PARTS, (p + 1) * tm // TAIL_PARTS) for p in range(TAIL_PARTS)]
    in_prompt = pl.program_id(0) < n_p // tm
    d = functools.partial(jnp.dot, preferred_element_type=F32)

    def combine(r):
        gates = gate_ref[r, :]
        lo, hi = _unpack_bf16_pairs(yk_ref[0, r, :])
        lo, hi = lo * gates[:, 0:1], hi * gates[:, 0:1]
        for k in range(1, TOP_K):
            lo_k, hi_k = _unpack_bf16_pairs(yk_ref[k, r, :])
            lo, hi = lo + lo_k * gates[:, k:k + 1], hi + hi_k * gates[:, k:k + 1]
        return x1_ref[r, :] + jnp.concatenate([lo, hi], axis=1)

    x2 = [combine(r) for r in parts]
    h3 = [_rms(x, g3_ref[...]) for x in x2]
    gate = [_sigmoid(d(h.astype(BF16), wg_ref[...])) for h in h3]
    pe = [d(jnp.where(in_prompt, pp_ref[r, :], ps_ref[r, :]).astype(BF16), wp_ref[...]) for r in parts]
    y = [_rms(x + g * p, gf_ref[...]) for x, g, p in zip(x2, gate, pe)]

    @pl.when(in_prompt)
    def _():
        for r, yy in zip(parts, y):
            yp_ref[r, :] = yy

    @pl.when(jnp.logical_not(in_prompt))
    def _():
        for r, yy in zip(parts, y):
            ys_ref[r, :] = yy


def _tail(x1, yk, gates, pp, ps, g3, wg, wp, gf):
    n_p, n_s = pp.shape[0], ps.shape[0]
    n = n_p + n_s
    tm = _row_tile(n_p, n_s)
    tok = pl.BlockSpec((tm, D_MODEL), lambda i: (i, 0))
    full = lambda r, c: pl.BlockSpec((r, c), lambda i: (0, 0))
    return pl.pallas_call(
        functools.partial(_tail_kernel, n_p=n_p),
        grid=(n // tm,),
        in_specs=[tok, pl.BlockSpec((TOP_K, tm, D_MODEL // 2), lambda i: (0, i, 0)),
                  pl.BlockSpec((tm, LANES), lambda i: (i, 0)),
                  *_group_specs(tm, PLE_DIM, n_p),
                  full(1, D_MODEL), full(D_MODEL, D_MODEL), full(PLE_DIM, D_MODEL), full(1, D_MODEL)],
        out_specs=_group_specs(tm, D_MODEL, n_p),
        out_shape=[jax.ShapeDtypeStruct((n_p, D_MODEL), F32), jax.ShapeDtypeStruct((n_s, D_MODEL), F32)],
        compiler_params=_params("arbitrary"),
        name="ple_final_norm",
    )(x1, yk, gates, pp, ps, g3, wg, wp, gf)


def _lane_pad(v, offset, fill=0.0):
    out = jnp.full((1, LANES), fill, F32)
    return out.at[0, offset:offset + v.shape[0]].set(v.astype(F32))


def kernel(x_prompt, x_sample, state_delta, state_conv, p_prompt, p_sample, norm1_g, w_in, conv_w, a_log, dt_bias, dn_norm_g, w_proj_a, gm_ln_g, gm_ln_b, gm_ws, gm_bs, w_proj_b, w_out, norm2_g, router_w, router_b, moe_w1, moe_b1, moe_w2, moe_b2, norm3_g, ple_w, ple_gate_w, final_norm_g):
    bp, lp, d = x_prompt.shape
    bs, ls, _ = x_sample.shape
    depth = w_in.shape[0]
    assert depth == 1 and d == D_MODEL
    assert lp % GM_CHUNK == 0 and GM_CHUNK % ls == 0 and ls >= DN_CONV - 1
    n_p, n_s = bp * lp, bs * ls
    n = n_p + n_s
    i = 0

    xp, xs = x_prompt.reshape(n_p, d), x_sample.reshape(n_s, d)

    ab0 = QKV_W
    w = w_in[i]
    w_main = jnp.concatenate([w[:, :ab0], w[:, ab0 + 2 * DN_HEADS:]], axis=1).astype(BF16)
    w_ab = jnp.pad(w[:, ab0:ab0 + 2 * DN_HEADS], ((0, 0), (0, LANES - 2 * DN_HEADS))).astype(BF16)
    row2 = lambda v: v.reshape(1, -1).astype(F32)

    z, ab = _in_proj(xp, xs, row2(norm1_g[i]), w_main, w_ab)

    alog_p = _lane_pad(a_log[i], 0)
    dtb_p = _lane_pad(dt_bias[i], 0)
    cw = conv_w[i].astype(F32)
    ong = row2(dn_norm_g[i])
    zero_s = jnp.zeros((bp, DN_HEADS, DN_DK, DN_DV), F32)
    zero_buf = jnp.zeros((bp, DN_CONV - 1, QKV_W), F32)
    o_p, sd_p, sc_p = _delta_branch(z, ab, zero_buf, zero_s, cw, alog_p, dtb_p, ong,
                                    tok0=0, L=lp, G=4)
    o_s, sd_s, sc_s = _delta_branch(z, ab, state_conv[i], state_delta[i], cw, alog_p, dtb_p, ong,
                                    tok0=n_p, L=ls, G=8)

    t = GM_CHUNK
    tri = jnp.tril(jnp.ones((t, t), bool))
    ws = gm_ws[i]
    mix_p = jnp.where(tri, ws, 0.0)
    small = jnp.where(tri[:ls, :ls], ws[:, :ls, :ls], 0.0)
    mix_s = jnp.einsum('ab,gts->gatbs', jnp.eye(t // ls, dtype=F32), small).reshape(GM_GROUPS, t, t)
    mix = jnp.stack([mix_p, mix_s]).astype(BF16)
    gw = GM_WIDTH // GM_GROUPS
    bias_p = jnp.repeat(gm_bs[i].T, gw, axis=1)
    bias_s = jnp.tile(bias_p[:ls], (t // ls, 1))
    bias = jnp.stack([bias_p, bias_s]).astype(F32)
    rw = jnp.pad(router_w[i].astype(F32), ((0, 0), (0, LANES - N_EXPERTS)))
    rwh = rw.astype(BF16)
    rwl = (rw - rwh.astype(F32)).astype(BF16)
    rb = _lane_pad(router_b[i], 0, fill=-jnp.inf)
    x1, h2, gates, route, v_s, counts = _post_mix(
        o_p, o_s, z, xp, xs, row2(gm_ln_g[i]), row2(gm_ln_b[i]), mix, bias, w_proj_a[i].astype(BF16),
        w_proj_b[i].astype(BF16), w_out[i].astype(BF16), row2(norm2_g[i]), rwh, rwl, rb)
    dest_t, rows, block_e, first, valid, next_e, slot = _moe_dispatch(
        route[:TOP_K], route[TOP_K:], counts[0, :N_EXPERTS].astype(jnp.int32), n)
    xb = _sc_scatter_rows(h2, dest_t, rows)
    yb = _moe_experts(block_e, first, valid, next_e, slot, xb, moe_w1[i], moe_b1[i][:, None, :],
                      moe_w2[i], moe_b2[i][:, None, :])
    yk = _sc_gather_rows(yb, dest_t.reshape(TOP_K * n)).reshape(TOP_K, n, d // 2)

    y_p, y_s = _tail(x1, yk, gates, p_prompt[i].reshape(n_p, PLE_DIM), p_sample[i].reshape(n_s, PLE_DIM),
                     row2(norm3_g[i]), ple_gate_w[i].astype(BF16), ple_w[i].astype(BF16), row2(final_norm_g))

    return (y_p.reshape(bp, lp, d), y_s.reshape(bs, ls, d),
            sd_p[None], sc_p[None], sd_s[None], sc_s[None], v_s.reshape(1, bs, ls, GM_WIDTH))
```

```python
import functools
import math

import jax
import jax.numpy as jnp
from jax import lax
from jax.experimental import pallas as pl
from jax.experimental.pallas import tpu as pltpu
from jax.experimental.pallas import tpu_sc as plsc

F32 = jnp.float32
BF16 = jnp.bfloat16

D_MODEL = 1024
DN_HEADS = 8
DN_DK = 128
DN_DV = 128
DN_CONV = 4
DN_CHUNK = 64
GM_WIDTH = 1024
GM_GROUPS = 8
GM_CHUNK = 128
N_EXPERTS = 32
TOP_K = 4
D_EXPERT = 1024
SWIGLU_LIMIT = 7.0
SWIGLU_ALPHA = 1.702
PLE_DIM = 256
EPS = 1e-6
QK_W = DN_HEADS * DN_DK
V_W = DN_HEADS * DN_DV
QKV_W = 2 * QK_W + V_W

LANES = 128
SUBLANES = 8
TAIL_PARTS = 4
MOE_ROWS = 512
VMEM_LIMIT = 56 << 20
SC_CORES, SC_SUBCORES = 2, 16
SC_WINDOW = 128
SC_TOKENS = SC_WINDOW // TOP_K
SC_CHUNK = 64


def _params(*sem):
    return pltpu.CompilerParams(dimension_semantics=sem, vmem_limit_bytes=VMEM_LIMIT)


def _dot(a, b):
    return jnp.dot(a.astype(BF16), b.astype(BF16), preferred_element_type=F32)


def _split(a):
    hi = a.astype(BF16)
    return hi, (a - hi.astype(F32)).astype(BF16)


def _dot3(a, b):
    ah, al = _split(a)
    bh, bl = _split(b)
    d = functools.partial(jnp.dot, preferred_element_type=F32)
    return d(ah, bh) + (d(ah, bl) + d(al, bh))


def _dot_nt(a, b):
    return lax.dot_general(a.astype(BF16), b.astype(BF16), (((1,), (1,)), ((), ())),
                           preferred_element_type=F32)


def _dot_tn(a, b):
    return lax.dot_general(a.astype(BF16), b.astype(BF16), (((0,), (0,)), ((), ())),
                           preferred_element_type=F32)


def _sigmoid(x):
    return 0.5 * jnp.tanh(0.5 * x) + 0.5


def _pack_bf16_pairs(x):
    bits = lax.bitcast_convert_type(x.astype(BF16).astype(F32), jnp.uint32)
    half = x.shape[1] // 2
    return lax.bitcast_convert_type((bits[:, :half] >> 16) | bits[:, half:], jnp.int32)


def _unpack_bf16_pairs(w):
    bits = lax.bitcast_convert_type(w, jnp.uint32)
    return (lax.bitcast_convert_type(bits << 16, F32),
            lax.bitcast_convert_type(bits & jnp.uint32(0xFFFF0000), F32))


def _rms(x, g):
    return x * lax.rsqrt(jnp.mean(x * x, axis=-1, keepdims=True) + EPS) * g


def _gelu(x):
    return 0.5 * x * (1.0 + lax.erf(x * (1.0 / math.sqrt(2.0))))


def _row_tile(n_p, n_s, cap=512):
    for t in (1024, 512, 256, 128):
        if t > cap:
            continue
        if n_p % t == 0 and n_s % t == 0:
            return t
    raise ValueError(f"token counts {n_p}, {n_s} must be multiples of 128")


def _group_specs(tm, width, n_p):
    pt = n_p // tm
    return [pl.BlockSpec((tm, width), lambda i, *_: (jnp.minimum(i, pt - 1), 0)),
            pl.BlockSpec((tm, width), lambda i, *_: (jnp.maximum(i - pt, 0), 0))]


def _group_pick(prompt_ref, sample_ref, n_p):
    pt = n_p // prompt_ref.shape[0]
    return jnp.where(pl.program_id(0) < pt, prompt_ref[...], sample_ref[...])


def _in_proj_kernel(xp_ref, xs_ref, g_ref, w_ref, wab_ref, z_ref, ab_ref, h_scr, *, n_p):
    @pl.when(pl.program_id(1) == 0)
    def _():
        hb = _rms(_group_pick(xp_ref, xs_ref, n_p), g_ref[...]).astype(BF16)
        h_scr[...] = hb
        ab_ref[...] = jnp.dot(hb, wab_ref[...], preferred_element_type=F32)

    z_ref[...] = jnp.dot(h_scr[...], w_ref[...], preferred_element_type=F32).astype(z_ref.dtype)


def _in_proj(xp, xs, g, w_main, w_ab):
    n_p, n_s = xp.shape[0], xs.shape[0]
    n = n_p + n_s
    tm, tn = _row_tile(n_p, n_s, cap=1024), 2048
    cols = w_main.shape[1]
    return pl.pallas_call(
        functools.partial(_in_proj_kernel, n_p=n_p),
        grid=(n // tm, cols // tn),
        in_specs=_group_specs(tm, D_MODEL, n_p) + [
            pl.BlockSpec((1, D_MODEL), lambda i, j: (0, 0)),
            pl.BlockSpec((D_MODEL, tn), lambda i, j: (0, j)),
            pl.BlockSpec((D_MODEL, LANES), lambda i, j: (0, 0)),
        ],
        out_specs=[
            pl.BlockSpec((tm, tn), lambda i, j: (i, j)),
            pl.BlockSpec((tm, LANES), lambda i, j: (i, 0)),
        ],
        out_shape=[jax.ShapeDtypeStruct((n, cols), BF16), jax.ShapeDtypeStruct((n, LANES), F32)],
        scratch_shapes=[pltpu.VMEM((tm, D_MODEL), BF16)],
        compiler_params=_params("parallel", "arbitrary"),
        name="in_proj",
    )(xp, xs, g, w_main, w_ab)


def _delta_kernel(*refs, C, G, carry):
    (qkv_ref, zg_ref, ab_ref, buf_ref, s0_ref, cw_ref, alog_ref, dtb_ref, ong_ref) = refs[:9]
    refs = refs[9:]
    o_ref, snew_ref, bufnew_ref, xc_scr = refs[:4]
    H, DK, DV = DN_HEADS, DN_DK, DN_DV
    T = G * C
    halo = DN_CONV - 1
    base = SUBLANES
    cw = cw_ref[...]

    def conv(window):
        y = window(0) * cw[0:1]
        for i in range(1, DN_CONV):
            y = y + window(i) * cw[i:i + 1]
        return y

    if carry:
        s_scr = refs[4]
        c = pl.program_id(1)
        last = pl.num_programs(1) - 1

        @pl.when(c == 0)
        def _():
            xc_scr[base - halo:base, :] = buf_ref[0]
            s_scr[...] = s0_ref[0]

        xb = qkv_ref[...]
        xf = xb.astype(F32)
        xc_scr[base:base + SUBLANES, :] = xf[0:SUBLANES]
        y_head = conv(lambda i: xc_scr[base - halo + i:base - halo + i + SUBLANES, :])
        ri = lax.broadcasted_iota(jnp.int32, (halo * T, T), 0)
        ci = lax.broadcasted_iota(jnp.int32, (halo * T, T), 1)
        src_row = (ri & (T - 1)) - (halo - (ri >> (T.bit_length() - 1)))
        shifted = jnp.dot(jnp.where(ci == src_row, 1.0, 0.0).astype(BF16), xb, preferred_element_type=F32)
        y = conv(lambda i: shifted[i * T:(i + 1) * T] if i < halo else xf)
        y = jnp.concatenate([y_head, y[SUBLANES:]], axis=0)
        tail = xf[T - halo:T]
        xc_scr[base - halo:base, :] = tail

        @pl.when(c == last)
        def _():
            bufnew_ref[0] = tail
    else:
        ys = []
        x_new = qkv_ref[...].astype(F32)
        for g in range(G):
            xc_scr[g, base - halo:base, :] = buf_ref[g]
            xc_scr[g, base:base + C, :] = x_new[g * C:(g + 1) * C, :]
            ys.append(conv(lambda i: xc_scr[g, base - halo + i:base - halo + i + C, :]))
            bufnew_ref[g] = xc_scr[g, base + C - halo:base + C, :]
        y = jnp.concatenate(ys, axis=0) if G > 1 else ys[0]
    qkv = y * _sigmoid(y)

    ab = ab_ref[...]
    g_all = -jnp.exp(alog_ref[...]) * jax.nn.softplus(ab + dtb_ref[...])
    beta_all = _sigmoid(ab)
    shift = C.bit_length() - 1
    rt = lax.broadcasted_iota(jnp.int32, (T, T), 0)
    ct = lax.broadcasted_iota(jnp.int32, (T, T), 1)
    chunk_tril = ((rt >> shift) == (ct >> shift)) & (rt >= ct)
    gcum = _dot3(chunk_tril.astype(F32), g_all)
    gam_all = jnp.exp(gcum)
    wide = C >= LANES // 2
    if wide:
        gcum_t, beta_t = gcum.T, beta_all.T

    row = lax.broadcasted_iota(jnp.int32, (C, C), 0)
    col = lax.broadcasted_iota(jnp.int32, (C, C), 1)
    incl, strict, eye = row >= col, row > col, row == col
    eyef = eye.astype(F32)
    units = [(g, h) for g in range(G) for h in range(H)]
    rows = lambda a, g: a[g * C:(g + 1) * C]

    qn, kn, vv = [], [], []
    for h in range(H):
        q = qkv[:, h * DK:(h + 1) * DK]
        k = qkv[:, QK_W + h * DK:QK_W + (h + 1) * DK]
        qn.append(q * (lax.rsqrt(jnp.sum(q * q, axis=-1, keepdims=True) + EPS) * (DK ** -0.5)))
        kn.append(k * lax.rsqrt(jnp.sum(k * k, axis=-1, keepdims=True) + EPS))
        vv.append(qkv[:, 2 * QK_W + h * DV:2 * QK_W + (h + 1) * DV])
    qb = [q.astype(BF16) for q in qn]
    kb = [k.astype(BF16) for k in kn]

    gc, bc, gl, a_low, m_intra = {}, {}, {}, {}, {}
    for u in units:
        g, h = u
        gc[u] = rows(gcum, g)[:, h:h + 1]
        bc[u] = rows(beta_all, g)[:, H + h:H + h + 1]
        gl[u] = gc[u][C - 1:C, :]
        if wide:
            gr = gcum_t[h:h + 1, g * C:(g + 1) * C]
            br = beta_t[H + h:H + h + 1, g * C:(g + 1) * C]
        else:
            gr = jnp.sum(jnp.where(eye, gc[u], 0.0), axis=0, keepdims=True)
            br = jnp.sum(jnp.where(eye, bc[u], 0.0), axis=0, keepdims=True)
        db = jnp.where(incl, jnp.exp(gc[u] - gr), 0.0) * br
        k = rows(kb[h], g)
        kq = _dot_nt(jnp.concatenate([k, rows(qb[h], g)], axis=0), k)
        a_low[u] = jnp.where(strict, kq[:C] * db, 0.0)
        m_intra[u] = kq[C:] * db

    t_inv = {u: eyef - a_low[u] for u in units}
    a_pow = {u: _dot(a_low[u], a_low[u]) for u in units}
    n = 2
    while n < C:
        t_inv = {u: t_inv[u] + _dot(t_inv[u], a_pow[u]) for u in units}
        n *= 2
        if n < C:
            a_pow = {u: _dot(a_pow[u], a_pow[u]) for u in units}

    u_base, wq, k_dec = {}, {}, {}
    for u in units:
        g, h = u
        gam = rows(gam_all, g)[:, h:h + 1]
        k = rows(kn[h], g)
        sol = _dot(t_inv[u], jnp.concatenate([rows(vv[h], g), gam * k], axis=1))
        u_base[u] = sol[:, :DV]
        wq[u] = jnp.concatenate([sol[:, DV:], gam * rows(qn[h], g)], axis=0)
        k_dec[u] = k * (bc[u] * jnp.exp(gl[u] - gc[u]))

    state = [s_scr[h] for h in range(H)] if carry else None
    outs = {}
    for g in range(G):
        s_in = state if carry else [s0_ref[g, h] for h in range(H)]
        wqs = [_dot(wq[(g, h)], s_in[h]) for h in range(H)]
        us = [u_base[(g, h)] - wqs[h][:C] for h in range(H)]
        for h in range(H):
            outs[(g, h)] = wqs[h][C:] + _dot(m_intra[(g, h)], us[h])
        s_out = [jnp.exp(gl[(g, h)]) * s_in[h] + _dot_tn(k_dec[(g, h)], us[h]) for h in range(H)]
        if carry:
            state = s_out
        else:
            for h in range(H):
                snew_ref[g, h] = s_out[h]

    for h in range(H):
        o = jnp.concatenate([outs[(g, h)] for g in range(G)], axis=0) if G > 1 else outs[(0, h)]
        zg = zg_ref[:, h * DV:(h + 1) * DV].astype(F32)
        o_ref[:, h * DV:(h + 1) * DV] = (_rms(o, ong_ref[...]) * (zg * _sigmoid(zg))).astype(o_ref.dtype)

    if carry:
        for h in range(H):
            s_scr[h] = state[h]

        @pl.when(c == last)
        def _():
            for h in range(H):
                snew_ref[0, h] = state[h]


def _delta_branch(z, ab, conv_buf, s0, conv_w, alog_p, dtb_p, onorm_g, *, tok0, L, G):
    B = conv_buf.shape[0]
    C = math.gcd(L, DN_CHUNK)
    nc = L // C
    carry = nc > 1
    T = G * C
    blk0 = tok0 // T
    if carry:
        assert nc % G == 0
        grid = (B, nc // G)
        own_blk = lambda b, c: b * (nc // G) + c
        gs = 1
    else:
        assert B % G == 0
        grid = (B // G, 1)
        own_blk = lambda b, c: b
        gs = G
    tok = lambda b, c: (blk0 + own_blk(b, c), 0)
    seq3 = lambda b, c: (b, 0, 0)
    seq4 = lambda b, c: (b, 0, 0, 0)
    const = lambda b, c: (0, 0)
    if carry:
        scratch = [pltpu.VMEM((2 * SUBLANES, QKV_W), F32), pltpu.VMEM((DN_HEADS, DN_DK, DN_DV), F32)]
    else:
        scratch = [pltpu.VMEM((G, SUBLANES + C, QKV_W), F32)]
    return pl.pallas_call(
        functools.partial(_delta_kernel, C=C, G=G, carry=carry),
        grid=grid,
        in_specs=[
            pl.BlockSpec((T, QKV_W), tok),
            pl.BlockSpec((T, V_W), lambda b, c: (blk0 + own_blk(b, c), QKV_W // V_W)),
            pl.BlockSpec((T, LANES), tok),
            pl.BlockSpec((gs, DN_CONV - 1, QKV_W), seq3),
            pl.BlockSpec((gs, DN_HEADS, DN_DK, DN_DV), seq4),
            pl.BlockSpec((DN_CONV, QKV_W), const),
            pl.BlockSpec((1, LANES), const),
            pl.BlockSpec((1, LANES), const),
            pl.BlockSpec((1, DN_DV), const),
        ],
        out_specs=[
            pl.BlockSpec((T, V_W), lambda b, c: (own_blk(b, c), 0)),
            pl.BlockSpec((gs, DN_HEADS, DN_DK, DN_DV), seq4),
            pl.BlockSpec((gs, DN_CONV - 1, QKV_W), seq3),
        ],
        out_shape=[
            jax.ShapeDtypeStruct((B * L, V_W), BF16),
            jax.ShapeDtypeStruct(s0.shape, F32),
            jax.ShapeDtypeStruct(conv_buf.shape, F32),
        ],
        scratch_shapes=scratch,
        compiler_params=_params("parallel", "arbitrary"),
        name=f"delta_rule_c{C}",
    )(z, z, ab, conv_buf, s0, conv_w, alog_p, dtb_p, onorm_g)


def _post_mix_kernel(op_ref, os_ref, gu_ref, gv_ref, ma_ref, mb_ref, xp_ref, xs_ref, lng_ref, lnb_ref,
                     mix_ref, bias_ref, wa_ref, wb_ref, wo_ref, g2_ref, rwh_ref, rwl_ref, rb_ref, tri_ref,
                     x1_ref, h2_ref, gate_ref, idx_ref, v_ref, cnt_ref, us_scr, cnt_scr, *, n_p):
    u = _gelu(gu_ref[...].astype(F32))
    a = _gelu(gv_ref[...].astype(F32))
    ac = a - jnp.mean(a, axis=-1, keepdims=True)
    v = ac * lax.rsqrt(jnp.mean(ac * ac, axis=-1, keepdims=True) + EPS) * lng_ref[...] + lnb_ref[...]
    v_ref[...] = v
    vb = v.astype(BF16)
    gw = GM_WIDTH // GM_GROUPS
    for c in range(u.shape[0] // GM_CHUNK):
        rs = slice(c * GM_CHUNK, (c + 1) * GM_CHUNK)
        for g in range(GM_GROUPS):
            sl = slice(g * gw, (g + 1) * gw)
            s = jnp.dot(mix_ref[0, g], vb[rs, sl], preferred_element_type=F32) + bias_ref[0, :, sl]
            us_scr[rs, sl] = (u[rs, sl] * s).astype(BF16)

    ya = jnp.dot(_group_pick(op_ref, os_ref, n_p).astype(BF16), wa_ref[...], preferred_element_type=F32)
    yb = jnp.dot(us_scr[...], wb_ref[...], preferred_element_type=F32)
    mixed = _sigmoid(ma_ref[...].astype(F32)) * ya + _sigmoid(mb_ref[...].astype(F32)) * yb
    x1 = _group_pick(xp_ref, xs_ref, n_p) + jnp.dot(mixed.astype(BF16), wo_ref[...], preferred_element_type=F32)
    x1_ref[...] = x1.astype(x1_ref.dtype)
    h2 = _rms(x1, g2_ref[...])
    h2_ref[...] = _pack_bf16_pairs(h2)
    hh, hl = _split(h2)
    d = functools.partial(jnp.dot, preferred_element_type=F32)
    logits = d(hh, rwh_ref[...]) + (d(hh, rwl_ref[...]) + d(hl, rwh_ref[...])) + rb_ref[...]
    lane = lax.broadcasted_iota(jnp.int32, logits.shape, 1).astype(F32)
    vals, idxs = [], []
    for _ in range(TOP_K):
        m = jnp.max(logits, axis=-1, keepdims=True)
        i = jnp.min(jnp.where(logits == m, lane, float(LANES)), axis=-1, keepdims=True)
        vals.append(m)
        idxs.append(i)
        logits = jnp.where(lane == i, -jnp.inf, logits)
    es = [jnp.exp(v - vals[0]) for v in vals]
    tot = es[0]
    for e in es[1:]:
        tot = tot + e

    @pl.when(pl.program_id(0) == 0)
    def _():
        cnt_scr[...] = jnp.zeros(cnt_scr.shape, F32)

    onehot = (lane == idxs[0]).astype(F32)
    for k in range(1, TOP_K):
        onehot = onehot + (lane == idxs[k]).astype(F32)
    before = jnp.dot(tri_ref[...], onehot.astype(BF16), preferred_element_type=F32) + cnt_scr[...]
    cnt_scr[...] = cnt_scr[...] + jnp.sum(onehot, axis=0, keepdims=True)
    cnt_ref[...] = cnt_scr[...]

    gates = jnp.zeros(logits.shape, F32)
    route = jnp.zeros(logits.shape, F32)
    for k in range(TOP_K):
        rank = jnp.sum(jnp.where(lane == idxs[k], before, 0.0), axis=-1, keepdims=True)
        gates = jnp.where(lane == float(k), es[k] / tot, gates)
        route = jnp.where(lane == float(k), idxs[k], route)
        route = jnp.where(lane == float(TOP_K + k), rank, route)
    gate_ref[...] = gates
    idx_ref[...] = route.T[:2 * TOP_K].astype(jnp.int32)


def _post_mix(o_p, o_s, z, xp, xs, ln_g, ln_b, mix, bias, wa, wb, wo, g2, rwh, rwl, rb):
    n_p, n_s = xp.shape[0], xs.shape[0]
    n = n_p + n_s
    tm = _row_tile(n_p, n_s)
    pt = n_p // tm
    tri = jnp.tril(jnp.ones((tm, tm), BF16), k=-1)
    grp = lambda i: jnp.where(i < pt, 0, 1)
    once = pl.Buffered(1)
    tok = pl.BlockSpec((tm, D_MODEL), lambda i: (i, 0))
    zcol = lambda c: pl.BlockSpec((tm, D_MODEL), lambda i: (i, c))
    full = lambda r, c: pl.BlockSpec((r, c), lambda i: (0, 0), pipeline_mode=once)
    narrow = pl.BlockSpec((tm, LANES), lambda i: (i, 0))
    op_spec, os_spec = _group_specs(tm, V_W, n_p)
    xp_spec, xs_spec = _group_specs(tm, D_MODEL, n_p)
    return pl.pallas_call(
        functools.partial(_post_mix_kernel, n_p=n_p),
        grid=(n // tm,),
        in_specs=[
            op_spec, os_spec, zcol(4), zcol(5), zcol(6), zcol(7), xp_spec, xs_spec,
            full(1, GM_WIDTH), full(1, GM_WIDTH),
            pl.BlockSpec((1, GM_GROUPS, GM_CHUNK, GM_CHUNK), lambda i: (grp(i), 0, 0, 0)),
            pl.BlockSpec((1, GM_CHUNK, GM_WIDTH), lambda i: (grp(i), 0, 0)),
            full(V_W, D_MODEL), full(GM_WIDTH, D_MODEL), full(D_MODEL, D_MODEL), full(1, D_MODEL),
            full(D_MODEL, LANES), full(D_MODEL, LANES), full(1, LANES), full(tm, tm),
        ],
        out_specs=[tok, pl.BlockSpec((tm, D_MODEL // 2), lambda i: (i, 0)), narrow,
                   pl.BlockSpec((2 * TOP_K, tm), lambda i: (0, i)),
                   pl.BlockSpec((tm, GM_WIDTH), lambda i: (jnp.maximum(i - pt, 0), 0)),
                   pl.BlockSpec((1, LANES), lambda i: (0, 0))],
        out_shape=[
            jax.ShapeDtypeStruct((n, D_MODEL), BF16),
            jax.ShapeDtypeStruct((n, D_MODEL // 2), jnp.int32),
            jax.ShapeDtypeStruct((n, LANES), F32),
            jax.ShapeDtypeStruct((2 * TOP_K, n), jnp.int32),
            jax.ShapeDtypeStruct((n_s, GM_WIDTH), F32),
            jax.ShapeDtypeStruct((1, LANES), F32),
        ],
        scratch_shapes=[pltpu.VMEM((tm, GM_WIDTH), BF16), pltpu.VMEM((1, LANES), F32)],
        compiler_params=_params("arbitrary"),
        name="chunk_mlp_post_mix_router",
    )(o_p, o_s, z, z, z, z, xp, xs, ln_g, ln_b, mix, bias, wa, wb, wo, g2, rwh, rwl, rb, tri)


def _moe_kernel(be_ref, first_ref, valid_ref, next_ref, slot_ref, x_ref, w1_hbm, b1_ref, w2_hbm, b2_ref, y_ref,
                w1_stage, w2_stage, sems):
    b = pl.program_id(0)
    slot = slot_ref[b]

    def weight_copies(e, s):
        return (pltpu.make_async_copy(w1_hbm.at[e], w1_stage.at[s], sems.at[0, s]),
                pltpu.make_async_copy(w2_hbm.at[e], w2_stage.at[s], sems.at[1, s]))

    @pl.when(b == 0)
    def _():
        for c in weight_copies(be_ref[0], slot):
            c.start()

    @pl.when(first_ref[b] == 1)
    def _():
        for c in weight_copies(be_ref[b], slot):
            c.wait()

        @pl.when(next_ref[b] >= 0)
        def _():
            for c in weight_copies(next_ref[b], 1 - slot):
                c.start()

    @pl.when(valid_ref[b] == 1)
    def _():
        x = jnp.concatenate(_unpack_bf16_pairs(x_ref[...]), axis=1).astype(BF16)
        hid = jnp.dot(x, w1_stage[slot].astype(BF16), preferred_element_type=F32) + b1_ref[0]
        gate = jnp.minimum(hid[:, :D_EXPERT], SWIGLU_LIMIT)
        up = jnp.clip(hid[:, D_EXPERT:], -SWIGLU_LIMIT, SWIGLU_LIMIT)
        act = gate * _sigmoid(SWIGLU_ALPHA * gate) * (up + 1.0)
        y = jnp.dot(act.astype(BF16), w2_stage[slot].astype(BF16), preferred_element_type=F32) + b2_ref[0]
        y_ref[...] = _pack_bf16_pairs(y)

    @pl.when(valid_ref[b] == 0)
    def _():
        y_ref[...] = jnp.zeros(y_ref.shape, jnp.int32)


def _moe_experts(block_e, first, valid, next_e, slot, xb, w1, b1, w2, b2):
    rows = xb.shape[0]
    nb = rows // MOE_ROWS
    smem4 = lambda f: (lambda b, be, fi, va, ne, sl: f(b, be))
    return pl.pallas_call(
        _moe_kernel,
        grid_spec=pltpu.PrefetchScalarGridSpec(
            num_scalar_prefetch=5,
            grid=(nb,),
            in_specs=[
                pl.BlockSpec((MOE_ROWS, D_MODEL // 2), smem4(lambda b, be: (b, 0))),
                pl.BlockSpec(memory_space=pl.ANY),
                pl.BlockSpec((1, 1, 2 * D_EXPERT), smem4(lambda b, be: (be[b], 0, 0))),
                pl.BlockSpec(memory_space=pl.ANY),
                pl.BlockSpec((1, 1, D_MODEL), smem4(lambda b, be: (be[b], 0, 0))),
            ],
            out_specs=pl.BlockSpec((MOE_ROWS, D_MODEL // 2), smem4(lambda b, be: (b, 0))),
            scratch_shapes=[
                pltpu.VMEM((2, D_MODEL, 2 * D_EXPERT), F32),
                pltpu.VMEM((2, D_EXPERT, D_MODEL), F32),
                pltpu.SemaphoreType.DMA((2, 2)),
            ],
        ),
        out_shape=jax.ShapeDtypeStruct((rows, D_MODEL // 2), jnp.int32),
        compiler_params=_params("arbitrary"),
        name="moe_experts",
    )(block_e, first, valid, next_e, slot, xb, w1, b1, w2, b2)


def _moe_dispatch(idx, rank, counts, n):
    experts = jnp.arange(N_EXPERTS, dtype=jnp.int32)
    padded = (counts + MOE_ROWS - 1) // MOE_ROWS * MOE_ROWS
    pad_end = jnp.cumsum(padded)
    pad_start = pad_end - padded
    start_of = jnp.sum(jnp.where(idx[None] == experts[:, None, None], pad_start[:, None, None], 0), axis=0)
    dest_t = start_of + rank
    nb = -(-n * TOP_K // MOE_ROWS) + N_EXPERTS
    rows = nb * MOE_ROWS
    starts = jnp.arange(nb, dtype=jnp.int32) * MOE_ROWS
    valid = (starts < pad_end[-1]).astype(jnp.int32)
    owner = lambda r: jnp.minimum(jnp.sum((pad_end[None, :] <= r[:, None]).astype(jnp.int32), axis=1),
                                  N_EXPERTS - 1)
    last_e = owner(pad_end[-1:] - 1)[0]
    block_e = jnp.where(valid == 1, owner(starts), last_e).astype(jnp.int32)
    first = jnp.concatenate([jnp.ones((1,), jnp.int32),
                             (block_e[1:] != block_e[:-1]).astype(jnp.int32)])
    blk = jnp.arange(nb, dtype=jnp.int32)
    later_first = (blk[None, :] > blk[:, None]) & (first[None, :] == 1)
    next_pos = jnp.min(jnp.where(later_first, blk[None, :], nb), axis=1)
    next_e = jnp.where(next_pos < nb, block_e[jnp.minimum(next_pos, nb - 1)], -1).astype(jnp.int32)
    slot = (jnp.cumsum(first) - 1) % 2
    return dest_t, rows, block_e, first, valid, next_e, slot.astype(jnp.int32)


def _sc_scatter_rows(src, dest_t, out_rows):
    n, d = src.shape
    nk = dest_t.shape[0]
    assert nk * SC_TOKENS == SC_WINDOW
    idx = dest_t.reshape(nk, n // SC_TOKENS, SC_TOKENS).transpose(1, 0, 2).reshape(n // SC_TOKENS, SC_WINDOW)
    mesh = plsc.VectorSubcoreMesh(core_axis_name="core", subcore_axis_name="subcore",
                                  num_cores=SC_CORES, num_subcores=SC_SUBCORES)

    @pl.kernel(out_type=jax.ShapeDtypeStruct((out_rows, d), src.dtype), mesh=mesh, scratch_types=[])
    def scatter_rows(src_hbm, di_hbm, out_hbm):
        def body(x_vmem, di_vmem):
            for k in range(nk):
                pltpu.sync_copy(x_vmem, out_hbm.at[di_vmem.at[0, pl.ds(k * SC_TOKENS, SC_TOKENS)]])

        pltpu.emit_pipeline(
            body,
            grid=(n // SC_TOKENS,),
            in_specs=[pl.BlockSpec((SC_TOKENS, d), lambda i: (i, 0)),
                      pl.BlockSpec((1, SC_WINDOW), lambda i: (i, 0))],
            out_specs=[],
            core_axis_name=("core", "subcore"),
            dimension_semantics=(pltpu.PARALLEL,),
        )(src_hbm, di_hbm)

    return scatter_rows(src, idx)


def _sc_gather_rows(src, idx):
    m = idx.shape[0]
    d = src.shape[1]
    idx_rows = jnp.pad(idx.reshape(m // SC_CHUNK, SC_CHUNK), ((0, 0), (0, SC_WINDOW - SC_CHUNK)))
    mesh = plsc.VectorSubcoreMesh(core_axis_name="core", subcore_axis_name="subcore",
                                  num_cores=SC_CORES, num_subcores=SC_SUBCORES)

    @pl.kernel(out_type=jax.ShapeDtypeStruct((m, d), src.dtype), mesh=mesh, scratch_types=[])
    def gather_rows(src_hbm, si_hbm, out_hbm):
        def body(si_vmem, o_vmem):
            pltpu.sync_copy(src_hbm.at[si_vmem.at[0, pl.ds(0, SC_CHUNK)]], o_vmem)

        pltpu.emit_pipeline(
            body,
            grid=(m // SC_CHUNK,),
            in_specs=[pl.BlockSpec((1, SC_WINDOW), lambda i: (i, 0))],
            out_specs=[pl.BlockSpec((SC_CHUNK, d), lambda i: (i, 0))],
            core_axis_name=("core", "subcore"),
            dimension_semantics=(pltpu.PARALLEL,),
        )(si_hbm, out_hbm)

    return gather_rows(src, idx_rows)


def _tail_kernel(x1_ref, yk_ref, gate_ref, pp_ref, ps_ref, g3_ref, wg_ref, wp_ref, gf_ref,
                 yp_ref, ys_ref, *, n_p):
    tm = x1_ref.shape[0]
    parts = [slice(p * tm // TAIL_PARTS, (p + 1) * tm // TAIL_PARTS) for p in range(TAIL_PARTS)]
    in_prompt = pl.program_id(0) < n_p // tm
    d = functools.partial(jnp.dot, preferred_element_type=F32)

    def combine(r):
        gates = gate_ref[r, :]
        lo, hi = _unpack_bf16_pairs(yk_ref[0, r, :])
        lo, hi = lo * gates[:, 0:1], hi * gates[:, 0:1]
        for k in range(1, TOP_K):
            lo_k, hi_k = _unpack_bf16_pairs(yk_ref[k, r, :])
            lo, hi = lo + lo_k * gates[:, k:k + 1], hi + hi_k * gates[:, k:k + 1]
        return x1_ref[r, :].astype(F32) + jnp.concatenate([lo, hi], axis=1)

    x2 = [combine(r) for r in parts]
    h3 = [_rms(x, g3_ref[...]) for x in x2]
    gate = [_sigmoid(d(h.astype(BF16), wg_ref[...])) for h in h3]
    pe = [d(jnp.where(in_prompt, pp_ref[r, :], ps_ref[r, :]).astype(BF16), wp_ref[...]) for r in parts]
    y = [_rms(x + g * p, gf_ref[...]) for x, g, p in zip(x2, gate, pe)]

    @pl.when(in_prompt)
    def _():
        for r, yy in zip(parts, y):
            yp_ref[r, :] = yy

    @pl.when(jnp.logical_not(in_prompt))
    def _():
        for r, yy in zip(parts, y):
            ys_ref[r, :] = yy


def _tail(x1, yk, gates, pp, ps, g3, wg, wp, gf):
    n_p, n_s = pp.shape[0], ps.shape[0]
    n = n_p + n_s
    tm = _row_tile(n_p, n_s)
    tok = pl.BlockSpec((tm, D_MODEL), lambda i: (i, 0))
    full = lambda r, c: pl.BlockSpec((r, c), lambda i: (0, 0))
    return pl.pallas_call(
        functools.partial(_tail_kernel, n_p=n_p),
        grid=(n // tm,),
        in_specs=[tok, pl.BlockSpec((TOP_K, tm, D_MODEL // 2), lambda i: (0, i, 0)),
                  pl.BlockSpec((tm, LANES), lambda i: (i, 0)),
                  *_group_specs(tm, PLE_DIM, n_p),
                  full(1, D_MODEL), full(D_MODEL, D_MODEL), full(PLE_DIM, D_MODEL), full(1, D_MODEL)],
        out_specs=_group_specs(tm, D_MODEL, n_p),
        out_shape=[jax.ShapeDtypeStruct((n_p, D_MODEL), F32), jax.ShapeDtypeStruct((n_s, D_MODEL), F32)],
        compiler_params=_params("arbitrary"),
        name="ple_final_norm",
    )(x1, yk, gates, pp, ps, g3, wg, wp, gf)


def _lane_pad(v, offset, fill=0.0):
    out = jnp.full((1, LANES), fill, F32)
    return out.at[0, offset:offset + v.shape[0]].set(v.astype(F32))


def kernel(x_prompt, x_sample, state_delta, state_conv, p_prompt, p_sample, norm1_g, w_in, conv_w, a_log, dt_bias, dn_norm_g, w_proj_a, gm_ln_g, gm_ln_b, gm_ws, gm_bs, w_proj_b, w_out, norm2_g, router_w, router_b, moe_w1, moe_b1, moe_w2, moe_b2, norm3_g, ple_w, ple_gate_w, final_norm_g):
    bp, lp, d = x_prompt.shape
    bs, ls, _ = x_sample.shape
    depth = w_in.shape[0]
    assert depth == 1 and d == D_MODEL
    assert lp % GM_CHUNK == 0 and GM_CHUNK % ls == 0 and ls >= DN_CONV - 1
    n_p, n_s = bp * lp, bs * ls
    n = n_p + n_s
    i = 0

    xp, xs = x_prompt.reshape(n_p, d), x_sample.reshape(n_s, d)

    ab0 = QKV_W
    w = w_in[i]
    w_main = jnp.concatenate([w[:, :ab0], w[:, ab0 + 2 * DN_HEADS:]], axis=1).astype(BF16)
    w_ab = jnp.pad(w[:, ab0:ab0 + 2 * DN_HEADS], ((0, 0), (0, LANES - 2 * DN_HEADS))).astype(BF16)
    row2 = lambda v: v.reshape(1, -1).astype(F32)

    z, ab = _in_proj(xp, xs, row2(norm1_g[i]), w_main, w_ab)

    alog_p = _lane_pad(a_log[i], 0)
    dtb_p = _lane_pad(dt_bias[i], 0)
    cw = conv_w[i].astype(F32)
    ong = row2(dn_norm_g[i])
    zero_s = jnp.zeros((bp, DN_HEADS, DN_DK, DN_DV), F32)
    zero_buf = jnp.zeros((bp, DN_CONV - 1, QKV_W), F32)
    o_p, sd_p, sc_p = _delta_branch(z, ab, zero_buf, zero_s, cw, alog_p, dtb_p, ong,
                                    tok0=0, L=lp, G=4)
    o_s, sd_s, sc_s = _delta_branch(z, ab, state_conv[i], state_delta[i], cw, alog_p, dtb_p, ong,
                                    tok0=n_p, L=ls, G=8)

    t = GM_CHUNK
    tri = jnp.tril(jnp.ones((t, t), bool))
    ws = gm_ws[i]
    mix_p = jnp.where(tri, ws, 0.0)
    small = jnp.where(tri[:ls, :ls], ws[:, :ls, :ls], 0.0)
    mix_s = jnp.einsum('ab,gts->gatbs', jnp.eye(t // ls, dtype=F32), small).reshape(GM_GROUPS, t, t)
    mix = jnp.stack([mix_p, mix_s]).astype(BF16)
    gw = GM_WIDTH // GM_GROUPS
    bias_p = jnp.repeat(gm_bs[i].T, gw, axis=1)
    bias_s = jnp.tile(bias_p[:ls], (t // ls, 1))
    bias = jnp.stack([bias_p, bias_s]).astype(F32)
    rw = jnp.pad(router_w[i].astype(F32), ((0, 0), (0, LANES - N_EXPERTS)))
    rwh = rw.astype(BF16)
    rwl = (rw - rwh.astype(F32)).astype(BF16)
    rb = _lane_pad(router_b[i], 0, fill=-jnp.inf)
    x1, h2, gates, route, v_s, counts = _post_mix(
        o_p, o_s, z, xp, xs, row2(gm_ln_g[i]), row2(gm_ln_b[i]), mix, bias, w_proj_a[i].astype(BF16),
        w_proj_b[i].astype(BF16), w_out[i].astype(BF16), row2(norm2_g[i]), rwh, rwl, rb)
    dest_t, rows, block_e, first, valid, next_e, slot = _moe_dispatch(
        route[:TOP_K], route[TOP_K:], counts[0, :N_EXPERTS].astype(jnp.int32), n)
    xb = _sc_scatter_rows(h2, dest_t, rows)
    yb = _moe_experts(block_e, first, valid, next_e, slot, xb, moe_w1[i], moe_b1[i][:, None, :],
                      moe_w2[i], moe_b2[i][:, None, :])
    yk = _sc_gather_rows(yb, dest_t.reshape(TOP_K * n)).reshape(TOP_K, n, d // 2)

    y_p, y_s = _tail(x1, yk, gates, p_prompt[i].reshape(n_p, PLE_DIM), p_sample[i].reshape(n_s, PLE_DIM),
                     row2(norm3_g[i]), ple_gate_w[i].astype(BF16), ple_w[i].astype(BF16), row2(final_norm_g))

    return (y_p.reshape(bp, lp, d), y_s.reshape(bs, ls, d),
            sd_p[None], sc_p[None], sd_s[None], sc_s[None], v_s.reshape(1, bs, ls, GM_WIDTH))
```

```python
import functools
import math

import jax
import jax.numpy as jnp
from jax import lax
from jax.experimental import pallas as pl
from jax.experimental.pallas import tpu as pltpu
from jax.experimental.pallas import tpu_sc as plsc

F32 = jnp.float32
BF16 = jnp.bfloat16

D_MODEL = 1024
DN_HEADS = 8
DN_DK = 128
DN_DV = 128
DN_CONV = 4
DN_CHUNK = 64
GM_WIDTH = 1024
GM_GROUPS = 8
GM_CHUNK = 128
N_EXPERTS = 32
TOP_K = 4
D_EXPERT = 1024
SWIGLU_LIMIT = 7.0
SWIGLU_ALPHA = 1.702
PLE_DIM = 256
EPS = 1e-6
QK_W = DN_HEADS * DN_DK
V_W = DN_HEADS * DN_DV
QKV_W = 2 * QK_W + V_W

LANES = 128
SUBLANES = 8
TAIL_PARTS = 4
MOE_ROWS = 512
VMEM_LIMIT = 56 << 20
SC_CORES, SC_SUBCORES = 2, 16
SC_WINDOW = 128
SC_TOKENS = SC_WINDOW // TOP_K
SC_CHUNK = 64


def _params(*sem):
    return pltpu.CompilerParams(dimension_semantics=sem, vmem_limit_bytes=VMEM_LIMIT)


def _dot(a, b):
    return jnp.dot(a.astype(BF16), b.astype(BF16), preferred_element_type=F32)


def _split(a):
    hi = a.astype(BF16)
    return hi, (a - hi.astype(F32)).astype(BF16)


def _dot3(a, b):
    ah, al = _split(a)
    bh, bl = _split(b)
    d = functools.partial(jnp.dot, preferred_element_type=F32)
    return d(ah, bh) + (d(ah, bl) + d(al, bh))


def _dot_nt(a, b):
    return lax.dot_general(a.astype(BF16), b.astype(BF16), (((1,), (1,)), ((), ())),
                           preferred_element_type=F32)


def _dot_tn(a, b):
    return lax.dot_general(a.astype(BF16), b.astype(BF16), (((0,), (0,)), ((), ())),
                           preferred_element_type=F32)


def _sigmoid(x):
    return 0.5 * jnp.tanh(0.5 * x) + 0.5


def _pack_bf16_pairs(x):
    bits = lax.bitcast_convert_type(x.astype(BF16).astype(F32), jnp.uint32)
    half = x.shape[1] // 2
    return lax.bitcast_convert_type((bits[:, :half] >> 16) | bits[:, half:], jnp.int32)


def _unpack_bf16_pairs(w):
    bits = lax.bitcast_convert_type(w, jnp.uint32)
    return (lax.bitcast_convert_type(bits << 16, F32),
            lax.bitcast_convert_type(bits & jnp.uint32(0xFFFF0000), F32))


def _rms(x, g):
    return x * lax.rsqrt(jnp.mean(x * x, axis=-1, keepdims=True) + EPS) * g


def _gelu(x):
    return 0.5 * x * (1.0 + lax.erf(x * (1.0 / math.sqrt(2.0))))


def _row_tile(n_p, n_s, cap=512):
    for t in (1024, 512, 256, 128):
        if t > cap:
            continue
        if n_p % t == 0 and n_s % t == 0:
            return t
    raise ValueError(f"token counts {n_p}, {n_s} must be multiples of 128")


def _group_specs(tm, width, n_p):
    pt = n_p // tm
    return [pl.BlockSpec((tm, width), lambda i, *_: (jnp.minimum(i, pt - 1), 0)),
            pl.BlockSpec((tm, width), lambda i, *_: (jnp.maximum(i - pt, 0), 0))]


def _group_pick(prompt_ref, sample_ref, n_p):
    pt = n_p // prompt_ref.shape[0]
    return jnp.where(pl.program_id(0) < pt, prompt_ref[...], sample_ref[...])


def _in_proj_kernel(xp_ref, xs_ref, g_ref, w_ref, wab_ref, z_ref, ab_ref, h_scr, *, n_p):
    @pl.when(pl.program_id(1) == 0)
    def _():
        hb = _rms(_group_pick(xp_ref, xs_ref, n_p), g_ref[...]).astype(BF16)
        h_scr[...] = hb
        ab_ref[...] = jnp.dot(hb, wab_ref[...], preferred_element_type=F32)

    z_ref[...] = jnp.dot(h_scr[...], w_ref[...], preferred_element_type=F32).astype(z_ref.dtype)


def _in_proj(xp, xs, g, w_main, w_ab):
    n_p, n_s = xp.shape[0], xs.shape[0]
    n = n_p + n_s
    tm, tn = _row_tile(n_p, n_s, cap=1024), 2048
    cols = w_main.shape[1]
    return pl.pallas_call(
        functools.partial(_in_proj_kernel, n_p=n_p),
        grid=(n // tm, cols // tn),
        in_specs=_group_specs(tm, D_MODEL, n_p) + [
            pl.BlockSpec((1, D_MODEL), lambda i, j: (0, 0)),
            pl.BlockSpec((D_MODEL, tn), lambda i, j: (0, j)),
            pl.BlockSpec((D_MODEL, LANES), lambda i, j: (0, 0)),
        ],
        out_specs=[
            pl.BlockSpec((tm, tn), lambda i, j: (i, j)),
            pl.BlockSpec((tm, LANES), lambda i, j: (i, 0)),
        ],
        out_shape=[jax.ShapeDtypeStruct((n, cols), BF16), jax.ShapeDtypeStruct((n, LANES), F32)],
        scratch_shapes=[pltpu.VMEM((tm, D_MODEL), BF16)],
        compiler_params=_params("parallel", "arbitrary"),
        name="in_proj",
    )(xp, xs, g, w_main, w_ab)


def _delta_kernel(*refs, C, G, carry):
    (qkv_ref, zg_ref, ab_ref, buf_ref, s0_ref, cw_ref, alog_ref, dtb_ref, ong_ref) = refs[:9]
    refs = refs[9:]
    o_ref, snew_ref, bufnew_ref, xc_scr = refs[:4]
    H, DK, DV = DN_HEADS, DN_DK, DN_DV
    T = G * C
    halo = DN_CONV - 1
    base = SUBLANES
    cw = cw_ref[...]

    def conv(window):
        y = window(0) * cw[0:1]
        for i in range(1, DN_CONV):
            y = y + window(i) * cw[i:i + 1]
        return y

    if carry:
        s_scr = refs[4]
        c = pl.program_id(1)
        last = pl.num_programs(1) - 1

        @pl.when(c == 0)
        def _():
            xc_scr[base - halo:base, :] = buf_ref[0]
            s_scr[...] = s0_ref[0]

        xb = qkv_ref[...]
        xf = xb.astype(F32)
        xc_scr[base:base + SUBLANES, :] = xf[0:SUBLANES]
        y_head = conv(lambda i: xc_scr[base - halo + i:base - halo + i + SUBLANES, :])
        ri = lax.broadcasted_iota(jnp.int32, (halo * T, T), 0)
        ci = lax.broadcasted_iota(jnp.int32, (halo * T, T), 1)
        src_row = (ri & (T - 1)) - (halo - (ri >> (T.bit_length() - 1)))
        shifted = jnp.dot(jnp.where(ci == src_row, 1.0, 0.0).astype(BF16), xb, preferred_element_type=F32)
        y = conv(lambda i: shifted[i * T:(i + 1) * T] if i < halo else xf)
        y = jnp.concatenate([y_head, y[SUBLANES:]], axis=0)
        tail = xf[T - halo:T]
        xc_scr[base - halo:base, :] = tail

        @pl.when(c == last)
        def _():
            bufnew_ref[0] = tail
    else:
        ys = []
        x_new = qkv_ref[...].astype(F32)
        for g in range(G):
            xc_scr[g, base - halo:base, :] = buf_ref[g]
            xc_scr[g, base:base + C, :] = x_new[g * C:(g + 1) * C, :]
            ys.append(conv(lambda i: xc_scr[g, base - halo + i:base - halo + i + C, :]))
            bufnew_ref[g] = xc_scr[g, base + C - halo:base + C, :]
        y = jnp.concatenate(ys, axis=0) if G > 1 else ys[0]
    qkv = y * _sigmoid(y)

    ab = ab_ref[...]
    g_all = -jnp.exp(alog_ref[...]) * jax.nn.softplus(ab + dtb_ref[...])
    beta_all = _sigmoid(ab)
    shift = C.bit_length() - 1
    rt = lax.broadcasted_iota(jnp.int32, (T, T), 0)
    ct = lax.broadcasted_iota(jnp.int32, (T, T), 1)
    chunk_tril = ((rt >> shift) == (ct >> shift)) & (rt >= ct)
    gcum = _dot3(chunk_tril.astype(F32), g_all)
    gam_all = jnp.exp(gcum)
    wide = C >= LANES // 2
    if wide:
        gcum_t, beta_t = gcum.T, beta_all.T

    row = lax.broadcasted_iota(jnp.int32, (C, C), 0)
    col = lax.broadcasted_iota(jnp.int32, (C, C), 1)
    incl, strict, eye = row >= col, row > col, row == col
    eyef = eye.astype(F32)
    units = [(g, h) for g in range(G) for h in range(H)]
    rows = lambda a, g: a[g * C:(g + 1) * C]

    qn, kn, vv = [], [], []
    for h in range(H):
        q = qkv[:, h * DK:(h + 1) * DK]
        k = qkv[:, QK_W + h * DK:QK_W + (h + 1) * DK]
        qn.append(q * (lax.rsqrt(jnp.sum(q * q, axis=-1, keepdims=True) + EPS) * (DK ** -0.5)))
        kn.append(k * lax.rsqrt(jnp.sum(k * k, axis=-1, keepdims=True) + EPS))
        vv.append(qkv[:, 2 * QK_W + h * DV:2 * QK_W + (h + 1) * DV])
    qb = [q.astype(BF16) for q in qn]
    kb = [k.astype(BF16) for k in kn]

    gc, bc, gl, a_low, m_intra = {}, {}, {}, {}, {}
    for u in units:
        g, h = u
        gc[u] = rows(gcum, g)[:, h:h + 1]
        bc[u] = rows(beta_all, g)[:, H + h:H + h + 1]
        gl[u] = gc[u][C - 1:C, :]
        if wide:
            gr = gcum_t[h:h + 1, g * C:(g + 1) * C]
            br = beta_t[H + h:H + h + 1, g * C:(g + 1) * C]
        else:
            gr = jnp.sum(jnp.where(eye, gc[u], 0.0), axis=0, keepdims=True)
            br = jnp.sum(jnp.where(eye, bc[u], 0.0), axis=0, keepdims=True)
        db = jnp.where(incl, jnp.exp(gc[u] - gr), 0.0) * br
        k = rows(kb[h], g)
        kq = _dot_nt(jnp.concatenate([k, rows(qb[h], g)], axis=0), k)
        a_low[u] = jnp.where(strict, kq[:C] * db, 0.0)
        m_intra[u] = kq[C:] * db

    t_inv = {u: eyef - a_low[u] for u in units}
    a_pow = {u: _dot(a_low[u], a_low[u]) for u in units}
    n = 2
    while n < C:
        t_inv = {u: t_inv[u] + _dot(t_inv[u], a_pow[u]) for u in units}
        n *= 2
        if n < C:
            a_pow = {u: _dot(a_pow[u], a_pow[u]) for u in units}

    u_base, wq, k_dec = {}, {}, {}
    for u in units:
        g, h = u
        gam = rows(gam_all, g)[:, h:h + 1]
        k = rows(kn[h], g)
        sol = _dot(t_inv[u], jnp.concatenate([rows(vv[h], g), gam * k], axis=1))
        u_base[u] = sol[:, :DV]
        wq[u] = jnp.concatenate([sol[:, DV:], gam * rows(qn[h], g)], axis=0)
        k_dec[u] = k * (bc[u] * jnp.exp(gl[u] - gc[u]))

    state = [s_scr[h] for h in range(H)] if carry else None
    outs = {}
    for g in range(G):
        s_in = state if carry else [s0_ref[g, h] for h in range(H)]
        wqs = [_dot(wq[(g, h)], s_in[h]) for h in range(H)]
        us = [u_base[(g, h)] - wqs[h][:C] for h in range(H)]
        for h in range(H):
            outs[(g, h)] = wqs[h][C:] + _dot(m_intra[(g, h)], us[h])
        s_out = [jnp.exp(gl[(g, h)]) * s_in[h] + _dot_tn(k_dec[(g, h)], us[h]) for h in range(H)]
        if carry:
            state = s_out
        else:
            for h in range(H):
                snew_ref[g, h] = s_out[h]

    for h in range(H):
        o = jnp.concatenate([outs[(g, h)] for g in range(G)], axis=0) if G > 1 else outs[(0, h)]
        zg = zg_ref[:, h * DV:(h + 1) * DV].astype(F32)
        o_ref[:, h * DV:(h + 1) * DV] = (_rms(o, ong_ref[...]) * (zg * _sigmoid(zg))).astype(o_ref.dtype)

    if carry:
        for h in range(H):
            s_scr[h] = state[h]

        @pl.when(c == last)
        def _():
            for h in range(H):
                snew_ref[0, h] = state[h]


def _delta_branch(z, ab, conv_buf, s0, conv_w, alog_p, dtb_p, onorm_g, *, tok0, L, G):
    B = conv_buf.shape[0]
    C = math.gcd(L, DN_CHUNK)
    nc = L // C
    carry = nc > 1
    T = G * C
    blk0 = tok0 // T
    if carry:
        assert nc % G == 0
        grid = (B, nc // G)
        own_blk = lambda b, c: b * (nc // G) + c
        gs = 1
    else:
        assert B % G == 0
        grid = (B // G, 1)
        own_blk = lambda b, c: b
        gs = G
    tok = lambda b, c: (blk0 + own_blk(b, c), 0)
    seq3 = lambda b, c: (b, 0, 0)
    seq4 = lambda b, c: (b, 0, 0, 0)
    const = lambda b, c: (0, 0)
    if carry:
        scratch = [pltpu.VMEM((2 * SUBLANES, QKV_W), F32), pltpu.VMEM((DN_HEADS, DN_DK, DN_DV), F32)]
    else:
        scratch = [pltpu.VMEM((G, SUBLANES + C, QKV_W), F32)]
    return pl.pallas_call(
        functools.partial(_delta_kernel, C=C, G=G, carry=carry),
        grid=grid,
        in_specs=[
            pl.BlockSpec((T, QKV_W), tok),
            pl.BlockSpec((T, V_W), lambda b, c: (blk0 + own_blk(b, c), QKV_W // V_W)),
            pl.BlockSpec((T, LANES), tok),
            pl.BlockSpec((gs, DN_CONV - 1, QKV_W), seq3),
            pl.BlockSpec((gs, DN_HEADS, DN_DK, DN_DV), seq4),
            pl.BlockSpec((DN_CONV, QKV_W), const),
            pl.BlockSpec((1, LANES), const),
            pl.BlockSpec((1, LANES), const),
            pl.BlockSpec((1, DN_DV), const),
        ],
        out_specs=[
            pl.BlockSpec((T, V_W), lambda b, c: (own_blk(b, c), 0)),
            pl.BlockSpec((gs, DN_HEADS, DN_DK, DN_DV), seq4),
            pl.BlockSpec((gs, DN_CONV - 1, QKV_W), seq3),
        ],
        out_shape=[
            jax.ShapeDtypeStruct((B * L, V_W), BF16),
            jax.ShapeDtypeStruct(s0.shape, F32),
            jax.ShapeDtypeStruct(conv_buf.shape, F32),
        ],
        scratch_shapes=scratch,
        compiler_params=_params("parallel", "arbitrary"),
        name=f"delta_rule_c{C}",
    )(z, z, ab, conv_buf, s0, conv_w, alog_p, dtb_p, onorm_g)


def _post_mix_kernel(op_ref, os_ref, gu_ref, gv_ref, ma_ref, mb_ref, xp_ref, xs_ref, lng_ref, lnb_ref,
                     mix_ref, bias_ref, wa_ref, wb_ref, wo_ref, g2_ref, rwh_ref, rwl_ref, rb_ref, tri_ref,
                     x1_ref, h2_ref, gate_ref, idx_ref, v_ref, cnt_ref, us_scr, cnt_scr, *, n_p):
    u = _gelu(gu_ref[...].astype(F32))
    a = _gelu(gv_ref[...].astype(F32))
    ac = a - jnp.mean(a, axis=-1, keepdims=True)
    v = ac * lax.rsqrt(jnp.mean(ac * ac, axis=-1, keepdims=True) + EPS) * lng_ref[...] + lnb_ref[...]
    v_ref[...] = v
    vb = v.astype(BF16)
    gw = GM_WIDTH // GM_GROUPS
    for c in range(u.shape[0] // GM_CHUNK):
        rs = slice(c * GM_CHUNK, (c + 1) * GM_CHUNK)
        for g in range(GM_GROUPS):
            sl = slice(g * gw, (g + 1) * gw)
            s = jnp.dot(mix_ref[0, g], vb[rs, sl], preferred_element_type=F32) + bias_ref[0, :, sl]
            us_scr[rs, sl] = (u[rs, sl] * s).astype(BF16)

    ya = jnp.dot(_group_pick(op_ref, os_ref, n_p).astype(BF16), wa_ref[...], preferred_element_type=F32)
    yb = jnp.dot(us_scr[...], wb_ref[...], preferred_element_type=F32)
    mixed = _sigmoid(ma_ref[...].astype(F32)) * ya + _sigmoid(mb_ref[...].astype(F32)) * yb
    x1 = _group_pick(xp_ref, xs_ref, n_p) + jnp.dot(mixed.astype(BF16), wo_ref[...], preferred_element_type=F32)
    x1_ref[...] = x1
    h2 = _rms(x1, g2_ref[...])
    h2_ref[...] = _pack_bf16_pairs(h2)
    hh, hl = _split(h2)
    d = functools.partial(jnp.dot, preferred_element_type=F32)
    logits = d(hh, rwh_ref[...]) + (d(hh, rwl_ref[...]) + d(hl, rwh_ref[...])) + rb_ref[...]
    lane = lax.broadcasted_iota(jnp.int32, logits.shape, 1).astype(F32)
    vals, idxs = [], []
    for _ in range(TOP_K):
        m = jnp.max(logits, axis=-1, keepdims=True)
        i = jnp.min(jnp.where(logits == m, lane, float(LANES)), axis=-1, keepdims=True)
        vals.append(m)
        idxs.append(i)
        logits = jnp.where(lane == i, -jnp.inf, logits)
    es = [jnp.exp(v - vals[0]) for v in vals]
    tot = es[0]
    for e in es[1:]:
        tot = tot + e

    @pl.when(pl.program_id(0) == 0)
    def _():
        cnt_scr[...] = jnp.zeros(cnt_scr.shape, F32)

    onehot = (lane == idxs[0]).astype(F32)
    for k in range(1, TOP_K):
        onehot = onehot + (lane == idxs[k]).astype(F32)
    before = jnp.dot(tri_ref[...], onehot.astype(BF16), preferred_element_type=F32) + cnt_scr[...]
    cnt_scr[...] = cnt_scr[...] + jnp.sum(onehot, axis=0, keepdims=True)
    cnt_ref[...] = cnt_scr[...]

    gates = jnp.zeros(logits.shape, F32)
    route = jnp.zeros(logits.shape, F32)
    for k in range(TOP_K):
        rank = jnp.sum(jnp.where(lane == idxs[k], before, 0.0), axis=-1, keepdims=True)
        gates = jnp.where(lane == float(k), es[k] / tot, gates)
        route = jnp.where(lane == float(k), idxs[k], route)
        route = jnp.where(lane == float(TOP_K + k), rank, route)
    gate_ref[...] = gates
    idx_ref[...] = route.T[:2 * TOP_K].astype(jnp.int32)


def _post_mix(o_p, o_s, z, xp, xs, ln_g, ln_b, mix, bias, wa, wb, wo, g2, rwh, rwl, rb):
    n_p, n_s = xp.shape[0], xs.shape[0]
    n = n_p + n_s
    tm = _row_tile(n_p, n_s)
    pt = n_p // tm
    tri = jnp.tril(jnp.ones((tm, tm), BF16), k=-1)
    grp = lambda i: jnp.where(i < pt, 0, 1)
    once = pl.Buffered(1)
    tok = pl.BlockSpec((tm, D_MODEL), lambda i: (i, 0))
    zcol = lambda c: pl.BlockSpec((tm, D_MODEL), lambda i: (i, c))
    full = lambda r, c: pl.BlockSpec((r, c), lambda i: (0, 0), pipeline_mode=once)
    narrow = pl.BlockSpec((tm, LANES), lambda i: (i, 0))
    op_spec, os_spec = _group_specs(tm, V_W, n_p)
    xp_spec, xs_spec = _group_specs(tm, D_MODEL, n_p)
    return pl.pallas_call(
        functools.partial(_post_mix_kernel, n_p=n_p),
        grid=(n // tm,),
        in_specs=[
            op_spec, os_spec, zcol(4), zcol(5), zcol(6), zcol(7), xp_spec, xs_spec,
            full(1, GM_WIDTH), full(1, GM_WIDTH),
            pl.BlockSpec((1, GM_GROUPS, GM_CHUNK, GM_CHUNK), lambda i: (grp(i), 0, 0, 0)),
            pl.BlockSpec((1, GM_CHUNK, GM_WIDTH), lambda i: (grp(i), 0, 0)),
            full(V_W, D_MODEL), full(GM_WIDTH, D_MODEL), full(D_MODEL, D_MODEL), full(1, D_MODEL),
            full(D_MODEL, LANES), full(D_MODEL, LANES), full(1, LANES), full(tm, tm),
        ],
        out_specs=[tok, pl.BlockSpec((tm, D_MODEL // 2), lambda i: (i, 0)), narrow,
                   pl.BlockSpec((2 * TOP_K, tm), lambda i: (0, i)),
                   pl.BlockSpec((tm, GM_WIDTH), lambda i: (jnp.maximum(i - pt, 0), 0)),
                   pl.BlockSpec((1, LANES), lambda i: (0, 0))],
        out_shape=[
            jax.ShapeDtypeStruct((n, D_MODEL), F32),
            jax.ShapeDtypeStruct((n, D_MODEL // 2), jnp.int32),
            jax.ShapeDtypeStruct((n, LANES), F32),
            jax.ShapeDtypeStruct((2 * TOP_K, n), jnp.int32),
            jax.ShapeDtypeStruct((n_s, GM_WIDTH), F32),
            jax.ShapeDtypeStruct((1, LANES), F32),
        ],
        scratch_shapes=[pltpu.VMEM((tm, GM_WIDTH), BF16), pltpu.VMEM((1, LANES), F32)],
        compiler_params=_params("arbitrary"),
        name="chunk_mlp_post_mix_router",
    )(o_p, o_s, z, z, z, z, xp, xs, ln_g, ln_b, mix, bias, wa, wb, wo, g2, rwh, rwl, rb, tri)


def _moe_kernel(be_ref, first_ref, valid_ref, next_ref, slot_ref, x_ref, w1_hbm, b1_hbm, w2_hbm, b2_hbm, y_ref,
                w1_stage, w2_stage, b1_all, b2_all, sems, bias_sems):
    b = pl.program_id(0)
    slot = slot_ref[b]

    def weight_copies(e, s):
        return (pltpu.make_async_copy(w1_hbm.at[e], w1_stage.at[s], sems.at[0, s]),
                pltpu.make_async_copy(w2_hbm.at[e], w2_stage.at[s], sems.at[1, s]))

    @pl.when(b == 0)
    def _():
        for c in weight_copies(be_ref[0], slot):
            c.start()
        bias_copies = (pltpu.make_async_copy(b1_hbm, b1_all, bias_sems.at[0]),
                       pltpu.make_async_copy(b2_hbm, b2_all, bias_sems.at[1]))
        for c in bias_copies:
            c.start()
        for c in bias_copies:
            c.wait()

    @pl.when(first_ref[b] == 1)
    def _():
        for c in weight_copies(be_ref[b], slot):
            c.wait()

        @pl.when(next_ref[b] >= 0)
        def _():
            for c in weight_copies(next_ref[b], 1 - slot):
                c.start()

    @pl.when(valid_ref[b] == 1)
    def _():
        x = jnp.concatenate(_unpack_bf16_pairs(x_ref[...]), axis=1).astype(BF16)
        hid = jnp.dot(x, w1_stage[slot].astype(BF16), preferred_element_type=F32) + b1_all[be_ref[b]]
        gate = jnp.minimum(hid[:, :D_EXPERT], SWIGLU_LIMIT)
        up = jnp.clip(hid[:, D_EXPERT:], -SWIGLU_LIMIT, SWIGLU_LIMIT)
        act = gate * _sigmoid(SWIGLU_ALPHA * gate) * (up + 1.0)
        y = jnp.dot(act.astype(BF16), w2_stage[slot].astype(BF16), preferred_element_type=F32) + b2_all[be_ref[b]]
        y_ref[...] = _pack_bf16_pairs(y)

    @pl.when(valid_ref[b] == 0)
    def _():
        y_ref[...] = jnp.zeros(y_ref.shape, jnp.int32)


def _moe_experts(block_e, first, valid, next_e, slot, xb, w1, b1, w2, b2):
    rows = xb.shape[0]
    nb = rows // MOE_ROWS
    smem4 = lambda f: (lambda b, be, fi, va, ne, sl: f(b, be))
    return pl.pallas_call(
        _moe_kernel,
        grid_spec=pltpu.PrefetchScalarGridSpec(
            num_scalar_prefetch=5,
            grid=(nb,),
            in_specs=[
                pl.BlockSpec((MOE_ROWS, D_MODEL // 2), smem4(lambda b, be: (b, 0))),
                pl.BlockSpec(memory_space=pl.ANY),
                pl.BlockSpec(memory_space=pl.ANY),
                pl.BlockSpec(memory_space=pl.ANY),
                pl.BlockSpec(memory_space=pl.ANY),
            ],
            out_specs=pl.BlockSpec((MOE_ROWS, D_MODEL // 2), smem4(lambda b, be: (b, 0))),
            scratch_shapes=[
                pltpu.VMEM((2, D_MODEL, 2 * D_EXPERT), F32),
                pltpu.VMEM((2, D_EXPERT, D_MODEL), F32),
                pltpu.VMEM(b1.shape, F32),
                pltpu.VMEM(b2.shape, F32),
                pltpu.SemaphoreType.DMA((2, 2)),
                pltpu.SemaphoreType.DMA((2,)),
            ],
        ),
        out_shape=jax.ShapeDtypeStruct((rows, D_MODEL // 2), jnp.int32),
        compiler_params=_params("arbitrary"),
        name="moe_experts",
    )(block_e, first, valid, next_e, slot, xb, w1, b1, w2, b2)


def _moe_dispatch(idx, rank, counts, n):
    experts = jnp.arange(N_EXPERTS, dtype=jnp.int32)
    padded = (counts + MOE_ROWS - 1) // MOE_ROWS * MOE_ROWS
    pad_end = jnp.cumsum(padded)
    pad_start = pad_end - padded
    start_of = jnp.sum(jnp.where(idx[None] == experts[:, None, None], pad_start[:, None, None], 0), axis=0)
    dest_t = start_of + rank
    nb = -(-n * TOP_K // MOE_ROWS) + N_EXPERTS
    rows = nb * MOE_ROWS
    starts = jnp.arange(nb, dtype=jnp.int32) * MOE_ROWS
    valid = (starts < pad_end[-1]).astype(jnp.int32)
    owner = lambda r: jnp.minimum(jnp.sum((pad_end[None, :] <= r[:, None]).astype(jnp.int32), axis=1),
                                  N_EXPERTS - 1)
    last_e = owner(pad_end[-1:] - 1)[0]
    block_e = jnp.where(valid == 1, owner(starts), last_e).astype(jnp.int32)
    first = jnp.concatenate([jnp.ones((1,), jnp.int32),
                             (block_e[1:] != block_e[:-1]).astype(jnp.int32)])
    blk = jnp.arange(nb, dtype=jnp.int32)
    later_first = (blk[None, :] > blk[:, None]) & (first[None, :] == 1)
    next_pos = jnp.min(jnp.where(later_first, blk[None, :], nb), axis=1)
    next_e = jnp.where(next_pos < nb, block_e[jnp.minimum(next_pos, nb - 1)], -1).astype(jnp.int32)
    slot = (jnp.cumsum(first) - 1) % 2
    return dest_t, rows, block_e, first, valid, next_e, slot.astype(jnp.int32)


def _sc_scatter_rows(src, dest_t, out_rows):
    n, d = src.shape
    nk = dest_t.shape[0]
    assert nk * SC_TOKENS == SC_WINDOW
    idx = dest_t.reshape(nk, n // SC_TOKENS, SC_TOKENS).transpose(1, 0, 2).reshape(n // SC_TOKENS, SC_WINDOW)
    mesh = plsc.VectorSubcoreMesh(core_axis_name="core", subcore_axis_name="subcore",
                                  num_cores=SC_CORES, num_subcores=SC_SUBCORES)

    @pl.kernel(out_type=jax.ShapeDtypeStruct((out_rows, d), src.dtype), mesh=mesh, scratch_types=[])
    def scatter_rows(src_hbm, di_hbm, out_hbm):
        def body(x_vmem, di_vmem):
            for k in range(nk):
                pltpu.sync_copy(x_vmem, out_hbm.at[di_vmem.at[0, pl.ds(k * SC_TOKENS, SC_TOKENS)]])

        pltpu.emit_pipeline(
            body,
            grid=(n // SC_TOKENS,),
            in_specs=[pl.BlockSpec((SC_TOKENS, d), lambda i: (i, 0)),
                      pl.BlockSpec((1, SC_WINDOW), lambda i: (i, 0))],
            out_specs=[],
            core_axis_name=("core", "subcore"),
            dimension_semantics=(pltpu.PARALLEL,),
        )(src_hbm, di_hbm)

    return scatter_rows(src, idx)


def _sc_gather_rows(src, idx):
    m = idx.shape[0]
    d = src.shape[1]
    idx_rows = jnp.pad(idx.reshape(m // SC_CHUNK, SC_CHUNK), ((0, 0), (0, SC_WINDOW - SC_CHUNK)))
    mesh = plsc.VectorSubcoreMesh(core_axis_name="core", subcore_axis_name="subcore",
                                  num_cores=SC_CORES, num_subcores=SC_SUBCORES)

    @pl.kernel(out_type=jax.ShapeDtypeStruct((m, d), src.dtype), mesh=mesh, scratch_types=[])
    def gather_rows(src_hbm, si_hbm, out_hbm):
        def body(si_vmem, o_vmem):
            pltpu.sync_copy(src_hbm.at[si_vmem.at[0, pl.ds(0, SC_CHUNK)]], o_vmem)

        pltpu.emit_pipeline(
            body,
            grid=(m // SC_CHUNK,),
            in_specs=[pl.BlockSpec((1, SC_WINDOW), lambda i: (i, 0))],
            out_specs=[pl.BlockSpec((SC_CHUNK, d), lambda i: (i, 0))],
            core_axis_name=("core", "subcore"),
            dimension_semantics=(pltpu.PARALLEL,),
        )(si_hbm, out_hbm)

    return gather_rows(src, idx_rows)


def _tail_kernel(x1_ref, yk_ref, gate_ref, pp_ref, ps_ref, g3_ref, wg_ref, wp_ref, gf_ref,
                 yp_ref, ys_ref, *, n_p):
    tm = x1_ref.shape[0]
    parts = [slice(p * tm // TAIL_PARTS, (p + 1) * tm // TAIL_PARTS) for p in range(TAIL_PARTS)]
    in_prompt = pl.program_id(0) < n_p // tm
    d = functools.partial(jnp.dot, preferred_element_type=F32)

    def combine(r):
        gates = gate_ref[r, :]
        lo, hi = _unpack_bf16_pairs(yk_ref[0, r, :])
        lo, hi = lo * gates[:, 0:1], hi * gates[:, 0:1]
        for k in range(1, TOP_K):
            lo_k, hi_k = _unpack_bf16_pairs(yk_ref[k, r, :])
            lo, hi = lo + lo_k * gates[:, k:k + 1], hi + hi_k * gates[:, k:k + 1]
        return x1_ref[r, :] + jnp.concatenate([lo, hi], axis=1)

    x2 = [combine(r) for r in parts]
    h3 = [_rms(x, g3_ref[...]) for x in x2]
    gate = [_sigmoid(d(h.astype(BF16), wg_ref[...])) for h in h3]
    pe = [d(jnp.where(in_prompt, pp_ref[r, :], ps_ref[r, :]).astype(BF16), wp_ref[...]) for r in parts]
    y = [_rms(x + g * p, gf_ref[...]) for x, g, p in zip(x2, gate, pe)]

    @pl.when(in_prompt)
    def _():
        for r, yy in zip(parts, y):
            yp_ref[r, :] = yy

    @pl.when(jnp.logical_not(in_prompt))
    def _():
        for r, yy in zip(parts, y):
            ys_ref[r, :] = yy


def _tail(x1, yk, gates, pp, ps, g3, wg, wp, gf):
    n_p, n_s = pp.shape[0], ps.shape[0]
    n = n_p + n_s
    tm = _row_tile(n_p, n_s)
    tok = pl.BlockSpec((tm, D_MODEL), lambda i: (i, 0))
    full = lambda r, c: pl.BlockSpec((r, c), lambda i: (0, 0))
    return pl.pallas_call(
        functools.partial(_tail_kernel, n_p=n_p),
        grid=(n // tm,),
        in_specs=[tok, pl.BlockSpec((TOP_K, tm, D_MODEL // 2), lambda i: (0, i, 0)),
                  pl.BlockSpec((tm, LANES), lambda i: (i, 0)),
                  *_group_specs(tm, PLE_DIM, n_p),
                  full(1, D_MODEL), full(D_MODEL, D_MODEL), full(PLE_DIM, D_MODEL), full(1, D_MODEL)],
        out_specs=_group_specs(tm, D_MODEL, n_p),
        out_shape=[jax.ShapeDtypeStruct((n_p, D_MODEL), F32), jax.ShapeDtypeStruct((n_s, D_MODEL), F32)],
        compiler_params=_params("arbitrary"),
        name="ple_final_norm",
    )(x1, yk, gates, pp, ps, g3, wg, wp, gf)


def _lane_pad(v, offset, fill=0.0):
    out = jnp.full((1, LANES), fill, F32)
    return out.at[0, offset:offset + v.shape[0]].set(v.astype(F32))


def kernel(x_prompt, x_sample, state_delta, state_conv, p_prompt, p_sample, norm1_g, w_in, conv_w, a_log, dt_bias, dn_norm_g, w_proj_a, gm_ln_g, gm_ln_b, gm_ws, gm_bs, w_proj_b, w_out, norm2_g, router_w, router_b, moe_w1, moe_b1, moe_w2, moe_b2, norm3_g, ple_w, ple_gate_w, final_norm_g):
    bp, lp, d = x_prompt.shape
    bs, ls, _ = x_sample.shape
    depth = w_in.shape[0]
    assert depth == 1 and d == D_MODEL
    assert lp % GM_CHUNK == 0 and GM_CHUNK % ls == 0 and ls >= DN_CONV - 1
    n_p, n_s = bp * lp, bs * ls
    n = n_p + n_s
    i = 0

    xp, xs = x_prompt.reshape(n_p, d), x_sample.reshape(n_s, d)

    ab0 = QKV_W
    w = w_in[i]
    w_main = jnp.concatenate([w[:, :ab0], w[:, ab0 + 2 * DN_HEADS:]], axis=1).astype(BF16)
    w_ab = jnp.pad(w[:, ab0:ab0 + 2 * DN_HEADS], ((0, 0), (0, LANES - 2 * DN_HEADS))).astype(BF16)
    row2 = lambda v: v.reshape(1, -1).astype(F32)

    z, ab = _in_proj(xp, xs, row2(norm1_g[i]), w_main, w_ab)

    alog_p = _lane_pad(a_log[i], 0)
    dtb_p = _lane_pad(dt_bias[i], 0)
    cw = conv_w[i].astype(F32)
    ong = row2(dn_norm_g[i])
    zero_s = jnp.zeros((bp, DN_HEADS, DN_DK, DN_DV), F32)
    zero_buf = jnp.zeros((bp, DN_CONV - 1, QKV_W), F32)
    o_p, sd_p, sc_p = _delta_branch(z, ab, zero_buf, zero_s, cw, alog_p, dtb_p, ong,
                                    tok0=0, L=lp, G=4)
    o_s, sd_s, sc_s = _delta_branch(z, ab, state_conv[i], state_delta[i], cw, alog_p, dtb_p, ong,
                                    tok0=n_p, L=ls, G=8)

    t = GM_CHUNK
    tri = jnp.tril(jnp.ones((t, t), bool))
    ws = gm_ws[i]
    mix_p = jnp.where(tri, ws, 0.0)
    small = jnp.where(tri[:ls, :ls], ws[:, :ls, :ls], 0.0)
    mix_s = jnp.einsum('ab,gts->gatbs', jnp.eye(t // ls, dtype=F32), small).reshape(GM_GROUPS, t, t)
    mix = jnp.stack([mix_p, mix_s]).astype(BF16)
    gw = GM_WIDTH // GM_GROUPS
    bias_p = jnp.repeat(gm_bs[i].T, gw, axis=1)
    bias_s = jnp.tile(bias_p[:ls], (t // ls, 1))
    bias = jnp.stack([bias_p, bias_s]).astype(F32)
    rw = jnp.pad(router_w[i].astype(F32), ((0, 0), (0, LANES - N_EXPERTS)))
    rwh = rw.astype(BF16)
    rwl = (rw - rwh.astype(F32)).astype(BF16)
    rb = _lane_pad(router_b[i], 0, fill=-jnp.inf)
    x1, h2, gates, route, v_s, counts = _post_mix(
        o_p, o_s, z, xp, xs, row2(gm_ln_g[i]), row2(gm_ln_b[i]), mix, bias, w_proj_a[i].astype(BF16),
        w_proj_b[i].astype(BF16), w_out[i].astype(BF16), row2(norm2_g[i]), rwh, rwl, rb)
    dest_t, rows, block_e, first, valid, next_e, slot = _moe_dispatch(
        route[:TOP_K], route[TOP_K:], counts[0, :N_EXPERTS].astype(jnp.int32), n)
    xb = _sc_scatter_rows(h2, dest_t, rows)
    yb = _moe_experts(block_e, first, valid, next_e, slot, xb, moe_w1[i], moe_b1[i][:, None, :],
                      moe_w2[i], moe_b2[i][:, None, :])
    yk = _sc_gather_rows(yb, dest_t.reshape(TOP_K * n)).reshape(TOP_K, n, d // 2)

    y_p, y_s = _tail(x1, yk, gates, p_prompt[i].reshape(n_p, PLE_DIM), p_sample[i].reshape(n_s, PLE_DIM),
                     row2(norm3_g[i]), ple_gate_w[i].astype(BF16), ple_w[i].astype(BF16), row2(final_norm_g))

    return (y_p.reshape(bp, lp, d), y_s.reshape(bs, ls, d),
            sd_p[None], sc_p[None], sd_s[None], sc_s[None], v_s.reshape(1, bs, ls, GM_WIDTH))
```

```python
import functools
import math

import jax
import jax.numpy as jnp
from jax import lax
from jax.experimental import pallas as pl
from jax.experimental.pallas import tpu as pltpu
from jax.experimental.pallas import tpu_sc as plsc

F32 = jnp.float32
BF16 = jnp.bfloat16

D_MODEL = 1024
DN_HEADS = 8
DN_DK = 128
DN_DV = 128
DN_CONV = 4
DN_CHUNK = 64
GM_WIDTH = 1024
GM_GROUPS = 8
GM_CHUNK = 128
N_EXPERTS = 32
TOP_K = 4
D_EXPERT = 1024
SWIGLU_LIMIT = 7.0
SWIGLU_ALPHA = 1.702
PLE_DIM = 256
EPS = 1e-6
QK_W = DN_HEADS * DN_DK
V_W = DN_HEADS * DN_DV
QKV_W = 2 * QK_W + V_W

LANES = 128
SUBLANES = 8
TAIL_PARTS = 4
MOE_HIDDEN_CHUNKS = 2
MOE_ROWS = 512
VMEM_LIMIT = 56 << 20
SC_CORES, SC_SUBCORES = 2, 16
SC_WINDOW = 128
SC_TOKENS = SC_WINDOW // TOP_K
SC_CHUNK = 64


def _params(*sem):
    return pltpu.CompilerParams(dimension_semantics=sem, vmem_limit_bytes=VMEM_LIMIT)


def _dot(a, b):
    return jnp.dot(a.astype(BF16), b.astype(BF16), preferred_element_type=F32)


def _split(a):
    hi = a.astype(BF16)
    return hi, (a - hi.astype(F32)).astype(BF16)


def _dot3(a, b):
    ah, al = _split(a)
    bh, bl = _split(b)
    d = functools.partial(jnp.dot, preferred_element_type=F32)
    return d(ah, bh) + (d(ah, bl) + d(al, bh))


def _dot_nt(a, b):
    return lax.dot_general(a.astype(BF16), b.astype(BF16), (((1,), (1,)), ((), ())),
                           preferred_element_type=F32)


def _dot_tn(a, b):
    return lax.dot_general(a.astype(BF16), b.astype(BF16), (((0,), (0,)), ((), ())),
                           preferred_element_type=F32)


def _sigmoid(x):
    return 0.5 * jnp.tanh(0.5 * x) + 0.5


def _pack_bf16_pairs(x):
    bits = lax.bitcast_convert_type(x.astype(BF16).astype(F32), jnp.uint32)
    half = x.shape[1] // 2
    return lax.bitcast_convert_type((bits[:, :half] >> 16) | bits[:, half:], jnp.int32)


def _unpack_bf16_pairs(w):
    bits = lax.bitcast_convert_type(w, jnp.uint32)
    return (lax.bitcast_convert_type(bits << 16, F32),
            lax.bitcast_convert_type(bits & jnp.uint32(0xFFFF0000), F32))


def _rms(x, g):
    return x * lax.rsqrt(jnp.mean(x * x, axis=-1, keepdims=True) + EPS) * g


def _gelu(x):
    return 0.5 * x * (1.0 + lax.erf(x * (1.0 / math.sqrt(2.0))))


def _row_tile(n_p, n_s, cap=512):
    for t in (1024, 512, 256, 128):
        if t > cap:
            continue
        if n_p % t == 0 and n_s % t == 0:
            return t
    raise ValueError(f"token counts {n_p}, {n_s} must be multiples of 128")


def _group_specs(tm, width, n_p):
    pt = n_p // tm
    return [pl.BlockSpec((tm, width), lambda i, *_: (jnp.minimum(i, pt - 1), 0)),
            pl.BlockSpec((tm, width), lambda i, *_: (jnp.maximum(i - pt, 0), 0))]


def _group_pick(prompt_ref, sample_ref, n_p):
    pt = n_p // prompt_ref.shape[0]
    return jnp.where(pl.program_id(0) < pt, prompt_ref[...], sample_ref[...])


def _in_proj_kernel(xp_ref, xs_ref, g_ref, w_ref, wab_ref, z_ref, ab_ref, h_scr, *, n_p):
    @pl.when(pl.program_id(1) == 0)
    def _():
        hb = _rms(_group_pick(xp_ref, xs_ref, n_p), g_ref[...]).astype(BF16)
        h_scr[...] = hb
        ab_ref[...] = jnp.dot(hb, wab_ref[...], preferred_element_type=F32)

    z_ref[...] = jnp.dot(h_scr[...], w_ref[...], preferred_element_type=F32).astype(z_ref.dtype)


def _in_proj(xp, xs, g, w_main, w_ab):
    n_p, n_s = xp.shape[0], xs.shape[0]
    n = n_p + n_s
    tm, tn = _row_tile(n_p, n_s, cap=1024), 2048
    cols = w_main.shape[1]
    return pl.pallas_call(
        functools.partial(_in_proj_kernel, n_p=n_p),
        grid=(n // tm, cols // tn),
        in_specs=_group_specs(tm, D_MODEL, n_p) + [
            pl.BlockSpec((1, D_MODEL), lambda i, j: (0, 0)),
            pl.BlockSpec((D_MODEL, tn), lambda i, j: (0, j)),
            pl.BlockSpec((D_MODEL, LANES), lambda i, j: (0, 0)),
        ],
        out_specs=[
            pl.BlockSpec((tm, tn), lambda i, j: (i, j)),
            pl.BlockSpec((tm, LANES), lambda i, j: (i, 0)),
        ],
        out_shape=[jax.ShapeDtypeStruct((n, cols), BF16), jax.ShapeDtypeStruct((n, LANES), F32)],
        scratch_shapes=[pltpu.VMEM((tm, D_MODEL), BF16)],
        compiler_params=_params("parallel", "arbitrary"),
        name="in_proj",
    )(xp, xs, g, w_main, w_ab)


def _delta_kernel(*refs, C, G, carry):
    (qkv_ref, zg_ref, ab_ref, buf_ref, s0_ref, cw_ref, alog_ref, dtb_ref, ong_ref) = refs[:9]
    refs = refs[9:]
    o_ref, snew_ref, bufnew_ref, xc_scr = refs[:4]
    H, DK, DV = DN_HEADS, DN_DK, DN_DV
    T = G * C
    halo = DN_CONV - 1
    base = SUBLANES
    cw = cw_ref[...]

    def conv(window):
        y = window(0) * cw[0:1]
        for i in range(1, DN_CONV):
            y = y + window(i) * cw[i:i + 1]
        return y

    if carry:
        s_scr = refs[4]
        c = pl.program_id(1)
        last = pl.num_programs(1) - 1

        @pl.when(c == 0)
        def _():
            xc_scr[base - halo:base, :] = buf_ref[0]
            s_scr[...] = s0_ref[0]

        xb = qkv_ref[...]
        xf = xb.astype(F32)
        xc_scr[base:base + SUBLANES, :] = xf[0:SUBLANES]
        y_head = conv(lambda i: xc_scr[base - halo + i:base - halo + i + SUBLANES, :])
        ri = lax.broadcasted_iota(jnp.int32, (halo * T, T), 0)
        ci = lax.broadcasted_iota(jnp.int32, (halo * T, T), 1)
        src_row = (ri & (T - 1)) - (halo - (ri >> (T.bit_length() - 1)))
        shifted = jnp.dot(jnp.where(ci == src_row, 1.0, 0.0).astype(BF16), xb, preferred_element_type=F32)
        y = conv(lambda i: shifted[i * T:(i + 1) * T] if i < halo else xf)
        y = jnp.concatenate([y_head, y[SUBLANES:]], axis=0)
        tail = xf[T - halo:T]
        xc_scr[base - halo:base, :] = tail

        @pl.when(c == last)
        def _():
            bufnew_ref[0] = tail
    else:
        ys = []
        x_new = qkv_ref[...].astype(F32)
        for g in range(G):
            xc_scr[g, base - halo:base, :] = buf_ref[g]
            xc_scr[g, base:base + C, :] = x_new[g * C:(g + 1) * C, :]
            ys.append(conv(lambda i: xc_scr[g, base - halo + i:base - halo + i + C, :]))
            bufnew_ref[g] = xc_scr[g, base + C - halo:base + C, :]
        y = jnp.concatenate(ys, axis=0) if G > 1 else ys[0]
    qkv = y * _sigmoid(y)

    ab = ab_ref[...]
    g_all = -jnp.exp(alog_ref[...]) * jax.nn.softplus(ab + dtb_ref[...])
    beta_all = _sigmoid(ab)
    shift = C.bit_length() - 1
    rt = lax.broadcasted_iota(jnp.int32, (T, T), 0)
    ct = lax.broadcasted_iota(jnp.int32, (T, T), 1)
    chunk_tril = ((rt >> shift) == (ct >> shift)) & (rt >= ct)
    gcum = _dot3(chunk_tril.astype(F32), g_all)
    gam_all = jnp.exp(gcum)
    wide = C >= LANES // 2
    if wide:
        gcum_t, beta_t = gcum.T, beta_all.T

    row = lax.broadcasted_iota(jnp.int32, (C, C), 0)
    col = lax.broadcasted_iota(jnp.int32, (C, C), 1)
    incl, strict, eye = row >= col, row > col, row == col
    eyef = eye.astype(F32)
    units = [(g, h) for g in range(G) for h in range(H)]
    rows = lambda a, g: a[g * C:(g + 1) * C]

    qn, kn, vv = [], [], []
    for h in range(H):
        q = qkv[:, h * DK:(h + 1) * DK]
        k = qkv[:, QK_W + h * DK:QK_W + (h + 1) * DK]
        qn.append(q * (lax.rsqrt(jnp.sum(q * q, axis=-1, keepdims=True) + EPS) * (DK ** -0.5)))
        kn.append(k * lax.rsqrt(jnp.sum(k * k, axis=-1, keepdims=True) + EPS))
        vv.append(qkv[:, 2 * QK_W + h * DV:2 * QK_W + (h + 1) * DV])
    qb = [q.astype(BF16) for q in qn]
    kb = [k.astype(BF16) for k in kn]

    gc, bc, gl, a_low, m_intra = {}, {}, {}, {}, {}
    for u in units:
        g, h = u
        gc[u] = rows(gcum, g)[:, h:h + 1]
        bc[u] = rows(beta_all, g)[:, H + h:H + h + 1]
        gl[u] = gc[u][C - 1:C, :]
        if wide:
            gr = gcum_t[h:h + 1, g * C:(g + 1) * C]
            br = beta_t[H + h:H + h + 1, g * C:(g + 1) * C]
        else:
            gr = jnp.sum(jnp.where(eye, gc[u], 0.0), axis=0, keepdims=True)
            br = jnp.sum(jnp.where(eye, bc[u], 0.0), axis=0, keepdims=True)
        db = jnp.where(incl, jnp.exp(gc[u] - gr), 0.0) * br
        k = rows(kb[h], g)
        kq = _dot_nt(jnp.concatenate([k, rows(qb[h], g)], axis=0), k)
        a_low[u] = jnp.where(strict, kq[:C] * db, 0.0)
        m_intra[u] = kq[C:] * db

    t_inv = {u: eyef - a_low[u] for u in units}
    a_pow = {u: _dot(a_low[u], a_low[u]) for u in units}
    n = 2
    while n < C:
        t_inv = {u: t_inv[u] + _dot(t_inv[u], a_pow[u]) for u in units}
        n *= 2
        if n < C:
            a_pow = {u: _dot(a_pow[u], a_pow[u]) for u in units}

    u_base, wq, k_dec = {}, {}, {}
    for u in units:
        g, h = u
        gam = rows(gam_all, g)[:, h:h + 1]
        k = rows(kn[h], g)
        sol = _dot(t_inv[u], jnp.concatenate([rows(vv[h], g), gam * k], axis=1))
        u_base[u] = sol[:, :DV]
        wq[u] = jnp.concatenate([sol[:, DV:], gam * rows(qn[h], g)], axis=0)
        k_dec[u] = k * (bc[u] * jnp.exp(gl[u] - gc[u]))

    state = [s_scr[h] for h in range(H)] if carry else None
    outs = {}
    for g in range(G):
        s_in = state if carry else [s0_ref[g, h] for h in range(H)]
        wqs = [_dot(wq[(g, h)], s_in[h]) for h in range(H)]
        us = [u_base[(g, h)] - wqs[h][:C] for h in range(H)]
        for h in range(H):
            outs[(g, h)] = wqs[h][C:] + _dot(m_intra[(g, h)], us[h])
        s_out = [jnp.exp(gl[(g, h)]) * s_in[h] + _dot_tn(k_dec[(g, h)], us[h]) for h in range(H)]
        if carry:
            state = s_out
        else:
            for h in range(H):
                snew_ref[g, h] = s_out[h]

    for h in range(H):
        o = jnp.concatenate([outs[(g, h)] for g in range(G)], axis=0) if G > 1 else outs[(0, h)]
        zg = zg_ref[:, h * DV:(h + 1) * DV].astype(F32)
        o_ref[:, h * DV:(h + 1) * DV] = (_rms(o, ong_ref[...]) * (zg * _sigmoid(zg))).astype(o_ref.dtype)

    if carry:
        for h in range(H):
            s_scr[h] = state[h]

        @pl.when(c == last)
        def _():
            for h in range(H):
                snew_ref[0, h] = state[h]


def _delta_branch(z, ab, conv_buf, s0, conv_w, alog_p, dtb_p, onorm_g, *, tok0, L, G):
    B = conv_buf.shape[0]
    C = math.gcd(L, DN_CHUNK)
    nc = L // C
    carry = nc > 1
    T = G * C
    blk0 = tok0 // T
    if carry:
        assert nc % G == 0
        grid = (B, nc // G)
        own_blk = lambda b, c: b * (nc // G) + c
        gs = 1
    else:
        assert B % G == 0
        grid = (B // G, 1)
        own_blk = lambda b, c: b
        gs = G
    tok = lambda b, c: (blk0 + own_blk(b, c), 0)
    seq3 = lambda b, c: (b, 0, 0)
    seq4 = lambda b, c: (b, 0, 0, 0)
    const = lambda b, c: (0, 0)
    if carry:
        scratch = [pltpu.VMEM((2 * SUBLANES, QKV_W), F32), pltpu.VMEM((DN_HEADS, DN_DK, DN_DV), F32)]
    else:
        scratch = [pltpu.VMEM((G, SUBLANES + C, QKV_W), F32)]
    return pl.pallas_call(
        functools.partial(_delta_kernel, C=C, G=G, carry=carry),
        grid=grid,
        in_specs=[
            pl.BlockSpec((T, QKV_W), tok),
            pl.BlockSpec((T, V_W), lambda b, c: (blk0 + own_blk(b, c), QKV_W // V_W)),
            pl.BlockSpec((T, LANES), tok),
            pl.BlockSpec((gs, DN_CONV - 1, QKV_W), seq3),
            pl.BlockSpec((gs, DN_HEADS, DN_DK, DN_DV), seq4),
            pl.BlockSpec((DN_CONV, QKV_W), const),
            pl.BlockSpec((1, LANES), const),
            pl.BlockSpec((1, LANES), const),
            pl.BlockSpec((1, DN_DV), const),
        ],
        out_specs=[
            pl.BlockSpec((T, V_W), lambda b, c: (own_blk(b, c), 0)),
            pl.BlockSpec((gs, DN_HEADS, DN_DK, DN_DV), seq4),
            pl.BlockSpec((gs, DN_CONV - 1, QKV_W), seq3),
        ],
        out_shape=[
            jax.ShapeDtypeStruct((B * L, V_W), BF16),
            jax.ShapeDtypeStruct(s0.shape, F32),
            jax.ShapeDtypeStruct(conv_buf.shape, F32),
        ],
        scratch_shapes=scratch,
        compiler_params=_params("parallel", "arbitrary"),
        name=f"delta_rule_c{C}",
    )(z, z, ab, conv_buf, s0, conv_w, alog_p, dtb_p, onorm_g)


def _post_mix_kernel(op_ref, os_ref, gu_ref, gv_ref, ma_ref, mb_ref, xp_ref, xs_ref, lng_ref, lnb_ref,
                     mix_ref, bias_ref, wa_ref, wb_ref, wo_ref, g2_ref, rwh_ref, rwl_ref, rb_ref, tri_ref,
                     x1_ref, h2_ref, gate_ref, idx_ref, v_ref, cnt_ref, us_scr, cnt_scr, *, n_p):
    u = _gelu(gu_ref[...].astype(F32))
    a = _gelu(gv_ref[...].astype(F32))
    ac = a - jnp.mean(a, axis=-1, keepdims=True)
    v = ac * lax.rsqrt(jnp.mean(ac * ac, axis=-1, keepdims=True) + EPS) * lng_ref[...] + lnb_ref[...]
    v_ref[...] = v
    vb = v.astype(BF16)
    gw = GM_WIDTH // GM_GROUPS
    for c in range(u.shape[0] // GM_CHUNK):
        rs = slice(c * GM_CHUNK, (c + 1) * GM_CHUNK)
        for g in range(GM_GROUPS):
            sl = slice(g * gw, (g + 1) * gw)
            s = jnp.dot(mix_ref[0, g], vb[rs, sl], preferred_element_type=F32) + bias_ref[0, :, sl]
            us_scr[rs, sl] = (u[rs, sl] * s).astype(BF16)

    ya = jnp.dot(_group_pick(op_ref, os_ref, n_p).astype(BF16), wa_ref[...], preferred_element_type=F32)
    yb = jnp.dot(us_scr[...], wb_ref[...], preferred_element_type=F32)
    mixed = _sigmoid(ma_ref[...].astype(F32)) * ya + _sigmoid(mb_ref[...].astype(F32)) * yb
    x1 = _group_pick(xp_ref, xs_ref, n_p) + jnp.dot(mixed.astype(BF16), wo_ref[...], preferred_element_type=F32)
    x1_ref[...] = x1
    h2 = _rms(x1, g2_ref[...])
    h2_ref[...] = _pack_bf16_pairs(h2)
    hh, hl = _split(h2)
    d = functools.partial(jnp.dot, preferred_element_type=F32)
    logits = d(hh, rwh_ref[...]) + (d(hh, rwl_ref[...]) + d(hl, rwh_ref[...])) + rb_ref[...]
    lane = lax.broadcasted_iota(jnp.int32, logits.shape, 1).astype(F32)
    vals, idxs = [], []
    for _ in range(TOP_K):
        m = jnp.max(logits, axis=-1, keepdims=True)
        i = jnp.min(jnp.where(logits == m, lane, float(LANES)), axis=-1, keepdims=True)
        vals.append(m)
        idxs.append(i)
        logits = jnp.where(lane == i, -jnp.inf, logits)
    es = [jnp.exp(v - vals[0]) for v in vals]
    tot = es[0]
    for e in es[1:]:
        tot = tot + e

    @pl.when(pl.program_id(0) == 0)
    def _():
        cnt_scr[...] = jnp.zeros(cnt_scr.shape, F32)

    onehot = (lane == idxs[0]).astype(F32)
    for k in range(1, TOP_K):
        onehot = onehot + (lane == idxs[k]).astype(F32)
    before = jnp.dot(tri_ref[...], onehot.astype(BF16), preferred_element_type=F32) + cnt_scr[...]
    cnt_scr[...] = cnt_scr[...] + jnp.sum(onehot, axis=0, keepdims=True)
    cnt_ref[...] = cnt_scr[...]

    gates = jnp.zeros(logits.shape, F32)
    route = jnp.zeros(logits.shape, F32)
    for k in range(TOP_K):
        rank = jnp.sum(jnp.where(lane == idxs[k], before, 0.0), axis=-1, keepdims=True)
        gates = jnp.where(lane == float(k), es[k] / tot, gates)
        route = jnp.where(lane == float(k), idxs[k], route)
        route = jnp.where(lane == float(TOP_K + k), rank, route)
    gate_ref[...] = gates
    idx_ref[...] = route.T[:2 * TOP_K].astype(jnp.int32)


def _post_mix(o_p, o_s, z, xp, xs, ln_g, ln_b, mix, bias, wa, wb, wo, g2, rwh, rwl, rb):
    n_p, n_s = xp.shape[0], xs.shape[0]
    n = n_p + n_s
    tm = _row_tile(n_p, n_s)
    pt = n_p // tm
    tri = jnp.tril(jnp.ones((tm, tm), BF16), k=-1)
    grp = lambda i: jnp.where(i < pt, 0, 1)
    once = pl.Buffered(1)
    tok = pl.BlockSpec((tm, D_MODEL), lambda i: (i, 0))
    zcol = lambda c: pl.BlockSpec((tm, D_MODEL), lambda i: (i, c))
    full = lambda r, c: pl.BlockSpec((r, c), lambda i: (0, 0), pipeline_mode=once)
    narrow = pl.BlockSpec((tm, LANES), lambda i: (i, 0))
    op_spec, os_spec = _group_specs(tm, V_W, n_p)
    xp_spec, xs_spec = _group_specs(tm, D_MODEL, n_p)
    return pl.pallas_call(
        functools.partial(_post_mix_kernel, n_p=n_p),
        grid=(n // tm,),
        in_specs=[
            op_spec, os_spec, zcol(4), zcol(5), zcol(6), zcol(7), xp_spec, xs_spec,
            full(1, GM_WIDTH), full(1, GM_WIDTH),
            pl.BlockSpec((1, GM_GROUPS, GM_CHUNK, GM_CHUNK), lambda i: (grp(i), 0, 0, 0)),
            pl.BlockSpec((1, GM_CHUNK, GM_WIDTH), lambda i: (grp(i), 0, 0)),
            full(V_W, D_MODEL), full(GM_WIDTH, D_MODEL), full(D_MODEL, D_MODEL), full(1, D_MODEL),
            full(D_MODEL, LANES), full(D_MODEL, LANES), full(1, LANES), full(tm, tm),
        ],
        out_specs=[tok, pl.BlockSpec((tm, D_MODEL // 2), lambda i: (i, 0)), narrow,
                   pl.BlockSpec((2 * TOP_K, tm), lambda i: (0, i)),
                   pl.BlockSpec((tm, GM_WIDTH), lambda i: (jnp.maximum(i - pt, 0), 0)),
                   pl.BlockSpec((1, LANES), lambda i: (0, 0))],
        out_shape=[
            jax.ShapeDtypeStruct((n, D_MODEL), F32),
            jax.ShapeDtypeStruct((n, D_MODEL // 2), jnp.int32),
            jax.ShapeDtypeStruct((n, LANES), F32),
            jax.ShapeDtypeStruct((2 * TOP_K, n), jnp.int32),
            jax.ShapeDtypeStruct((n_s, GM_WIDTH), F32),
            jax.ShapeDtypeStruct((1, LANES), F32),
        ],
        scratch_shapes=[pltpu.VMEM((tm, GM_WIDTH), BF16), pltpu.VMEM((1, LANES), F32)],
        compiler_params=_params("arbitrary"),
        name="chunk_mlp_post_mix_router",
    )(o_p, o_s, z, z, z, z, xp, xs, ln_g, ln_b, mix, bias, wa, wb, wo, g2, rwh, rwl, rb, tri)


def _moe_kernel(be_ref, first_ref, valid_ref, next_ref, slot_ref, x_ref, w1_hbm, b1_ref, w2_hbm, b2_ref, y_ref,
                w1_stage, w2_stage, sems):
    b = pl.program_id(0)
    slot = slot_ref[b]

    def weight_copies(e, s):
        return (pltpu.make_async_copy(w1_hbm.at[e], w1_stage.at[s], sems.at[0, s]),
                pltpu.make_async_copy(w2_hbm.at[e], w2_stage.at[s], sems.at[1, s]))

    @pl.when(b == 0)
    def _():
        for c in weight_copies(be_ref[0], slot):
            c.start()

    @pl.when(first_ref[b] == 1)
    def _():
        for c in weight_copies(be_ref[b], slot):
            c.wait()

        @pl.when(next_ref[b] >= 0)
        def _():
            for c in weight_copies(next_ref[b], 1 - slot):
                c.start()

    @pl.when(valid_ref[b] == 1)
    def _():
        x = jnp.concatenate(_unpack_bf16_pairs(x_ref[...]), axis=1).astype(BF16)
        cw = D_EXPERT // MOE_HIDDEN_CHUNKS
        d = functools.partial(jnp.dot, preferred_element_type=F32)
        y = b2_ref[0]
        for c in range(MOE_HIDDEN_CHUNKS):
            g_cols = pl.ds(c * cw, cw)
            u_cols = pl.ds(D_EXPERT + c * cw, cw)
            gate = d(x, w1_stage[slot, :, g_cols].astype(BF16)) + b1_ref[0, :, g_cols]
            up = d(x, w1_stage[slot, :, u_cols].astype(BF16)) + b1_ref[0, :, u_cols]
            gate = jnp.minimum(gate, SWIGLU_LIMIT)
            up = jnp.clip(up, -SWIGLU_LIMIT, SWIGLU_LIMIT)
            act = gate * _sigmoid(SWIGLU_ALPHA * gate) * (up + 1.0)
            y = y + d(act.astype(BF16), w2_stage[slot, pl.ds(c * cw, cw), :].astype(BF16))
        y_ref[...] = _pack_bf16_pairs(y)

    @pl.when(valid_ref[b] == 0)
    def _():
        y_ref[...] = jnp.zeros(y_ref.shape, jnp.int32)


def _moe_experts(block_e, first, valid, next_e, slot, xb, w1, b1, w2, b2):
    rows = xb.shape[0]
    nb = rows // MOE_ROWS
    smem4 = lambda f: (lambda b, be, fi, va, ne, sl: f(b, be))
    return pl.pallas_call(
        _moe_kernel,
        grid_spec=pltpu.PrefetchScalarGridSpec(
            num_scalar_prefetch=5,
            grid=(nb,),
            in_specs=[
                pl.BlockSpec((MOE_ROWS, D_MODEL // 2), smem4(lambda b, be: (b, 0))),
                pl.BlockSpec(memory_space=pl.ANY),
                pl.BlockSpec((1, 1, 2 * D_EXPERT), smem4(lambda b, be: (be[b], 0, 0))),
                pl.BlockSpec(memory_space=pl.ANY),
                pl.BlockSpec((1, 1, D_MODEL), smem4(lambda b, be: (be[b], 0, 0))),
            ],
            out_specs=pl.BlockSpec((MOE_ROWS, D_MODEL // 2), smem4(lambda b, be: (b, 0))),
            scratch_shapes=[
                pltpu.VMEM((2, D_MODEL, 2 * D_EXPERT), F32),
                pltpu.VMEM((2, D_EXPERT, D_MODEL), F32),
                pltpu.SemaphoreType.DMA((2, 2)),
            ],
        ),
        out_shape=jax.ShapeDtypeStruct((rows, D_MODEL // 2), jnp.int32),
        compiler_params=_params("arbitrary"),
        name="moe_experts",
    )(block_e, first, valid, next_e, slot, xb, w1, b1, w2, b2)


def _moe_dispatch(idx, rank, counts, n):
    experts = jnp.arange(N_EXPERTS, dtype=jnp.int32)
    padded = (counts + MOE_ROWS - 1) // MOE_ROWS * MOE_ROWS
    pad_end = jnp.cumsum(padded)
    pad_start = pad_end - padded
    start_of = jnp.sum(jnp.where(idx[None] == experts[:, None, None], pad_start[:, None, None], 0), axis=0)
    dest_t = start_of + rank
    nb = -(-n * TOP_K // MOE_ROWS) + N_EXPERTS
    rows = nb * MOE_ROWS
    starts = jnp.arange(nb, dtype=jnp.int32) * MOE_ROWS
    valid = (starts < pad_end[-1]).astype(jnp.int32)
    owner = lambda r: jnp.minimum(jnp.sum((pad_end[None, :] <= r[:, None]).astype(jnp.int32), axis=1),
                                  N_EXPERTS - 1)
    last_e = owner(pad_end[-1:] - 1)[0]
    block_e = jnp.where(valid == 1, owner(starts), last_e).astype(jnp.int32)
    first = jnp.concatenate([jnp.ones((1,), jnp.int32),
                             (block_e[1:] != block_e[:-1]).astype(jnp.int32)])
    blk = jnp.arange(nb, dtype=jnp.int32)
    later_first = (blk[None, :] > blk[:, None]) & (first[None, :] == 1)
    next_pos = jnp.min(jnp.where(later_first, blk[None, :], nb), axis=1)
    next_e = jnp.where(next_pos < nb, block_e[jnp.minimum(next_pos, nb - 1)], -1).astype(jnp.int32)
    slot = (jnp.cumsum(first) - 1) % 2
    return dest_t, rows, block_e, first, valid, next_e, slot.astype(jnp.int32)


def _sc_scatter_rows(src, dest_t, out_rows):
    n, d = src.shape
    nk = dest_t.shape[0]
    assert nk * SC_TOKENS == SC_WINDOW
    idx = dest_t.reshape(nk, n // SC_TOKENS, SC_TOKENS).transpose(1, 0, 2).reshape(n // SC_TOKENS, SC_WINDOW)
    mesh = plsc.VectorSubcoreMesh(core_axis_name="core", subcore_axis_name="subcore",
                                  num_cores=SC_CORES, num_subcores=SC_SUBCORES)

    @pl.kernel(out_type=jax.ShapeDtypeStruct((out_rows, d), src.dtype), mesh=mesh, scratch_types=[])
    def scatter_rows(src_hbm, di_hbm, out_hbm):
        def body(x_vmem, di_vmem):
            for k in range(nk):
                pltpu.sync_copy(x_vmem, out_hbm.at[di_vmem.at[0, pl.ds(k * SC_TOKENS, SC_TOKENS)]])

        pltpu.emit_pipeline(
            body,
            grid=(n // SC_TOKENS,),
            in_specs=[pl.BlockSpec((SC_TOKENS, d), lambda i: (i, 0)),
                      pl.BlockSpec((1, SC_WINDOW), lambda i: (i, 0))],
            out_specs=[],
            core_axis_name=("core", "subcore"),
            dimension_semantics=(pltpu.PARALLEL,),
        )(src_hbm, di_hbm)

    return scatter_rows(src, idx)


def _sc_gather_rows(src, idx):
    m = idx.shape[0]
    d = src.shape[1]
    idx_rows = jnp.pad(idx.reshape(m // SC_CHUNK, SC_CHUNK), ((0, 0), (0, SC_WINDOW - SC_CHUNK)))
    mesh = plsc.VectorSubcoreMesh(core_axis_name="core", subcore_axis_name="subcore",
                                  num_cores=SC_CORES, num_subcores=SC_SUBCORES)

    @pl.kernel(out_type=jax.ShapeDtypeStruct((m, d), src.dtype), mesh=mesh, scratch_types=[])
    def gather_rows(src_hbm, si_hbm, out_hbm):
        def body(si_vmem, o_vmem):
            pltpu.sync_copy(src_hbm.at[si_vmem.at[0, pl.ds(0, SC_CHUNK)]], o_vmem)

        pltpu.emit_pipeline(
            body,
            grid=(m // SC_CHUNK,),
            in_specs=[pl.BlockSpec((1, SC_WINDOW), lambda i: (i, 0))],
            out_specs=[pl.BlockSpec((SC_CHUNK, d), lambda i: (i, 0))],
            core_axis_name=("core", "subcore"),
            dimension_semantics=(pltpu.PARALLEL,),
        )(si_hbm, out_hbm)

    return gather_rows(src, idx_rows)


def _tail_kernel(x1_ref, yk_ref, gate_ref, pp_ref, ps_ref, g3_ref, wg_ref, wp_ref, gf_ref,
                 yp_ref, ys_ref, *, n_p):
    tm = x1_ref.shape[0]
    parts = [slice(p * tm // TAIL_PARTS, (p + 1) * tm // TAIL_PARTS) for p in range(TAIL_PARTS)]
    in_prompt = pl.program_id(0) < n_p // tm
    d = functools.partial(jnp.dot, preferred_element_type=F32)

    def combine(r):
        gates = gate_ref[r, :]
        lo, hi = _unpack_bf16_pairs(yk_ref[0, r, :])
        lo, hi = lo * gates[:, 0:1], hi * gates[:, 0:1]
        for k in range(1, TOP_K):
            lo_k, hi_k = _unpack_bf16_pairs(yk_ref[k, r, :])
            lo, hi = lo + lo_k * gates[:, k:k + 1], hi + hi_k * gates[:, k:k + 1]
        return x1_ref[r, :] + jnp.concatenate([lo, hi], axis=1)

    x2 = [combine(r) for r in parts]
    h3 = [_rms(x, g3_ref[...]) for x in x2]
    gate = [_sigmoid(d(h.astype(BF16), wg_ref[...])) for h in h3]
    pe = [d(jnp.where(in_prompt, pp_ref[r, :], ps_ref[r, :]).astype(BF16), wp_ref[...]) for r in parts]
    y = [_rms(x + g * p, gf_ref[...]) for x, g, p in zip(x2, gate, pe)]

    @pl.when(in_prompt)
    def _():
        for r, yy in zip(parts, y):
            yp_ref[r, :] = yy

    @pl.when(jnp.logical_not(in_prompt))
    def _():
        for r, yy in zip(parts, y):
            ys_ref[r, :] = yy


def _tail(x1, yk, gates, pp, ps, g3, wg, wp, gf):
    n_p, n_s = pp.shape[0], ps.shape[0]
    n = n_p + n_s
    tm = _row_tile(n_p, n_s)
    tok = pl.BlockSpec((tm, D_MODEL), lambda i: (i, 0))
    full = lambda r, c: pl.BlockSpec((r, c), lambda i: (0, 0))
    return pl.pallas_call(
        functools.partial(_tail_kernel, n_p=n_p),
        grid=(n // tm,),
        in_specs=[tok, pl.BlockSpec((TOP_K, tm, D_MODEL // 2), lambda i: (0, i, 0)),
                  pl.BlockSpec((tm, LANES), lambda i: (i, 0)),
                  *_group_specs(tm, PLE_DIM, n_p),
                  full(1, D_MODEL), full(D_MODEL, D_MODEL), full(PLE_DIM, D_MODEL), full(1, D_MODEL)],
        out_specs=_group_specs(tm, D_MODEL, n_p),
        out_shape=[jax.ShapeDtypeStruct((n_p, D_MODEL), F32), jax.ShapeDtypeStruct((n_s, D_MODEL), F32)],
        compiler_params=_params("arbitrary"),
        name="ple_final_norm",
    )(x1, yk, gates, pp, ps, g3, wg, wp, gf)


def _lane_pad(v, offset, fill=0.0):
    out = jnp.full((1, LANES), fill, F32)
    return out.at[0, offset:offset + v.shape[0]].set(v.astype(F32))


def kernel(x_prompt, x_sample, state_delta, state_conv, p_prompt, p_sample, norm1_g, w_in, conv_w, a_log, dt_bias, dn_norm_g, w_proj_a, gm_ln_g, gm_ln_b, gm_ws, gm_bs, w_proj_b, w_out, norm2_g, router_w, router_b, moe_w1, moe_b1, moe_w2, moe_b2, norm3_g, ple_w, ple_gate_w, final_norm_g):
    bp, lp, d = x_prompt.shape
    bs, ls, _ = x_sample.shape
    depth = w_in.shape[0]
    assert depth == 1 and d == D_MODEL
    assert lp % GM_CHUNK == 0 and GM_CHUNK % ls == 0 and ls >= DN_CONV - 1
    n_p, n_s = bp * lp, bs * ls
    n = n_p + n_s
    i = 0

    xp, xs = x_prompt.reshape(n_p, d), x_sample.reshape(n_s, d)

    ab0 = QKV_W
    w = w_in[i]
    w_main = jnp.concatenate([w[:, :ab0], w[:, ab0 + 2 * DN_HEADS:]], axis=1).astype(BF16)
    w_ab = jnp.pad(w[:, ab0:ab0 + 2 * DN_HEADS], ((0, 0), (0, LANES - 2 * DN_HEADS))).astype(BF16)
    row2 = lambda v: v.reshape(1, -1).astype(F32)

    z, ab = _in_proj(xp, xs, row2(norm1_g[i]), w_main, w_ab)

    alog_p = _lane_pad(a_log[i], 0)
    dtb_p = _lane_pad(dt_bias[i], 0)
    cw = conv_w[i].astype(F32)
    ong = row2(dn_norm_g[i])
    zero_s = jnp.zeros((bp, DN_HEADS, DN_DK, DN_DV), F32)
    zero_buf = jnp.zeros((bp, DN_CONV - 1, QKV_W), F32)
    o_p, sd_p, sc_p = _delta_branch(z, ab, zero_buf, zero_s, cw, alog_p, dtb_p, ong,
                                    tok0=0, L=lp, G=4)
    o_s, sd_s, sc_s = _delta_branch(z, ab, state_conv[i], state_delta[i], cw, alog_p, dtb_p, ong,
                                    tok0=n_p, L=ls, G=8)

    t = GM_CHUNK
    tri = jnp.tril(jnp.ones((t, t), bool))
    ws = gm_ws[i]
    mix_p = jnp.where(tri, ws, 0.0)
    small = jnp.where(tri[:ls, :ls], ws[:, :ls, :ls], 0.0)
    mix_s = jnp.einsum('ab,gts->gatbs', jnp.eye(t // ls, dtype=F32), small).reshape(GM_GROUPS, t, t)
    mix = jnp.stack([mix_p, mix_s]).astype(BF16)
    gw = GM_WIDTH // GM_GROUPS
    bias_p = jnp.repeat(gm_bs[i].T, gw, axis=1)
    bias_s = jnp.tile(bias_p[:ls], (t // ls, 1))
    bias = jnp.stack([bias_p, bias_s]).astype(F32)
    rw = jnp.pad(router_w[i].astype(F32), ((0, 0), (0, LANES - N_EXPERTS)))
    rwh = rw.astype(BF16)
    rwl = (rw - rwh.astype(F32)).astype(BF16)
    rb = _lane_pad(router_b[i], 0, fill=-jnp.inf)
    x1, h2, gates, route, v_s, counts = _post_mix(
        o_p, o_s, z, xp, xs, row2(gm_ln_g[i]), row2(gm_ln_b[i]), mix, bias, w_proj_a[i].astype(BF16),
        w_proj_b[i].astype(BF16), w_out[i].astype(BF16), row2(norm2_g[i]), rwh, rwl, rb)
    dest_t, rows, block_e, first, valid, next_e, slot = _moe_dispatch(
        route[:TOP_K], route[TOP_K:], counts[0, :N_EXPERTS].astype(jnp.int32), n)
    xb = _sc_scatter_rows(h2, dest_t, rows)
    yb = _moe_experts(block_e, first, valid, next_e, slot, xb, moe_w1[i], moe_b1[i][:, None, :],
                      moe_w2[i], moe_b2[i][:, None, :])
    yk = _sc_gather_rows(yb, dest_t.reshape(TOP_K * n)).reshape(TOP_K, n, d // 2)

    y_p, y_s = _tail(x1, yk, gates, p_prompt[i].reshape(n_p, PLE_DIM), p_sample[i].reshape(n_s, PLE_DIM),
                     row2(norm3_g[i]), ple_gate_w[i].astype(BF16), ple_w[i].astype(BF16), row2(final_norm_g))

    return (y_p.reshape(bp, lp, d), y_s.reshape(bs, ls, d),
            sd_p[None], sc_p[None], sd_s[None], sc_s[None], v_s.reshape(1, bs, ls, GM_WIDTH))
```
